```python
import jax, jax.numpy as jnp
from jax import lax
import numpy as np

D_MODEL = 2048
BATCH = 1
SEQ = 16384
DEPTH = 2
DEC_BATCH = 16
DEC_SEQ = 32
PAST_LEN = 1024

CHUNK = 64
N_EVEN = (DEPTH + 1) // 2
N_ODD = DEPTH // 2
EPS = 1e-6
F32 = jnp.float32

A_HEADS = 8
A_DK = 128
A_DV = 128
A_WIDTH = A_HEADS * A_DK
B_WIDTH = D_MODEL // 2
CONV_W = 3
IN_SPLITS = (A_WIDTH, 2 * A_WIDTH, 3 * A_WIDTH, 4 * A_WIDTH, 4 * A_WIDTH + B_WIDTH, 4 * A_WIDTH + 2 * B_WIDTH)
IN_COLS = 4 * A_WIDTH + 3 * B_WIDTH
MIX0_OUT = A_WIDTH + B_WIDTH
C_HEADS = 16
Q_LORA = 512
KV_LORA = 512
QK_NOPE = 128
QK_ROPE = 64
V_DIM = 128
MLA_SCALE = (QK_NOPE + QK_ROPE) ** -0.5
ROPE_THETA = 10000.0
Q_BLOCK = 128
N_MEM = 256
M_HEADS = 4
M_DIM = 128
D_FF = 5632
N_NORMS = 8

kernel_name = 'hybrid_streaming_encoder_step'


def rms_norm(x, g):
    xf = x.astype(F32)
    y = xf * lax.rsqrt(jnp.mean(xf * xf, axis=-1, keepdims=True) + EPS)
    return (y * g.astype(F32)).astype(x.dtype)


def swiglu(x, w1, w2):
    gu = x @ w1
    return (jax.nn.silu(gu[..., :D_FF]) * gu[..., D_FF:]) @ w2


def rope(x, pos):
    half = x.shape[-1] // 2
    inv = ROPE_THETA ** (-jnp.arange(half, dtype=F32) / half)
    ang = pos.astype(F32)[:, None] * inv
    ang = ang.reshape(ang.shape[:1] + (1,) * (x.ndim - 3) + (half,))
    cos, sin = jnp.cos(ang), jnp.sin(ang)
    x1, x2 = x[..., :half].astype(F32), x[..., half:].astype(F32)
    return jnp.concatenate([x1 * cos - x2 * sin, x1 * sin + x2 * cos], axis=-1).astype(x.dtype)


def hgrn2_scan(q, k, v, logf, s0):
    B, T, H, _ = q.shape
    L = min(CHUNK, T)
    n = T // L
    to_blocks = lambda a: a.reshape(B, n, L, H, a.shape[-1]).transpose(1, 0, 3, 2, 4)
    causal = jnp.tril(jnp.ones((L, L), dtype=bool))[:, :, None]

    def step(S, blk):
        qc, kc, vc, gc = blk
        b = jnp.cumsum(gc, axis=-2)
        rel = b[..., :, None, :] - b[..., None, :, :]
        decay = jnp.exp(jnp.where(causal, rel, -jnp.inf))
        scores = jnp.einsum('bhtd,bhsd,bhtsd->bhts', qc, kc, decay)
        o = jnp.einsum('bhts,bhse->bhte', scores, vc) + jnp.einsum('bhtd,bhde->bhte', qc * jnp.exp(b), S)
        b_last = b[..., -1:, :]
        S_new = jnp.exp(b_last[..., 0, :])[..., None] * S + jnp.einsum('bhsd,bhse->bhde', kc * jnp.exp(b_last - b), vc)
        return S_new, o

    S, ob = lax.scan(step, s0, (to_blocks(q), to_blocks(k), to_blocks(v), to_blocks(logf)))
    o = ob.transpose(1, 0, 3, 2, 4).reshape(B, T, H, v.shape[-1])
    return o, S


def even_mixer(xn, lb, w_in, gnorm, conv_w, w_out, s0, buf):
    B, T, _ = xn.shape
    qa, fa, ia, ga, bg, cg, hb = jnp.split(xn @ w_in, IN_SPLITS, axis=-1)
    heads = lambda a: a.reshape(B, T, A_HEADS, -1)
    q = jax.nn.silu(qa.astype(F32)) * (A_DK ** -0.5)
    f = lb + (1.0 - lb) * jax.nn.sigmoid(fa.astype(F32))
    o, S = hgrn2_scan(heads(q), heads(1.0 - f), heads(ia.astype(F32)), heads(jnp.log(f)), s0.astype(F32))
    o = rms_norm(o, gnorm) * jax.nn.silu(heads(ga).astype(F32))
    oa = o.reshape(B, T, A_WIDTH).astype(xn.dtype)
    u = cg * hb
    up = jnp.concatenate([buf.astype(u.dtype), u], axis=1)
    yb = sum(up[:, j:j + T] * conv_w[:, j] for j in range(CONV_W))
    ob = bg * yb
    y = jnp.concatenate([oa, ob], axis=-1) @ w_out
    return y, S.astype(s0.dtype), up[:, -(CONV_W - 1):]


def mla_attend(q_nope, q_pe, k_nope, k_pe, v, q_pos, k_pos):
    B, T, H, _ = q_nope.shape
    k_chunk = k_pos // CHUNK

    def block(args):
        qn, qp, qc = args
        s = (jnp.einsum('bqhd,bkhd->bhqk', qn, k_nope, preferred_element_type=F32)
             + jnp.einsum('bqhr,bkr->bhqk', qp, k_pe, preferred_element_type=F32))
        mask = (k_chunk[None, :] <= qc[:, None])[None, None]
        p = jax.nn.softmax(jnp.where(mask, s * MLA_SCALE, -jnp.inf), axis=-1).astype(v.dtype)
        return jnp.einsum('bhqk,bkhd->bqhd', p, v)

    q_chunk = q_pos // CHUNK
    if T > Q_BLOCK and T % Q_BLOCK == 0:
        n = T // Q_BLOCK
        split = lambda a: jnp.moveaxis(a.reshape((B, n, Q_BLOCK) + a.shape[2:]), 1, 0)
        out = lax.map(block, (split(q_nope), split(q_pe), q_chunk.reshape(n, Q_BLOCK)))
        out = jnp.moveaxis(out, 0, 1).reshape(B, T, H, V_DIM)
    else:
        out = block((q_nope, q_pe, q_chunk))
    return out.reshape(B, T, H * V_DIM)


def mla_mixer(xn, ckv_past, kpe_past, q_pos, w_dq, q_norm, w_uq, w_dkv, kv_norm, w_ukv, w_o):
    B, T, _ = xn.shape
    cq = rms_norm(xn @ w_dq, q_norm)
    q = (cq @ w_uq).reshape(B, T, C_HEADS, QK_NOPE + QK_ROPE)
    q_nope, q_pe = q[..., :QK_NOPE], rope(q[..., QK_NOPE:], q_pos)
    ckr = xn @ w_dkv
    ckv_new = rms_norm(ckr[..., :KV_LORA], kv_norm)
    kpe_new = rope(ckr[..., KV_LORA:], q_pos)
    ckv = jnp.concatenate([ckv_past.astype(ckv_new.dtype), ckv_new], axis=1)
    kpe = jnp.concatenate([kpe_past.astype(kpe_new.dtype), kpe_new], axis=1)
    S = ckv.shape[1]
    kv = (ckv @ w_ukv).reshape(B, S, C_HEADS, QK_NOPE + V_DIM)
    o = mla_attend(q_nope, q_pe, kv[..., :QK_NOPE], kpe, kv[..., QK_NOPE:], q_pos, jnp.arange(S, dtype=jnp.int32))
    return o @ w_o, ckv_new, kpe_new


def mem_kv(mem, g, w_kv):
    B = mem.shape[0]
    kv = (rms_norm(mem, g) @ w_kv).reshape(B, N_MEM, 2, M_HEADS, M_DIM)
    return kv[:, :, 0], kv[:, :, 1]


def mem_attend(xn, mk, mv, w_q, w_o):
    B, T, _ = xn.shape
    q = (xn @ w_q).reshape(B, T, M_HEADS, M_DIM)
    s = jnp.einsum('bthd,bmhd->bhtm', q, mk, preferred_element_type=F32) * (M_DIM ** -0.5)
    p = jax.nn.softmax(s, axis=-1).astype(mv.dtype)
    o = jnp.einsum('bhtm,bmhd->bthd', p, mv).reshape(B, T, M_HEADS * M_DIM)
    return o @ w_o


def trunk(x, past_len, mem_k, mem_v, s_hgrn, s_conv, ckv_past, kpe_past, norm_g, ffn_w1, ffn_w2,
          w_in0, hgrn_lb, hgrn_gnorm, conv_w, w_out0, mla_w_dq, mla_q_norm, mla_w_uq, mla_w_dkv,
          mla_kv_norm, mla_w_ukv, mla_w_o, w_mem_q, w_mem_o):
    T = x.shape[1]
    q_pos = past_len + jnp.arange(T, dtype=jnp.int32)
    lbs = jnp.cumsum(jax.nn.softmax(hgrn_lb.astype(F32), axis=0), axis=0)
    hs, cs, ckvs, kpes = [], [], [], []
    for l in range(DEPTH):
        g = norm_g[l]
        x = x + 0.5 * rms_norm(swiglu(rms_norm(x, g[0]), ffn_w1[l, 0], ffn_w2[l, 0]), g[1])
        xn = rms_norm(x, g[2])
        if l % 2 == 0:
            e = l // 2
            y, s_new, c_new = even_mixer(xn, lbs[l], w_in0[e], hgrn_gnorm[e], conv_w[e], w_out0[e], s_hgrn[e], s_conv[e])
            hs.append(s_new)
            cs.append(c_new)
        else:
            o = l // 2
            y, ckv_new, kpe_new = mla_mixer(xn, ckv_past[o], kpe_past[o], q_pos, mla_w_dq[o], mla_q_norm[o], mla_w_uq[o],
                                            mla_w_dkv[o], mla_kv_norm[o], mla_w_ukv[o], mla_w_o[o])
            ckvs.append(ckv_new)
            kpes.append(kpe_new)
        x = x + rms_norm(y, g[3])
        x = x + rms_norm(mem_attend(rms_norm(x, g[4]), mem_k[l], mem_v[l], w_mem_q[l], w_mem_o[l]), g[5])
        x = x + 0.5 * rms_norm(swiglu(rms_norm(x, g[6]), ffn_w1[l, 1], ffn_w2[l, 1]), g[7])
    return x, jnp.stack(hs), jnp.stack(cs), jnp.stack(ckvs), jnp.stack(kpes)


def setup_inputs(seed: int = 0) -> dict:
    key = jax.random.key(seed)
    ks = iter(jax.random.split(key, 40))
    nrm = lambda shape, scale: jax.random.normal(next(ks), shape, F32) * scale
    gain = lambda shape: 1.0 + 0.02 * jax.random.normal(next(ks), shape, F32)
    return {
        'x_prompt': nrm((BATCH, SEQ, D_MODEL), 1.0),
        'x_sample': nrm((DEC_BATCH, DEC_SEQ, D_MODEL), 1.0),
        'mem_prompt': nrm((BATCH, N_MEM, D_MODEL), 1.0),
        'state_hgrn': nrm((N_EVEN, DEC_BATCH, A_HEADS, A_DK, A_DV), 0.5),
        'state_conv': nrm((N_EVEN, DEC_BATCH, CONV_W - 1, B_WIDTH), 1.0),
        'cache_ckv': nrm((N_ODD, DEC_BATCH, PAST_LEN, KV_LORA), 1.0),
        'cache_kpe': nrm((N_ODD, DEC_BATCH, PAST_LEN, QK_ROPE), 1.0),
        'cache_mem_k': nrm((DEPTH, DEC_BATCH, N_MEM, M_HEADS, M_DIM), 1.0),
        'cache_mem_v': nrm((DEPTH, DEC_BATCH, N_MEM, M_HEADS, M_DIM), 1.0),
        'norm_g': gain((DEPTH, N_NORMS, D_MODEL)),
        'ffn_w1': nrm((DEPTH, 2, D_MODEL, 2 * D_FF), D_MODEL ** -0.5),
        'ffn_w2': nrm((DEPTH, 2, D_FF, D_MODEL), D_FF ** -0.5),
        'w_in0': nrm((N_EVEN, D_MODEL, IN_COLS), D_MODEL ** -0.5),
        'hgrn_lb': nrm((DEPTH + 1, A_WIDTH), 0.1),
        'hgrn_gnorm': gain((N_EVEN, A_DV)),
        'conv_w': nrm((N_EVEN, B_WIDTH, CONV_W), CONV_W ** -0.5),
        'w_out0': nrm((N_EVEN, MIX0_OUT, D_MODEL), MIX0_OUT ** -0.5),
        'mla_w_dq': nrm((N_ODD, D_MODEL, Q_LORA), D_MODEL ** -0.5),
        'mla_q_norm': gain((N_ODD, Q_LORA)),
        'mla_w_uq': nrm((N_ODD, Q_LORA, C_HEADS * (QK_NOPE + QK_ROPE)), Q_LORA ** -0.5),
        'mla_w_dkv': nrm((N_ODD, D_MODEL, KV_LORA + QK_ROPE), D_MODEL ** -0.5),
        'mla_kv_norm': gain((N_ODD, KV_LORA)),
        'mla_w_ukv': nrm((N_ODD, KV_LORA, C_HEADS * (QK_NOPE + V_DIM)), KV_LORA ** -0.5),
        'mla_w_o': nrm((N_ODD, C_HEADS * V_DIM, D_MODEL), (C_HEADS * V_DIM) ** -0.5),
        'mem_norm': gain((DEPTH, D_MODEL)),
        'w_mem_q': nrm((DEPTH, D_MODEL, M_HEADS * M_DIM), D_MODEL ** -0.5),
        'w_mem_kv': nrm((DEPTH, D_MODEL, 2 * M_HEADS * M_DIM), D_MODEL ** -0.5),
        'w_mem_o': nrm((DEPTH, M_HEADS * M_DIM, D_MODEL), (M_HEADS * M_DIM) ** -0.5),
    }


def reference(x_prompt, x_sample, mem_prompt, state_hgrn, state_conv, cache_ckv, cache_kpe, cache_mem_k, cache_mem_v,
              norm_g, ffn_w1, ffn_w2, w_in0, hgrn_lb, hgrn_gnorm, conv_w, w_out0, mla_w_dq, mla_q_norm, mla_w_uq,
              mla_w_dkv, mla_kv_norm, mla_w_ukv, mla_w_o, mem_norm, w_mem_q, w_mem_kv, w_mem_o):
    dt = x_prompt.dtype
    mks, mvs = [], []
    for l in range(DEPTH):
        mk, mv = mem_kv(mem_prompt, mem_norm[l], w_mem_kv[l])
        mks.append(mk)
        mvs.append(mv)
    mem_k_p = jnp.stack(mks)
    mem_v_p = jnp.stack(mvs)
    h0 = jnp.zeros((N_EVEN, BATCH, A_HEADS, A_DK, A_DV), dt)
    c0 = jnp.zeros((N_EVEN, BATCH, CONV_W - 1, B_WIDTH), dt)
    ckv0 = jnp.zeros((N_ODD, BATCH, 0, KV_LORA), dt)
    kpe0 = jnp.zeros((N_ODD, BATCH, 0, QK_ROPE), dt)
    y_prompt, p_hgrn, p_conv, p_ckv, p_kpe = trunk(
        x_prompt, 0, mem_k_p, mem_v_p, h0, c0, ckv0, kpe0, norm_g, ffn_w1, ffn_w2, w_in0, hgrn_lb, hgrn_gnorm,
        conv_w, w_out0, mla_w_dq, mla_q_norm, mla_w_uq, mla_w_dkv, mla_kv_norm, mla_w_ukv, mla_w_o, w_mem_q, w_mem_o)
    y_sample, s_hgrn, s_conv, s_ckv, s_kpe = trunk(
        x_sample, PAST_LEN, cache_mem_k, cache_mem_v, state_hgrn, state_conv, cache_ckv, cache_kpe, norm_g, ffn_w1,
        ffn_w2, w_in0, hgrn_lb, hgrn_gnorm, conv_w, w_out0, mla_w_dq, mla_q_norm, mla_w_uq, mla_w_dkv, mla_kv_norm,
        mla_w_ukv, mla_w_o, w_mem_q, w_mem_o)
    return (y_prompt, y_sample, p_hgrn, p_conv, p_ckv, p_kpe, mem_k_p, mem_v_p, s_hgrn, s_conv, s_ckv, s_kpe)
```

```python
import functools

import jax
import jax.numpy as jnp
import numpy as np
from jax import lax
from jax.experimental import pallas as pl
from jax.experimental.pallas import tpu as pltpu

F32 = jnp.float32
BF16 = jnp.bfloat16

V7X_LANES = 128
V7X_VMEM_BYTES = 64 * 1024 * 1024
MIB = 1024 * 1024
SPILL_ROOM_BYTES = 8 * MIB

EPS = 1e-6
CHUNK = 64
HGRN_HEADS = 8
HGRN_DK = 128
HGRN_SUB = 16
MLA_HEADS = 16
QK_NOPE = 128
QK_ROPE = 64
V_DIM = 128
MLA_QK_PAD = 256
MLA_SCALE = (QK_NOPE + QK_ROPE) ** -0.5
ROPE_THETA = 10000.0
MEM_HEADS = 4
MEM_DIM = 128
CONV_W = 3
NEG_BIG = -1e30

NT_DIMS = (((1,), (1,)), ((), ()))
TN_DIMS = (((0,), (0,)), ((), ()))


def _vmem_limit(pipelined_bytes, resident_bytes):
    want = 2 * pipelined_bytes + resident_bytes + SPILL_ROOM_BYTES
    return int(min(want, V7X_VMEM_BYTES * 15 // 16))


def _params(semantics, pipelined_bytes, resident_bytes):
    return pltpu.CompilerParams(
        dimension_semantics=semantics,
        vmem_limit_bytes=_vmem_limit(pipelined_bytes, resident_bytes))


def _row_tile(rows, want):
    t = min(rows, want)
    assert rows % t == 0, (rows, t)
    return t


def _rms(x, g):
    ms = jnp.mean(x * x, axis=-1, keepdims=True)
    return x * lax.rsqrt(ms + EPS) * g


def _sigmoid(x):
    return 1.0 / (1.0 + jnp.exp(-x))


def _dot(a, b):
    return jnp.dot(a, b, preferred_element_type=F32)


def _ffn_body(x_ref, gpre_ref, w1g_ref, w1u_ref, w2_ref, gpost_ref, o_ref, xn_ref, acc_ref):
    k = pl.program_id(1)

    @pl.when(k == 0)
    def _():
        xn_ref[...] = _rms(x_ref[...], gpre_ref[...]).astype(BF16)
        acc_ref[...] = jnp.zeros_like(acc_ref)

    xn = xn_ref[...]
    gate = _dot(xn, w1g_ref[...])
    up = _dot(xn, w1u_ref[...])
    h = (gate * _sigmoid(gate) * up).astype(BF16)
    acc_ref[...] += _dot(h, w2_ref[...])

    @pl.when(k == pl.num_programs(1) - 1)
    def _():
        o_ref[...] = x_ref[...] + 0.5 * _rms(acc_ref[...], gpost_ref[...])


def _ffn(x, g_pre, w1, w2, g_post, layer, half, tm_want=512, tf=512):
    rows, d = x.shape
    dff = w2.shape[-2]
    tm = _row_tile(rows, tm_want)
    nk = dff // tf
    assert dff % tf == 0
    blocks = tm * d * 4 * 2 + 3 * d * tf * 2
    resident = tm * d * (2 + 4) + 4 * tm * tf * 4
    return pl.pallas_call(
        _ffn_body,
        grid=(rows // tm, nk),
        in_specs=[
            pl.BlockSpec((tm, d), lambda i, k: (i, 0)),
            pl.BlockSpec((1, d), lambda i, k: (0, 0)),
            pl.BlockSpec((None, None, d, tf), lambda i, k: (layer, half, 0, k)),
            pl.BlockSpec((None, None, d, tf), lambda i, k: (layer, half, 0, nk + k)),
            pl.BlockSpec((None, None, tf, d), lambda i, k: (layer, half, k, 0)),
            pl.BlockSpec((1, d), lambda i, k: (0, 0)),
        ],
        out_specs=pl.BlockSpec((tm, d), lambda i, k: (i, 0)),
        out_shape=jax.ShapeDtypeStruct((rows, d), F32),
        scratch_shapes=[pltpu.VMEM((tm, d), BF16), pltpu.VMEM((tm, d), F32)],
        compiler_params=_params(("parallel", "arbitrary"), blocks, resident),
        name="ffn",
    )(x, g_pre, w1, w1, w2, g_post)


def _norm_matmul_body(x_ref, g_ref, w_ref, o_ref, xn_ref):
    @pl.when(pl.program_id(1) == 0)
    def _():
        xn_ref[...] = _rms(x_ref[...], g_ref[...]).astype(BF16)

    o_ref[...] = _dot(xn_ref[...], w_ref[...]).astype(o_ref.dtype)


def _norm_matmul(x, g, w, out_dtype, tm_want=512, tn_want=1024):
    rows, d = x.shape
    n = w.shape[1]
    tm = _row_tile(rows, tm_want)
    tn = _row_tile(n, tn_want)
    blocks = tm * d * 4 + d * tn * 2 + tm * tn * 4
    resident = tm * d * 2 + tm * tn * 4
    return pl.pallas_call(
        _norm_matmul_body,
        grid=(rows // tm, n // tn),
        in_specs=[
            pl.BlockSpec((tm, d), lambda i, j: (i, 0)),
            pl.BlockSpec((1, d), lambda i, j: (0, 0)),
            pl.BlockSpec((d, tn), lambda i, j: (0, j)),
        ],
        out_specs=pl.BlockSpec((tm, tn), lambda i, j: (i, j)),
        out_shape=jax.ShapeDtypeStruct((rows, n), out_dtype),
        scratch_shapes=[pltpu.VMEM((tm, d), BF16)],
        compiler_params=_params(("parallel", "arbitrary"), blocks, resident),
        name="norm_matmul",
    )(x, g, w)


def _proj_residual_body(a_ref, w_ref, g_ref, x_ref, o_ref):
    y = _dot(a_ref[...], w_ref[...])
    o_ref[...] = x_ref[...] + _rms(y, g_ref[...])


def _proj_residual(a, w, g, x, tm_want=512):
    rows, kdim = a.shape
    d = w.shape[1]
    tm = _row_tile(rows, tm_want)
    blocks = tm * kdim * 2 + kdim * d * 2 + 2 * tm * d * 4
    resident = 2 * tm * d * 4
    return pl.pallas_call(
        _proj_residual_body,
        grid=(rows // tm,),
        in_specs=[
            pl.BlockSpec((tm, kdim), lambda i: (i, 0)),
            pl.BlockSpec((kdim, d), lambda i: (0, 0)),
            pl.BlockSpec((1, d), lambda i: (0, 0)),
            pl.BlockSpec((tm, d), lambda i: (i, 0)),
        ],
        out_specs=pl.BlockSpec((tm, d), lambda i: (i, 0)),
        out_shape=jax.ShapeDtypeStruct((rows, d), F32),
        compiler_params=_params(("parallel",), blocks, resident),
        name="proj_residual",
    )(a, w, g, x)


def _hgrn_body(qa_ref, fa_ref, ia_ref, ga_ref, lb_ref, gn_ref, s0_ref,
               oa_ref, sout_ref, st_ref, o_ref):
    c = pl.program_id(1)
    rows = qa_ref.shape[0]
    width = qa_ref.shape[1]
    nsub = rows // HGRN_SUB

    @pl.when(c == 0)
    def _():
        for h in range(HGRN_HEADS):
            st_ref[h] = s0_ref[h].T

    lb = lb_ref[...]
    qa = qa_ref[...]
    q = qa * _sigmoid(qa) * (HGRN_DK ** -0.5)
    f = lb + (1.0 - lb) * _sigmoid(fa_ref[...])
    k = 1.0 - f
    g = jnp.log(f)
    v = ia_ref[...]

    r_i = lax.broadcasted_iota(jnp.int32, (rows, rows), 0)
    c_i = lax.broadcasted_iota(jnp.int32, (rows, rows), 1)
    tri = ((c_i <= r_i) & ((c_i // HGRN_SUB) == (r_i // HGRN_SUB))).astype(BF16)
    g1 = g.astype(BF16)
    rem = g - g1.astype(F32)
    g2 = rem.astype(BF16)
    g3 = (rem - g2.astype(F32)).astype(BF16)
    bl = _dot(tri, g1) + _dot(tri, g2) + _dot(tri, g3)

    qe = (q * jnp.exp(bl)).astype(BF16)
    t_idx = lax.broadcasted_iota(jnp.int32, (HGRN_SUB, width), 0)

    for i in range(nsub):
        r0 = i * HGRN_SUB
        bl_i = bl[r0:r0 + HGRN_SUB]
        q_i = q[r0:r0 + HGRN_SUB]
        k_i = k[r0:r0 + HGRN_SUB]
        v_i = v[r0:r0 + HGRN_SUB]
        b_end = bl_i[HGRN_SUB - 1:HGRN_SUB]
        ke_i = (k_i * jnp.exp(b_end - bl_i)).astype(BF16)
        dec_i = jnp.exp(b_end)
        v_bf = v_i.astype(BF16)

        diag = [jnp.zeros((HGRN_SUB, HGRN_DK), F32) for _ in range(HGRN_HEADS)]
        for s in range(HGRN_SUB):
            rel = bl_i - bl_i[s:s + 1]
            e = jnp.exp(jnp.where(t_idx >= s, rel, -jnp.inf))
            w = q_i * k_i[s:s + 1] * e
            for h in range(HGRN_HEADS):
                hs = slice(h * HGRN_DK, (h + 1) * HGRN_DK)
                col = jnp.sum(w[:, hs], axis=1, keepdims=True)
                diag[h] = diag[h] + col * v_i[s:s + 1, hs]

        for h in range(HGRN_HEADS):
            hs = slice(h * HGRN_DK, (h + 1) * HGRN_DK)
            st = st_ref[h]
            inter = lax.dot_general(qe[r0:r0 + HGRN_SUB, hs], st.astype(BF16), NT_DIMS,
                                    preferred_element_type=F32)
            upd = lax.dot_general(v_bf[:, hs], ke_i[:, hs], TN_DIMS,
                                  preferred_element_type=F32)
            st_ref[h] = st * dec_i[:, hs] + upd
            o_ref[r0:r0 + HGRN_SUB, hs] = inter + diag[h]

    gn = gn_ref[...]
    for h in range(HGRN_HEADS):
        hs = slice(h * HGRN_DK, (h + 1) * HGRN_DK)
        ga = ga_ref[:, hs]
        oa_ref[:, hs] = (_rms(o_ref[:, hs], gn) * (ga * _sigmoid(ga))).astype(oa_ref.dtype)

    @pl.when(c == pl.num_programs(1) - 1)
    def _():
        for h in range(HGRN_HEADS):
            sout_ref[h] = st_ref[h].T


def _hgrn(proj, lb, gnorm, s0, batch, seq):
    width = HGRN_HEADS * HGRN_DK
    rows = min(CHUNK, seq)
    nc = seq // rows
    assert seq % rows == 0 and rows % HGRN_SUB == 0
    col = lambda j: pl.BlockSpec((rows, width), lambda b, c: (b * nc + c, j))
    state_spec = pl.BlockSpec((None, HGRN_HEADS, HGRN_DK, HGRN_DK), lambda b, c: (b, 0, 0, 0))
    blocks = 4 * rows * width * 4 + rows * width * 2 + 2 * HGRN_HEADS * HGRN_DK * HGRN_DK * 4
    resident = HGRN_HEADS * HGRN_DK * HGRN_DK * 4 + 16 * rows * width * 4
    return pl.pallas_call(
        _hgrn_body,
        grid=(batch, nc),
        in_specs=[col(0), col(1), col(2), col(3),
                  pl.BlockSpec((1, width), lambda b, c: (0, 0)),
                  pl.BlockSpec((1, HGRN_DK), lambda b, c: (0, 0)),
                  state_spec],
        out_specs=[pl.BlockSpec((rows, width), lambda b, c: (b * nc + c, 0)), state_spec],
        out_shape=[jax.ShapeDtypeStruct((batch * seq, width), BF16),
                   jax.ShapeDtypeStruct(s0.shape, F32)],
        scratch_shapes=[pltpu.VMEM((HGRN_HEADS, HGRN_DK, HGRN_DK), F32),
                        pltpu.VMEM((rows, width), F32)],
        compiler_params=_params(("parallel", "arbitrary"), blocks, resident),
        name="hgrn2",
    )(proj, proj, proj, proj, lb, gnorm, s0)


CONV_PAD = 8


def _even_out_body(oa_ref, bg_ref, cg_ref, hb_ref, cw_ref, cs_ref, w_ref, g_ref, x_ref,
                   o_ref, cnew_ref, ubuf_ref):
    t = pl.program_id(1)
    tm = oa_ref.shape[0]
    half = oa_ref.shape[1]
    lo = CONV_PAD - (CONV_W - 1)

    @pl.when(t == 0)
    def _():
        ubuf_ref[lo:CONV_PAD, :] = cs_ref[...]

    u = cg_ref[...] * hb_ref[...]
    ubuf_ref[CONV_PAD:CONV_PAD + tm, :] = u
    yb = u * cw_ref[CONV_W - 1:CONV_W, :]
    for j in range(CONV_W - 1):
        yb = yb + ubuf_ref[lo + j:lo + j + tm, :] * cw_ref[j:j + 1, :]
    ob = (bg_ref[...] * yb).astype(BF16)
    y = _dot(oa_ref[...], w_ref[0:half, :]) + _dot(ob, w_ref[half:2 * half, :])
    o_ref[...] = x_ref[...] + _rms(y, g_ref[...])

    tail = ubuf_ref[lo + tm:CONV_PAD + tm, :]
    ubuf_ref[lo:CONV_PAD, :] = tail

    @pl.when(t == pl.num_programs(1) - 1)
    def _():
        cnew_ref[...] = tail


def _even_out(oa, proj, conv_wt, conv_state, w_out, g, x, batch, seq, tm_want=512):
    half = oa.shape[1]
    d = x.shape[1]
    tm = _row_tile(seq, tm_want)
    nt = seq // tm
    row = lambda b, t: (b * nt + t, 0)
    pcol = lambda j: pl.BlockSpec((tm, half), lambda b, t: (b * nt + t, j))
    cstate = pl.BlockSpec((None, CONV_W - 1, half), lambda b, t: (b, 0, 0))
    blocks = tm * half * (2 + 3 * 4) + 2 * half * d * 2 + 2 * tm * d * 4
    resident = (tm + CONV_PAD) * half * 4 + 4 * tm * half * 4 + 2 * tm * d * 4
    return pl.pallas_call(
        _even_out_body,
        grid=(batch, nt),
        in_specs=[pl.BlockSpec((tm, half), row), pcol(4), pcol(5), pcol(6),
                  pl.BlockSpec((CONV_W, half), lambda b, t: (0, 0)),
                  cstate,
                  pl.BlockSpec((2 * half, d), lambda b, t: (0, 0)),
                  pl.BlockSpec((1, d), lambda b, t: (0, 0)),
                  pl.BlockSpec((tm, d), row)],
        out_specs=[pl.BlockSpec((tm, d), row), cstate],
        out_shape=[jax.ShapeDtypeStruct(x.shape, F32),
                   jax.ShapeDtypeStruct(conv_state.shape, F32)],
        scratch_shapes=[pltpu.VMEM((tm + CONV_PAD, half), F32)],
        compiler_params=_params(("parallel", "arbitrary"), blocks, resident),
        name="even_out",
    )(oa, proj, proj, proj, conv_wt, conv_state, w_out, g, x)


def _rope(x, cos, sin_lo, sin_hi):
    half = QK_ROPE // 2
    return (x * cos
            + pltpu.roll(x, V7X_LANES - half, axis=1) * sin_lo
            + pltpu.roll(x, half, axis=1) * sin_hi)


def _mla_proj_body(x_ref, g_ref, wdq_ref, qn_ref, wuq_ref, wdkv_ref, kvn_ref,
                   cos_ref, slo_ref, shi_ref, q_ref, ckv_ref, kpe_ref, kpad_ref):
    cos, slo, shi = cos_ref[...], slo_ref[...], shi_ref[...]
    lora = ckv_ref.shape[1]
    xn = _rms(x_ref[...], g_ref[...]).astype(BF16)
    cq = _rms(_dot(xn, wdq_ref[...]), qn_ref[...]).astype(BF16)
    ckr = _dot(xn, wdkv_ref[...])
    ckv_ref[...] = _rms(ckr[:, :lora], kvn_ref[...])
    kp = _rope(ckr[:, lora:lora + V7X_LANES], cos, slo, shi)
    kpe_ref[...] = kp[:, :QK_ROPE]
    kpad_ref[...] = kp.astype(BF16)
    q = _dot(cq, wuq_ref[...])
    for h in range(MLA_HEADS):
        c0 = h * MLA_QK_PAD
        q_ref[:, c0:c0 + QK_NOPE] = q[:, c0:c0 + QK_NOPE].astype(BF16)
        q_ref[:, c0 + QK_NOPE:c0 + MLA_QK_PAD] = _rope(
            q[:, c0 + QK_NOPE:c0 + MLA_QK_PAD], cos, slo, shi).astype(BF16)


def _mla_proj(x, g, w_dq, q_norm, w_uq_pad, w_dkv_pad, kv_norm, cos, slo, shi, tm_want=512):
    rows, d = x.shape
    qlora = w_dq.shape[1]
    lora = kv_norm.shape[1]
    qw = w_uq_pad.shape[1]
    kw = w_dkv_pad.shape[1]
    tm = _row_tile(rows, tm_want)
    full = lambda a: pl.BlockSpec(a.shape, lambda i: (0,) * a.ndim)
    rowspec = lambda n: pl.BlockSpec((tm, n), lambda i: (i, 0))
    blocks = (tm * d * 4 + (d * qlora + qlora * qw + d * kw) * 2 + 3 * tm * V7X_LANES * 4
              + tm * (qw * 2 + lora * 4 + QK_ROPE * 4 + V7X_LANES * 2))
    resident = tm * (d * 6 + qw * 4 + kw * 4 + qlora * 8)
    return pl.pallas_call(
        _mla_proj_body,
        grid=(rows // tm,),
        in_specs=[rowspec(d), full(g), full(w_dq), full(q_norm), full(w_uq_pad),
                  full(w_dkv_pad), full(kv_norm),
                  rowspec(V7X_LANES), rowspec(V7X_LANES), rowspec(V7X_LANES)],
        out_specs=[rowspec(qw), rowspec(lora), rowspec(QK_ROPE), rowspec(V7X_LANES)],
        out_shape=[jax.ShapeDtypeStruct((rows, qw), BF16),
                   jax.ShapeDtypeStruct((rows, lora), F32),
                   jax.ShapeDtypeStruct((rows, QK_ROPE), F32),
                   jax.ShapeDtypeStruct((rows, V7X_LANES), BF16)],
        compiler_params=_params(("parallel",), blocks, resident),
        name="mla_proj",
    )(x, g, w_dq, q_norm, w_uq_pad, w_dkv_pad, kv_norm, cos, slo, shi)


def _kv_up_body(ckv_ref, kpad_ref, wuk_ref, wuv_ref, k_ref, v_ref):
    c = ckv_ref[...].astype(BF16)
    kn = _dot(c, wuk_ref[...])
    v_ref[...] = _dot(c, wuv_ref[...]).astype(BF16)
    kpad = kpad_ref[...]
    for h in range(MLA_HEADS):
        c0 = h * MLA_QK_PAD
        k_ref[:, c0:c0 + QK_NOPE] = kn[:, h * QK_NOPE:(h + 1) * QK_NOPE].astype(BF16)
        k_ref[:, c0 + QK_NOPE:c0 + MLA_QK_PAD] = kpad


def _kv_up(ckv, kpad, w_uk, w_uv, ts_want=512):
    rows, lora = ckv.shape
    ts = _row_tile(rows, ts_want)
    kw = MLA_HEADS * MLA_QK_PAD
    vw = MLA_HEADS * V_DIM
    rowspec = lambda n: pl.BlockSpec((ts, n), lambda i: (i, 0))
    full = lambda a: pl.BlockSpec(a.shape, lambda i: (0,) * a.ndim)
    blocks = ts * (lora * 4 + V7X_LANES * 2 + kw * 2 + vw * 2) + 2 * lora * vw * 2
    resident = ts * (kw + vw) * 4
    return pl.pallas_call(
        _kv_up_body,
        grid=(rows // ts,),
        in_specs=[rowspec(lora), rowspec(V7X_LANES), full(w_uk), full(w_uv)],
        out_specs=[rowspec(kw), rowspec(vw)],
        out_shape=[jax.ShapeDtypeStruct((rows, kw), BF16),
                   jax.ShapeDtypeStruct((rows, vw), BF16)],
        compiler_params=_params(("parallel",), blocks, resident),
        name="kv_up",
    )(ckv, kpad, w_uk, w_uv)


def _attn_init(m_ref, l_ref, acc_ref):
    m_ref[...] = jnp.full(m_ref.shape, NEG_BIG, F32)
    l_ref[...] = jnp.zeros_like(l_ref)
    acc_ref[...] = jnp.zeros_like(acc_ref)


def _attn_update(q_ref, k_ref, v_ref, m_ref, l_ref, acc_ref, mask):
    for h in range(MLA_HEADS):
        qh = q_ref[:, h * MLA_QK_PAD:(h + 1) * MLA_QK_PAD]
        kh = k_ref[:, h * MLA_QK_PAD:(h + 1) * MLA_QK_PAD]
        s = lax.dot_general(qh, kh, NT_DIMS, preferred_element_type=F32) * MLA_SCALE
        if mask is not None:
            s = jnp.where(mask, s, -jnp.inf)
        m_prev = m_ref[h]
        m_next = jnp.maximum(m_prev, jnp.max(s, axis=1, keepdims=True))
        alpha = jnp.exp(m_prev - m_next)
        p = jnp.exp(s - m_next[:, :1])
        l_ref[h] = alpha * l_ref[h] + jnp.sum(p, axis=1, keepdims=True)
        m_ref[h] = m_next
        acc_ref[h] = alpha * acc_ref[h] + _dot(p.astype(BF16), v_ref[:, h * V_DIM:(h + 1) * V_DIM])


def _attn_finish(o_ref, l_ref, acc_ref):
    for h in range(MLA_HEADS):
        o_ref[:, h * V_DIM:(h + 1) * V_DIM] = (acc_ref[h] / l_ref[h]).astype(o_ref.dtype)


def _attn_causal_body(qi_ref, kj_ref, q_ref, k_ref, v_ref, o_ref, m_ref, l_ref, acc_ref):
    p = pl.program_id(0)
    qi = qi_ref[p]
    kj = kj_ref[p]
    tq, tk = q_ref.shape[0], k_ref.shape[0]

    @pl.when(kj == 0)
    def _():
        _attn_init(m_ref, l_ref, acc_ref)

    @pl.when(kj < qi)
    def _():
        _attn_update(q_ref, k_ref, v_ref, m_ref, l_ref, acc_ref, None)

    @pl.when(kj == qi)
    def _():
        r = lax.broadcasted_iota(jnp.int32, (tq, tk), 0) // CHUNK
        c = lax.broadcasted_iota(jnp.int32, (tq, tk), 1) // CHUNK
        _attn_update(q_ref, k_ref, v_ref, m_ref, l_ref, acc_ref, c <= r)
        _attn_finish(o_ref, l_ref, acc_ref)


def _attn_scratch(tq):
    shape = (MLA_HEADS, tq, V7X_LANES)
    return [pltpu.VMEM(shape, F32), pltpu.VMEM(shape, F32), pltpu.VMEM(shape, F32)]


def _attn_causal(q, k, v, tile_want=512):
    rows = q.shape[0]
    tile = _row_tile(rows, tile_want)
    assert tile % CHUNK == 0 or tile == rows
    nq = rows // tile
    pairs = [(i, j) for i in range(nq) for j in range(i + 1)]
    qi = jnp.asarray(np.array([a for a, _ in pairs], np.int32))
    kj = jnp.asarray(np.array([b for _, b in pairs], np.int32))
    kw, vw = q.shape[1], v.shape[1]
    blocks = tile * (2 * kw + 2 * vw) * 2
    resident = 3 * MLA_HEADS * tile * V7X_LANES * 4 + 2 * MLA_HEADS * tile * tile * 4
    grid_spec = pltpu.PrefetchScalarGridSpec(
        num_scalar_prefetch=2,
        grid=(len(pairs),),
        in_specs=[pl.BlockSpec((tile, kw), lambda p, qi, kj: (qi[p], 0)),
                  pl.BlockSpec((tile, kw), lambda p, qi, kj: (kj[p], 0)),
                  pl.BlockSpec((tile, vw), lambda p, qi, kj: (kj[p], 0))],
        out_specs=pl.BlockSpec((tile, vw), lambda p, qi, kj: (qi[p], 0)),
        scratch_shapes=_attn_scratch(tile))
    return pl.pallas_call(
        _attn_causal_body,
        grid_spec=grid_spec,
        out_shape=jax.ShapeDtypeStruct((rows, vw), BF16),
        compiler_params=_params(("arbitrary",), blocks, resident),
        name="mla_attn_causal",
    )(qi, kj, q, k, v)


def _attn_cached_body(q_ref, k_ref, v_ref, o_ref, m_ref, l_ref, acc_ref, *, q_pos0, n_valid):
    tq, tk = q_ref.shape[0], k_ref.shape[0]
    r = (lax.broadcasted_iota(jnp.int32, (tq, tk), 0) + q_pos0) // CHUNK
    kpos = lax.broadcasted_iota(jnp.int32, (tq, tk), 1)
    mask = (kpos // CHUNK <= r) & (kpos < n_valid)
    _attn_init(m_ref, l_ref, acc_ref)
    _attn_update(q_ref, k_ref, v_ref, m_ref, l_ref, acc_ref, mask)
    _attn_finish(o_ref, l_ref, acc_ref)


def _attn_cached(q, k, v, batch, q_pos0, n_valid):
    tq = q.shape[0] // batch
    tk = k.shape[0] // batch
    kw, vw = q.shape[1], v.shape[1]
    blocks = (tq + tk) * kw * 2 + (tq + tk) * vw * 2
    resident = 3 * MLA_HEADS * tq * V7X_LANES * 4 + 6 * tq * tk * 4
    return pl.pallas_call(
        functools.partial(_attn_cached_body, q_pos0=q_pos0, n_valid=n_valid),
        grid=(batch,),
        in_specs=[pl.BlockSpec((tq, kw), lambda b: (b, 0)),
                  pl.BlockSpec((tk, kw), lambda b: (b, 0)),
                  pl.BlockSpec((tk, vw), lambda b: (b, 0))],
        out_specs=pl.BlockSpec((tq, vw), lambda b: (b, 0)),
        out_shape=jax.ShapeDtypeStruct((q.shape[0], vw), BF16),
        scratch_shapes=_attn_scratch(tq),
        compiler_params=_params(("parallel",), blocks, resident),
        name="mla_attn_cached",
    )(q, k, v)


def _mem_attn_body(x_ref, gpre_ref, wq_ref, mk_ref, mv_ref, wo_ref, gpost_ref, o_ref):
    x = x_ref[...]
    q = _dot(_rms(x, gpre_ref[...]).astype(BF16), wq_ref[...])
    outs = []
    for h in range(MEM_HEADS):
        hs = slice(h * MEM_DIM, (h + 1) * MEM_DIM)
        kh = mk_ref[:, hs].astype(BF16)
        vh = mv_ref[:, hs].astype(BF16)
        s = lax.dot_general(q[:, hs].astype(BF16), kh, NT_DIMS,
                            preferred_element_type=F32) * (MEM_DIM ** -0.5)
        p = jnp.exp(s - jnp.max(s, axis=1, keepdims=True))
        p = p * (1.0 / jnp.sum(p, axis=1, keepdims=True))
        outs.append(_dot(p.astype(BF16), vh).astype(BF16))
    o = jnp.concatenate(outs, axis=1)
    o_ref[...] = x + _rms(_dot(o, wo_ref[...]), gpost_ref[...])


def _mem_attn(x, g_pre, w_q, mem_k, mem_v, w_o, g_post, batch, seq, tm_want=512):
    d = x.shape[1]
    n_mem, width = mem_k.shape[1], mem_k.shape[2]
    tm = _row_tile(seq, tm_want)
    nt = seq // tm
    row = pl.BlockSpec((tm, d), lambda b, t: (b * nt + t, 0))
    full = lambda a: pl.BlockSpec(a.shape, lambda b, t: (0,) * a.ndim)
    mem = pl.BlockSpec((None, n_mem, width), lambda b, t: (b, 0, 0))
    blocks = 2 * tm * d * 4 + 2 * d * width * 2 + 2 * n_mem * width * 4
    resident = tm * (2 * d * 4 + width * 8 + n_mem * 12)
    return pl.pallas_call(
        _mem_attn_body,
        grid=(batch, nt),
        in_specs=[row, full(g_pre), full(w_q), mem, mem, full(w_o), full(g_post)],
        out_specs=row,
        out_shape=jax.ShapeDtypeStruct(x.shape, F32),
        compiler_params=_params(("parallel", "parallel"), blocks, resident),
        name="mem_attn",
    )(x, g_pre, w_q, mem_k, mem_v, w_o, g_post)


def _rope_tables(pos):
    half = QK_ROPE // 2
    inv = ROPE_THETA ** (-jnp.arange(half, dtype=F32) / half)
    ang = pos.astype(F32)[:, None] * inv
    cos, sin = jnp.cos(ang), jnp.sin(ang)
    z = lambda n: jnp.zeros((pos.shape[0], n), F32)
    pad = V7X_LANES - QK_ROPE
    return (jnp.concatenate([cos, cos, z(pad)], axis=1),
            jnp.concatenate([-sin, z(half + pad)], axis=1),
            jnp.concatenate([z(half), sin, z(pad)], axis=1))


def _prep_mla(w_uq, w_dkv, w_ukv):
    qlora = w_uq.shape[0]
    lora = w_ukv.shape[0]
    wq = w_uq.reshape(qlora, MLA_HEADS, QK_NOPE + QK_ROPE)
    wq = jnp.pad(wq, ((0, 0), (0, 0), (0, MLA_QK_PAD - QK_NOPE - QK_ROPE)))
    w_uq_pad = wq.reshape(qlora, MLA_HEADS * MLA_QK_PAD).astype(BF16)
    w_dkv_pad = jnp.pad(w_dkv, ((0, 0), (0, V7X_LANES - QK_ROPE))).astype(BF16)
    wkv = w_ukv.reshape(lora, MLA_HEADS, QK_NOPE + V_DIM)
    w_uk = wkv[:, :, :QK_NOPE].reshape(lora, MLA_HEADS * QK_NOPE).astype(BF16)
    w_uv = wkv[:, :, QK_NOPE:].reshape(lora, MLA_HEADS * V_DIM).astype(BF16)
    return w_uq_pad, w_dkv_pad, w_uk, w_uv


def _trunk(x, batch, seq, past_len, mem_k, mem_v, s_hgrn, s_conv, ckv_past, kpe_past, p):
    depth = p["norm_g"].shape[0]
    rows = batch * seq
    pos = past_len + jnp.arange(seq, dtype=jnp.int32)
    cos, slo, shi = (jnp.tile(t, (batch, 1)) for t in _rope_tables(pos))
    hs, cs, ckvs, kpes = [], [], [], []
    for l in range(depth):
        g = lambda i: p["norm_g"][l, i][None, :]
        x = _ffn(x, g(0), p["ffn_w1"], p["ffn_w2"], g(1), l, 0)
        if l % 2 == 0:
            e = l // 2
            proj = _norm_matmul(x, g(2), p["w_in0"][e], F32)
            oa, s_new = _hgrn(proj, p["lbs"][l][None, :], p["hgrn_gnorm"][e][None, :],
                              s_hgrn[e], batch, seq)
            x, c_new = _even_out(oa, proj, p["conv_wt"][e], s_conv[e], p["w_out0"][e],
                                 g(3), x, batch, seq)
            hs.append(s_new)
            cs.append(c_new)
        else:
            o = l // 2
            q, ckv_new, kpe_new, kpad_new = _mla_proj(
                x, g(2), p["mla_w_dq"][o], p["mla_q_norm"][o][None, :], p["w_uq_pad"][o],
                p["w_dkv_pad"][o], p["mla_kv_norm"][o][None, :], cos, slo, shi)
            if past_len == 0:
                k, v = _kv_up(ckv_new, kpad_new, p["w_uk"][o], p["w_uv"][o])
                att = _attn_causal(q, k, v)
            else:
                lora = ckv_new.shape[1]
                n_valid = past_len + seq
                n_keys = -(-n_valid // V7X_LANES) * V7X_LANES
                fill = n_keys - n_valid
                ckv_all = jnp.concatenate(
                    [ckv_past[o], ckv_new.reshape(batch, seq, lora),
                     jnp.zeros((batch, fill, lora), F32)], axis=1)
                kpad_past = jnp.pad(kpe_past[o], ((0, 0), (0, 0), (0, V7X_LANES - QK_ROPE)))
                kpad_all = jnp.concatenate(
                    [kpad_past.astype(BF16), kpad_new.reshape(batch, seq, V7X_LANES),
                     jnp.zeros((batch, fill, V7X_LANES), BF16)], axis=1)
                k, v = _kv_up(ckv_all.reshape(batch * n_keys, lora),
                              kpad_all.reshape(batch * n_keys, V7X_LANES),
                              p["w_uk"][o], p["w_uv"][o], ts_want=n_keys)
                att = _attn_cached(q, k, v, batch, past_len, n_valid)
            x = _proj_residual(att, p["mla_w_o"][o], g(3), x)
            ckvs.append(ckv_new.reshape(batch, seq, -1))
            kpes.append(kpe_new.reshape(batch, seq, -1))
        x = _mem_attn(x, g(4), p["w_mem_q"][l], mem_k[l], mem_v[l], p["w_mem_o"][l], g(5),
                      batch, seq)
        x = _ffn(x, g(6), p["ffn_w1"], p["ffn_w2"], g(7), l, 1)
    return (x.reshape(batch, seq, -1), jnp.stack(hs), jnp.stack(cs), jnp.stack(ckvs),
            jnp.stack(kpes))


def kernel(x_prompt, x_sample, mem_prompt, state_hgrn, state_conv, cache_ckv, cache_kpe, cache_mem_k, cache_mem_v, norm_g, ffn_w1, ffn_w2, w_in0, hgrn_lb, hgrn_gnorm, conv_w, w_out0, mla_w_dq, mla_q_norm, mla_w_uq, mla_w_dkv, mla_kv_norm, mla_w_ukv, mla_w_o, mem_norm, w_mem_q, w_mem_kv, w_mem_o):
    batch, seq, d = x_prompt.shape
    dec_batch, dec_seq, _ = x_sample.shape
    past_len = cache_ckv.shape[2]
    depth = norm_g.shape[0]
    n_mem = mem_prompt.shape[1]
    mem_width = MEM_HEADS * MEM_DIM

    prepped = [_prep_mla(mla_w_uq[o], mla_w_dkv[o], mla_w_ukv[o]) for o in range(mla_w_uq.shape[0])]
    p = dict(
        norm_g=norm_g,
        ffn_w1=ffn_w1.astype(BF16), ffn_w2=ffn_w2.astype(BF16),
        w_in0=w_in0.astype(BF16), w_out0=w_out0.astype(BF16),
        lbs=jnp.cumsum(jax.nn.softmax(hgrn_lb.astype(F32), axis=0), axis=0),
        hgrn_gnorm=hgrn_gnorm,
        conv_wt=jnp.swapaxes(conv_w, 1, 2),
        mla_w_dq=mla_w_dq.astype(BF16), mla_q_norm=mla_q_norm, mla_kv_norm=mla_kv_norm,
        w_uq_pad=[t[0] for t in prepped], w_dkv_pad=[t[1] for t in prepped],
        w_uk=[t[2] for t in prepped], w_uv=[t[3] for t in prepped],
        mla_w_o=mla_w_o.astype(BF16),
        w_mem_q=w_mem_q.astype(BF16), w_mem_o=w_mem_o.astype(BF16),
    )

    mem_rows = mem_prompt.reshape(batch * n_mem, d)
    mks, mvs = [], []
    for l in range(depth):
        kv = _norm_matmul(mem_rows, mem_norm[l][None, :], w_mem_kv[l].astype(BF16), F32,
                          tn_want=2 * mem_width)
        kv = kv.reshape(batch, n_mem, 2, mem_width)
        mks.append(kv[:, :, 0])
        mvs.append(kv[:, :, 1])
    mem_k_p = jnp.stack(mks)
    mem_v_p = jnp.stack(mvs)

    n_even = state_hgrn.shape[0]
    n_odd = cache_ckv.shape[0]
    h0 = jnp.zeros((n_even, batch) + state_hgrn.shape[2:], F32)
    c0 = jnp.zeros((n_even, batch) + state_conv.shape[2:], F32)
    y_p, p_hgrn, p_conv, p_ckv, p_kpe = _trunk(
        x_prompt.reshape(batch * seq, d), batch, seq, 0, mem_k_p, mem_v_p, h0, c0, None, None, p)
    y_s, s_hgrn, s_conv, s_ckv, s_kpe = _trunk(
        x_sample.reshape(dec_batch * dec_seq, d), dec_batch, dec_seq, past_len,
        cache_mem_k.reshape(depth, dec_batch, n_mem, mem_width),
        cache_mem_v.reshape(depth, dec_batch, n_mem, mem_width),
        state_hgrn, state_conv, cache_ckv, cache_kpe, p)

    mem_shape = (depth, batch, n_mem, MEM_HEADS, MEM_DIM)
    return (y_p, y_s, p_hgrn, p_conv, p_ckv, p_kpe,
            mem_k_p.reshape(mem_shape), mem_v_p.reshape(mem_shape),
            s_hgrn, s_conv, s_ckv, s_kpe)
```

```python
import functools

import jax
import jax.numpy as jnp
import numpy as np
from jax import lax
from jax.experimental import pallas as pl
from jax.experimental.pallas import tpu as pltpu

F32 = jnp.float32
BF16 = jnp.bfloat16

V7X_LANES = 128
V7X_VMEM_BYTES = 64 * 1024 * 1024
MIB = 1024 * 1024
SPILL_ROOM_BYTES = 8 * MIB

EPS = 1e-6
CHUNK = 64
HGRN_HEADS = 8
HGRN_DK = 128
HGRN_SUB = 16
MLA_HEADS = 16
QK_NOPE = 128
QK_ROPE = 64
V_DIM = 128
MLA_QK_PAD = 256
MLA_SCALE = (QK_NOPE + QK_ROPE) ** -0.5
ROPE_THETA = 10000.0
MEM_HEADS = 4
MEM_DIM = 128
CONV_W = 3
NEG_BIG = -1e30

NT_DIMS = (((1,), (1,)), ((), ()))
TN_DIMS = (((0,), (0,)), ((), ()))


def _vmem_limit(pipelined_bytes, resident_bytes):
    want = 2 * pipelined_bytes + resident_bytes + SPILL_ROOM_BYTES
    return int(min(want, V7X_VMEM_BYTES * 15 // 16))


def _params(semantics, pipelined_bytes, resident_bytes):
    return pltpu.CompilerParams(
        dimension_semantics=semantics,
        vmem_limit_bytes=_vmem_limit(pipelined_bytes, resident_bytes))


def _row_tile(rows, want):
    if rows <= want:
        return rows
    t = want - want % V7X_LANES
    while t > 0 and rows % t:
        t -= V7X_LANES
    assert t > 0, (rows, want)
    return t


def _rms(x, g):
    ms = jnp.mean(x * x, axis=-1, keepdims=True)
    return x * lax.rsqrt(ms + EPS) * g


def _sigmoid(x):
    return 1.0 / (1.0 + jnp.exp(-x))


def _dot(a, b):
    return jnp.dot(a, b, preferred_element_type=F32)


def _ffn_body(x_ref, gpre_ref, w1g_ref, w1u_ref, w2_ref, gpost_ref, o_ref, xn_ref, acc_ref):
    k = pl.program_id(1)

    @pl.when(k == 0)
    def _():
        xn_ref[...] = _rms(x_ref[...], gpre_ref[...]).astype(BF16)
        acc_ref[...] = jnp.zeros_like(acc_ref)

    xn = xn_ref[...]
    gate = _dot(xn, w1g_ref[...])
    up = _dot(xn, w1u_ref[...])
    h = (gate * _sigmoid(gate) * up).astype(BF16)
    acc_ref[...] += _dot(h, w2_ref[...])

    @pl.when(k == pl.num_programs(1) - 1)
    def _():
        o_ref[...] = x_ref[...] + 0.5 * _rms(acc_ref[...], gpost_ref[...])


def _ffn(x, g_pre, w1, w2, g_post, layer, half, tm_want=512, tf=512):
    rows, d = x.shape
    dff = w2.shape[-2]
    tm = _row_tile(rows, tm_want)
    nk = dff // tf
    assert dff % tf == 0
    blocks = tm * d * 4 * 2 + 3 * d * tf * 2
    resident = tm * d * (2 + 4) + 4 * tm * tf * 4
    return pl.pallas_call(
        _ffn_body,
        grid=(rows // tm, nk),
        in_specs=[
            pl.BlockSpec((tm, d), lambda i, k: (i, 0)),
            pl.BlockSpec((1, d), lambda i, k: (0, 0)),
            pl.BlockSpec((None, None, d, tf), lambda i, k: (layer, half, 0, k)),
            pl.BlockSpec((None, None, d, tf), lambda i, k: (layer, half, 0, nk + k)),
            pl.BlockSpec((None, None, tf, d), lambda i, k: (layer, half, k, 0)),
            pl.BlockSpec((1, d), lambda i, k: (0, 0)),
        ],
        out_specs=pl.BlockSpec((tm, d), lambda i, k: (i, 0)),
        out_shape=jax.ShapeDtypeStruct((rows, d), F32),
        scratch_shapes=[pltpu.VMEM((tm, d), BF16), pltpu.VMEM((tm, d), F32)],
        compiler_params=_params(("parallel", "arbitrary"), blocks, resident),
        name="ffn",
    )(x, g_pre, w1, w1, w2, g_post)


def _norm_matmul_body(x_ref, g_ref, w_ref, o_ref, xn_ref):
    @pl.when(pl.program_id(1) == 0)
    def _():
        xn_ref[...] = _rms(x_ref[...], g_ref[...]).astype(BF16)

    o_ref[...] = _dot(xn_ref[...], w_ref[...]).astype(o_ref.dtype)


def _norm_matmul(x, g, w, out_dtype, tm_want=512, tn_want=1024):
    rows, d = x.shape
    n = w.shape[1]
    tm = _row_tile(rows, tm_want)
    tn = _row_tile(n, tn_want)
    blocks = tm * d * 4 + d * tn * 2 + tm * tn * 4
    resident = tm * d * 2 + tm * tn * 4
    return pl.pallas_call(
        _norm_matmul_body,
        grid=(rows // tm, n // tn),
        in_specs=[
            pl.BlockSpec((tm, d), lambda i, j: (i, 0)),
            pl.BlockSpec((1, d), lambda i, j: (0, 0)),
            pl.BlockSpec((d, tn), lambda i, j: (0, j)),
        ],
        out_specs=pl.BlockSpec((tm, tn), lambda i, j: (i, j)),
        out_shape=jax.ShapeDtypeStruct((rows, n), out_dtype),
        scratch_shapes=[pltpu.VMEM((tm, d), BF16)],
        compiler_params=_params(("parallel", "arbitrary"), blocks, resident),
        name="norm_matmul",
    )(x, g, w)


def _proj_residual_body(a_ref, w_ref, g_ref, x_ref, o_ref):
    y = _dot(a_ref[...], w_ref[...])
    o_ref[...] = x_ref[...] + _rms(y, g_ref[...])


def _proj_residual(a, w, g, x, tm_want=512):
    rows, kdim = a.shape
    d = w.shape[1]
    tm = _row_tile(rows, tm_want)
    blocks = tm * kdim * 2 + kdim * d * 2 + 2 * tm * d * 4
    resident = 2 * tm * d * 4
    return pl.pallas_call(
        _proj_residual_body,
        grid=(rows // tm,),
        in_specs=[
            pl.BlockSpec((tm, kdim), lambda i: (i, 0)),
            pl.BlockSpec((kdim, d), lambda i: (0, 0)),
            pl.BlockSpec((1, d), lambda i: (0, 0)),
            pl.BlockSpec((tm, d), lambda i: (i, 0)),
        ],
        out_specs=pl.BlockSpec((tm, d), lambda i: (i, 0)),
        out_shape=jax.ShapeDtypeStruct((rows, d), F32),
        compiler_params=_params(("parallel",), blocks, resident),
        name="proj_residual",
    )(a, w, g, x)


def _hgrn_body(qa_ref, fa_ref, ia_ref, ga_ref, lb_ref, gn_ref, s0_ref,
               oa_ref, sout_ref, st_ref, o_ref):
    c = pl.program_id(1)
    rows = qa_ref.shape[0]
    width = qa_ref.shape[1]
    nsub = rows // HGRN_SUB

    @pl.when(c == 0)
    def _():
        for h in range(HGRN_HEADS):
            st_ref[h] = s0_ref[h].T

    lb = lb_ref[...]
    qa = qa_ref[...]
    q = qa * _sigmoid(qa) * (HGRN_DK ** -0.5)
    f = lb + (1.0 - lb) * _sigmoid(fa_ref[...])
    k = 1.0 - f
    g = jnp.log(f)
    v = ia_ref[...]

    r_i = lax.broadcasted_iota(jnp.int32, (rows, rows), 0)
    c_i = lax.broadcasted_iota(jnp.int32, (rows, rows), 1)
    tri = ((c_i <= r_i) & ((c_i // HGRN_SUB) == (r_i // HGRN_SUB))).astype(BF16)
    g1 = g.astype(BF16)
    rem = g - g1.astype(F32)
    g2 = rem.astype(BF16)
    g3 = (rem - g2.astype(F32)).astype(BF16)
    bl = _dot(tri, g1) + _dot(tri, g2) + _dot(tri, g3)

    qe = (q * jnp.exp(bl)).astype(BF16)
    t_idx = lax.broadcasted_iota(jnp.int32, (HGRN_SUB, width), 0)

    for i in range(nsub):
        r0 = i * HGRN_SUB
        bl_i = bl[r0:r0 + HGRN_SUB]
        q_i = q[r0:r0 + HGRN_SUB]
        k_i = k[r0:r0 + HGRN_SUB]
        v_i = v[r0:r0 + HGRN_SUB]
        b_end = bl_i[HGRN_SUB - 1:HGRN_SUB]
        ke_i = (k_i * jnp.exp(b_end - bl_i)).astype(BF16)
        dec_i = jnp.exp(b_end)
        v_bf = v_i.astype(BF16)

        diag = [jnp.zeros((HGRN_SUB, HGRN_DK), F32) for _ in range(HGRN_HEADS)]
        for s in range(HGRN_SUB):
            rel = bl_i - bl_i[s:s + 1]
            e = jnp.exp(jnp.where(t_idx >= s, rel, -jnp.inf))
            w = q_i * k_i[s:s + 1] * e
            for h in range(HGRN_HEADS):
                hs = slice(h * HGRN_DK, (h + 1) * HGRN_DK)
                col = jnp.sum(w[:, hs], axis=1, keepdims=True)
                diag[h] = diag[h] + col * v_i[s:s + 1, hs]

        for h in range(HGRN_HEADS):
            hs = slice(h * HGRN_DK, (h + 1) * HGRN_DK)
            st = st_ref[h]
            inter = lax.dot_general(qe[r0:r0 + HGRN_SUB, hs], st.astype(BF16), NT_DIMS,
                                    preferred_element_type=F32)
            upd = lax.dot_general(v_bf[:, hs], ke_i[:, hs], TN_DIMS,
                                  preferred_element_type=F32)
            st_ref[h] = st * dec_i[:, hs] + upd
            o_ref[r0:r0 + HGRN_SUB, hs] = inter + diag[h]

    gn = gn_ref[...]
    for h in range(HGRN_HEADS):
        hs = slice(h * HGRN_DK, (h + 1) * HGRN_DK)
        ga = ga_ref[:, hs]
        oa_ref[:, hs] = (_rms(o_ref[:, hs], gn) * (ga * _sigmoid(ga))).astype(oa_ref.dtype)

    @pl.when(c == pl.num_programs(1) - 1)
    def _():
        for h in range(HGRN_HEADS):
            sout_ref[h] = st_ref[h].T


def _hgrn(proj, lb, gnorm, s0, batch, seq):
    width = HGRN_HEADS * HGRN_DK
    rows = min(CHUNK, seq)
    nc = seq // rows
    assert seq % rows == 0 and rows % HGRN_SUB == 0
    col = lambda j: pl.BlockSpec((rows, width), lambda b, c: (b * nc + c, j))
    state_spec = pl.BlockSpec((None, HGRN_HEADS, HGRN_DK, HGRN_DK), lambda b, c: (b, 0, 0, 0))
    blocks = 4 * rows * width * 4 + rows * width * 2 + 2 * HGRN_HEADS * HGRN_DK * HGRN_DK * 4
    resident = HGRN_HEADS * HGRN_DK * HGRN_DK * 4 + 16 * rows * width * 4
    return pl.pallas_call(
        _hgrn_body,
        grid=(batch, nc),
        in_specs=[col(0), col(1), col(2), col(3),
                  pl.BlockSpec((1, width), lambda b, c: (0, 0)),
                  pl.BlockSpec((1, HGRN_DK), lambda b, c: (0, 0)),
                  state_spec],
        out_specs=[pl.BlockSpec((rows, width), lambda b, c: (b * nc + c, 0)), state_spec],
        out_shape=[jax.ShapeDtypeStruct((batch * seq, width), BF16),
                   jax.ShapeDtypeStruct(s0.shape, F32)],
        scratch_shapes=[pltpu.VMEM((HGRN_HEADS, HGRN_DK, HGRN_DK), F32),
                        pltpu.VMEM((rows, width), F32)],
        compiler_params=_params(("parallel", "arbitrary"), blocks, resident),
        name="hgrn2",
    )(proj, proj, proj, proj, lb, gnorm, s0)


CONV_PAD = 8


def _even_out_body(oa_ref, bg_ref, cg_ref, hb_ref, cw_ref, cs_ref, w_ref, g_ref, x_ref,
                   o_ref, cnew_ref, ubuf_ref):
    t = pl.program_id(1)
    tm = oa_ref.shape[0]
    half = oa_ref.shape[1]
    lo = CONV_PAD - (CONV_W - 1)

    @pl.when(t == 0)
    def _():
        ubuf_ref[lo:CONV_PAD, :] = cs_ref[...]

    u = cg_ref[...] * hb_ref[...]
    ubuf_ref[CONV_PAD:CONV_PAD + tm, :] = u
    yb = u * cw_ref[CONV_W - 1:CONV_W, :]
    for j in range(CONV_W - 1):
        yb = yb + ubuf_ref[lo + j:lo + j + tm, :] * cw_ref[j:j + 1, :]
    ob = (bg_ref[...] * yb).astype(BF16)
    y = _dot(oa_ref[...], w_ref[0:half, :]) + _dot(ob, w_ref[half:2 * half, :])
    o_ref[...] = x_ref[...] + _rms(y, g_ref[...])

    tail = ubuf_ref[lo + tm:CONV_PAD + tm, :]
    ubuf_ref[lo:CONV_PAD, :] = tail

    @pl.when(t == pl.num_programs(1) - 1)
    def _():
        cnew_ref[...] = tail


def _even_out(oa, proj, conv_wt, conv_state, w_out, g, x, batch, seq, tm_want=512):
    half = oa.shape[1]
    d = x.shape[1]
    tm = _row_tile(seq, tm_want)
    nt = seq // tm
    row = lambda b, t: (b * nt + t, 0)
    pcol = lambda j: pl.BlockSpec((tm, half), lambda b, t: (b * nt + t, j))
    cstate = pl.BlockSpec((None, CONV_W - 1, half), lambda b, t: (b, 0, 0))
    blocks = tm * half * (2 + 3 * 4) + 2 * half * d * 2 + 2 * tm * d * 4
    resident = (tm + CONV_PAD) * half * 4 + 4 * tm * half * 4 + 2 * tm * d * 4
    return pl.pallas_call(
        _even_out_body,
        grid=(batch, nt),
        in_specs=[pl.BlockSpec((tm, half), row), pcol(4), pcol(5), pcol(6),
                  pl.BlockSpec((CONV_W, half), lambda b, t: (0, 0)),
                  cstate,
                  pl.BlockSpec((2 * half, d), lambda b, t: (0, 0)),
                  pl.BlockSpec((1, d), lambda b, t: (0, 0)),
                  pl.BlockSpec((tm, d), row)],
        out_specs=[pl.BlockSpec((tm, d), row), cstate],
        out_shape=[jax.ShapeDtypeStruct(x.shape, F32),
                   jax.ShapeDtypeStruct(conv_state.shape, F32)],
        scratch_shapes=[pltpu.VMEM((tm + CONV_PAD, half), F32)],
        compiler_params=_params(("parallel", "arbitrary"), blocks, resident),
        name="even_out",
    )(oa, proj, proj, proj, conv_wt, conv_state, w_out, g, x)


LOG2E = 1.4426950408889634
Q_PRESCALE = MLA_SCALE * LOG2E


def _rope(x, cos, sin_lo, sin_hi):
    half = QK_ROPE // 2
    return (x * cos
            + pltpu.roll(x, V7X_LANES - half, axis=1) * sin_lo
            + pltpu.roll(x, half, axis=1) * sin_hi)


def _mla_proj_body(x_ref, g_ref, wdq_ref, qn_ref, wuq_ref, wdkv_ref, kvn_ref,
                   cos_ref, slo_ref, shi_ref, q_ref, ckv_ref, kpe_ref, kpad_ref):
    cos, slo, shi = cos_ref[...], slo_ref[...], shi_ref[...]
    lora = ckv_ref.shape[1]
    xn = _rms(x_ref[...], g_ref[...]).astype(BF16)
    cq = _rms(_dot(xn, wdq_ref[...]), qn_ref[...]).astype(BF16)
    ckr = _dot(xn, wdkv_ref[...])
    ckv_ref[...] = _rms(ckr[:, :lora], kvn_ref[...])
    kp = _rope(ckr[:, lora:lora + V7X_LANES], cos, slo, shi)
    kpe_ref[...] = kp[:, :QK_ROPE]
    kpad_ref[...] = kp.astype(BF16)
    q = _dot(cq, wuq_ref[...]) * Q_PRESCALE
    for h in range(MLA_HEADS):
        c0 = h * MLA_QK_PAD
        q_ref[h, :, 0:QK_NOPE] = q[:, c0:c0 + QK_NOPE].astype(BF16)
        q_ref[h, :, QK_NOPE:MLA_QK_PAD] = _rope(
            q[:, c0 + QK_NOPE:c0 + MLA_QK_PAD], cos, slo, shi).astype(BF16)


def _mla_proj(x, g, w_dq, q_norm, w_uq_pad, w_dkv_pad, kv_norm, cos, slo, shi, tm_want=512):
    rows, d = x.shape
    qlora = w_dq.shape[1]
    lora = kv_norm.shape[1]
    qw = w_uq_pad.shape[1]
    kw = w_dkv_pad.shape[1]
    tm = _row_tile(rows, tm_want)
    full = lambda a: pl.BlockSpec(a.shape, lambda i: (0,) * a.ndim)
    rowspec = lambda n: pl.BlockSpec((tm, n), lambda i: (i, 0))
    blocks = (tm * d * 4 + (d * qlora + qlora * qw + d * kw) * 2 + 3 * tm * V7X_LANES * 4
              + tm * (qw * 2 + lora * 4 + QK_ROPE * 4 + V7X_LANES * 2))
    resident = tm * (d * 6 + qw * 4 + kw * 4 + qlora * 8)
    return pl.pallas_call(
        _mla_proj_body,
        grid=(rows // tm,),
        in_specs=[rowspec(d), full(g), full(w_dq), full(q_norm), full(w_uq_pad),
                  full(w_dkv_pad), full(kv_norm),
                  rowspec(V7X_LANES), rowspec(V7X_LANES), rowspec(V7X_LANES)],
        out_specs=[pl.BlockSpec((MLA_HEADS, tm, MLA_QK_PAD), lambda i: (0, i, 0)),
                   rowspec(lora), rowspec(QK_ROPE), rowspec(V7X_LANES)],
        out_shape=[jax.ShapeDtypeStruct((MLA_HEADS, rows, MLA_QK_PAD), BF16),
                   jax.ShapeDtypeStruct((rows, lora), F32),
                   jax.ShapeDtypeStruct((rows, QK_ROPE), F32),
                   jax.ShapeDtypeStruct((rows, V7X_LANES), BF16)],
        compiler_params=_params(("parallel",), blocks, resident),
        name="mla_proj",
    )(x, g, w_dq, q_norm, w_uq_pad, w_dkv_pad, kv_norm, cos, slo, shi)


def _kv_up_body(ckv_ref, kpad_ref, wuk_ref, wuvt_ref, k_ref, vt_ref):
    c = ckv_ref[...].astype(BF16)
    kn = _dot(c, wuk_ref[...])
    vt = lax.dot_general(wuvt_ref[...], c, NT_DIMS, preferred_element_type=F32)
    vt_ref[...] = vt.astype(BF16).reshape(vt_ref.shape)
    kpad = kpad_ref[...]
    for h in range(MLA_HEADS):
        k_ref[h, :, 0:QK_NOPE] = kn[:, h * QK_NOPE:(h + 1) * QK_NOPE].astype(BF16)
        k_ref[h, :, QK_NOPE:MLA_QK_PAD] = kpad


def _kv_up(ckv, kpad, w_uk, w_uvt, ts_want=512):
    rows, lora = ckv.shape
    ts = _row_tile(rows, ts_want)
    rowspec = lambda n: pl.BlockSpec((ts, n), lambda i: (i, 0))
    full = lambda a: pl.BlockSpec(a.shape, lambda i: (0,) * a.ndim)
    kw = MLA_HEADS * MLA_QK_PAD
    vw = MLA_HEADS * V_DIM
    blocks = ts * (lora * 4 + V7X_LANES * 2 + kw * 2 + vw * 2) + 2 * lora * vw * 2
    resident = ts * (kw + vw) * 4
    return pl.pallas_call(
        _kv_up_body,
        grid=(rows // ts,),
        in_specs=[rowspec(lora), rowspec(V7X_LANES), full(w_uk), full(w_uvt)],
        out_specs=[pl.BlockSpec((MLA_HEADS, ts, MLA_QK_PAD), lambda i: (0, i, 0)),
                   pl.BlockSpec((MLA_HEADS, V_DIM, ts), lambda i: (0, 0, i))],
        out_shape=[jax.ShapeDtypeStruct((MLA_HEADS, rows, MLA_QK_PAD), BF16),
                   jax.ShapeDtypeStruct((MLA_HEADS, V_DIM, rows), BF16)],
        compiler_params=_params(("parallel",), blocks, resident),
        name="kv_up",
    )(ckv, kpad, w_uk, w_uvt)


ATTN_HEAD_UNROLL = 4


def _scores_t(h, q_ref, k_ref):
    return lax.dot_general(k_ref[h], q_ref[h], NT_DIMS, preferred_element_type=F32)


def _softmax_pv_t(h, st, vt_ref, m_ref, l_ref, acc_ref, mask):
    if mask is not None:
        st = jnp.where(mask, st, -jnp.inf)
    m_prev = m_ref[h]
    m_next = jnp.maximum(m_prev, jnp.max(st, axis=0, keepdims=True))
    alpha = jnp.exp2(m_prev - m_next)
    p = jnp.exp2(st - m_next)
    l_ref[h] = alpha * l_ref[h] + jnp.sum(p, axis=0, keepdims=True)
    m_ref[h] = m_next
    acc_ref[h] = alpha * acc_ref[h] + _dot(vt_ref[h], p.astype(BF16))


def _attn_block_t(q_ref, k_ref, vt_ref, m_ref, l_ref, acc_ref, s_ref, mask):
    s_ref[0] = _scores_t(0, q_ref, k_ref)

    def pair(g, carry):
        h = 2 * g
        s_ref[1] = _scores_t(h + 1, q_ref, k_ref)
        _softmax_pv_t(h, s_ref[0], vt_ref, m_ref, l_ref, acc_ref, mask)
        s_ref[0] = _scores_t(jnp.minimum(h + 2, MLA_HEADS - 1), q_ref, k_ref)
        _softmax_pv_t(h + 1, s_ref[1], vt_ref, m_ref, l_ref, acc_ref, mask)
        return carry

    lax.fori_loop(0, MLA_HEADS // 2, pair, 0, unroll=ATTN_HEAD_UNROLL)


def _attn_causal_body(qi_ref, kj_ref, q_ref, k_ref, vt_ref, o_ref, m_ref, l_ref, acc_ref, s_ref):
    step = pl.program_id(0)
    qi = qi_ref[step]
    kj = kj_ref[step]
    tq, tk = q_ref.shape[1], k_ref.shape[1]

    @pl.when(kj == 0)
    def _():
        m_ref[...] = jnp.full(m_ref.shape, NEG_BIG, F32)
        l_ref[...] = jnp.zeros_like(l_ref)
        acc_ref[...] = jnp.zeros_like(acc_ref)

    @pl.when(kj < qi)
    def _():
        _attn_block_t(q_ref, k_ref, vt_ref, m_ref, l_ref, acc_ref, s_ref, None)

    @pl.when(kj == qi)
    def _():
        key_chunk = lax.broadcasted_iota(jnp.int32, (tk, tq), 0) // CHUNK
        query_chunk = lax.broadcasted_iota(jnp.int32, (tk, tq), 1) // CHUNK
        _attn_block_t(q_ref, k_ref, vt_ref, m_ref, l_ref, acc_ref, s_ref,
                      key_chunk <= query_chunk)
        for h in range(MLA_HEADS):
            o_ref[:, h * V_DIM:(h + 1) * V_DIM] = (acc_ref[h] / l_ref[h]).T.astype(o_ref.dtype)


def _attn_causal(q, k, vt, tile_want=512):
    rows = q.shape[1]
    tile = _row_tile(rows, tile_want)
    assert tile % CHUNK == 0 and MLA_HEADS % (2 * ATTN_HEAD_UNROLL) == 0
    nq = rows // tile
    pairs = [(i, j) for i in range(nq) for j in range(i + 1)]
    qi = jnp.asarray(np.array([a for a, _ in pairs], np.int32))
    kj = jnp.asarray(np.array([b for _, b in pairs], np.int32))
    vw = MLA_HEADS * V_DIM
    blocks = MLA_HEADS * tile * (2 * MLA_QK_PAD + V_DIM) * 2 + tile * vw * 2
    resident = (MLA_HEADS * (V_DIM + 16) * tile + 8 * tile * tile) * 4
    grid_spec = pltpu.PrefetchScalarGridSpec(
        num_scalar_prefetch=2,
        grid=(len(pairs),),
        in_specs=[pl.BlockSpec((MLA_HEADS, tile, MLA_QK_PAD), lambda p, qi, kj: (0, qi[p], 0)),
                  pl.BlockSpec((MLA_HEADS, tile, MLA_QK_PAD), lambda p, qi, kj: (0, kj[p], 0)),
                  pl.BlockSpec((MLA_HEADS, V_DIM, tile), lambda p, qi, kj: (0, 0, kj[p]))],
        out_specs=pl.BlockSpec((tile, vw), lambda p, qi, kj: (qi[p], 0)),
        scratch_shapes=[pltpu.VMEM((MLA_HEADS, 1, tile), F32),
                        pltpu.VMEM((MLA_HEADS, 1, tile), F32),
                        pltpu.VMEM((MLA_HEADS, V_DIM, tile), F32),
                        pltpu.VMEM((2, tile, tile), F32)])
    return pl.pallas_call(
        _attn_causal_body,
        grid_spec=grid_spec,
        out_shape=jax.ShapeDtypeStruct((rows, vw), BF16),
        compiler_params=_params(("arbitrary",), blocks, resident),
        name="mla_attn_causal",
    )(qi, kj, q, k, vt)


def _attn_cached_body(q_ref, k_ref, vt_ref, o_ref, *, q_pos0, n_valid):
    tq, tk = q_ref.shape[1], k_ref.shape[1]
    query_chunk = (lax.broadcasted_iota(jnp.int32, (tq, tk), 0) + q_pos0) // CHUNK
    kpos = lax.broadcasted_iota(jnp.int32, (tq, tk), 1)
    mask = (kpos // CHUNK <= query_chunk) & (kpos < n_valid)
    for h in range(MLA_HEADS):
        s = lax.dot_general(q_ref[h], k_ref[h], NT_DIMS, preferred_element_type=F32)
        s = jnp.where(mask, s, -jnp.inf)
        p = jnp.exp2(s - jnp.max(s, axis=1, keepdims=True))
        l = jnp.sum(p, axis=1, keepdims=True)
        o = lax.dot_general(p.astype(BF16), vt_ref[h], NT_DIMS, preferred_element_type=F32)
        o_ref[:, h * V_DIM:(h + 1) * V_DIM] = (o / l).astype(o_ref.dtype)


def _attn_cached(q, k, vt, batch, q_pos0, n_valid):
    tq = q.shape[1] // batch
    tk = k.shape[1] // batch
    vw = MLA_HEADS * V_DIM
    blocks = MLA_HEADS * ((tq + tk) * MLA_QK_PAD + V_DIM * tk) * 2 + tq * vw * 2
    resident = 2 * MLA_HEADS * tq * tk * 4
    return pl.pallas_call(
        functools.partial(_attn_cached_body, q_pos0=q_pos0, n_valid=n_valid),
        grid=(batch,),
        in_specs=[pl.BlockSpec((MLA_HEADS, tq, MLA_QK_PAD), lambda b: (0, b, 0)),
                  pl.BlockSpec((MLA_HEADS, tk, MLA_QK_PAD), lambda b: (0, b, 0)),
                  pl.BlockSpec((MLA_HEADS, V_DIM, tk), lambda b: (0, 0, b))],
        out_specs=pl.BlockSpec((tq, vw), lambda b: (b, 0)),
        out_shape=jax.ShapeDtypeStruct((q.shape[1], vw), BF16),
        compiler_params=_params(("parallel",), blocks, resident),
        name="mla_attn_cached",
    )(q, k, vt)


def _mem_attn_body(x_ref, gpre_ref, wq_ref, mk_ref, mv_ref, wo_ref, gpost_ref, o_ref):
    x = x_ref[...]
    q = _dot(_rms(x, gpre_ref[...]).astype(BF16), wq_ref[...])
    outs = []
    for h in range(MEM_HEADS):
        hs = slice(h * MEM_DIM, (h + 1) * MEM_DIM)
        kh = mk_ref[:, hs].astype(BF16)
        vh = mv_ref[:, hs].astype(BF16)
        s = lax.dot_general(q[:, hs].astype(BF16), kh, NT_DIMS,
                            preferred_element_type=F32) * (MEM_DIM ** -0.5)
        p = jnp.exp(s - jnp.max(s, axis=1, keepdims=True))
        p = p * (1.0 / jnp.sum(p, axis=1, keepdims=True))
        outs.append(_dot(p.astype(BF16), vh).astype(BF16))
    o = jnp.concatenate(outs, axis=1)
    o_ref[...] = x + _rms(_dot(o, wo_ref[...]), gpost_ref[...])


def _mem_attn(x, g_pre, w_q, mem_k, mem_v, w_o, g_post, batch, seq, tm_want=512):
    d = x.shape[1]
    n_mem, width = mem_k.shape[1], mem_k.shape[2]
    tm = _row_tile(seq, tm_want)
    nt = seq // tm
    row = pl.BlockSpec((tm, d), lambda b, t: (b * nt + t, 0))
    full = lambda a: pl.BlockSpec(a.shape, lambda b, t: (0,) * a.ndim)
    mem = pl.BlockSpec((None, n_mem, width), lambda b, t: (b, 0, 0))
    blocks = 2 * tm * d * 4 + 2 * d * width * 2 + 2 * n_mem * width * 4
    resident = tm * (2 * d * 4 + width * 8 + n_mem * 12)
    return pl.pallas_call(
        _mem_attn_body,
        grid=(batch, nt),
        in_specs=[row, full(g_pre), full(w_q), mem, mem, full(w_o), full(g_post)],
        out_specs=row,
        out_shape=jax.ShapeDtypeStruct(x.shape, F32),
        compiler_params=_params(("parallel", "parallel"), blocks, resident),
        name="mem_attn",
    )(x, g_pre, w_q, mem_k, mem_v, w_o, g_post)


def _rope_tables(pos):
    half = QK_ROPE // 2
    inv = ROPE_THETA ** (-jnp.arange(half, dtype=F32) / half)
    ang = pos.astype(F32)[:, None] * inv
    cos, sin = jnp.cos(ang), jnp.sin(ang)
    z = lambda n: jnp.zeros((pos.shape[0], n), F32)
    pad = V7X_LANES - QK_ROPE
    return (jnp.concatenate([cos, cos, z(pad)], axis=1),
            jnp.concatenate([-sin, z(half + pad)], axis=1),
            jnp.concatenate([z(half), sin, z(pad)], axis=1))


def _prep_mla(w_uq, w_dkv, w_ukv):
    qlora = w_uq.shape[0]
    lora = w_ukv.shape[0]
    wq = w_uq.reshape(qlora, MLA_HEADS, QK_NOPE + QK_ROPE)
    wq = jnp.pad(wq, ((0, 0), (0, 0), (0, MLA_QK_PAD - QK_NOPE - QK_ROPE)))
    w_uq_pad = wq.reshape(qlora, MLA_HEADS * MLA_QK_PAD).astype(BF16)
    w_dkv_pad = jnp.pad(w_dkv, ((0, 0), (0, V7X_LANES - QK_ROPE))).astype(BF16)
    wkv = w_ukv.reshape(lora, MLA_HEADS, QK_NOPE + V_DIM)
    w_uk = wkv[:, :, :QK_NOPE].reshape(lora, MLA_HEADS * QK_NOPE).astype(BF16)
    w_uvt = wkv[:, :, QK_NOPE:].reshape(lora, MLA_HEADS * V_DIM).T.astype(BF16)
    return w_uq_pad, w_dkv_pad, w_uk, w_uvt


def _trunk(x, batch, seq, past_len, mem_k, mem_v, s_hgrn, s_conv, ckv_past, kpe_past, p):
    depth = p["norm_g"].shape[0]
    rows = batch * seq
    pos = past_len + jnp.arange(seq, dtype=jnp.int32)
    cos, slo, shi = (jnp.tile(t, (batch, 1)) for t in _rope_tables(pos))
    hs, cs, ckvs, kpes = [], [], [], []
    for l in range(depth):
        g = lambda i: p["norm_g"][l, i][None, :]
        x = _ffn(x, g(0), p["ffn_w1"], p["ffn_w2"], g(1), l, 0)
        if l % 2 == 0:
            e = l // 2
            proj = _norm_matmul(x, g(2), p["w_in0"][e], F32)
            oa, s_new = _hgrn(proj, p["lbs"][l][None, :], p["hgrn_gnorm"][e][None, :],
                              s_hgrn[e], batch, seq)
            x, c_new = _even_out(oa, proj, p["conv_wt"][e], s_conv[e], p["w_out0"][e],
                                 g(3), x, batch, seq)
            hs.append(s_new)
            cs.append(c_new)
        else:
            o = l // 2
            q, ckv_new, kpe_new, kpad_new = _mla_proj(
                x, g(2), p["mla_w_dq"][o], p["mla_q_norm"][o][None, :], p["w_uq_pad"][o],
                p["w_dkv_pad"][o], p["mla_kv_norm"][o][None, :], cos, slo, shi)
            if past_len == 0:
                assert batch == 1
                k, vt = _kv_up(ckv_new, kpad_new, p["w_uk"][o], p["w_uvt"][o])
                att = _attn_causal(q, k, vt)
            else:
                lora = ckv_new.shape[1]
                n_valid = past_len + seq
                n_keys = -(-n_valid // V7X_LANES) * V7X_LANES
                fill = n_keys - n_valid
                ckv_all = jnp.concatenate(
                    [ckv_past[o], ckv_new.reshape(batch, seq, lora),
                     jnp.zeros((batch, fill, lora), F32)], axis=1)
                kpad_past = jnp.pad(kpe_past[o], ((0, 0), (0, 0), (0, V7X_LANES - QK_ROPE)))
                kpad_all = jnp.concatenate(
                    [kpad_past.astype(BF16), kpad_new.reshape(batch, seq, V7X_LANES),
                     jnp.zeros((batch, fill, V7X_LANES), BF16)], axis=1)
                k, vt = _kv_up(ckv_all.reshape(batch * n_keys, lora),
                               kpad_all.reshape(batch * n_keys, V7X_LANES),
                               p["w_uk"][o], p["w_uvt"][o])
                att = _attn_cached(q, k, vt, batch, past_len, n_valid)
            x = _proj_residual(att, p["mla_w_o"][o], g(3), x)
            ckvs.append(ckv_new.reshape(batch, seq, -1))
            kpes.append(kpe_new.reshape(batch, seq, -1))
        x = _mem_attn(x, g(4), p["w_mem_q"][l], mem_k[l], mem_v[l], p["w_mem_o"][l], g(5),
                      batch, seq)
        x = _ffn(x, g(6), p["ffn_w1"], p["ffn_w2"], g(7), l, 1)
    return (x.reshape(batch, seq, -1), jnp.stack(hs), jnp.stack(cs), jnp.stack(ckvs),
            jnp.stack(kpes))


def kernel(x_prompt, x_sample, mem_prompt, state_hgrn, state_conv, cache_ckv, cache_kpe, cache_mem_k, cache_mem_v, norm_g, ffn_w1, ffn_w2, w_in0, hgrn_lb, hgrn_gnorm, conv_w, w_out0, mla_w_dq, mla_q_norm, mla_w_uq, mla_w_dkv, mla_kv_norm, mla_w_ukv, mla_w_o, mem_norm, w_mem_q, w_mem_kv, w_mem_o):
    batch, seq, d = x_prompt.shape
    dec_batch, dec_seq, _ = x_sample.shape
    past_len = cache_ckv.shape[2]
    depth = norm_g.shape[0]
    n_mem = mem_prompt.shape[1]
    mem_width = MEM_HEADS * MEM_DIM

    prepped = [_prep_mla(mla_w_uq[o], mla_w_dkv[o], mla_w_ukv[o]) for o in range(mla_w_uq.shape[0])]
    p = dict(
        norm_g=norm_g,
        ffn_w1=ffn_w1.astype(BF16), ffn_w2=ffn_w2.astype(BF16),
        w_in0=w_in0.astype(BF16), w_out0=w_out0.astype(BF16),
        lbs=jnp.cumsum(jax.nn.softmax(hgrn_lb.astype(F32), axis=0), axis=0),
        hgrn_gnorm=hgrn_gnorm,
        conv_wt=jnp.swapaxes(conv_w, 1, 2),
        mla_w_dq=mla_w_dq.astype(BF16), mla_q_norm=mla_q_norm, mla_kv_norm=mla_kv_norm,
        w_uq_pad=[t[0] for t in prepped], w_dkv_pad=[t[1] for t in prepped],
        w_uk=[t[2] for t in prepped], w_uvt=[t[3] for t in prepped],
        mla_w_o=mla_w_o.astype(BF16),
        w_mem_q=w_mem_q.astype(BF16), w_mem_o=w_mem_o.astype(BF16),
    )

    mem_rows = mem_prompt.reshape(batch * n_mem, d)
    mks, mvs = [], []
    for l in range(depth):
        kv = _norm_matmul(mem_rows, mem_norm[l][None, :], w_mem_kv[l].astype(BF16), F32,
                          tn_want=2 * mem_width)
        kv = kv.reshape(batch, n_mem, 2, mem_width)
        mks.append(kv[:, :, 0])
        mvs.append(kv[:, :, 1])
    mem_k_p = jnp.stack(mks)
    mem_v_p = jnp.stack(mvs)

    n_even = state_hgrn.shape[0]
    n_odd = cache_ckv.shape[0]
    h0 = jnp.zeros((n_even, batch) + state_hgrn.shape[2:], F32)
    c0 = jnp.zeros((n_even, batch) + state_conv.shape[2:], F32)
    y_p, p_hgrn, p_conv, p_ckv, p_kpe = _trunk(
        x_prompt.reshape(batch * seq, d), batch, seq, 0, mem_k_p, mem_v_p, h0, c0, None, None, p)
    y_s, s_hgrn, s_conv, s_ckv, s_kpe = _trunk(
        x_sample.reshape(dec_batch * dec_seq, d), dec_batch, dec_seq, past_len,
        cache_mem_k.reshape(depth, dec_batch, n_mem, mem_width),
        cache_mem_v.reshape(depth, dec_batch, n_mem, mem_width),
        state_hgrn, state_conv, cache_ckv, cache_kpe, p)

    mem_shape = (depth, batch, n_mem, MEM_HEADS, MEM_DIM)
    return (y_p, y_s, p_hgrn, p_conv, p_ckv, p_kpe,
            mem_k_p.reshape(mem_shape), mem_v_p.reshape(mem_shape),
            s_hgrn, s_conv, s_ckv, s_kpe)
```

```python
import functools

import jax
import jax.numpy as jnp
import numpy as np
from jax import lax
from jax.experimental import pallas as pl
from jax.experimental.pallas import tpu as pltpu

F32 = jnp.float32
BF16 = jnp.bfloat16

V7X_LANES = 128
V7X_SUBLANES = 8
V7X_VMEM_BYTES = 64 * 1024 * 1024
MIB = 1024 * 1024
SPILL_ROOM_BYTES = 8 * MIB

EPS = 1e-6
CHUNK = 64
HGRN_HEADS = 8
HGRN_DK = 128
HGRN_SUB = 16
MLA_HEADS = 16
QK_NOPE = 128
QK_ROPE = 64
V_DIM = 128
MLA_QK_PAD = 256
MLA_SCALE = (QK_NOPE + QK_ROPE) ** -0.5
ROPE_THETA = 10000.0
MEM_HEADS = 4
MEM_DIM = 128
CONV_W = 3
NEG_BIG = -1e30
LOG2E = 1.4426950408889634

NT_DIMS = (((1,), (1,)), ((), ()))
TN_DIMS = (((0,), (0,)), ((), ()))


def _vmem_limit(pipelined_bytes, resident_bytes):
    want = 2 * pipelined_bytes + resident_bytes + SPILL_ROOM_BYTES
    return int(min(want, V7X_VMEM_BYTES * 15 // 16))


def _params(semantics, pipelined_bytes, resident_bytes):
    return pltpu.CompilerParams(
        dimension_semantics=semantics,
        vmem_limit_bytes=_vmem_limit(pipelined_bytes, resident_bytes))


def _row_tile(rows, want):
    if rows <= want:
        return rows
    t = want - want % V7X_LANES
    while t > 0 and rows % t:
        t -= V7X_LANES
    assert t > 0, (rows, want)
    return t


def _rms(x, g):
    ms = jnp.mean(x * x, axis=-1, keepdims=True)
    return x * lax.rsqrt(ms + EPS) * g


def _sigmoid(x):
    return 1.0 / (1.0 + jnp.exp(-x))


def _dot(a, b):
    return jnp.dot(a, b, preferred_element_type=F32)


def _ffn_body(x_ref, gpre_ref, w1g_ref, w1u_ref, w2_ref, gpost_ref, o_ref, xn_ref, acc_ref):
    k = pl.program_id(1)

    @pl.when(k == 0)
    def _():
        xn_ref[...] = _rms(x_ref[...], gpre_ref[...]).astype(BF16)
        acc_ref[...] = jnp.zeros_like(acc_ref)

    xn = xn_ref[...]
    gate = _dot(xn, w1g_ref[...])
    up = _dot(xn, w1u_ref[...])
    h = (gate * _sigmoid(gate) * up).astype(BF16)
    acc_ref[...] += _dot(h, w2_ref[...])

    @pl.when(k == pl.num_programs(1) - 1)
    def _():
        o_ref[...] = x_ref[...] + 0.5 * _rms(acc_ref[...], gpost_ref[...])


def _ffn(x, g_pre, w1, w2, g_post, layer, half, tm_want=512, tf=512):
    rows, d = x.shape
    dff = w2.shape[-2]
    tm = _row_tile(rows, tm_want)
    nk = dff // tf
    assert dff % tf == 0
    blocks = tm * d * 4 * 2 + 3 * d * tf * 2
    resident = tm * d * (2 + 4) + 4 * tm * tf * 4
    return pl.pallas_call(
        _ffn_body,
        grid=(rows // tm, nk),
        in_specs=[
            pl.BlockSpec((tm, d), lambda i, k: (i, 0)),
            pl.BlockSpec((1, d), lambda i, k: (0, 0)),
            pl.BlockSpec((None, None, d, tf), lambda i, k: (layer, half, 0, k)),
            pl.BlockSpec((None, None, d, tf), lambda i, k: (layer, half, 0, nk + k)),
            pl.BlockSpec((None, None, tf, d), lambda i, k: (layer, half, k, 0)),
            pl.BlockSpec((1, d), lambda i, k: (0, 0)),
        ],
        out_specs=pl.BlockSpec((tm, d), lambda i, k: (i, 0)),
        out_shape=jax.ShapeDtypeStruct((rows, d), F32),
        scratch_shapes=[pltpu.VMEM((tm, d), BF16), pltpu.VMEM((tm, d), F32)],
        compiler_params=_params(("parallel", "arbitrary"), blocks, resident),
        name="ffn",
    )(x, g_pre, w1, w1, w2, g_post)


def _norm_matmul_body(x_ref, g_ref, w_ref, o_ref, xn_ref):
    @pl.when(pl.program_id(1) == 0)
    def _():
        xn_ref[...] = _rms(x_ref[...], g_ref[...]).astype(BF16)

    o_ref[...] = _dot(xn_ref[...], w_ref[...]).astype(o_ref.dtype)


def _norm_matmul(x, g, w, out_dtype, tm_want=512, tn_want=1024):
    rows, d = x.shape
    n = w.shape[1]
    tm = _row_tile(rows, tm_want)
    tn = _row_tile(n, tn_want)
    blocks = tm * d * 4 + d * tn * 2 + tm * tn * 4
    resident = tm * d * 2 + tm * tn * 4
    return pl.pallas_call(
        _norm_matmul_body,
        grid=(rows // tm, n // tn),
        in_specs=[
            pl.BlockSpec((tm, d), lambda i, j: (i, 0)),
            pl.BlockSpec((1, d), lambda i, j: (0, 0)),
            pl.BlockSpec((d, tn), lambda i, j: (0, j)),
        ],
        out_specs=pl.BlockSpec((tm, tn), lambda i, j: (i, j)),
        out_shape=jax.ShapeDtypeStruct((rows, n), out_dtype),
        scratch_shapes=[pltpu.VMEM((tm, d), BF16)],
        compiler_params=_params(("parallel", "arbitrary"), blocks, resident),
        name="norm_matmul",
    )(x, g, w)


def _proj_residual_body(a_ref, w_ref, g_ref, x_ref, o_ref):
    y = _dot(a_ref[...], w_ref[...])
    o_ref[...] = x_ref[...] + _rms(y, g_ref[...])


def _proj_residual(a, w, g, x, tm_want=512):
    rows, kdim = a.shape
    d = w.shape[1]
    tm = _row_tile(rows, tm_want)
    blocks = tm * kdim * 2 + kdim * d * 2 + 2 * tm * d * 4
    resident = 2 * tm * d * 4
    return pl.pallas_call(
        _proj_residual_body,
        grid=(rows // tm,),
        in_specs=[
            pl.BlockSpec((tm, kdim), lambda i: (i, 0)),
            pl.BlockSpec((kdim, d), lambda i: (0, 0)),
            pl.BlockSpec((1, d), lambda i: (0, 0)),
            pl.BlockSpec((tm, d), lambda i: (i, 0)),
        ],
        out_specs=pl.BlockSpec((tm, d), lambda i: (i, 0)),
        out_shape=jax.ShapeDtypeStruct((rows, d), F32),
        compiler_params=_params(("parallel",), blocks, resident),
        name="proj_residual",
    )(a, w, g, x)


def _hgrn_body(qa_ref, fa_ref, ia_ref, ga_ref, lb_ref, gn_ref, s0_ref,
               oa_ref, sout_ref, st_ref, o_ref):
    c = pl.program_id(1)
    rows = qa_ref.shape[0]
    width = qa_ref.shape[1]
    nsub = rows // HGRN_SUB

    @pl.when(c == 0)
    def _():
        for h in range(HGRN_HEADS):
            st_ref[h] = s0_ref[h].T

    lb = lb_ref[...]
    qa = qa_ref[...]
    q = qa * _sigmoid(qa) * (HGRN_DK ** -0.5)
    f = lb + (1.0 - lb) * _sigmoid(fa_ref[...])
    k = 1.0 - f
    g = jnp.log(f) * LOG2E
    v = ia_ref[...]

    r_i = lax.broadcasted_iota(jnp.int32, (rows, rows), 0)
    c_i = lax.broadcasted_iota(jnp.int32, (rows, rows), 1)
    tri = ((c_i <= r_i) & ((c_i // HGRN_SUB) == (r_i // HGRN_SUB))).astype(BF16)
    g1 = g.astype(BF16)
    rem = g - g1.astype(F32)
    g2 = rem.astype(BF16)
    g3 = (rem - g2.astype(F32)).astype(BF16)
    bl = _dot(tri, g1) + _dot(tri, g2) + _dot(tri, g3)

    qe = (q * jnp.exp2(bl)).astype(BF16)
    row8 = lax.broadcasted_iota(jnp.int32, (V7X_SUBLANES, width), 0)
    tiles = HGRN_SUB // V7X_SUBLANES

    for i in range(nsub):
        r0 = i * HGRN_SUB
        bl_i = bl[r0:r0 + HGRN_SUB]
        q_i = q[r0:r0 + HGRN_SUB]
        k_i = k[r0:r0 + HGRN_SUB]
        v_i = v[r0:r0 + HGRN_SUB]
        b_end = bl_i[HGRN_SUB - 1:HGRN_SUB]
        ke_i = (k_i * jnp.exp2(b_end - bl_i)).astype(BF16)
        dec_i = jnp.exp2(b_end)
        v_bf = v_i.astype(BF16)

        diag = [[jnp.zeros((V7X_SUBLANES, HGRN_DK), F32) for _ in range(tiles)]
                for _ in range(HGRN_HEADS)]
        for s in range(HGRN_SUB):
            ks, bs, vs = k_i[s:s + 1], bl_i[s:s + 1], v_i[s:s + 1]
            for rt in range(s // V7X_SUBLANES, tiles):
                rsl = slice(rt * V7X_SUBLANES, (rt + 1) * V7X_SUBLANES)
                rel = bl_i[rsl] - bs
                if rt == s // V7X_SUBLANES:
                    rel = jnp.where(row8 >= s % V7X_SUBLANES, rel, -jnp.inf)
                w = q_i[rsl] * ks * jnp.exp2(rel)
                for h in range(HGRN_HEADS):
                    hs = slice(h * HGRN_DK, (h + 1) * HGRN_DK)
                    col = jnp.sum(w[:, hs], axis=1, keepdims=True)
                    diag[h][rt] = diag[h][rt] + col * vs[:, hs]

        for h in range(HGRN_HEADS):
            hs = slice(h * HGRN_DK, (h + 1) * HGRN_DK)
            st = st_ref[h]
            inter = lax.dot_general(qe[r0:r0 + HGRN_SUB, hs], st.astype(BF16), NT_DIMS,
                                    preferred_element_type=F32)
            upd = lax.dot_general(v_bf[:, hs], ke_i[:, hs], TN_DIMS,
                                  preferred_element_type=F32)
            st_ref[h] = st * dec_i[:, hs] + upd
            o_ref[r0:r0 + HGRN_SUB, hs] = inter + jnp.concatenate(diag[h], axis=0)

    gn = gn_ref[...]
    for h in range(HGRN_HEADS):
        hs = slice(h * HGRN_DK, (h + 1) * HGRN_DK)
        ga = ga_ref[:, hs]
        oa_ref[:, hs] = (_rms(o_ref[:, hs], gn) * (ga * _sigmoid(ga))).astype(oa_ref.dtype)

    @pl.when(c == pl.num_programs(1) - 1)
    def _():
        for h in range(HGRN_HEADS):
            sout_ref[h] = st_ref[h].T


def _hgrn(proj, lb, gnorm, s0, batch, seq):
    width = HGRN_HEADS * HGRN_DK
    rows = min(CHUNK, seq)
    nc = seq // rows
    assert seq % rows == 0 and rows % HGRN_SUB == 0
    col = lambda j: pl.BlockSpec((rows, width), lambda b, c: (b * nc + c, j))
    state_spec = pl.BlockSpec((None, HGRN_HEADS, HGRN_DK, HGRN_DK), lambda b, c: (b, 0, 0, 0))
    blocks = 4 * rows * width * 4 + rows * width * 2 + 2 * HGRN_HEADS * HGRN_DK * HGRN_DK * 4
    resident = HGRN_HEADS * HGRN_DK * HGRN_DK * 4 + 16 * rows * width * 4
    return pl.pallas_call(
        _hgrn_body,
        grid=(batch, nc),
        in_specs=[col(0), col(1), col(2), col(3),
                  pl.BlockSpec((1, width), lambda b, c: (0, 0)),
                  pl.BlockSpec((1, HGRN_DK), lambda b, c: (0, 0)),
                  state_spec],
        out_specs=[pl.BlockSpec((rows, width), lambda b, c: (b * nc + c, 0)), state_spec],
        out_shape=[jax.ShapeDtypeStruct((batch * seq, width), BF16),
                   jax.ShapeDtypeStruct(s0.shape, F32)],
        scratch_shapes=[pltpu.VMEM((HGRN_HEADS, HGRN_DK, HGRN_DK), F32),
                        pltpu.VMEM((rows, width), F32)],
        compiler_params=_params(("parallel", "arbitrary"), blocks, resident),
        name="hgrn2",
    )(proj, proj, proj, proj, lb, gnorm, s0)


CONV_PAD = 8


def _even_out_body(oa_ref, bg_ref, cg_ref, hb_ref, cw_ref, cs_ref, w_ref, g_ref, x_ref,
                   o_ref, cnew_ref, ubuf_ref):
    t = pl.program_id(1)
    tm = oa_ref.shape[0]
    half = oa_ref.shape[1]
    lo = CONV_PAD - (CONV_W - 1)

    @pl.when(t == 0)
    def _():
        ubuf_ref[lo:CONV_PAD, :] = cs_ref[...]

    u = cg_ref[...] * hb_ref[...]
    ubuf_ref[CONV_PAD:CONV_PAD + tm, :] = u
    yb = u * cw_ref[CONV_W - 1:CONV_W, :]
    for j in range(CONV_W - 1):
        yb = yb + ubuf_ref[lo + j:lo + j + tm, :] * cw_ref[j:j + 1, :]
    ob = (bg_ref[...] * yb).astype(BF16)
    y = _dot(oa_ref[...], w_ref[0:half, :]) + _dot(ob, w_ref[half:2 * half, :])
    o_ref[...] = x_ref[...] + _rms(y, g_ref[...])

    tail = ubuf_ref[lo + tm:CONV_PAD + tm, :]
    ubuf_ref[lo:CONV_PAD, :] = tail

    @pl.when(t == pl.num_programs(1) - 1)
    def _():
        cnew_ref[...] = tail


def _even_out(oa, proj, conv_wt, conv_state, w_out, g, x, batch, seq, tm_want=512):
    half = oa.shape[1]
    d = x.shape[1]
    tm = _row_tile(seq, tm_want)
    nt = seq // tm
    row = lambda b, t: (b * nt + t, 0)
    pcol = lambda j: pl.BlockSpec((tm, half), lambda b, t: (b * nt + t, j))
    cstate = pl.BlockSpec((None, CONV_W - 1, half), lambda b, t: (b, 0, 0))
    blocks = tm * half * (2 + 3 * 4) + 2 * half * d * 2 + 2 * tm * d * 4
    resident = (tm + CONV_PAD) * half * 4 + 4 * tm * half * 4 + 2 * tm * d * 4
    return pl.pallas_call(
        _even_out_body,
        grid=(batch, nt),
        in_specs=[pl.BlockSpec((tm, half), row), pcol(4), pcol(5), pcol(6),
                  pl.BlockSpec((CONV_W, half), lambda b, t: (0, 0)),
                  cstate,
                  pl.BlockSpec((2 * half, d), lambda b, t: (0, 0)),
                  pl.BlockSpec((1, d), lambda b, t: (0, 0)),
                  pl.BlockSpec((tm, d), row)],
        out_specs=[pl.BlockSpec((tm, d), row), cstate],
        out_shape=[jax.ShapeDtypeStruct(x.shape, F32),
                   jax.ShapeDtypeStruct(conv_state.shape, F32)],
        scratch_shapes=[pltpu.VMEM((tm + CONV_PAD, half), F32)],
        compiler_params=_params(("parallel", "arbitrary"), blocks, resident),
        name="even_out",
    )(oa, proj, proj, proj, conv_wt, conv_state, w_out, g, x)


Q_PRESCALE = MLA_SCALE * LOG2E


def _rope(x, cos, sin_lo, sin_hi):
    half = QK_ROPE // 2
    return (x * cos
            + pltpu.roll(x, V7X_LANES - half, axis=1) * sin_lo
            + pltpu.roll(x, half, axis=1) * sin_hi)


def _mla_proj_body(x_ref, g_ref, wdq_ref, qn_ref, wuq_ref, wdkv_ref, kvn_ref,
                   cos_ref, slo_ref, shi_ref, q_ref, ckv_ref, kpe_ref, kpad_ref):
    cos, slo, shi = cos_ref[...], slo_ref[...], shi_ref[...]
    lora = ckv_ref.shape[1]
    xn = _rms(x_ref[...], g_ref[...]).astype(BF16)
    cq = _rms(_dot(xn, wdq_ref[...]), qn_ref[...]).astype(BF16)
    ckr = _dot(xn, wdkv_ref[...])
    ckv_ref[...] = _rms(ckr[:, :lora], kvn_ref[...])
    kp = _rope(ckr[:, lora:lora + V7X_LANES], cos, slo, shi)
    kpe_ref[...] = kp[:, :QK_ROPE]
    kpad_ref[...] = kp.astype(BF16)
    q = _dot(cq, wuq_ref[...]) * Q_PRESCALE
    for h in range(MLA_HEADS):
        c0 = h * MLA_QK_PAD
        q_ref[h, :, 0:QK_NOPE] = q[:, c0:c0 + QK_NOPE].astype(BF16)
        q_ref[h, :, QK_NOPE:MLA_QK_PAD] = _rope(
            q[:, c0 + QK_NOPE:c0 + MLA_QK_PAD], cos, slo, shi).astype(BF16)


def _mla_proj(x, g, w_dq, q_norm, w_uq_pad, w_dkv_pad, kv_norm, cos, slo, shi, tm_want=512):
    rows, d = x.shape
    qlora = w_dq.shape[1]
    lora = kv_norm.shape[1]
    qw = w_uq_pad.shape[1]
    kw = w_dkv_pad.shape[1]
    tm = _row_tile(rows, tm_want)
    full = lambda a: pl.BlockSpec(a.shape, lambda i: (0,) * a.ndim)
    rowspec = lambda n: pl.BlockSpec((tm, n), lambda i: (i, 0))
    blocks = (tm * d * 4 + (d * qlora + qlora * qw + d * kw) * 2 + 3 * tm * V7X_LANES * 4
              + tm * (qw * 2 + lora * 4 + QK_ROPE * 4 + V7X_LANES * 2))
    resident = tm * (d * 6 + qw * 4 + kw * 4 + qlora * 8)
    return pl.pallas_call(
        _mla_proj_body,
        grid=(rows // tm,),
        in_specs=[rowspec(d), full(g), full(w_dq), full(q_norm), full(w_uq_pad),
                  full(w_dkv_pad), full(kv_norm),
                  rowspec(V7X_LANES), rowspec(V7X_LANES), rowspec(V7X_LANES)],
        out_specs=[pl.BlockSpec((MLA_HEADS, tm, MLA_QK_PAD), lambda i: (0, i, 0)),
                   rowspec(lora), rowspec(QK_ROPE), rowspec(V7X_LANES)],
        out_shape=[jax.ShapeDtypeStruct((MLA_HEADS, rows, MLA_QK_PAD), BF16),
                   jax.ShapeDtypeStruct((rows, lora), F32),
                   jax.ShapeDtypeStruct((rows, QK_ROPE), F32),
                   jax.ShapeDtypeStruct((rows, V7X_LANES), BF16)],
        compiler_params=_params(("parallel",), blocks, resident),
        name="mla_proj",
    )(x, g, w_dq, q_norm, w_uq_pad, w_dkv_pad, kv_norm, cos, slo, shi)


def _kv_up_body(ckv_ref, kpad_ref, wuk_ref, wuvt_ref, k_ref, vt_ref):
    c = ckv_ref[...].astype(BF16)
    kn = _dot(c, wuk_ref[...])
    vt = lax.dot_general(wuvt_ref[...], c, NT_DIMS, preferred_element_type=F32)
    vt_ref[...] = vt.astype(BF16).reshape(vt_ref.shape)
    kpad = kpad_ref[...]
    for h in range(MLA_HEADS):
        k_ref[h, :, 0:QK_NOPE] = kn[:, h * QK_NOPE:(h + 1) * QK_NOPE].astype(BF16)
        k_ref[h, :, QK_NOPE:MLA_QK_PAD] = kpad


def _kv_up(ckv, kpad, w_uk, w_uvt, ts):
    rows, lora = ckv.shape
    assert rows % ts == 0
    rowspec = lambda n: pl.BlockSpec((ts, n), lambda i: (i, 0))
    full = lambda a: pl.BlockSpec(a.shape, lambda i: (0,) * a.ndim)
    kw = MLA_HEADS * MLA_QK_PAD
    vw = MLA_HEADS * V_DIM
    blocks = ts * (lora * 4 + V7X_LANES * 2 + kw * 2 + vw * 2) + 2 * lora * vw * 2
    resident = ts * (kw + vw) * 4
    return pl.pallas_call(
        _kv_up_body,
        grid=(rows // ts,),
        in_specs=[rowspec(lora), rowspec(V7X_LANES), full(w_uk), full(w_uvt)],
        out_specs=[pl.BlockSpec((MLA_HEADS, ts, MLA_QK_PAD), lambda i: (0, i, 0)),
                   pl.BlockSpec((None, MLA_HEADS, V_DIM, ts), lambda i: (i, 0, 0, 0))],
        out_shape=[jax.ShapeDtypeStruct((MLA_HEADS, rows, MLA_QK_PAD), BF16),
                   jax.ShapeDtypeStruct((rows // ts, MLA_HEADS, V_DIM, ts), BF16)],
        compiler_params=_params(("parallel",), blocks, resident),
        name="kv_up",
    )(ckv, kpad, w_uk, w_uvt)


ATTN_TILE = 512


def _scores_t(h, q_ref, k_ref):
    return lax.dot_general(k_ref[h], q_ref[h], NT_DIMS, preferred_element_type=F32)


def _softmax_pv_t(h, st, vt_ref, m_ref, l_ref, acc_ref, mask):
    if mask is not None:
        st = jnp.where(mask, st, -jnp.inf)
    m_prev = m_ref[h]
    m_next = jnp.maximum(m_prev, jnp.max(st, axis=0, keepdims=True))
    alpha = jnp.exp2(m_prev - m_next)
    p = jnp.exp2(st - m_next)
    l_ref[h] = alpha * l_ref[h] + jnp.sum(p, axis=0, keepdims=True)
    m_ref[h] = m_next
    acc_ref[h] = alpha * acc_ref[h] + _dot(vt_ref[h], p.astype(BF16))


def _attn_block_t(q_ref, k_ref, vt_ref, m_ref, l_ref, acc_ref, s_ref, mask):
    s_ref[0] = _scores_t(0, q_ref, k_ref)
    for h in range(MLA_HEADS):
        if h + 1 < MLA_HEADS:
            s_ref[(h + 1) % 2] = _scores_t(h + 1, q_ref, k_ref)
        _softmax_pv_t(h, s_ref[h % 2], vt_ref, m_ref, l_ref, acc_ref, mask)


def _attn_causal_body(qi_ref, kj_ref, q_ref, k_ref, vt_ref, o_ref, m_ref, l_ref, acc_ref, s_ref):
    step = pl.program_id(0)
    qi = qi_ref[step]
    kj = kj_ref[step]
    tq, tk = q_ref.shape[1], k_ref.shape[1]

    @pl.when(kj == 0)
    def _():
        m_ref[...] = jnp.full(m_ref.shape, NEG_BIG, F32)
        l_ref[...] = jnp.zeros_like(l_ref)
        acc_ref[...] = jnp.zeros_like(acc_ref)

    @pl.when(kj < qi)
    def _():
        _attn_block_t(q_ref, k_ref, vt_ref, m_ref, l_ref, acc_ref, s_ref, None)

    @pl.when(kj == qi)
    def _():
        key_chunk = lax.broadcasted_iota(jnp.int32, (tk, tq), 0) // CHUNK
        query_chunk = lax.broadcasted_iota(jnp.int32, (tk, tq), 1) // CHUNK
        _attn_block_t(q_ref, k_ref, vt_ref, m_ref, l_ref, acc_ref, s_ref,
                      key_chunk <= query_chunk)
        for h in range(MLA_HEADS):
            o_ref[:, h * V_DIM:(h + 1) * V_DIM] = (acc_ref[h] / l_ref[h]).T.astype(o_ref.dtype)


def _attn_causal(q, k, vt, tile):
    rows = q.shape[1]
    assert tile % CHUNK == 0 and rows % tile == 0 and vt.shape[3] == tile
    nq = rows // tile
    pairs = [(i, j) for i in range(nq) for j in range(i + 1)]
    qi = jnp.asarray(np.array([a for a, _ in pairs], np.int32))
    kj = jnp.asarray(np.array([b for _, b in pairs], np.int32))
    vw = MLA_HEADS * V_DIM
    blocks = MLA_HEADS * tile * (2 * MLA_QK_PAD + V_DIM) * 2 + tile * vw * 2
    resident = (MLA_HEADS * (V_DIM + 16) * tile + 8 * tile * tile) * 4
    grid_spec = pltpu.PrefetchScalarGridSpec(
        num_scalar_prefetch=2,
        grid=(len(pairs),),
        in_specs=[pl.BlockSpec((MLA_HEADS, tile, MLA_QK_PAD), lambda p, qi, kj: (0, qi[p], 0)),
                  pl.BlockSpec((MLA_HEADS, tile, MLA_QK_PAD), lambda p, qi, kj: (0, kj[p], 0)),
                  pl.BlockSpec((None, MLA_HEADS, V_DIM, tile), lambda p, qi, kj: (kj[p], 0, 0, 0))],
        out_specs=pl.BlockSpec((tile, vw), lambda p, qi, kj: (qi[p], 0)),
        scratch_shapes=[pltpu.VMEM((MLA_HEADS, 1, tile), F32),
                        pltpu.VMEM((MLA_HEADS, 1, tile), F32),
                        pltpu.VMEM((MLA_HEADS, V_DIM, tile), F32),
                        pltpu.VMEM((2, tile, tile), F32)])
    return pl.pallas_call(
        _attn_causal_body,
        grid_spec=grid_spec,
        out_shape=jax.ShapeDtypeStruct((rows, vw), BF16),
        compiler_params=_params(("arbitrary",), blocks, resident),
        name="mla_attn_causal",
    )(qi, kj, q, k, vt)


def _attn_cached_body(q_ref, k_ref, vt_ref, o_ref, *, q_pos0, n_valid):
    tq, tk = q_ref.shape[1], k_ref.shape[1]
    query_chunk = (lax.broadcasted_iota(jnp.int32, (tq, tk), 0) + q_pos0) // CHUNK
    kpos = lax.broadcasted_iota(jnp.int32, (tq, tk), 1)
    mask = (kpos // CHUNK <= query_chunk) & (kpos < n_valid)
    slab = vt_ref.shape[3]
    for h in range(MLA_HEADS):
        s = lax.dot_general(q_ref[h], k_ref[h], NT_DIMS, preferred_element_type=F32)
        s = jnp.where(mask, s, -jnp.inf)
        p = jnp.exp2(s - jnp.max(s, axis=1, keepdims=True))
        l = jnp.sum(p, axis=1, keepdims=True)
        p = p.astype(BF16)
        o = jnp.zeros((tq, V_DIM), F32)
        for j in range(vt_ref.shape[0]):
            o = o + lax.dot_general(p[:, j * slab:(j + 1) * slab], vt_ref[j, h], NT_DIMS,
                                    preferred_element_type=F32)
        o_ref[:, h * V_DIM:(h + 1) * V_DIM] = (o / l).astype(o_ref.dtype)


def _attn_cached(q, k, vt, batch, q_pos0, n_valid):
    tq = q.shape[1] // batch
    tk = k.shape[1] // batch
    slabs = vt.shape[0] // batch
    slab = vt.shape[3]
    assert slabs * slab == tk
    vw = MLA_HEADS * V_DIM
    blocks = MLA_HEADS * ((tq + tk) * MLA_QK_PAD + V_DIM * tk) * 2 + tq * vw * 2
    resident = 2 * MLA_HEADS * tq * tk * 4
    return pl.pallas_call(
        functools.partial(_attn_cached_body, q_pos0=q_pos0, n_valid=n_valid),
        grid=(batch,),
        in_specs=[pl.BlockSpec((MLA_HEADS, tq, MLA_QK_PAD), lambda b: (0, b, 0)),
                  pl.BlockSpec((MLA_HEADS, tk, MLA_QK_PAD), lambda b: (0, b, 0)),
                  pl.BlockSpec((slabs, MLA_HEADS, V_DIM, slab), lambda b: (b, 0, 0, 0))],
        out_specs=pl.BlockSpec((tq, vw), lambda b: (b, 0)),
        out_shape=jax.ShapeDtypeStruct((q.shape[1], vw), BF16),
        compiler_params=_params(("parallel",), blocks, resident),
        name="mla_attn_cached",
    )(q, k, vt)


def _mem_attn_body(x_ref, gpre_ref, wq_ref, mk_ref, mv_ref, wo_ref, gpost_ref, o_ref):
    x = x_ref[...]
    q = _dot(_rms(x, gpre_ref[...]).astype(BF16), wq_ref[...])
    outs = []
    for h in range(MEM_HEADS):
        hs = slice(h * MEM_DIM, (h + 1) * MEM_DIM)
        kh = mk_ref[:, hs].astype(BF16)
        vh = mv_ref[:, hs].astype(BF16)
        s = lax.dot_general(q[:, hs].astype(BF16), kh, NT_DIMS,
                            preferred_element_type=F32) * (MEM_DIM ** -0.5)
        p = jnp.exp(s - jnp.max(s, axis=1, keepdims=True))
        p = p * (1.0 / jnp.sum(p, axis=1, keepdims=True))
        outs.append(_dot(p.astype(BF16), vh).astype(BF16))
    o = jnp.concatenate(outs, axis=1)
    o_ref[...] = x + _rms(_dot(o, wo_ref[...]), gpost_ref[...])


def _mem_attn(x, g_pre, w_q, mem_k, mem_v, w_o, g_post, batch, seq, tm_want=512):
    d = x.shape[1]
    n_mem, width = mem_k.shape[1], mem_k.shape[2]
    tm = _row_tile(seq, tm_want)
    nt = seq // tm
    row = pl.BlockSpec((tm, d), lambda b, t: (b * nt + t, 0))
    full = lambda a: pl.BlockSpec(a.shape, lambda b, t: (0,) * a.ndim)
    mem = pl.BlockSpec((None, n_mem, width), lambda b, t: (b, 0, 0))
    blocks = 2 * tm * d * 4 + 2 * d * width * 2 + 2 * n_mem * width * 4
    resident = tm * (2 * d * 4 + width * 8 + n_mem * 12)
    return pl.pallas_call(
        _mem_attn_body,
        grid=(batch, nt),
        in_specs=[row, full(g_pre), full(w_q), mem, mem, full(w_o), full(g_post)],
        out_specs=row,
        out_shape=jax.ShapeDtypeStruct(x.shape, F32),
        compiler_params=_params(("parallel", "parallel"), blocks, resident),
        name="mem_attn",
    )(x, g_pre, w_q, mem_k, mem_v, w_o, g_post)


def _rope_tables(pos):
    half = QK_ROPE // 2
    inv = ROPE_THETA ** (-jnp.arange(half, dtype=F32) / half)
    ang = pos.astype(F32)[:, None] * inv
    cos, sin = jnp.cos(ang), jnp.sin(ang)
    z = lambda n: jnp.zeros((pos.shape[0], n), F32)
    pad = V7X_LANES - QK_ROPE
    return (jnp.concatenate([cos, cos, z(pad)], axis=1),
            jnp.concatenate([-sin, z(half + pad)], axis=1),
            jnp.concatenate([z(half), sin, z(pad)], axis=1))


def _prep_mla(w_uq, w_dkv, w_ukv):
    qlora = w_uq.shape[0]
    lora = w_ukv.shape[0]
    wq = w_uq.reshape(qlora, MLA_HEADS, QK_NOPE + QK_ROPE)
    wq = jnp.pad(wq, ((0, 0), (0, 0), (0, MLA_QK_PAD - QK_NOPE - QK_ROPE)))
    w_uq_pad = wq.reshape(qlora, MLA_HEADS * MLA_QK_PAD).astype(BF16)
    w_dkv_pad = jnp.pad(w_dkv, ((0, 0), (0, V7X_LANES - QK_ROPE))).astype(BF16)
    wkv = w_ukv.reshape(lora, MLA_HEADS, QK_NOPE + V_DIM)
    w_uk = wkv[:, :, :QK_NOPE].reshape(lora, MLA_HEADS * QK_NOPE).astype(BF16)
    w_uvt = wkv[:, :, QK_NOPE:].reshape(lora, MLA_HEADS * V_DIM).T.astype(BF16)
    return w_uq_pad, w_dkv_pad, w_uk, w_uvt


def _trunk(x, batch, seq, past_len, mem_k, mem_v, s_hgrn, s_conv, ckv_past, kpe_past, p):
    depth = p["norm_g"].shape[0]
    rows = batch * seq
    pos = past_len + jnp.arange(seq, dtype=jnp.int32)
    cos, slo, shi = (jnp.tile(t, (batch, 1)) for t in _rope_tables(pos))
    hs, cs, ckvs, kpes = [], [], [], []
    for l in range(depth):
        g = lambda i: p["norm_g"][l, i][None, :]
        x = _ffn(x, g(0), p["ffn_w1"], p["ffn_w2"], g(1), l, 0)
        if l % 2 == 0:
            e = l // 2
            proj = _norm_matmul(x, g(2), p["w_in0"][e], F32)
            oa, s_new = _hgrn(proj, p["lbs"][l][None, :], p["hgrn_gnorm"][e][None, :],
                              s_hgrn[e], batch, seq)
            x, c_new = _even_out(oa, proj, p["conv_wt"][e], s_conv[e], p["w_out0"][e],
                                 g(3), x, batch, seq)
            hs.append(s_new)
            cs.append(c_new)
        else:
            o = l // 2
            q, ckv_new, kpe_new, kpad_new = _mla_proj(
                x, g(2), p["mla_w_dq"][o], p["mla_q_norm"][o][None, :], p["w_uq_pad"][o],
                p["w_dkv_pad"][o], p["mla_kv_norm"][o][None, :], cos, slo, shi)
            if past_len == 0:
                assert batch == 1
                tile = _row_tile(seq, ATTN_TILE)
                k, vt = _kv_up(ckv_new, kpad_new, p["w_uk"][o], p["w_uvt"][o], tile)
                att = _attn_causal(q, k, vt, tile)
            else:
                lora = ckv_new.shape[1]
                n_valid = past_len + seq
                n_keys = -(-n_valid // V7X_LANES) * V7X_LANES
                fill = n_keys - n_valid
                ckv_all = jnp.concatenate(
                    [ckv_past[o], ckv_new.reshape(batch, seq, lora),
                     jnp.zeros((batch, fill, lora), F32)], axis=1)
                kpad_past = jnp.pad(kpe_past[o], ((0, 0), (0, 0), (0, V7X_LANES - QK_ROPE)))
                kpad_all = jnp.concatenate(
                    [kpad_past.astype(BF16), kpad_new.reshape(batch, seq, V7X_LANES),
                     jnp.zeros((batch, fill, V7X_LANES), BF16)], axis=1)
                k, vt = _kv_up(ckv_all.reshape(batch * n_keys, lora),
                               kpad_all.reshape(batch * n_keys, V7X_LANES),
                               p["w_uk"][o], p["w_uvt"][o], _row_tile(n_keys, ATTN_TILE))
                att = _attn_cached(q, k, vt, batch, past_len, n_valid)
            x = _proj_residual(att, p["mla_w_o"][o], g(3), x)
            ckvs.append(ckv_new.reshape(batch, seq, -1))
            kpes.append(kpe_new.reshape(batch, seq, -1))
        x = _mem_attn(x, g(4), p["w_mem_q"][l], mem_k[l], mem_v[l], p["w_mem_o"][l], g(5),
                      batch, seq)
        x = _ffn(x, g(6), p["ffn_w1"], p["ffn_w2"], g(7), l, 1)
    return (x.reshape(batch, seq, -1), jnp.stack(hs), jnp.stack(cs), jnp.stack(ckvs),
            jnp.stack(kpes))


def kernel(x_prompt, x_sample, mem_prompt, state_hgrn, state_conv, cache_ckv, cache_kpe, cache_mem_k, cache_mem_v, norm_g, ffn_w1, ffn_w2, w_in0, hgrn_lb, hgrn_gnorm, conv_w, w_out0, mla_w_dq, mla_q_norm, mla_w_uq, mla_w_dkv, mla_kv_norm, mla_w_ukv, mla_w_o, mem_norm, w_mem_q, w_mem_kv, w_mem_o):
    batch, seq, d = x_prompt.shape
    dec_batch, dec_seq, _ = x_sample.shape
    past_len = cache_ckv.shape[2]
    depth = norm_g.shape[0]
    n_mem = mem_prompt.shape[1]
    mem_width = MEM_HEADS * MEM_DIM

    prepped = [_prep_mla(mla_w_uq[o], mla_w_dkv[o], mla_w_ukv[o]) for o in range(mla_w_uq.shape[0])]
    p = dict(
        norm_g=norm_g,
        ffn_w1=ffn_w1.astype(BF16), ffn_w2=ffn_w2.astype(BF16),
        w_in0=w_in0.astype(BF16), w_out0=w_out0.astype(BF16),
        lbs=jnp.cumsum(jax.nn.softmax(hgrn_lb.astype(F32), axis=0), axis=0),
        hgrn_gnorm=hgrn_gnorm,
        conv_wt=jnp.swapaxes(conv_w, 1, 2),
        mla_w_dq=mla_w_dq.astype(BF16), mla_q_norm=mla_q_norm, mla_kv_norm=mla_kv_norm,
        w_uq_pad=[t[0] for t in prepped], w_dkv_pad=[t[1] for t in prepped],
        w_uk=[t[2] for t in prepped], w_uvt=[t[3] for t in prepped],
        mla_w_o=mla_w_o.astype(BF16),
        w_mem_q=w_mem_q.astype(BF16), w_mem_o=w_mem_o.astype(BF16),
    )

    mem_rows = mem_prompt.reshape(batch * n_mem, d)
    mks, mvs = [], []
    for l in range(depth):
        kv = _norm_matmul(mem_rows, mem_norm[l][None, :], w_mem_kv[l].astype(BF16), F32,
                          tn_want=2 * mem_width)
        kv = kv.reshape(batch, n_mem, 2, mem_width)
        mks.append(kv[:, :, 0])
        mvs.append(kv[:, :, 1])
    mem_k_p = jnp.stack(mks)
    mem_v_p = jnp.stack(mvs)

    n_even = state_hgrn.shape[0]
    n_odd = cache_ckv.shape[0]
    h0 = jnp.zeros((n_even, batch) + state_hgrn.shape[2:], F32)
    c0 = jnp.zeros((n_even, batch) + state_conv.shape[2:], F32)
    y_p, p_hgrn, p_conv, p_ckv, p_kpe = _trunk(
        x_prompt.reshape(batch * seq, d), batch, seq, 0, mem_k_p, mem_v_p, h0, c0, None, None, p)
    y_s, s_hgrn, s_conv, s_ckv, s_kpe = _trunk(
        x_sample.reshape(dec_batch * dec_seq, d), dec_batch, dec_seq, past_len,
        cache_mem_k.reshape(depth, dec_batch, n_mem, mem_width),
        cache_mem_v.reshape(depth, dec_batch, n_mem, mem_width),
        state_hgrn, state_conv, cache_ckv, cache_kpe, p)

    mem_shape = (depth, batch, n_mem, MEM_HEADS, MEM_DIM)
    return (y_p, y_s, p_hgrn, p_conv, p_ckv, p_kpe,
            mem_k_p.reshape(mem_shape), mem_v_p.reshape(mem_shape),
            s_hgrn, s_conv, s_ckv, s_kpe)
```

```python
import functools

import jax
import jax.numpy as jnp
import numpy as np
from jax import lax
from jax.experimental import pallas as pl
from jax.experimental.pallas import tpu as pltpu

F32 = jnp.float32
BF16 = jnp.bfloat16

V7X_LANES = 128
V7X_SUBLANES = 8
V7X_VMEM_BYTES = 64 * 1024 * 1024
MIB = 1024 * 1024
SPILL_ROOM_BYTES = 8 * MIB

EPS = 1e-6
CHUNK = 64
HGRN_HEADS = 8
HGRN_DK = 128
HGRN_SUB = 16
MLA_HEADS = 16
QK_NOPE = 128
QK_ROPE = 64
V_DIM = 128
MLA_QK_PAD = 256
MLA_SCALE = (QK_NOPE + QK_ROPE) ** -0.5
ROPE_THETA = 10000.0
MEM_HEADS = 4
MEM_DIM = 128
CONV_W = 3
NEG_BIG = -1e30
LOG2E = 1.4426950408889634

NT_DIMS = (((1,), (1,)), ((), ()))
TN_DIMS = (((0,), (0,)), ((), ()))


def _vmem_limit(pipelined_bytes, resident_bytes):
    want = 2 * pipelined_bytes + resident_bytes + SPILL_ROOM_BYTES
    return int(min(want, V7X_VMEM_BYTES * 15 // 16))


def _params(semantics, pipelined_bytes, resident_bytes):
    return pltpu.CompilerParams(
        dimension_semantics=semantics,
        vmem_limit_bytes=_vmem_limit(pipelined_bytes, resident_bytes))


def _row_tile(rows, want):
    if rows <= want:
        return rows
    t = want - want % V7X_LANES
    while t > 0 and rows % t:
        t -= V7X_LANES
    assert t > 0, (rows, want)
    return t


def _rms(x, g):
    ms = jnp.mean(x * x, axis=-1, keepdims=True)
    return x * lax.rsqrt(ms + EPS) * g


def _sigmoid(x):
    return 1.0 / (1.0 + jnp.exp(-x))


def _dot(a, b):
    return jnp.dot(a, b, preferred_element_type=F32)


def _ffn_body(x_ref, gpre_ref, w1g_ref, w1u_ref, w2_ref, gpost_ref, o_ref, xn_ref, acc_ref):
    k = pl.program_id(1)

    @pl.when(k == 0)
    def _():
        xn_ref[...] = _rms(x_ref[...], gpre_ref[...]).astype(BF16)
        acc_ref[...] = jnp.zeros_like(acc_ref)

    xn = xn_ref[...]
    gate = _dot(xn, w1g_ref[...])
    up = _dot(xn, w1u_ref[...])
    h = (gate * _sigmoid(gate) * up).astype(BF16)
    acc_ref[...] += _dot(h, w2_ref[...])

    @pl.when(k == pl.num_programs(1) - 1)
    def _():
        o_ref[...] = x_ref[...] + 0.5 * _rms(acc_ref[...], gpost_ref[...])


def _ffn(x, g_pre, w1, w2, g_post, layer, half, tm_want=512, tf=512):
    rows, d = x.shape
    dff = w2.shape[-2]
    tm = _row_tile(rows, tm_want)
    nk = dff // tf
    assert dff % tf == 0
    blocks = tm * d * 4 * 2 + 3 * d * tf * 2
    resident = tm * d * (2 + 4) + 4 * tm * tf * 4
    return pl.pallas_call(
        _ffn_body,
        grid=(rows // tm, nk),
        in_specs=[
            pl.BlockSpec((tm, d), lambda i, k: (i, 0)),
            pl.BlockSpec((1, d), lambda i, k: (0, 0)),
            pl.BlockSpec((None, None, d, tf), lambda i, k: (layer, half, 0, k)),
            pl.BlockSpec((None, None, d, tf), lambda i, k: (layer, half, 0, nk + k)),
            pl.BlockSpec((None, None, tf, d), lambda i, k: (layer, half, k, 0)),
            pl.BlockSpec((1, d), lambda i, k: (0, 0)),
        ],
        out_specs=pl.BlockSpec((tm, d), lambda i, k: (i, 0)),
        out_shape=jax.ShapeDtypeStruct((rows, d), F32),
        scratch_shapes=[pltpu.VMEM((tm, d), BF16), pltpu.VMEM((tm, d), F32)],
        compiler_params=_params(("parallel", "arbitrary"), blocks, resident),
        name="ffn",
    )(x, g_pre, w1, w1, w2, g_post)


def _norm_matmul_body(x_ref, g_ref, w_ref, o_ref, xn_ref):
    @pl.when(pl.program_id(1) == 0)
    def _():
        xn_ref[...] = _rms(x_ref[...], g_ref[...]).astype(BF16)

    o_ref[...] = _dot(xn_ref[...], w_ref[...]).astype(o_ref.dtype)


def _norm_matmul(x, g, w, out_dtype, tm_want=512, tn_want=1024):
    rows, d = x.shape
    n = w.shape[1]
    tm = _row_tile(rows, tm_want)
    tn = _row_tile(n, tn_want)
    blocks = tm * d * 4 + d * tn * 2 + tm * tn * 4
    resident = tm * d * 2 + tm * tn * 4
    return pl.pallas_call(
        _norm_matmul_body,
        grid=(rows // tm, n // tn),
        in_specs=[
            pl.BlockSpec((tm, d), lambda i, j: (i, 0)),
            pl.BlockSpec((1, d), lambda i, j: (0, 0)),
            pl.BlockSpec((d, tn), lambda i, j: (0, j)),
        ],
        out_specs=pl.BlockSpec((tm, tn), lambda i, j: (i, j)),
        out_shape=jax.ShapeDtypeStruct((rows, n), out_dtype),
        scratch_shapes=[pltpu.VMEM((tm, d), BF16)],
        compiler_params=_params(("parallel", "arbitrary"), blocks, resident),
        name="norm_matmul",
    )(x, g, w)


def _mem_sublayer(x, gpre_ref, wq_ref, mk_ref, mv_ref, wo_ref, gpost_ref):
    q = _dot(_rms(x, gpre_ref[...]).astype(BF16), wq_ref[...])
    outs = []
    for h in range(MEM_HEADS):
        hs = slice(h * MEM_DIM, (h + 1) * MEM_DIM)
        kh = mk_ref[:, hs].astype(BF16)
        vh = mv_ref[:, hs].astype(BF16)
        s = lax.dot_general(q[:, hs].astype(BF16), kh, NT_DIMS,
                            preferred_element_type=F32) * (MEM_DIM ** -0.5)
        p = jnp.exp(s - jnp.max(s, axis=1, keepdims=True))
        p = p * (1.0 / jnp.sum(p, axis=1, keepdims=True))
        outs.append(_dot(p.astype(BF16), vh).astype(BF16))
    o = jnp.concatenate(outs, axis=1)
    return x + _rms(_dot(o, wo_ref[...]), gpost_ref[...])


def _const_spec(a):
    return pl.BlockSpec(a.shape, lambda *_: (0,) * a.ndim, pipeline_mode=pl.Buffered(1))


def _mem_specs(mem_k, w_q, w_o, g_pre, g_post):
    n_mem, width = mem_k.shape[1], mem_k.shape[2]
    mem = pl.BlockSpec((None, n_mem, width), lambda b, t: (b, 0, 0))
    specs = [_const_spec(g_pre), _const_spec(w_q), mem, mem, _const_spec(w_o),
             _const_spec(g_post)]
    nbytes = (w_q.size + w_o.size) * 2 + 4 * n_mem * width * 4
    return specs, nbytes


def _proj_residual_body(a_ref, w_ref, g_ref, x_ref,
                        gpre_ref, wq_ref, mk_ref, mv_ref, wo_ref, gpost_ref, o_ref):
    x1 = x_ref[...] + _rms(_dot(a_ref[...], w_ref[...]), g_ref[...])
    o_ref[...] = _mem_sublayer(x1, gpre_ref, wq_ref, mk_ref, mv_ref, wo_ref, gpost_ref)


def _proj_residual(a, w, g, x, mem, batch, seq, tm_want=512):
    g_pre, w_q, mem_k, mem_v, w_o, g_post = mem
    kdim = a.shape[1]
    d = w.shape[1]
    tm = _row_tile(seq, tm_want)
    nt = seq // tm
    row = lambda n: pl.BlockSpec((tm, n), lambda b, t: (b * nt + t, 0))
    mem_specs, mem_bytes = _mem_specs(mem_k, w_q, w_o, g_pre, g_post)
    blocks = tm * kdim * 2 + 2 * tm * d * 4
    resident = kdim * d * 2 + mem_bytes + 4 * tm * d * 4
    return pl.pallas_call(
        _proj_residual_body,
        grid=(batch, nt),
        in_specs=[row(kdim), _const_spec(w), _const_spec(g), row(d)] + mem_specs,
        out_specs=row(d),
        out_shape=jax.ShapeDtypeStruct(x.shape, F32),
        compiler_params=_params(("parallel", "parallel"), blocks, resident),
        name="proj_residual_mem",
    )(a, w, g, x, g_pre, w_q, mem_k, mem_v, w_o, g_post)


def _hgrn_body(qa_ref, fa_ref, ia_ref, ga_ref, lb_ref, gn_ref, s0_ref,
               oa_ref, sout_ref, st_ref, o_ref):
    c = pl.program_id(1)
    rows = qa_ref.shape[0]
    width = qa_ref.shape[1]
    nsub = rows // HGRN_SUB

    @pl.when(c == 0)
    def _():
        for h in range(HGRN_HEADS):
            st_ref[h] = s0_ref[h].T

    lb = lb_ref[...]
    qa = qa_ref[...].astype(F32)
    q = qa * _sigmoid(qa) * (HGRN_DK ** -0.5)
    f = lb + (1.0 - lb) * _sigmoid(fa_ref[...].astype(F32))
    k = 1.0 - f
    g = jnp.log(f) * LOG2E
    v = ia_ref[...].astype(F32)

    r_i = lax.broadcasted_iota(jnp.int32, (rows, rows), 0)
    c_i = lax.broadcasted_iota(jnp.int32, (rows, rows), 1)
    tri = ((c_i <= r_i) & ((c_i // HGRN_SUB) == (r_i // HGRN_SUB))).astype(BF16)
    g1 = g.astype(BF16)
    rem = g - g1.astype(F32)
    g2 = rem.astype(BF16)
    g3 = (rem - g2.astype(F32)).astype(BF16)
    bl = _dot(tri, g1) + _dot(tri, g2) + _dot(tri, g3)

    qe = (q * jnp.exp2(bl)).astype(BF16)
    row8 = lax.broadcasted_iota(jnp.int32, (V7X_SUBLANES, width), 0)
    tiles = HGRN_SUB // V7X_SUBLANES

    for i in range(nsub):
        r0 = i * HGRN_SUB
        bl_i = bl[r0:r0 + HGRN_SUB]
        q_i = q[r0:r0 + HGRN_SUB]
        k_i = k[r0:r0 + HGRN_SUB]
        v_i = v[r0:r0 + HGRN_SUB]
        b_end = bl_i[HGRN_SUB - 1:HGRN_SUB]
        ke_i = (k_i * jnp.exp2(b_end - bl_i)).astype(BF16)
        dec_i = jnp.exp2(b_end)
        v_bf = v_i.astype(BF16)

        diag = [[jnp.zeros((V7X_SUBLANES, HGRN_DK), F32) for _ in range(tiles)]
                for _ in range(HGRN_HEADS)]
        for s in range(HGRN_SUB):
            ks, bs, vs = k_i[s:s + 1], bl_i[s:s + 1], v_i[s:s + 1]
            for rt in range(s // V7X_SUBLANES, tiles):
                rsl = slice(rt * V7X_SUBLANES, (rt + 1) * V7X_SUBLANES)
                rel = bl_i[rsl] - bs
                if rt == s // V7X_SUBLANES:
                    rel = jnp.where(row8 >= s % V7X_SUBLANES, rel, -jnp.inf)
                w = q_i[rsl] * ks * jnp.exp2(rel)
                for h in range(HGRN_HEADS):
                    hs = slice(h * HGRN_DK, (h + 1) * HGRN_DK)
                    col = jnp.sum(w[:, hs], axis=1, keepdims=True)
                    diag[h][rt] = diag[h][rt] + col * vs[:, hs]

        for h in range(HGRN_HEADS):
            hs = slice(h * HGRN_DK, (h + 1) * HGRN_DK)
            st = st_ref[h]
            inter = lax.dot_general(qe[r0:r0 + HGRN_SUB, hs], st.astype(BF16), NT_DIMS,
                                    preferred_element_type=F32)
            upd = lax.dot_general(v_bf[:, hs], ke_i[:, hs], TN_DIMS,
                                  preferred_element_type=F32)
            st_ref[h] = st * dec_i[:, hs] + upd
            o_ref[r0:r0 + HGRN_SUB, hs] = inter + jnp.concatenate(diag[h], axis=0)

    gn = gn_ref[...]
    for h in range(HGRN_HEADS):
        hs = slice(h * HGRN_DK, (h + 1) * HGRN_DK)
        ga = ga_ref[:, hs].astype(F32)
        oa_ref[:, hs] = (_rms(o_ref[:, hs], gn) * (ga * _sigmoid(ga))).astype(oa_ref.dtype)

    @pl.when(c == pl.num_programs(1) - 1)
    def _():
        for h in range(HGRN_HEADS):
            sout_ref[h] = st_ref[h].T


def _hgrn(proj, lb, gnorm, s0, batch, seq):
    width = HGRN_HEADS * HGRN_DK
    rows = min(CHUNK, seq)
    nc = seq // rows
    assert seq % rows == 0 and rows % HGRN_SUB == 0
    col = lambda j: pl.BlockSpec((rows, width), lambda b, c: (b * nc + c, j))
    state_spec = pl.BlockSpec((None, HGRN_HEADS, HGRN_DK, HGRN_DK), lambda b, c: (b, 0, 0, 0))
    blocks = 4 * rows * width * 4 + rows * width * 2 + 2 * HGRN_HEADS * HGRN_DK * HGRN_DK * 4
    resident = HGRN_HEADS * HGRN_DK * HGRN_DK * 4 + 16 * rows * width * 4
    return pl.pallas_call(
        _hgrn_body,
        grid=(batch, nc),
        in_specs=[col(0), col(1), col(2), col(3),
                  pl.BlockSpec((1, width), lambda b, c: (0, 0)),
                  pl.BlockSpec((1, HGRN_DK), lambda b, c: (0, 0)),
                  state_spec],
        out_specs=[pl.BlockSpec((rows, width), lambda b, c: (b * nc + c, 0)), state_spec],
        out_shape=[jax.ShapeDtypeStruct((batch * seq, width), BF16),
                   jax.ShapeDtypeStruct(s0.shape, F32)],
        scratch_shapes=[pltpu.VMEM((HGRN_HEADS, HGRN_DK, HGRN_DK), F32),
                        pltpu.VMEM((rows, width), F32)],
        compiler_params=_params(("parallel", "arbitrary"), blocks, resident),
        name="hgrn2",
    )(proj, proj, proj, proj, lb, gnorm, s0)


CONV_PAD = 8


def _even_out_body(oa_ref, bg_ref, cg_ref, hb_ref, cw_ref, cs_ref, w_ref, g_ref, x_ref,
                   gpre_ref, wq_ref, mk_ref, mv_ref, wo_ref, gpost_ref,
                   o_ref, cnew_ref, ubuf_ref):
    t = pl.program_id(1)
    tm = oa_ref.shape[0]
    half = oa_ref.shape[1]
    lo = CONV_PAD - (CONV_W - 1)

    @pl.when(t == 0)
    def _():
        ubuf_ref[lo:CONV_PAD, :] = cs_ref[...]

    u = cg_ref[...].astype(F32) * hb_ref[...].astype(F32)
    ubuf_ref[CONV_PAD:CONV_PAD + tm, :] = u
    yb = u * cw_ref[CONV_W - 1:CONV_W, :]
    for j in range(CONV_W - 1):
        yb = yb + ubuf_ref[lo + j:lo + j + tm, :] * cw_ref[j:j + 1, :]
    ob = (bg_ref[...].astype(F32) * yb).astype(BF16)
    y = _dot(oa_ref[...], w_ref[0:half, :]) + _dot(ob, w_ref[half:2 * half, :])
    x1 = x_ref[...] + _rms(y, g_ref[...])
    o_ref[...] = _mem_sublayer(x1, gpre_ref, wq_ref, mk_ref, mv_ref, wo_ref, gpost_ref)

    tail = ubuf_ref[lo + tm:CONV_PAD + tm, :]
    ubuf_ref[lo:CONV_PAD, :] = tail

    @pl.when(t == pl.num_programs(1) - 1)
    def _():
        cnew_ref[...] = tail


def _even_out(oa, proj, conv_wt, conv_state, w_out, g, x, mem, batch, seq, tm_want=512):
    g_pre, w_q, mem_k, mem_v, w_o, g_post = mem
    half = oa.shape[1]
    d = x.shape[1]
    tm = _row_tile(seq, tm_want)
    nt = seq // tm
    row = lambda b, t: (b * nt + t, 0)
    pcol = lambda j: pl.BlockSpec((tm, half), lambda b, t: (b * nt + t, j))
    cstate = pl.BlockSpec((None, CONV_W - 1, half), lambda b, t: (b, 0, 0))
    mem_specs, mem_bytes = _mem_specs(mem_k, w_q, w_o, g_pre, g_post)
    blocks = tm * half * 4 * proj.dtype.itemsize + 2 * tm * d * 4
    resident = (2 * half * d * 2 + mem_bytes + (tm + CONV_PAD) * half * 4 + 4 * tm * half * 4
                + 4 * tm * d * 4)
    return pl.pallas_call(
        _even_out_body,
        grid=(batch, nt),
        in_specs=[pl.BlockSpec((tm, half), row), pcol(4), pcol(5), pcol(6),
                  _const_spec(conv_wt), cstate, _const_spec(w_out), _const_spec(g),
                  pl.BlockSpec((tm, d), row)] + mem_specs,
        out_specs=[pl.BlockSpec((tm, d), row), cstate],
        out_shape=[jax.ShapeDtypeStruct(x.shape, F32),
                   jax.ShapeDtypeStruct(conv_state.shape, F32)],
        scratch_shapes=[pltpu.VMEM((tm + CONV_PAD, half), F32)],
        compiler_params=_params(("parallel", "arbitrary"), blocks, resident),
        name="even_out_mem",
    )(oa, proj, proj, proj, conv_wt, conv_state, w_out, g, x,
      g_pre, w_q, mem_k, mem_v, w_o, g_post)


Q_PRESCALE = MLA_SCALE * LOG2E


def _rope(x, cos, sin_lo, sin_hi):
    half = QK_ROPE // 2
    return (x * cos
            + pltpu.roll(x, V7X_LANES - half, axis=1) * sin_lo
            + pltpu.roll(x, half, axis=1) * sin_hi)


def _mla_proj_body(x_ref, g_ref, wdq_ref, qn_ref, wuq_ref, wdkv_ref, kvn_ref,
                   cos_ref, slo_ref, shi_ref, q_ref, ckv_ref, kpe_ref, kpad_ref):
    cos, slo, shi = cos_ref[...], slo_ref[...], shi_ref[...]
    lora = ckv_ref.shape[1]
    xn = _rms(x_ref[...], g_ref[...]).astype(BF16)
    cq = _rms(_dot(xn, wdq_ref[...]), qn_ref[...]).astype(BF16)
    ckr = _dot(xn, wdkv_ref[...])
    ckv_ref[...] = _rms(ckr[:, :lora], kvn_ref[...])
    kp = _rope(ckr[:, lora:lora + V7X_LANES], cos, slo, shi)
    kpe_ref[...] = kp[:, :QK_ROPE]
    kpad_ref[...] = kp.astype(BF16)
    q = _dot(cq, wuq_ref[...]) * Q_PRESCALE
    for h in range(MLA_HEADS):
        c0 = h * MLA_QK_PAD
        q_ref[h, :, 0:QK_NOPE] = q[:, c0:c0 + QK_NOPE].astype(BF16)
        q_ref[h, :, QK_NOPE:MLA_QK_PAD] = _rope(
            q[:, c0 + QK_NOPE:c0 + MLA_QK_PAD], cos, slo, shi).astype(BF16)


def _mla_proj(x, g, w_dq, q_norm, w_uq_pad, w_dkv_pad, kv_norm, cos, slo, shi, tm_want=512):
    rows, d = x.shape
    qlora = w_dq.shape[1]
    lora = kv_norm.shape[1]
    qw = w_uq_pad.shape[1]
    kw = w_dkv_pad.shape[1]
    tm = _row_tile(rows, tm_want)
    full = lambda a: pl.BlockSpec(a.shape, lambda i: (0,) * a.ndim)
    rowspec = lambda n: pl.BlockSpec((tm, n), lambda i: (i, 0))
    blocks = (tm * d * 4 + (d * qlora + qlora * qw + d * kw) * 2 + 3 * tm * V7X_LANES * 4
              + tm * (qw * 2 + lora * 4 + QK_ROPE * 4 + V7X_LANES * 2))
    resident = tm * (d * 6 + qw * 4 + kw * 4 + qlora * 8)
    return pl.pallas_call(
        _mla_proj_body,
        grid=(rows // tm,),
        in_specs=[rowspec(d), full(g), full(w_dq), full(q_norm), full(w_uq_pad),
                  full(w_dkv_pad), full(kv_norm),
                  rowspec(V7X_LANES), rowspec(V7X_LANES), rowspec(V7X_LANES)],
        out_specs=[pl.BlockSpec((MLA_HEADS, tm, MLA_QK_PAD), lambda i: (0, i, 0)),
                   rowspec(lora), rowspec(QK_ROPE), rowspec(V7X_LANES)],
        out_shape=[jax.ShapeDtypeStruct((MLA_HEADS, rows, MLA_QK_PAD), BF16),
                   jax.ShapeDtypeStruct((rows, lora), F32),
                   jax.ShapeDtypeStruct((rows, QK_ROPE), F32),
                   jax.ShapeDtypeStruct((rows, V7X_LANES), BF16)],
        compiler_params=_params(("parallel",), blocks, resident),
        name="mla_proj",
    )(x, g, w_dq, q_norm, w_uq_pad, w_dkv_pad, kv_norm, cos, slo, shi)


def _kv_up_body(ckv_ref, kpad_ref, wuk_ref, wuvt_ref, k_ref, vt_ref):
    c = ckv_ref[...].astype(BF16)
    kn = _dot(c, wuk_ref[...])
    vt = lax.dot_general(wuvt_ref[...], c, NT_DIMS, preferred_element_type=F32)
    vt_ref[...] = vt.astype(BF16).reshape(vt_ref.shape)
    kpad = kpad_ref[...]
    for h in range(MLA_HEADS):
        k_ref[h, :, 0:QK_NOPE] = kn[:, h * QK_NOPE:(h + 1) * QK_NOPE].astype(BF16)
        k_ref[h, :, QK_NOPE:MLA_QK_PAD] = kpad


def _kv_up(ckv, kpad, w_uk, w_uvt, ts):
    rows, lora = ckv.shape
    assert rows % ts == 0
    rowspec = lambda n: pl.BlockSpec((ts, n), lambda i: (i, 0))
    full = lambda a: pl.BlockSpec(a.shape, lambda i: (0,) * a.ndim)
    kw = MLA_HEADS * MLA_QK_PAD
    vw = MLA_HEADS * V_DIM
    blocks = ts * (lora * 4 + V7X_LANES * 2 + kw * 2 + vw * 2) + 2 * lora * vw * 2
    resident = ts * (kw + vw) * 4
    return pl.pallas_call(
        _kv_up_body,
        grid=(rows // ts,),
        in_specs=[rowspec(lora), rowspec(V7X_LANES), full(w_uk), full(w_uvt)],
        out_specs=[pl.BlockSpec((MLA_HEADS, ts, MLA_QK_PAD), lambda i: (0, i, 0)),
                   pl.BlockSpec((None, MLA_HEADS, V_DIM, ts), lambda i: (i, 0, 0, 0))],
        out_shape=[jax.ShapeDtypeStruct((MLA_HEADS, rows, MLA_QK_PAD), BF16),
                   jax.ShapeDtypeStruct((rows // ts, MLA_HEADS, V_DIM, ts), BF16)],
        compiler_params=_params(("parallel",), blocks, resident),
        name="kv_up",
    )(ckv, kpad, w_uk, w_uvt)


ATTN_TILE = 512


def _scores_t(h, q_ref, k_ref):
    return lax.dot_general(k_ref[h], q_ref[h], NT_DIMS, preferred_element_type=F32)


def _softmax_pv_t(h, st, vt_ref, m_ref, l_ref, acc_ref, mask):
    if mask is not None:
        st = jnp.where(mask, st, -jnp.inf)
    m_prev = m_ref[h]
    m_next = jnp.maximum(m_prev, jnp.max(st, axis=0, keepdims=True))
    alpha = jnp.exp2(m_prev - m_next)
    p = jnp.exp2(st - m_next)
    l_ref[h] = alpha * l_ref[h] + jnp.sum(p, axis=0, keepdims=True)
    m_ref[h] = m_next
    acc_ref[h] = alpha * acc_ref[h] + _dot(vt_ref[h], p.astype(BF16))


def _attn_block_t(q_ref, k_ref, vt_ref, m_ref, l_ref, acc_ref, s_ref, mask):
    s_ref[0] = _scores_t(0, q_ref, k_ref)
    for h in range(MLA_HEADS):
        if h + 1 < MLA_HEADS:
            s_ref[(h + 1) % 2] = _scores_t(h + 1, q_ref, k_ref)
        _softmax_pv_t(h, s_ref[h % 2], vt_ref, m_ref, l_ref, acc_ref, mask)


def _attn_causal_body(qi_ref, kj_ref, q_ref, k_ref, vt_ref, o_ref, m_ref, l_ref, acc_ref, s_ref):
    step = pl.program_id(0)
    qi = qi_ref[step]
    kj = kj_ref[step]
    tq, tk = q_ref.shape[1], k_ref.shape[1]

    @pl.when(kj == 0)
    def _():
        m_ref[...] = jnp.full(m_ref.shape, NEG_BIG, F32)
        l_ref[...] = jnp.zeros_like(l_ref)
        acc_ref[...] = jnp.zeros_like(acc_ref)

    @pl.when(kj < qi)
    def _():
        _attn_block_t(q_ref, k_ref, vt_ref, m_ref, l_ref, acc_ref, s_ref, None)

    @pl.when(kj == qi)
    def _():
        key_chunk = lax.broadcasted_iota(jnp.int32, (tk, tq), 0) // CHUNK
        query_chunk = lax.broadcasted_iota(jnp.int32, (tk, tq), 1) // CHUNK
        _attn_block_t(q_ref, k_ref, vt_ref, m_ref, l_ref, acc_ref, s_ref,
                      key_chunk <= query_chunk)
        for h in range(MLA_HEADS):
            o_ref[:, h * V_DIM:(h + 1) * V_DIM] = (acc_ref[h] / l_ref[h]).T.astype(o_ref.dtype)


def _attn_causal(q, k, vt, tile):
    rows = q.shape[1]
    assert tile % CHUNK == 0 and rows % tile == 0 and vt.shape[3] == tile
    nq = rows // tile
    pairs = [(i, j) for i in range(nq) for j in range(i + 1)]
    qi = jnp.asarray(np.array([a for a, _ in pairs], np.int32))
    kj = jnp.asarray(np.array([b for _, b in pairs], np.int32))
    vw = MLA_HEADS * V_DIM
    blocks = MLA_HEADS * tile * (2 * MLA_QK_PAD + V_DIM) * 2 + tile * vw * 2
    resident = (MLA_HEADS * (V_DIM + 16) * tile + 8 * tile * tile) * 4
    grid_spec = pltpu.PrefetchScalarGridSpec(
        num_scalar_prefetch=2,
        grid=(len(pairs),),
        in_specs=[pl.BlockSpec((MLA_HEADS, tile, MLA_QK_PAD), lambda p, qi, kj: (0, qi[p], 0)),
                  pl.BlockSpec((MLA_HEADS, tile, MLA_QK_PAD), lambda p, qi, kj: (0, kj[p], 0)),
                  pl.BlockSpec((None, MLA_HEADS, V_DIM, tile), lambda p, qi, kj: (kj[p], 0, 0, 0))],
        out_specs=pl.BlockSpec((tile, vw), lambda p, qi, kj: (qi[p], 0)),
        scratch_shapes=[pltpu.VMEM((MLA_HEADS, 1, tile), F32),
                        pltpu.VMEM((MLA_HEADS, 1, tile), F32),
                        pltpu.VMEM((MLA_HEADS, V_DIM, tile), F32),
                        pltpu.VMEM((2, tile, tile), F32)])
    return pl.pallas_call(
        _attn_causal_body,
        grid_spec=grid_spec,
        out_shape=jax.ShapeDtypeStruct((rows, vw), BF16),
        compiler_params=_params(("arbitrary",), blocks, resident),
        name="mla_attn_causal",
    )(qi, kj, q, k, vt)


def _attn_cached_body(q_ref, k_ref, vt_ref, o_ref, s_ref, p_ref, *, q_pos0, n_valid):
    tq, tk = q_ref.shape[1], k_ref.shape[1]
    query_chunk = (lax.broadcasted_iota(jnp.int32, (tq, tk), 0) + q_pos0) // CHUNK
    kpos = lax.broadcasted_iota(jnp.int32, (tq, tk), 1)
    mask = (kpos // CHUNK <= query_chunk) & (kpos < n_valid)
    slab = vt_ref.shape[3]
    for h in range(MLA_HEADS):
        s_ref[h] = lax.dot_general(q_ref[h], k_ref[h], NT_DIMS, preferred_element_type=F32)
    s = jnp.where(mask[None], s_ref[...], -jnp.inf)
    p = jnp.exp2(s - jnp.max(s, axis=2, keepdims=True))
    inv_l = 1.0 / jnp.sum(p, axis=2, keepdims=True)
    p_ref[...] = p.astype(BF16)
    for h in range(MLA_HEADS):
        o = jnp.zeros((tq, V_DIM), F32)
        for j in range(vt_ref.shape[0]):
            o = o + lax.dot_general(p_ref[h, :, j * slab:(j + 1) * slab], vt_ref[j, h], NT_DIMS,
                                    preferred_element_type=F32)
        o_ref[:, h * V_DIM:(h + 1) * V_DIM] = (o * inv_l[h]).astype(o_ref.dtype)


def _attn_cached(q, k, vt, batch, q_pos0, n_valid):
    tq = q.shape[1] // batch
    tk = k.shape[1] // batch
    slabs = vt.shape[0] // batch
    slab = vt.shape[3]
    assert slabs * slab == tk
    vw = MLA_HEADS * V_DIM
    blocks = MLA_HEADS * ((tq + tk) * MLA_QK_PAD + V_DIM * tk) * 2 + tq * vw * 2
    resident = 4 * MLA_HEADS * tq * tk * 4
    return pl.pallas_call(
        functools.partial(_attn_cached_body, q_pos0=q_pos0, n_valid=n_valid),
        grid=(batch,),
        in_specs=[pl.BlockSpec((MLA_HEADS, tq, MLA_QK_PAD), lambda b: (0, b, 0)),
                  pl.BlockSpec((MLA_HEADS, tk, MLA_QK_PAD), lambda b: (0, b, 0)),
                  pl.BlockSpec((slabs, MLA_HEADS, V_DIM, slab), lambda b: (b, 0, 0, 0))],
        out_specs=pl.BlockSpec((tq, vw), lambda b: (b, 0)),
        out_shape=jax.ShapeDtypeStruct((q.shape[1], vw), BF16),
        scratch_shapes=[pltpu.VMEM((MLA_HEADS, tq, tk), F32), pltpu.VMEM((MLA_HEADS, tq, tk), BF16)],
        compiler_params=_params(("parallel",), blocks, resident),
        name="mla_attn_cached",
    )(q, k, vt)


def _rope_tables(pos):
    half = QK_ROPE // 2
    inv = ROPE_THETA ** (-jnp.arange(half, dtype=F32) / half)
    ang = pos.astype(F32)[:, None] * inv
    cos, sin = jnp.cos(ang), jnp.sin(ang)
    z = lambda n: jnp.zeros((pos.shape[0], n), F32)
    pad = V7X_LANES - QK_ROPE
    return (jnp.concatenate([cos, cos, z(pad)], axis=1),
            jnp.concatenate([-sin, z(half + pad)], axis=1),
            jnp.concatenate([z(half), sin, z(pad)], axis=1))


def _prep_mla(w_uq, w_dkv, w_ukv):
    qlora = w_uq.shape[0]
    lora = w_ukv.shape[0]
    wq = w_uq.reshape(qlora, MLA_HEADS, QK_NOPE + QK_ROPE)
    wq = jnp.pad(wq, ((0, 0), (0, 0), (0, MLA_QK_PAD - QK_NOPE - QK_ROPE)))
    w_uq_pad = wq.reshape(qlora, MLA_HEADS * MLA_QK_PAD).astype(BF16)
    w_dkv_pad = jnp.pad(w_dkv, ((0, 0), (0, V7X_LANES - QK_ROPE))).astype(BF16)
    wkv = w_ukv.reshape(lora, MLA_HEADS, QK_NOPE + V_DIM)
    w_uk = wkv[:, :, :QK_NOPE].reshape(lora, MLA_HEADS * QK_NOPE).astype(BF16)
    w_uvt = wkv[:, :, QK_NOPE:].reshape(lora, MLA_HEADS * V_DIM).T.astype(BF16)
    return w_uq_pad, w_dkv_pad, w_uk, w_uvt


def _trunk(x, batch, seq, past_len, mem_k, mem_v, s_hgrn, s_conv, ckv_past, kpe_past, p):
    depth = p["norm_g"].shape[0]
    rows = batch * seq
    pos = past_len + jnp.arange(seq, dtype=jnp.int32)
    cos, slo, shi = (jnp.tile(t, (batch, 1)) for t in _rope_tables(pos))
    hs, cs, ckvs, kpes = [], [], [], []
    for l in range(depth):
        g = lambda i: p["norm_g"][l, i][None, :]
        x = _ffn(x, g(0), p["ffn_w1"], p["ffn_w2"], g(1), l, 0)
        mem = (g(4), p["w_mem_q"][l], mem_k[l], mem_v[l], p["w_mem_o"][l], g(5))
        if l % 2 == 0:
            e = l // 2
            proj = _norm_matmul(x, g(2), p["w_in0"][e], BF16, tm_want=1024)
            oa, s_new = _hgrn(proj, p["lbs"][l][None, :], p["hgrn_gnorm"][e][None, :],
                              s_hgrn[e], batch, seq)
            x, c_new = _even_out(oa, proj, p["conv_wt"][e], s_conv[e], p["w_out0"][e],
                                 g(3), x, mem, batch, seq)
            hs.append(s_new)
            cs.append(c_new)
        else:
            o = l // 2
            q, ckv_new, kpe_new, kpad_new = _mla_proj(
                x, g(2), p["mla_w_dq"][o], p["mla_q_norm"][o][None, :], p["w_uq_pad"][o],
                p["w_dkv_pad"][o], p["mla_kv_norm"][o][None, :], cos, slo, shi)
            if past_len == 0:
                assert batch == 1
                tile = _row_tile(seq, ATTN_TILE)
                k, vt = _kv_up(ckv_new, kpad_new, p["w_uk"][o], p["w_uvt"][o], tile)
                att = _attn_causal(q, k, vt, tile)
            else:
                lora = ckv_new.shape[1]
                n_valid = past_len + seq
                n_keys = -(-n_valid // V7X_LANES) * V7X_LANES
                fill = n_keys - n_valid
                ckv_all = jnp.concatenate(
                    [ckv_past[o], ckv_new.reshape(batch, seq, lora),
                     jnp.zeros((batch, fill, lora), F32)], axis=1)
                kpad_past = jnp.pad(kpe_past[o], ((0, 0), (0, 0), (0, V7X_LANES - QK_ROPE)))
                kpad_all = jnp.concatenate(
                    [kpad_past.astype(BF16), kpad_new.reshape(batch, seq, V7X_LANES),
                     jnp.zeros((batch, fill, V7X_LANES), BF16)], axis=1)
                k, vt = _kv_up(ckv_all.reshape(batch * n_keys, lora),
                               kpad_all.reshape(batch * n_keys, V7X_LANES),
                               p["w_uk"][o], p["w_uvt"][o], _row_tile(n_keys, ATTN_TILE))
                att = _attn_cached(q, k, vt, batch, past_len, n_valid)
            x = _proj_residual(att, p["mla_w_o"][o], g(3), x, mem, batch, seq)
            ckvs.append(ckv_new.reshape(batch, seq, -1))
            kpes.append(kpe_new.reshape(batch, seq, -1))
        x = _ffn(x, g(6), p["ffn_w1"], p["ffn_w2"], g(7), l, 1)
    return (x.reshape(batch, seq, -1), jnp.stack(hs), jnp.stack(cs), jnp.stack(ckvs),
            jnp.stack(kpes))


def kernel(x_prompt, x_sample, mem_prompt, state_hgrn, state_conv, cache_ckv, cache_kpe, cache_mem_k, cache_mem_v, norm_g, ffn_w1, ffn_w2, w_in0, hgrn_lb, hgrn_gnorm, conv_w, w_out0, mla_w_dq, mla_q_norm, mla_w_uq, mla_w_dkv, mla_kv_norm, mla_w_ukv, mla_w_o, mem_norm, w_mem_q, w_mem_kv, w_mem_o):
    batch, seq, d = x_prompt.shape
    dec_batch, dec_seq, _ = x_sample.shape
    past_len = cache_ckv.shape[2]
    depth = norm_g.shape[0]
    n_mem = mem_prompt.shape[1]
    mem_width = MEM_HEADS * MEM_DIM

    prepped = [_prep_mla(mla_w_uq[o], mla_w_dkv[o], mla_w_ukv[o]) for o in range(mla_w_uq.shape[0])]
    p = dict(
        norm_g=norm_g,
        ffn_w1=ffn_w1.astype(BF16), ffn_w2=ffn_w2.astype(BF16),
        w_in0=w_in0.astype(BF16), w_out0=w_out0.astype(BF16),
        lbs=jnp.cumsum(jax.nn.softmax(hgrn_lb.astype(F32), axis=0), axis=0),
        hgrn_gnorm=hgrn_gnorm,
        conv_wt=jnp.swapaxes(conv_w, 1, 2),
        mla_w_dq=mla_w_dq.astype(BF16), mla_q_norm=mla_q_norm, mla_kv_norm=mla_kv_norm,
        w_uq_pad=[t[0] for t in prepped], w_dkv_pad=[t[1] for t in prepped],
        w_uk=[t[2] for t in prepped], w_uvt=[t[3] for t in prepped],
        mla_w_o=mla_w_o.astype(BF16),
        w_mem_q=w_mem_q.astype(BF16), w_mem_o=w_mem_o.astype(BF16),
    )

    mem_rows = mem_prompt.reshape(batch * n_mem, d)
    mks, mvs = [], []
    for l in range(depth):
        kv = _norm_matmul(mem_rows, mem_norm[l][None, :], w_mem_kv[l].astype(BF16), F32,
                          tn_want=2 * mem_width)
        kv = kv.reshape(batch, n_mem, 2, mem_width)
        mks.append(kv[:, :, 0])
        mvs.append(kv[:, :, 1])
    mem_k_p = jnp.stack(mks)
    mem_v_p = jnp.stack(mvs)

    n_even = state_hgrn.shape[0]
    n_odd = cache_ckv.shape[0]
    h0 = jnp.zeros((n_even, batch) + state_hgrn.shape[2:], F32)
    c0 = jnp.zeros((n_even, batch) + state_conv.shape[2:], F32)
    y_p, p_hgrn, p_conv, p_ckv, p_kpe = _trunk(
        x_prompt.reshape(batch * seq, d), batch, seq, 0, mem_k_p, mem_v_p, h0, c0, None, None, p)
    y_s, s_hgrn, s_conv, s_ckv, s_kpe = _trunk(
        x_sample.reshape(dec_batch * dec_seq, d), dec_batch, dec_seq, past_len,
        cache_mem_k.reshape(depth, dec_batch, n_mem, mem_width),
        cache_mem_v.reshape(depth, dec_batch, n_mem, mem_width),
        state_hgrn, state_conv, cache_ckv, cache_kpe, p)

    mem_shape = (depth, batch, n_mem, MEM_HEADS, MEM_DIM)
    return (y_p, y_s, p_hgrn, p_conv, p_ckv, p_kpe,
            mem_k_p.reshape(mem_shape), mem_v_p.reshape(mem_shape),
            s_hgrn, s_conv, s_ckv, s_kpe)
```

```python
import functools

import jax
import jax.numpy as jnp
import numpy as np
from jax import lax
from jax.experimental import pallas as pl
from jax.experimental.pallas import tpu as pltpu

F32 = jnp.float32
BF16 = jnp.bfloat16

V7X_LANES = 128
V7X_SUBLANES = 8
V7X_VMEM_BYTES = 64 * 1024 * 1024
MIB = 1024 * 1024
SPILL_ROOM_BYTES = 8 * MIB
VMEM_RESERVED_BYTES = 2 * MIB

EPS = 1e-6
CHUNK = 64
HGRN_HEADS = 8
HGRN_DK = 128
HGRN_SUB = 16
MLA_HEADS = 16
QK_NOPE = 128
QK_ROPE = 64
V_DIM = 128
V_EXT = V_DIM + 16
MLA_QK_PAD = 256
MLA_SCALE = (QK_NOPE + QK_ROPE) ** -0.5
ROPE_THETA = 10000.0
MEM_HEADS = 4
MEM_DIM = 128
CONV_W = 3
NEG_BIG = -1e30
LOG2E = 1.4426950408889634

NT_DIMS = (((1,), (1,)), ((), ()))
TN_DIMS = (((0,), (0,)), ((), ()))


def _vmem_limit(pipelined_bytes, resident_bytes):
    want = 2 * pipelined_bytes + resident_bytes + SPILL_ROOM_BYTES
    return int(min(want, V7X_VMEM_BYTES - VMEM_RESERVED_BYTES))


def _params(semantics, pipelined_bytes, resident_bytes):
    return pltpu.CompilerParams(
        dimension_semantics=semantics,
        vmem_limit_bytes=_vmem_limit(pipelined_bytes, resident_bytes))


def _row_tile(rows, want):
    if rows <= want:
        return rows
    t = want - want % V7X_LANES
    while t > 0 and rows % t:
        t -= V7X_LANES
    assert t > 0, (rows, want)
    return t


def _rms(x, g):
    ms = jnp.mean(x * x, axis=-1, keepdims=True)
    return x * lax.rsqrt(ms + EPS) * g


def _sigmoid(x):
    return 1.0 / (1.0 + jnp.exp(-x))


def _dot(a, b):
    return jnp.dot(a, b, preferred_element_type=F32)


def _ffn_body(x_ref, gpre_ref, w1g_ref, w1u_ref, w2_ref, gpost_ref, o_ref, xn_ref):
    k = pl.program_id(1)

    @pl.when(k == 0)
    def _():
        xn_ref[...] = _rms(x_ref[...], gpre_ref[...]).astype(BF16)
        o_ref[...] = jnp.zeros_like(o_ref)

    xn = xn_ref[...]
    gate = _dot(xn, w1g_ref[...])
    up = _dot(xn, w1u_ref[...])
    h = (gate * _sigmoid(gate) * up).astype(BF16)
    o_ref[...] += _dot(h, w2_ref[...])

    @pl.when(k == pl.num_programs(1) - 1)
    def _():
        o_ref[...] = x_ref[...] + 0.5 * _rms(o_ref[...], gpost_ref[...])


def _ffn(x, g_pre, w1, w2, g_post, layer, half, tm_want=1024, tf=512):
    rows, d = x.shape
    dff = w2.shape[-2]
    tm = _row_tile(rows, tm_want)
    nk = dff // tf
    assert dff % tf == 0
    blocks = 2 * tm * d * 4 + 3 * d * tf * 2
    resident = tm * d * (2 + 4) + 4 * tm * tf * 4
    return pl.pallas_call(
        _ffn_body,
        grid=(rows // tm, nk),
        in_specs=[
            pl.BlockSpec((tm, d), lambda i, k: (i, 0)),
            pl.BlockSpec((1, d), lambda i, k: (0, 0)),
            pl.BlockSpec((None, None, d, tf), lambda i, k: (layer, half, 0, k)),
            pl.BlockSpec((None, None, d, tf), lambda i, k: (layer, half, 0, nk + k)),
            pl.BlockSpec((None, None, tf, d), lambda i, k: (layer, half, k, 0)),
            pl.BlockSpec((1, d), lambda i, k: (0, 0)),
        ],
        out_specs=pl.BlockSpec((tm, d), lambda i, k: (i, 0)),
        out_shape=jax.ShapeDtypeStruct((rows, d), F32),
        scratch_shapes=[pltpu.VMEM((tm, d), BF16)],
        compiler_params=_params(("parallel", "arbitrary"), blocks, resident),
        name="ffn",
    )(x, g_pre, w1, w1, w2, g_post)


def _norm_matmul_body(x_ref, g_ref, w_ref, o_ref, xn_ref):
    @pl.when(pl.program_id(1) == 0)
    def _():
        xn_ref[...] = _rms(x_ref[...], g_ref[...]).astype(BF16)

    o_ref[...] = _dot(xn_ref[...], w_ref[...]).astype(o_ref.dtype)


def _norm_matmul(x, g, w, out_dtype, tm_want=512, tn_want=1024):
    rows, d = x.shape
    n = w.shape[1]
    tm = _row_tile(rows, tm_want)
    tn = _row_tile(n, tn_want)
    blocks = tm * d * 4 + d * tn * 2 + tm * tn * 4
    resident = tm * d * 2 + tm * tn * 4
    return pl.pallas_call(
        _norm_matmul_body,
        grid=(rows // tm, n // tn),
        in_specs=[
            pl.BlockSpec((tm, d), lambda i, j: (i, 0)),
            pl.BlockSpec((1, d), lambda i, j: (0, 0)),
            pl.BlockSpec((d, tn), lambda i, j: (0, j)),
        ],
        out_specs=pl.BlockSpec((tm, tn), lambda i, j: (i, j)),
        out_shape=jax.ShapeDtypeStruct((rows, n), out_dtype),
        scratch_shapes=[pltpu.VMEM((tm, d), BF16)],
        compiler_params=_params(("parallel", "arbitrary"), blocks, resident),
        name="norm_matmul",
    )(x, g, w)


def _mem_sublayer(x, gpre_ref, wq_ref, mk_ref, mv_ref, wo_ref, gpost_ref):
    q = _dot(_rms(x, gpre_ref[...]).astype(BF16), wq_ref[...])
    outs = []
    for h in range(MEM_HEADS):
        hs = slice(h * MEM_DIM, (h + 1) * MEM_DIM)
        kh = mk_ref[:, hs].astype(BF16)
        vh = mv_ref[:, hs].astype(BF16)
        s = lax.dot_general(q[:, hs].astype(BF16), kh, NT_DIMS,
                            preferred_element_type=F32) * (MEM_DIM ** -0.5)
        p = jnp.exp(s - jnp.max(s, axis=1, keepdims=True))
        p = p * (1.0 / jnp.sum(p, axis=1, keepdims=True))
        outs.append(_dot(p.astype(BF16), vh).astype(BF16))
    o = jnp.concatenate(outs, axis=1)
    return x + _rms(_dot(o, wo_ref[...]), gpost_ref[...])


def _const_spec(a):
    return pl.BlockSpec(a.shape, lambda *_: (0,) * a.ndim, pipeline_mode=pl.Buffered(1))


def _mem_specs(mem_k, w_q, w_o, g_pre, g_post):
    n_mem, width = mem_k.shape[1], mem_k.shape[2]
    mem = pl.BlockSpec((None, n_mem, width), lambda b, t: (b, 0, 0))
    specs = [_const_spec(g_pre), _const_spec(w_q), mem, mem, _const_spec(w_o),
             _const_spec(g_post)]
    nbytes = (w_q.size + w_o.size) * 2 + 4 * n_mem * width * 4
    return specs, nbytes


def _proj_residual_body(a_ref, w_ref, g_ref, x_ref,
                        gpre_ref, wq_ref, mk_ref, mv_ref, wo_ref, gpost_ref, o_ref):
    x1 = x_ref[...] + _rms(_dot(a_ref[...], w_ref[...]), g_ref[...])
    o_ref[...] = _mem_sublayer(x1, gpre_ref, wq_ref, mk_ref, mv_ref, wo_ref, gpost_ref)


def _proj_residual(a, w, g, x, mem, batch, seq, tm_want=512):
    g_pre, w_q, mem_k, mem_v, w_o, g_post = mem
    kdim = a.shape[1]
    d = w.shape[1]
    tm = _row_tile(seq, tm_want)
    nt = seq // tm
    row = lambda n: pl.BlockSpec((tm, n), lambda b, t: (b * nt + t, 0))
    mem_specs, mem_bytes = _mem_specs(mem_k, w_q, w_o, g_pre, g_post)
    blocks = tm * kdim * 2 + 2 * tm * d * 4
    resident = kdim * d * 2 + mem_bytes + 4 * tm * d * 4
    return pl.pallas_call(
        _proj_residual_body,
        grid=(batch, nt),
        in_specs=[row(kdim), _const_spec(w), _const_spec(g), row(d)] + mem_specs,
        out_specs=row(d),
        out_shape=jax.ShapeDtypeStruct(x.shape, F32),
        compiler_params=_params(("parallel", "parallel"), blocks, resident),
        name="proj_residual_mem",
    )(a, w, g, x, g_pre, w_q, mem_k, mem_v, w_o, g_post)


def _hgrn_body(qa_ref, fa_ref, ia_ref, ga_ref, lb_ref, gn_ref, s0_ref,
               oa_ref, sout_ref, st_ref, o_ref):
    c = pl.program_id(1)
    rows = qa_ref.shape[0]
    width = qa_ref.shape[1]
    nsub = rows // HGRN_SUB

    @pl.when(c == 0)
    def _():
        for h in range(HGRN_HEADS):
            st_ref[h] = s0_ref[h].T

    lb = lb_ref[...]
    qa = qa_ref[...].astype(F32)
    q = qa * _sigmoid(qa) * (HGRN_DK ** -0.5)
    f = lb + (1.0 - lb) * _sigmoid(fa_ref[...].astype(F32))
    k = 1.0 - f
    g = jnp.log(f) * LOG2E
    v = ia_ref[...].astype(F32)

    r_i = lax.broadcasted_iota(jnp.int32, (rows, rows), 0)
    c_i = lax.broadcasted_iota(jnp.int32, (rows, rows), 1)
    tri = ((c_i <= r_i) & ((c_i // HGRN_SUB) == (r_i // HGRN_SUB))).astype(BF16)
    g1 = g.astype(BF16)
    rem = g - g1.astype(F32)
    g2 = rem.astype(BF16)
    g3 = (rem - g2.astype(F32)).astype(BF16)
    bl = _dot(tri, g1) + _dot(tri, g2) + _dot(tri, g3)

    qe = (q * jnp.exp2(bl)).astype(BF16)
    row8 = lax.broadcasted_iota(jnp.int32, (V7X_SUBLANES, width), 0)
    tiles = HGRN_SUB // V7X_SUBLANES

    for i in range(nsub):
        r0 = i * HGRN_SUB
        bl_i = bl[r0:r0 + HGRN_SUB]
        q_i = q[r0:r0 + HGRN_SUB]
        k_i = k[r0:r0 + HGRN_SUB]
        v_i = v[r0:r0 + HGRN_SUB]
        b_end = bl_i[HGRN_SUB - 1:HGRN_SUB]
        ke_i = (k_i * jnp.exp2(b_end - bl_i)).astype(BF16)
        dec_i = jnp.exp2(b_end)
        v_bf = v_i.astype(BF16)

        diag = [[jnp.zeros((V7X_SUBLANES, HGRN_DK), F32) for _ in range(tiles)]
                for _ in range(HGRN_HEADS)]
        for s in range(HGRN_SUB):
            ks, bs, vs = k_i[s:s + 1], bl_i[s:s + 1], v_i[s:s + 1]
            for rt in range(s // V7X_SUBLANES, tiles):
                rsl = slice(rt * V7X_SUBLANES, (rt + 1) * V7X_SUBLANES)
                rel = bl_i[rsl] - bs
                if rt == s // V7X_SUBLANES:
                    rel = jnp.where(row8 >= s % V7X_SUBLANES, rel, -jnp.inf)
                w = q_i[rsl] * ks * jnp.exp2(rel)
                for h in range(HGRN_HEADS):
                    hs = slice(h * HGRN_DK, (h + 1) * HGRN_DK)
                    col = jnp.sum(w[:, hs], axis=1, keepdims=True)
                    diag[h][rt] = diag[h][rt] + col * vs[:, hs]

        for h in range(HGRN_HEADS):
            hs = slice(h * HGRN_DK, (h + 1) * HGRN_DK)
            st = st_ref[h]
            inter = lax.dot_general(qe[r0:r0 + HGRN_SUB, hs], st.astype(BF16), NT_DIMS,
                                    preferred_element_type=F32)
            upd = lax.dot_general(v_bf[:, hs], ke_i[:, hs], TN_DIMS,
                                  preferred_element_type=F32)
            st_ref[h] = st * dec_i[:, hs] + upd
            o_ref[r0:r0 + HGRN_SUB, hs] = inter + jnp.concatenate(diag[h], axis=0)

    gn = gn_ref[...]
    for h in range(HGRN_HEADS):
        hs = slice(h * HGRN_DK, (h + 1) * HGRN_DK)
        ga = ga_ref[:, hs].astype(F32)
        oa_ref[:, hs] = (_rms(o_ref[:, hs], gn) * (ga * _sigmoid(ga))).astype(oa_ref.dtype)

    @pl.when(c == pl.num_programs(1) - 1)
    def _():
        for h in range(HGRN_HEADS):
            sout_ref[h] = st_ref[h].T


def _hgrn(proj, lb, gnorm, s0, batch, seq):
    width = HGRN_HEADS * HGRN_DK
    rows = min(CHUNK, seq)
    nc = seq // rows
    assert seq % rows == 0 and rows % HGRN_SUB == 0
    col = lambda j: pl.BlockSpec((rows, width), lambda b, c: (b * nc + c, j))
    state_spec = pl.BlockSpec((None, HGRN_HEADS, HGRN_DK, HGRN_DK), lambda b, c: (b, 0, 0, 0))
    blocks = 4 * rows * width * 4 + rows * width * 2 + 2 * HGRN_HEADS * HGRN_DK * HGRN_DK * 4
    resident = HGRN_HEADS * HGRN_DK * HGRN_DK * 4 + 16 * rows * width * 4
    return pl.pallas_call(
        _hgrn_body,
        grid=(batch, nc),
        in_specs=[col(0), col(1), col(2), col(3),
                  pl.BlockSpec((1, width), lambda b, c: (0, 0)),
                  pl.BlockSpec((1, HGRN_DK), lambda b, c: (0, 0)),
                  state_spec],
        out_specs=[pl.BlockSpec((rows, width), lambda b, c: (b * nc + c, 0)), state_spec],
        out_shape=[jax.ShapeDtypeStruct((batch * seq, width), BF16),
                   jax.ShapeDtypeStruct(s0.shape, F32)],
        scratch_shapes=[pltpu.VMEM((HGRN_HEADS, HGRN_DK, HGRN_DK), F32),
                        pltpu.VMEM((rows, width), F32)],
        compiler_params=_params(("parallel", "arbitrary"), blocks, resident),
        name="hgrn2",
    )(proj, proj, proj, proj, lb, gnorm, s0)


CONV_PAD = 8


def _even_out_body(oa_ref, bg_ref, cg_ref, hb_ref, cw_ref, cs_ref, w_ref, g_ref, x_ref,
                   gpre_ref, wq_ref, mk_ref, mv_ref, wo_ref, gpost_ref,
                   o_ref, cnew_ref, ubuf_ref):
    t = pl.program_id(1)
    tm = oa_ref.shape[0]
    half = oa_ref.shape[1]
    lo = CONV_PAD - (CONV_W - 1)

    @pl.when(t == 0)
    def _():
        ubuf_ref[lo:CONV_PAD, :] = cs_ref[...]

    u = cg_ref[...].astype(F32) * hb_ref[...].astype(F32)
    ubuf_ref[CONV_PAD:CONV_PAD + tm, :] = u
    yb = u * cw_ref[CONV_W - 1:CONV_W, :]
    for j in range(CONV_W - 1):
        yb = yb + ubuf_ref[lo + j:lo + j + tm, :] * cw_ref[j:j + 1, :]
    ob = (bg_ref[...].astype(F32) * yb).astype(BF16)
    y = _dot(oa_ref[...], w_ref[0:half, :]) + _dot(ob, w_ref[half:2 * half, :])
    x1 = x_ref[...] + _rms(y, g_ref[...])
    o_ref[...] = _mem_sublayer(x1, gpre_ref, wq_ref, mk_ref, mv_ref, wo_ref, gpost_ref)

    tail = ubuf_ref[lo + tm:CONV_PAD + tm, :]
    ubuf_ref[lo:CONV_PAD, :] = tail

    @pl.when(t == pl.num_programs(1) - 1)
    def _():
        cnew_ref[...] = tail


def _even_out(oa, proj, conv_wt, conv_state, w_out, g, x, mem, batch, seq, tm_want=512):
    g_pre, w_q, mem_k, mem_v, w_o, g_post = mem
    half = oa.shape[1]
    d = x.shape[1]
    tm = _row_tile(seq, tm_want)
    nt = seq // tm
    row = lambda b, t: (b * nt + t, 0)
    pcol = lambda j: pl.BlockSpec((tm, half), lambda b, t: (b * nt + t, j))
    cstate = pl.BlockSpec((None, CONV_W - 1, half), lambda b, t: (b, 0, 0))
    mem_specs, mem_bytes = _mem_specs(mem_k, w_q, w_o, g_pre, g_post)
    blocks = tm * half * 4 * proj.dtype.itemsize + 2 * tm * d * 4
    resident = (2 * half * d * 2 + mem_bytes + (tm + CONV_PAD) * half * 4 + 4 * tm * half * 4
                + 4 * tm * d * 4)
    return pl.pallas_call(
        _even_out_body,
        grid=(batch, nt),
        in_specs=[pl.BlockSpec((tm, half), row), pcol(4), pcol(5), pcol(6),
                  _const_spec(conv_wt), cstate, _const_spec(w_out), _const_spec(g),
                  pl.BlockSpec((tm, d), row)] + mem_specs,
        out_specs=[pl.BlockSpec((tm, d), row), cstate],
        out_shape=[jax.ShapeDtypeStruct(x.shape, F32),
                   jax.ShapeDtypeStruct(conv_state.shape, F32)],
        scratch_shapes=[pltpu.VMEM((tm + CONV_PAD, half), F32)],
        compiler_params=_params(("parallel", "arbitrary"), blocks, resident),
        name="even_out_mem",
    )(oa, proj, proj, proj, conv_wt, conv_state, w_out, g, x,
      g_pre, w_q, mem_k, mem_v, w_o, g_post)


Q_PRESCALE = MLA_SCALE * LOG2E


def _rope(x, cos, sin_lo, sin_hi):
    half = QK_ROPE // 2
    return (x * cos
            + pltpu.roll(x, V7X_LANES - half, axis=1) * sin_lo
            + pltpu.roll(x, half, axis=1) * sin_hi)


def _mla_proj_body(x_ref, g_ref, wdq_ref, qn_ref, wuq_ref, wdkv_ref, kvn_ref,
                   cos_ref, slo_ref, shi_ref, q_ref, ckv_ref, kpe_ref, kpad_ref):
    cos, slo, shi = cos_ref[...], slo_ref[...], shi_ref[...]
    lora = ckv_ref.shape[1]
    xn = _rms(x_ref[...], g_ref[...]).astype(BF16)
    cq = _rms(_dot(xn, wdq_ref[...]), qn_ref[...]).astype(BF16)
    ckr = _dot(xn, wdkv_ref[...])
    ckv_ref[...] = _rms(ckr[:, :lora], kvn_ref[...])
    kp = _rope(ckr[:, lora:lora + V7X_LANES], cos, slo, shi)
    kpe_ref[...] = kp[:, :QK_ROPE]
    kpad_ref[...] = kp.astype(BF16)
    q = _dot(cq, wuq_ref[...]) * Q_PRESCALE
    for h in range(MLA_HEADS):
        c0 = h * MLA_QK_PAD
        q_ref[h, :, 0:QK_NOPE] = q[:, c0:c0 + QK_NOPE].astype(BF16)
        q_ref[h, :, QK_NOPE:MLA_QK_PAD] = _rope(
            q[:, c0 + QK_NOPE:c0 + MLA_QK_PAD], cos, slo, shi).astype(BF16)


def _mla_proj(x, g, w_dq, q_norm, w_uq_pad, w_dkv_pad, kv_norm, cos, slo, shi, tm_want=512):
    rows, d = x.shape
    qlora = w_dq.shape[1]
    lora = kv_norm.shape[1]
    qw = w_uq_pad.shape[1]
    kw = w_dkv_pad.shape[1]
    tm = _row_tile(rows, tm_want)
    full = lambda a: pl.BlockSpec(a.shape, lambda i: (0,) * a.ndim)
    rowspec = lambda n: pl.BlockSpec((tm, n), lambda i: (i, 0))
    blocks = (tm * d * 4 + (d * qlora + qlora * qw + d * kw) * 2 + 3 * tm * V7X_LANES * 4
              + tm * (qw * 2 + lora * 4 + QK_ROPE * 4 + V7X_LANES * 2))
    resident = tm * (d * 6 + qw * 4 + kw * 4 + qlora * 8)
    return pl.pallas_call(
        _mla_proj_body,
        grid=(rows // tm,),
        in_specs=[rowspec(d), full(g), full(w_dq), full(q_norm), full(w_uq_pad),
                  full(w_dkv_pad), full(kv_norm),
                  rowspec(V7X_LANES), rowspec(V7X_LANES), rowspec(V7X_LANES)],
        out_specs=[pl.BlockSpec((MLA_HEADS, tm, MLA_QK_PAD), lambda i: (0, i, 0)),
                   rowspec(lora), rowspec(QK_ROPE), rowspec(V7X_LANES)],
        out_shape=[jax.ShapeDtypeStruct((MLA_HEADS, rows, MLA_QK_PAD), BF16),
                   jax.ShapeDtypeStruct((rows, lora), F32),
                   jax.ShapeDtypeStruct((rows, QK_ROPE), F32),
                   jax.ShapeDtypeStruct((rows, V7X_LANES), BF16)],
        compiler_params=_params(("parallel",), blocks, resident),
        name="mla_proj",
    )(x, g, w_dq, q_norm, w_uq_pad, w_dkv_pad, kv_norm, cos, slo, shi)


def _kv_up_body(ckv_ref, kpad_ref, wuk_ref, wuvt_ref, k_ref, vt_ref):
    c = ckv_ref[...].astype(BF16)
    kn = _dot(c, wuk_ref[...])
    vt = lax.dot_general(wuvt_ref[...], c, NT_DIMS, preferred_element_type=F32)
    ts = c.shape[0]
    vt_ref[:, 0:V_DIM, :] = vt.astype(BF16).reshape(MLA_HEADS, V_DIM, ts)
    ones_row = lax.broadcasted_iota(jnp.int32, (MLA_HEADS, V_EXT - V_DIM, ts), 1) == 0
    vt_ref[:, V_DIM:V_EXT, :] = ones_row.astype(BF16)
    kpad = kpad_ref[...]
    for h in range(MLA_HEADS):
        k_ref[h, :, 0:QK_NOPE] = kn[:, h * QK_NOPE:(h + 1) * QK_NOPE].astype(BF16)
        k_ref[h, :, QK_NOPE:MLA_QK_PAD] = kpad


def _kv_up(ckv, kpad, w_uk, w_uvt, ts):
    rows, lora = ckv.shape
    assert rows % ts == 0
    rowspec = lambda n: pl.BlockSpec((ts, n), lambda i: (i, 0))
    full = lambda a: pl.BlockSpec(a.shape, lambda i: (0,) * a.ndim)
    kw = MLA_HEADS * MLA_QK_PAD
    vw = MLA_HEADS * V_DIM
    blocks = ts * (lora * 4 + V7X_LANES * 2 + kw * 2 + vw * 2) + 2 * lora * vw * 2
    resident = ts * (kw + vw) * 4
    return pl.pallas_call(
        _kv_up_body,
        grid=(rows // ts,),
        in_specs=[rowspec(lora), rowspec(V7X_LANES), full(w_uk), full(w_uvt)],
        out_specs=[pl.BlockSpec((MLA_HEADS, ts, MLA_QK_PAD), lambda i: (0, i, 0)),
                   pl.BlockSpec((None, MLA_HEADS, V_EXT, ts), lambda i: (i, 0, 0, 0))],
        out_shape=[jax.ShapeDtypeStruct((MLA_HEADS, rows, MLA_QK_PAD), BF16),
                   jax.ShapeDtypeStruct((rows // ts, MLA_HEADS, V_EXT, ts), BF16)],
        compiler_params=_params(("parallel",), blocks, resident),
        name="kv_up",
    )(ckv, kpad, w_uk, w_uvt)


ATTN_TILE = 512


def _scores_t(h, q_ref, k_ref):
    return lax.dot_general(k_ref[h], q_ref[h], NT_DIMS, preferred_element_type=F32)


def _softmax_pv_t(h, st, vt_ref, m_ref, acc_ref, mask):
    if mask is not None:
        st = jnp.where(mask, st, -jnp.inf)
    m_prev = m_ref[h]
    m_next = jnp.maximum(m_prev, jnp.max(st, axis=0, keepdims=True))
    alpha = jnp.exp2(m_prev - m_next)
    p = jnp.exp2(st - m_next)
    m_ref[h] = m_next
    acc_ref[h] = alpha * acc_ref[h] + _dot(vt_ref[h], p.astype(BF16))


def _attn_block_t(q_ref, k_ref, vt_ref, m_ref, acc_ref, s_ref, mask):
    s_ref[0] = _scores_t(0, q_ref, k_ref)
    for h in range(MLA_HEADS):
        if h + 1 < MLA_HEADS:
            s_ref[(h + 1) % 2] = _scores_t(h + 1, q_ref, k_ref)
        _softmax_pv_t(h, s_ref[h % 2], vt_ref, m_ref, acc_ref, mask)


def _attn_causal_body(qi_ref, kj_ref, q_ref, k_ref, vt_ref, o_ref, m_ref, acc_ref, s_ref):
    step = pl.program_id(0)
    qi = qi_ref[step]
    kj = kj_ref[step]
    tq, tk = q_ref.shape[1], k_ref.shape[1]

    @pl.when(kj == 0)
    def _():
        m_ref[...] = jnp.full(m_ref.shape, NEG_BIG, F32)
        acc_ref[...] = jnp.zeros_like(acc_ref)

    @pl.when(kj < qi)
    def _():
        _attn_block_t(q_ref, k_ref, vt_ref, m_ref, acc_ref, s_ref, None)

    @pl.when(kj == qi)
    def _():
        key_chunk = lax.broadcasted_iota(jnp.int32, (tk, tq), 0) // CHUNK
        query_chunk = lax.broadcasted_iota(jnp.int32, (tk, tq), 1) // CHUNK
        _attn_block_t(q_ref, k_ref, vt_ref, m_ref, acc_ref, s_ref, key_chunk <= query_chunk)
        for h in range(MLA_HEADS):
            out = acc_ref[h, 0:V_DIM, :] / acc_ref[h, V_DIM:V_DIM + 1, :]
            o_ref[:, h * V_DIM:(h + 1) * V_DIM] = out.T.astype(o_ref.dtype)


def _attn_causal(q, k, vt, tile):
    rows = q.shape[1]
    assert tile % CHUNK == 0 and rows % tile == 0 and vt.shape[3] == tile
    nq = rows // tile
    pairs = [(i, j) for i in range(nq) for j in range(i + 1)]
    qi = jnp.asarray(np.array([a for a, _ in pairs], np.int32))
    kj = jnp.asarray(np.array([b for _, b in pairs], np.int32))
    vw = MLA_HEADS * V_DIM
    blocks = MLA_HEADS * tile * (2 * MLA_QK_PAD + V_EXT) * 2 + tile * vw * 2
    resident = (MLA_HEADS * (V_EXT + 8) * tile + 8 * tile * tile) * 4
    grid_spec = pltpu.PrefetchScalarGridSpec(
        num_scalar_prefetch=2,
        grid=(len(pairs),),
        in_specs=[pl.BlockSpec((MLA_HEADS, tile, MLA_QK_PAD), lambda p, qi, kj: (0, qi[p], 0)),
                  pl.BlockSpec((MLA_HEADS, tile, MLA_QK_PAD), lambda p, qi, kj: (0, kj[p], 0)),
                  pl.BlockSpec((None, MLA_HEADS, V_EXT, tile), lambda p, qi, kj: (kj[p], 0, 0, 0))],
        out_specs=pl.BlockSpec((tile, vw), lambda p, qi, kj: (qi[p], 0)),
        scratch_shapes=[pltpu.VMEM((MLA_HEADS, 1, tile), F32),
                        pltpu.VMEM((MLA_HEADS, V_EXT, tile), F32),
                        pltpu.VMEM((2, tile, tile), F32)])
    return pl.pallas_call(
        _attn_causal_body,
        grid_spec=grid_spec,
        out_shape=jax.ShapeDtypeStruct((rows, vw), BF16),
        compiler_params=_params(("arbitrary",), blocks, resident),
        name="mla_attn_causal",
    )(qi, kj, q, k, vt)


def _attn_cached_body(q_ref, k_ref, vt_ref, o_ref, s_ref, p_ref, *, q_pos0, n_valid):
    tq, tk = q_ref.shape[1], k_ref.shape[1]
    query_chunk = (lax.broadcasted_iota(jnp.int32, (tq, tk), 0) + q_pos0) // CHUNK
    kpos = lax.broadcasted_iota(jnp.int32, (tq, tk), 1)
    mask = (kpos // CHUNK <= query_chunk) & (kpos < n_valid)
    slab = vt_ref.shape[3]
    for h in range(MLA_HEADS):
        s_ref[h] = lax.dot_general(q_ref[h], k_ref[h], NT_DIMS, preferred_element_type=F32)
    s = jnp.where(mask[None], s_ref[...], -jnp.inf)
    p = jnp.exp2(s - jnp.max(s, axis=2, keepdims=True))
    inv_l = 1.0 / jnp.sum(p, axis=2, keepdims=True)
    p_ref[...] = p.astype(BF16)
    for h in range(MLA_HEADS):
        o = jnp.zeros((tq, V_DIM), F32)
        for j in range(vt_ref.shape[0]):
            o = o + lax.dot_general(p_ref[h, :, j * slab:(j + 1) * slab],
                                    vt_ref[j, h, 0:V_DIM, :], NT_DIMS,
                                    preferred_element_type=F32)
        o_ref[:, h * V_DIM:(h + 1) * V_DIM] = (o * inv_l[h]).astype(o_ref.dtype)


def _attn_cached(q, k, vt, batch, q_pos0, n_valid):
    tq = q.shape[1] // batch
    tk = k.shape[1] // batch
    slabs = vt.shape[0] // batch
    slab = vt.shape[3]
    assert slabs * slab == tk
    vw = MLA_HEADS * V_DIM
    blocks = MLA_HEADS * ((tq + tk) * MLA_QK_PAD + V_EXT * tk) * 2 + tq * vw * 2
    resident = 4 * MLA_HEADS * tq * tk * 4
    return pl.pallas_call(
        functools.partial(_attn_cached_body, q_pos0=q_pos0, n_valid=n_valid),
        grid=(batch,),
        in_specs=[pl.BlockSpec((MLA_HEADS, tq, MLA_QK_PAD), lambda b: (0, b, 0)),
                  pl.BlockSpec((MLA_HEADS, tk, MLA_QK_PAD), lambda b: (0, b, 0)),
                  pl.BlockSpec((slabs, MLA_HEADS, V_EXT, slab), lambda b: (b, 0, 0, 0))],
        out_specs=pl.BlockSpec((tq, vw), lambda b: (b, 0)),
        out_shape=jax.ShapeDtypeStruct((q.shape[1], vw), BF16),
        scratch_shapes=[pltpu.VMEM((MLA_HEADS, tq, tk), F32), pltpu.VMEM((MLA_HEADS, tq, tk), BF16)],
        compiler_params=_params(("parallel",), blocks, resident),
        name="mla_attn_cached",
    )(q, k, vt)


def _rope_tables(pos):
    half = QK_ROPE // 2
    inv = ROPE_THETA ** (-jnp.arange(half, dtype=F32) / half)
    ang = pos.astype(F32)[:, None] * inv
    cos, sin = jnp.cos(ang), jnp.sin(ang)
    z = lambda n: jnp.zeros((pos.shape[0], n), F32)
    pad = V7X_LANES - QK_ROPE
    return (jnp.concatenate([cos, cos, z(pad)], axis=1),
            jnp.concatenate([-sin, z(half + pad)], axis=1),
            jnp.concatenate([z(half), sin, z(pad)], axis=1))


def _prep_mla(w_uq, w_dkv, w_ukv):
    qlora = w_uq.shape[0]
    lora = w_ukv.shape[0]
    wq = w_uq.reshape(qlora, MLA_HEADS, QK_NOPE + QK_ROPE)
    wq = jnp.pad(wq, ((0, 0), (0, 0), (0, MLA_QK_PAD - QK_NOPE - QK_ROPE)))
    w_uq_pad = wq.reshape(qlora, MLA_HEADS * MLA_QK_PAD).astype(BF16)
    w_dkv_pad = jnp.pad(w_dkv, ((0, 0), (0, V7X_LANES - QK_ROPE))).astype(BF16)
    wkv = w_ukv.reshape(lora, MLA_HEADS, QK_NOPE + V_DIM)
    w_uk = wkv[:, :, :QK_NOPE].reshape(lora, MLA_HEADS * QK_NOPE).astype(BF16)
    w_uvt = wkv[:, :, QK_NOPE:].reshape(lora, MLA_HEADS * V_DIM).T.astype(BF16)
    return w_uq_pad, w_dkv_pad, w_uk, w_uvt


def _trunk(x, batch, seq, past_len, mem_k, mem_v, s_hgrn, s_conv, ckv_past, kpe_past, p):
    depth = p["norm_g"].shape[0]
    rows = batch * seq
    pos = past_len + jnp.arange(seq, dtype=jnp.int32)
    cos, slo, shi = (jnp.tile(t, (batch, 1)) for t in _rope_tables(pos))
    hs, cs, ckvs, kpes = [], [], [], []
    for l in range(depth):
        g = lambda i: p["norm_g"][l, i][None, :]
        x = _ffn(x, g(0), p["ffn_w1"], p["ffn_w2"], g(1), l, 0)
        mem = (g(4), p["w_mem_q"][l], mem_k[l], mem_v[l], p["w_mem_o"][l], g(5))
        if l % 2 == 0:
            e = l // 2
            proj = _norm_matmul(x, g(2), p["w_in0"][e], BF16, tm_want=1024)
            oa, s_new = _hgrn(proj, p["lbs"][l][None, :], p["hgrn_gnorm"][e][None, :],
                              s_hgrn[e], batch, seq)
            x, c_new = _even_out(oa, proj, p["conv_wt"][e], s_conv[e], p["w_out0"][e],
                                 g(3), x, mem, batch, seq)
            hs.append(s_new)
            cs.append(c_new)
        else:
            o = l // 2
            q, ckv_new, kpe_new, kpad_new = _mla_proj(
                x, g(2), p["mla_w_dq"][o], p["mla_q_norm"][o][None, :], p["w_uq_pad"][o],
                p["w_dkv_pad"][o], p["mla_kv_norm"][o][None, :], cos, slo, shi)
            if past_len == 0:
                assert batch == 1
                tile = _row_tile(seq, ATTN_TILE)
                k, vt = _kv_up(ckv_new, kpad_new, p["w_uk"][o], p["w_uvt"][o], tile)
                att = _attn_causal(q, k, vt, tile)
            else:
                lora = ckv_new.shape[1]
                n_valid = past_len + seq
                n_keys = -(-n_valid // V7X_LANES) * V7X_LANES
                fill = n_keys - n_valid
                ckv_all = jnp.concatenate(
                    [ckv_past[o], ckv_new.reshape(batch, seq, lora),
                     jnp.zeros((batch, fill, lora), F32)], axis=1)
                kpad_past = jnp.pad(kpe_past[o], ((0, 0), (0, 0), (0, V7X_LANES - QK_ROPE)))
                kpad_all = jnp.concatenate(
                    [kpad_past.astype(BF16), kpad_new.reshape(batch, seq, V7X_LANES),
                     jnp.zeros((batch, fill, V7X_LANES), BF16)], axis=1)
                k, vt = _kv_up(ckv_all.reshape(batch * n_keys, lora),
                               kpad_all.reshape(batch * n_keys, V7X_LANES),
                               p["w_uk"][o], p["w_uvt"][o], _row_tile(n_keys, ATTN_TILE))
                att = _attn_cached(q, k, vt, batch, past_len, n_valid)
            x = _proj_residual(att, p["mla_w_o"][o], g(3), x, mem, batch, seq)
            ckvs.append(ckv_new.reshape(batch, seq, -1))
            kpes.append(kpe_new.reshape(batch, seq, -1))
        x = _ffn(x, g(6), p["ffn_w1"], p["ffn_w2"], g(7), l, 1)
    return (x.reshape(batch, seq, -1), jnp.stack(hs), jnp.stack(cs), jnp.stack(ckvs),
            jnp.stack(kpes))


def kernel(x_prompt, x_sample, mem_prompt, state_hgrn, state_conv, cache_ckv, cache_kpe, cache_mem_k, cache_mem_v, norm_g, ffn_w1, ffn_w2, w_in0, hgrn_lb, hgrn_gnorm, conv_w, w_out0, mla_w_dq, mla_q_norm, mla_w_uq, mla_w_dkv, mla_kv_norm, mla_w_ukv, mla_w_o, mem_norm, w_mem_q, w_mem_kv, w_mem_o):
    batch, seq, d = x_prompt.shape
    dec_batch, dec_seq, _ = x_sample.shape
    past_len = cache_ckv.shape[2]
    depth = norm_g.shape[0]
    n_mem = mem_prompt.shape[1]
    mem_width = MEM_HEADS * MEM_DIM

    prepped = [_prep_mla(mla_w_uq[o], mla_w_dkv[o], mla_w_ukv[o]) for o in range(mla_w_uq.shape[0])]
    p = dict(
        norm_g=norm_g,
        ffn_w1=ffn_w1.astype(BF16), ffn_w2=ffn_w2.astype(BF16),
        w_in0=w_in0.astype(BF16), w_out0=w_out0.astype(BF16),
        lbs=jnp.cumsum(jax.nn.softmax(hgrn_lb.astype(F32), axis=0), axis=0),
        hgrn_gnorm=hgrn_gnorm,
        conv_wt=jnp.swapaxes(conv_w, 1, 2),
        mla_w_dq=mla_w_dq.astype(BF16), mla_q_norm=mla_q_norm, mla_kv_norm=mla_kv_norm,
        w_uq_pad=[t[0] for t in prepped], w_dkv_pad=[t[1] for t in prepped],
        w_uk=[t[2] for t in prepped], w_uvt=[t[3] for t in prepped],
        mla_w_o=mla_w_o.astype(BF16),
        w_mem_q=w_mem_q.astype(BF16), w_mem_o=w_mem_o.astype(BF16),
    )

    mem_rows = mem_prompt.reshape(batch * n_mem, d)
    mks, mvs = [], []
    for l in range(depth):
        kv = _norm_matmul(mem_rows, mem_norm[l][None, :], w_mem_kv[l].astype(BF16), F32,
                          tn_want=2 * mem_width)
        kv = kv.reshape(batch, n_mem, 2, mem_width)
        mks.append(kv[:, :, 0])
        mvs.append(kv[:, :, 1])
    mem_k_p = jnp.stack(mks)
    mem_v_p = jnp.stack(mvs)

    n_even = state_hgrn.shape[0]
    n_odd = cache_ckv.shape[0]
    h0 = jnp.zeros((n_even, batch) + state_hgrn.shape[2:], F32)
    c0 = jnp.zeros((n_even, batch) + state_conv.shape[2:], F32)
    y_p, p_hgrn, p_conv, p_ckv, p_kpe = _trunk(
        x_prompt.reshape(batch * seq, d), batch, seq, 0, mem_k_p, mem_v_p, h0, c0, None, None, p)
    y_s, s_hgrn, s_conv, s_ckv, s_kpe = _trunk(
        x_sample.reshape(dec_batch * dec_seq, d), dec_batch, dec_seq, past_len,
        cache_mem_k.reshape(depth, dec_batch, n_mem, mem_width),
        cache_mem_v.reshape(depth, dec_batch, n_mem, mem_width),
        state_hgrn, state_conv, cache_ckv, cache_kpe, p)

    mem_shape = (depth, batch, n_mem, MEM_HEADS, MEM_DIM)
    return (y_p, y_s, p_hgrn, p_conv, p_ckv, p_kpe,
            mem_k_p.reshape(mem_shape), mem_v_p.reshape(mem_shape),
            s_hgrn, s_conv, s_ckv, s_kpe)
```

```python
import functools

import jax
import jax.numpy as jnp
import numpy as np
from jax import lax
from jax.experimental import pallas as pl
from jax.experimental.pallas import tpu as pltpu

F32 = jnp.float32
BF16 = jnp.bfloat16

V7X_LANES = 128
V7X_SUBLANES = 8
V7X_VMEM_BYTES = 64 * 1024 * 1024
MIB = 1024 * 1024
SPILL_ROOM_BYTES = 8 * MIB
VMEM_RESERVED_BYTES = 2 * MIB

EPS = 1e-6
CHUNK = 64
HGRN_HEADS = 8
HGRN_DK = 128
HGRN_SUB = 16
MLA_HEADS = 16
QK_NOPE = 128
QK_ROPE = 64
V_DIM = 128
V_EXT = V_DIM + 16
MLA_QK_PAD = 256
MLA_SCALE = (QK_NOPE + QK_ROPE) ** -0.5
ROPE_THETA = 10000.0
MEM_HEADS = 4
MEM_DIM = 128
CONV_W = 3
NEG_BIG = -1e30
LOG2E = 1.4426950408889634

NT_DIMS = (((1,), (1,)), ((), ()))
TN_DIMS = (((0,), (0,)), ((), ()))


def _vmem_limit(pipelined_bytes, resident_bytes):
    want = 2 * pipelined_bytes + resident_bytes + SPILL_ROOM_BYTES
    return int(min(want, V7X_VMEM_BYTES - VMEM_RESERVED_BYTES))


def _params(semantics, pipelined_bytes, resident_bytes):
    return pltpu.CompilerParams(
        dimension_semantics=semantics,
        vmem_limit_bytes=_vmem_limit(pipelined_bytes, resident_bytes))


def _row_tile(rows, want):
    if rows <= want:
        return rows
    t = want - want % V7X_LANES
    while t > 0 and rows % t:
        t -= V7X_LANES
    assert t > 0, (rows, want)
    return t


def _rms(x, g):
    ms = jnp.mean(x * x, axis=-1, keepdims=True)
    return x * lax.rsqrt(ms + EPS) * g


def _sigmoid(x):
    return 1.0 / (1.0 + jnp.exp(-x))


def _dot(a, b):
    return jnp.dot(a, b, preferred_element_type=F32)


FFN_ROW_CHUNK = 256
FFN_COL_CHUNK = 512


def _ffn_hidden(xn, w1g_ref, w1u_ref):
    gate = _dot(xn, w1g_ref[...])
    up = _dot(xn, w1u_ref[...])
    return (gate * _sigmoid(gate) * up).astype(BF16)


def _ffn_body(x_ref, gpre_ref, w1g_ref, w1u_ref, w2_ref, gpost_ref, o_ref, xn_ref):
    k = pl.program_id(1)
    last = pl.num_programs(1) - 1
    tm, d = o_ref.shape

    @pl.when(k == 0)
    def _():
        gpre = gpre_ref[...]
        for r0 in range(0, tm, FFN_ROW_CHUNK):
            rs = slice(r0, min(r0 + FFN_ROW_CHUNK, tm))
            xn = _rms(x_ref[rs, :], gpre).astype(BF16)
            xn_ref[rs, :] = xn
            o_ref[rs, :] = _dot(_ffn_hidden(xn, w1g_ref, w1u_ref), w2_ref[...])

    @pl.when((k > 0) & (k < last))
    def _():
        o_ref[...] += _dot(_ffn_hidden(xn_ref[...], w1g_ref, w1u_ref), w2_ref[...])

    @pl.when(k == last)
    def _():
        h = _ffn_hidden(xn_ref[...], w1g_ref, w1u_ref)
        ss = jnp.zeros((tm, 1), F32)
        for c0 in range(0, d, FFN_COL_CHUNK):
            cs = slice(c0, c0 + FFN_COL_CHUNK)
            y = o_ref[:, cs] + _dot(h, w2_ref[:, cs])
            o_ref[:, cs] = y
            ss = ss + jnp.sum(y * y, axis=-1, keepdims=True)
        scale = 0.5 * lax.rsqrt(ss * (1.0 / d) + EPS)
        o_ref[...] = x_ref[...] + o_ref[...] * scale * gpost_ref[...]


def _ffn(x, g_pre, w1, w2, g_post, layer, half, tm_want=1024, tf=512):
    rows, d = x.shape
    dff = w2.shape[-2]
    tm = _row_tile(rows, tm_want)
    nk = dff // tf
    assert dff % tf == 0 and nk >= 2 and d % FFN_COL_CHUNK == 0
    blocks = 2 * tm * d * 4 + 3 * d * tf * 2
    resident = tm * d * (2 + 4) + 4 * tm * tf * 4
    return pl.pallas_call(
        _ffn_body,
        grid=(rows // tm, nk),
        in_specs=[
            pl.BlockSpec((tm, d), lambda i, k: (i, 0)),
            pl.BlockSpec((1, d), lambda i, k: (0, 0)),
            pl.BlockSpec((None, None, d, tf), lambda i, k: (layer, half, 0, k)),
            pl.BlockSpec((None, None, d, tf), lambda i, k: (layer, half, 0, nk + k)),
            pl.BlockSpec((None, None, tf, d), lambda i, k: (layer, half, k, 0)),
            pl.BlockSpec((1, d), lambda i, k: (0, 0)),
        ],
        out_specs=pl.BlockSpec((tm, d), lambda i, k: (i, 0)),
        out_shape=jax.ShapeDtypeStruct((rows, d), F32),
        scratch_shapes=[pltpu.VMEM((tm, d), BF16)],
        compiler_params=_params(("parallel", "arbitrary"), blocks, resident),
        name="ffn",
    )(x, g_pre, w1, w1, w2, g_post)


def _norm_matmul_body(x_ref, g_ref, w_ref, o_ref, xn_ref):
    @pl.when(pl.program_id(1) == 0)
    def _():
        xn_ref[...] = _rms(x_ref[...], g_ref[...]).astype(BF16)

    o_ref[...] = _dot(xn_ref[...], w_ref[...]).astype(o_ref.dtype)


def _norm_matmul(x, g, w, out_dtype, tm_want=512, tn_want=1024):
    rows, d = x.shape
    n = w.shape[1]
    tm = _row_tile(rows, tm_want)
    tn = _row_tile(n, tn_want)
    blocks = tm * d * 4 + d * tn * 2 + tm * tn * 4
    resident = tm * d * 2 + tm * tn * 4
    return pl.pallas_call(
        _norm_matmul_body,
        grid=(rows // tm, n // tn),
        in_specs=[
            pl.BlockSpec((tm, d), lambda i, j: (i, 0)),
            pl.BlockSpec((1, d), lambda i, j: (0, 0)),
            pl.BlockSpec((d, tn), lambda i, j: (0, j)),
        ],
        out_specs=pl.BlockSpec((tm, tn), lambda i, j: (i, j)),
        out_shape=jax.ShapeDtypeStruct((rows, n), out_dtype),
        scratch_shapes=[pltpu.VMEM((tm, d), BF16)],
        compiler_params=_params(("parallel", "arbitrary"), blocks, resident),
        name="norm_matmul",
    )(x, g, w)


def _mem_sublayer(x, gpre_ref, wq_ref, mk_ref, mv_ref, wo_ref, gpost_ref):
    q = _dot(_rms(x, gpre_ref[...]).astype(BF16), wq_ref[...])
    outs = []
    for h in range(MEM_HEADS):
        hs = slice(h * MEM_DIM, (h + 1) * MEM_DIM)
        kh = mk_ref[:, hs].astype(BF16)
        vh = mv_ref[:, hs].astype(BF16)
        s = lax.dot_general(q[:, hs].astype(BF16), kh, NT_DIMS,
                            preferred_element_type=F32) * (MEM_DIM ** -0.5)
        p = jnp.exp(s - jnp.max(s, axis=1, keepdims=True))
        p = p * (1.0 / jnp.sum(p, axis=1, keepdims=True))
        outs.append(_dot(p.astype(BF16), vh).astype(BF16))
    o = jnp.concatenate(outs, axis=1)
    return x + _rms(_dot(o, wo_ref[...]), gpost_ref[...])


def _const_spec(a):
    return pl.BlockSpec(a.shape, lambda *_: (0,) * a.ndim, pipeline_mode=pl.Buffered(1))


def _mem_specs(mem_k, w_q, w_o, g_pre, g_post):
    n_mem, width = mem_k.shape[1], mem_k.shape[2]
    mem = pl.BlockSpec((None, n_mem, width), lambda b, t: (b, 0, 0))
    specs = [_const_spec(g_pre), _const_spec(w_q), mem, mem, _const_spec(w_o),
             _const_spec(g_post)]
    nbytes = (w_q.size + w_o.size) * 2 + 4 * n_mem * width * 4
    return specs, nbytes


def _proj_residual_body(a_ref, w_ref, g_ref, x_ref,
                        gpre_ref, wq_ref, mk_ref, mv_ref, wo_ref, gpost_ref, o_ref):
    x1 = x_ref[...] + _rms(_dot(a_ref[...], w_ref[...]), g_ref[...])
    o_ref[...] = _mem_sublayer(x1, gpre_ref, wq_ref, mk_ref, mv_ref, wo_ref, gpost_ref)


def _proj_residual(a, w, g, x, mem, batch, seq, tm_want=512):
    g_pre, w_q, mem_k, mem_v, w_o, g_post = mem
    kdim = a.shape[1]
    d = w.shape[1]
    tm = _row_tile(seq, tm_want)
    nt = seq // tm
    row = lambda n: pl.BlockSpec((tm, n), lambda b, t: (b * nt + t, 0))
    mem_specs, mem_bytes = _mem_specs(mem_k, w_q, w_o, g_pre, g_post)
    blocks = tm * kdim * 2 + 2 * tm * d * 4
    resident = kdim * d * 2 + mem_bytes + 4 * tm * d * 4
    return pl.pallas_call(
        _proj_residual_body,
        grid=(batch, nt),
        in_specs=[row(kdim), _const_spec(w), _const_spec(g), row(d)] + mem_specs,
        out_specs=row(d),
        out_shape=jax.ShapeDtypeStruct(x.shape, F32),
        compiler_params=_params(("parallel", "parallel"), blocks, resident),
        name="proj_residual_mem",
    )(a, w, g, x, g_pre, w_q, mem_k, mem_v, w_o, g_post)


def _hgrn_body(qa_ref, fa_ref, ia_ref, ga_ref, lb_ref, gn_ref, s0_ref,
               oa_ref, sout_ref, st_ref, o_ref):
    c = pl.program_id(1)
    rows = qa_ref.shape[0]
    width = qa_ref.shape[1]
    nsub = rows // HGRN_SUB

    @pl.when(c == 0)
    def _():
        for h in range(HGRN_HEADS):
            st_ref[h] = s0_ref[h].T

    lb = lb_ref[...]
    qa = qa_ref[...].astype(F32)
    q = qa * _sigmoid(qa) * (HGRN_DK ** -0.5)
    f = lb + (1.0 - lb) * _sigmoid(fa_ref[...].astype(F32))
    k = 1.0 - f
    g = jnp.log(f) * LOG2E
    v = ia_ref[...].astype(F32)

    r_i = lax.broadcasted_iota(jnp.int32, (rows, rows), 0)
    c_i = lax.broadcasted_iota(jnp.int32, (rows, rows), 1)
    tri = ((c_i <= r_i) & ((c_i // HGRN_SUB) == (r_i // HGRN_SUB))).astype(BF16)
    g1 = g.astype(BF16)
    rem = g - g1.astype(F32)
    g2 = rem.astype(BF16)
    g3 = (rem - g2.astype(F32)).astype(BF16)
    bl = _dot(tri, g1) + _dot(tri, g2) + _dot(tri, g3)

    qe = (q * jnp.exp2(bl)).astype(BF16)
    row8 = lax.broadcasted_iota(jnp.int32, (V7X_SUBLANES, width), 0)
    tiles = HGRN_SUB // V7X_SUBLANES

    for i in range(nsub):
        r0 = i * HGRN_SUB
        bl_i = bl[r0:r0 + HGRN_SUB]
        q_i = q[r0:r0 + HGRN_SUB]
        k_i = k[r0:r0 + HGRN_SUB]
        v_i = v[r0:r0 + HGRN_SUB]
        b_end = bl_i[HGRN_SUB - 1:HGRN_SUB]
        ke_i = (k_i * jnp.exp2(b_end - bl_i)).astype(BF16)
        dec_i = jnp.exp2(b_end)
        v_bf = v_i.astype(BF16)

        diag = [[jnp.zeros((V7X_SUBLANES, HGRN_DK), F32) for _ in range(tiles)]
                for _ in range(HGRN_HEADS)]
        for s in range(HGRN_SUB):
            ks, bs, vs = k_i[s:s + 1], bl_i[s:s + 1], v_i[s:s + 1]
            for rt in range(s // V7X_SUBLANES, tiles):
                rsl = slice(rt * V7X_SUBLANES, (rt + 1) * V7X_SUBLANES)
                rel = bl_i[rsl] - bs
                if rt == s // V7X_SUBLANES:
                    rel = jnp.where(row8 >= s % V7X_SUBLANES, rel, -jnp.inf)
                w = q_i[rsl] * ks * jnp.exp2(rel)
                for h in range(HGRN_HEADS):
                    hs = slice(h * HGRN_DK, (h + 1) * HGRN_DK)
                    col = jnp.sum(w[:, hs], axis=1, keepdims=True)
                    diag[h][rt] = diag[h][rt] + col * vs[:, hs]

        for h in range(HGRN_HEADS):
            hs = slice(h * HGRN_DK, (h + 1) * HGRN_DK)
            st = st_ref[h]
            inter = lax.dot_general(qe[r0:r0 + HGRN_SUB, hs], st.astype(BF16), NT_DIMS,
                                    preferred_element_type=F32)
            upd = lax.dot_general(v_bf[:, hs], ke_i[:, hs], TN_DIMS,
                                  preferred_element_type=F32)
            st_ref[h] = st * dec_i[:, hs] + upd
            o_ref[r0:r0 + HGRN_SUB, hs] = inter + jnp.concatenate(diag[h], axis=0)

    gn = gn_ref[...]
    for h in range(HGRN_HEADS):
        hs = slice(h * HGRN_DK, (h + 1) * HGRN_DK)
        ga = ga_ref[:, hs].astype(F32)
        oa_ref[:, hs] = (_rms(o_ref[:, hs], gn) * (ga * _sigmoid(ga))).astype(oa_ref.dtype)

    @pl.when(c == pl.num_programs(1) - 1)
    def _():
        for h in range(HGRN_HEADS):
            sout_ref[h] = st_ref[h].T


def _hgrn(proj, lb, gnorm, s0, batch, seq):
    width = HGRN_HEADS * HGRN_DK
    rows = min(CHUNK, seq)
    nc = seq // rows
    assert seq % rows == 0 and rows % HGRN_SUB == 0
    col = lambda j: pl.BlockSpec((rows, width), lambda b, c: (b * nc + c, j))
    state_spec = pl.BlockSpec((None, HGRN_HEADS, HGRN_DK, HGRN_DK), lambda b, c: (b, 0, 0, 0))
    blocks = 4 * rows * width * 4 + rows * width * 2 + 2 * HGRN_HEADS * HGRN_DK * HGRN_DK * 4
    resident = HGRN_HEADS * HGRN_DK * HGRN_DK * 4 + 16 * rows * width * 4
    return pl.pallas_call(
        _hgrn_body,
        grid=(batch, nc),
        in_specs=[col(0), col(1), col(2), col(3),
                  pl.BlockSpec((1, width), lambda b, c: (0, 0)),
                  pl.BlockSpec((1, HGRN_DK), lambda b, c: (0, 0)),
                  state_spec],
        out_specs=[pl.BlockSpec((rows, width), lambda b, c: (b * nc + c, 0)), state_spec],
        out_shape=[jax.ShapeDtypeStruct((batch * seq, width), BF16),
                   jax.ShapeDtypeStruct(s0.shape, F32)],
        scratch_shapes=[pltpu.VMEM((HGRN_HEADS, HGRN_DK, HGRN_DK), F32),
                        pltpu.VMEM((rows, width), F32)],
        compiler_params=_params(("parallel", "arbitrary"), blocks, resident),
        name="hgrn2",
    )(proj, proj, proj, proj, lb, gnorm, s0)


CONV_PAD = 8


def _even_out_body(oa_ref, bg_ref, cg_ref, hb_ref, cw_ref, cs_ref, w_ref, g_ref, x_ref,
                   gpre_ref, wq_ref, mk_ref, mv_ref, wo_ref, gpost_ref,
                   o_ref, cnew_ref, ubuf_ref):
    t = pl.program_id(1)
    tm = oa_ref.shape[0]
    half = oa_ref.shape[1]
    lo = CONV_PAD - (CONV_W - 1)

    @pl.when(t == 0)
    def _():
        ubuf_ref[lo:CONV_PAD, :] = cs_ref[...]

    u = cg_ref[...].astype(F32) * hb_ref[...].astype(F32)
    ubuf_ref[CONV_PAD:CONV_PAD + tm, :] = u
    yb = u * cw_ref[CONV_W - 1:CONV_W, :]
    for j in range(CONV_W - 1):
        yb = yb + ubuf_ref[lo + j:lo + j + tm, :] * cw_ref[j:j + 1, :]
    ob = (bg_ref[...].astype(F32) * yb).astype(BF16)
    y = _dot(oa_ref[...], w_ref[0:half, :]) + _dot(ob, w_ref[half:2 * half, :])
    x1 = x_ref[...] + _rms(y, g_ref[...])
    o_ref[...] = _mem_sublayer(x1, gpre_ref, wq_ref, mk_ref, mv_ref, wo_ref, gpost_ref)

    tail = ubuf_ref[lo + tm:CONV_PAD + tm, :]
    ubuf_ref[lo:CONV_PAD, :] = tail

    @pl.when(t == pl.num_programs(1) - 1)
    def _():
        cnew_ref[...] = tail


def _even_out(oa, proj, conv_wt, conv_state, w_out, g, x, mem, batch, seq, tm_want=512):
    g_pre, w_q, mem_k, mem_v, w_o, g_post = mem
    half = oa.shape[1]
    d = x.shape[1]
    tm = _row_tile(seq, tm_want)
    nt = seq // tm
    row = lambda b, t: (b * nt + t, 0)
    pcol = lambda j: pl.BlockSpec((tm, half), lambda b, t: (b * nt + t, j))
    cstate = pl.BlockSpec((None, CONV_W - 1, half), lambda b, t: (b, 0, 0))
    mem_specs, mem_bytes = _mem_specs(mem_k, w_q, w_o, g_pre, g_post)
    blocks = tm * half * 4 * proj.dtype.itemsize + 2 * tm * d * 4
    resident = (2 * half * d * 2 + mem_bytes + (tm + CONV_PAD) * half * 4 + 4 * tm * half * 4
                + 4 * tm * d * 4)
    return pl.pallas_call(
        _even_out_body,
        grid=(batch, nt),
        in_specs=[pl.BlockSpec((tm, half), row), pcol(4), pcol(5), pcol(6),
                  _const_spec(conv_wt), cstate, _const_spec(w_out), _const_spec(g),
                  pl.BlockSpec((tm, d), row)] + mem_specs,
        out_specs=[pl.BlockSpec((tm, d), row), cstate],
        out_shape=[jax.ShapeDtypeStruct(x.shape, F32),
                   jax.ShapeDtypeStruct(conv_state.shape, F32)],
        scratch_shapes=[pltpu.VMEM((tm + CONV_PAD, half), F32)],
        compiler_params=_params(("parallel", "arbitrary"), blocks, resident),
        name="even_out_mem",
    )(oa, proj, proj, proj, conv_wt, conv_state, w_out, g, x,
      g_pre, w_q, mem_k, mem_v, w_o, g_post)


Q_PRESCALE = MLA_SCALE * LOG2E


def _rope(x, cos, sin_lo, sin_hi):
    half = QK_ROPE // 2
    return (x * cos
            + pltpu.roll(x, V7X_LANES - half, axis=1) * sin_lo
            + pltpu.roll(x, half, axis=1) * sin_hi)


def _mla_proj_body(x_ref, g_ref, wdq_ref, qn_ref, wuq_ref, wdkv_ref, kvn_ref,
                   cos_ref, slo_ref, shi_ref, q_ref, ckv_ref, kpe_ref, kpad_ref):
    cos, slo, shi = cos_ref[...], slo_ref[...], shi_ref[...]
    lora = ckv_ref.shape[1]
    xn = _rms(x_ref[...], g_ref[...]).astype(BF16)
    cq = _rms(_dot(xn, wdq_ref[...]), qn_ref[...]).astype(BF16)
    ckr = _dot(xn, wdkv_ref[...])
    ckv_ref[...] = _rms(ckr[:, :lora], kvn_ref[...])
    kp = _rope(ckr[:, lora:lora + V7X_LANES], cos, slo, shi)
    kpe_ref[...] = kp[:, :QK_ROPE]
    kpad_ref[...] = kp.astype(BF16)
    q = _dot(cq, wuq_ref[...]) * Q_PRESCALE
    for h in range(MLA_HEADS):
        c0 = h * MLA_QK_PAD
        q_ref[h, :, 0:QK_NOPE] = q[:, c0:c0 + QK_NOPE].astype(BF16)
        q_ref[h, :, QK_NOPE:MLA_QK_PAD] = _rope(
            q[:, c0 + QK_NOPE:c0 + MLA_QK_PAD], cos, slo, shi).astype(BF16)


def _mla_proj(x, g, w_dq, q_norm, w_uq_pad, w_dkv_pad, kv_norm, cos, slo, shi, tm_want=512):
    rows, d = x.shape
    qlora = w_dq.shape[1]
    lora = kv_norm.shape[1]
    qw = w_uq_pad.shape[1]
    kw = w_dkv_pad.shape[1]
    tm = _row_tile(rows, tm_want)
    full = lambda a: pl.BlockSpec(a.shape, lambda i: (0,) * a.ndim)
    rowspec = lambda n: pl.BlockSpec((tm, n), lambda i: (i, 0))
    blocks = (tm * d * 4 + (d * qlora + qlora * qw + d * kw) * 2 + 3 * tm * V7X_LANES * 4
              + tm * (qw * 2 + lora * 4 + QK_ROPE * 4 + V7X_LANES * 2))
    resident = tm * (d * 6 + qw * 4 + kw * 4 + qlora * 8)
    return pl.pallas_call(
        _mla_proj_body,
        grid=(rows // tm,),
        in_specs=[rowspec(d), full(g), full(w_dq), full(q_norm), full(w_uq_pad),
                  full(w_dkv_pad), full(kv_norm),
                  rowspec(V7X_LANES), rowspec(V7X_LANES), rowspec(V7X_LANES)],
        out_specs=[pl.BlockSpec((MLA_HEADS, tm, MLA_QK_PAD), lambda i: (0, i, 0)),
                   rowspec(lora), rowspec(QK_ROPE), rowspec(V7X_LANES)],
        out_shape=[jax.ShapeDtypeStruct((MLA_HEADS, rows, MLA_QK_PAD), BF16),
                   jax.ShapeDtypeStruct((rows, lora), F32),
                   jax.ShapeDtypeStruct((rows, QK_ROPE), F32),
                   jax.ShapeDtypeStruct((rows, V7X_LANES), BF16)],
        compiler_params=_params(("parallel",), blocks, resident),
        name="mla_proj",
    )(x, g, w_dq, q_norm, w_uq_pad, w_dkv_pad, kv_norm, cos, slo, shi)


def _kv_up_body(ckv_ref, kpad_ref, wuk_ref, wuvt_ref, k_ref, vt_ref):
    c = ckv_ref[...].astype(BF16)
    kn = _dot(c, wuk_ref[...])
    vt = lax.dot_general(wuvt_ref[...], c, NT_DIMS, preferred_element_type=F32)
    ts = c.shape[0]
    vt_ref[:, 0:V_DIM, :] = vt.astype(BF16).reshape(MLA_HEADS, V_DIM, ts)
    ones_row = lax.broadcasted_iota(jnp.int32, (MLA_HEADS, V_EXT - V_DIM, ts), 1) == 0
    vt_ref[:, V_DIM:V_EXT, :] = ones_row.astype(BF16)
    kpad = kpad_ref[...]
    for h in range(MLA_HEADS):
        k_ref[h, :, 0:QK_NOPE] = kn[:, h * QK_NOPE:(h + 1) * QK_NOPE].astype(BF16)
        k_ref[h, :, QK_NOPE:MLA_QK_PAD] = kpad


def _kv_up(ckv, kpad, w_uk, w_uvt, ts):
    rows, lora = ckv.shape
    assert rows % ts == 0
    rowspec = lambda n: pl.BlockSpec((ts, n), lambda i: (i, 0))
    full = lambda a: pl.BlockSpec(a.shape, lambda i: (0,) * a.ndim)
    kw = MLA_HEADS * MLA_QK_PAD
    vw = MLA_HEADS * V_DIM
    blocks = ts * (lora * 4 + V7X_LANES * 2 + kw * 2 + vw * 2) + 2 * lora * vw * 2
    resident = ts * (kw + vw) * 4
    return pl.pallas_call(
        _kv_up_body,
        grid=(rows // ts,),
        in_specs=[rowspec(lora), rowspec(V7X_LANES), full(w_uk), full(w_uvt)],
        out_specs=[pl.BlockSpec((MLA_HEADS, ts, MLA_QK_PAD), lambda i: (0, i, 0)),
                   pl.BlockSpec((None, MLA_HEADS, V_EXT, ts), lambda i: (i, 0, 0, 0))],
        out_shape=[jax.ShapeDtypeStruct((MLA_HEADS, rows, MLA_QK_PAD), BF16),
                   jax.ShapeDtypeStruct((rows // ts, MLA_HEADS, V_EXT, ts), BF16)],
        compiler_params=_params(("parallel",), blocks, resident),
        name="kv_up",
    )(ckv, kpad, w_uk, w_uvt)


ATTN_TILE = 512
ATTN_SCORE_BUFFERS = 3


def _scores_t(h, q_ref, k_ref):
    return lax.dot_general(k_ref[h], q_ref[h], NT_DIMS, preferred_element_type=F32)


def _softmax_pv_t(h, st, vt_ref, m_ref, acc_ref, mask):
    if mask is not None:
        st = jnp.where(mask, st, -jnp.inf)
    m_prev = m_ref[h]
    m_next = jnp.maximum(m_prev, jnp.max(st, axis=0, keepdims=True))
    alpha = jnp.exp2(m_prev - m_next)
    p = jnp.exp2(st - m_next)
    m_ref[h] = m_next
    acc_ref[h] = alpha * acc_ref[h] + _dot(vt_ref[h], p.astype(BF16))


def _attn_block_t(q_ref, k_ref, vt_ref, m_ref, acc_ref, s_ref, mask):
    nbuf = s_ref.shape[0]
    for h in range(nbuf - 1):
        s_ref[h] = _scores_t(h, q_ref, k_ref)
    for h in range(MLA_HEADS):
        ahead = h + nbuf - 1
        if ahead < MLA_HEADS:
            s_ref[ahead % nbuf] = _scores_t(ahead, q_ref, k_ref)
        _softmax_pv_t(h, s_ref[h % nbuf], vt_ref, m_ref, acc_ref, mask)


def _attn_causal_body(qi_ref, kj_ref, q_ref, k_ref, vt_ref, o_ref, m_ref, acc_ref, s_ref):
    step = pl.program_id(0)
    qi = qi_ref[step]
    kj = kj_ref[step]
    tq, tk = q_ref.shape[1], k_ref.shape[1]

    @pl.when(kj == 0)
    def _():
        m_ref[...] = jnp.full(m_ref.shape, NEG_BIG, F32)
        acc_ref[...] = jnp.zeros_like(acc_ref)

    @pl.when(kj < qi)
    def _():
        _attn_block_t(q_ref, k_ref, vt_ref, m_ref, acc_ref, s_ref, None)

    @pl.when(kj == qi)
    def _():
        key_chunk = lax.broadcasted_iota(jnp.int32, (tk, tq), 0) // CHUNK
        query_chunk = lax.broadcasted_iota(jnp.int32, (tk, tq), 1) // CHUNK
        _attn_block_t(q_ref, k_ref, vt_ref, m_ref, acc_ref, s_ref, key_chunk <= query_chunk)
        for h in range(MLA_HEADS):
            out = acc_ref[h, 0:V_DIM, :] / acc_ref[h, V_DIM:V_DIM + 1, :]
            o_ref[:, h * V_DIM:(h + 1) * V_DIM] = out.T.astype(o_ref.dtype)


def _attn_causal(q, k, vt, tile):
    rows = q.shape[1]
    assert tile % CHUNK == 0 and rows % tile == 0 and vt.shape[3] == tile
    nq = rows // tile
    pairs = [(i, j) for i in range(nq) for j in range(i + 1)]
    qi = jnp.asarray(np.array([a for a, _ in pairs], np.int32))
    kj = jnp.asarray(np.array([b for _, b in pairs], np.int32))
    vw = MLA_HEADS * V_DIM
    blocks = MLA_HEADS * tile * (2 * MLA_QK_PAD + V_EXT) * 2 + tile * vw * 2
    resident = (MLA_HEADS * (V_EXT + 8) * tile + 8 * tile * tile) * 4
    grid_spec = pltpu.PrefetchScalarGridSpec(
        num_scalar_prefetch=2,
        grid=(len(pairs),),
        in_specs=[pl.BlockSpec((MLA_HEADS, tile, MLA_QK_PAD), lambda p, qi, kj: (0, qi[p], 0)),
                  pl.BlockSpec((MLA_HEADS, tile, MLA_QK_PAD), lambda p, qi, kj: (0, kj[p], 0)),
                  pl.BlockSpec((None, MLA_HEADS, V_EXT, tile), lambda p, qi, kj: (kj[p], 0, 0, 0))],
        out_specs=pl.BlockSpec((tile, vw), lambda p, qi, kj: (qi[p], 0)),
        scratch_shapes=[pltpu.VMEM((MLA_HEADS, 1, tile), F32),
                        pltpu.VMEM((MLA_HEADS, V_EXT, tile), F32),
                        pltpu.VMEM((ATTN_SCORE_BUFFERS, tile, tile), F32)])
    return pl.pallas_call(
        _attn_causal_body,
        grid_spec=grid_spec,
        out_shape=jax.ShapeDtypeStruct((rows, vw), BF16),
        compiler_params=_params(("arbitrary",), blocks, resident),
        name="mla_attn_causal",
    )(qi, kj, q, k, vt)


def _attn_cached_body(q_ref, k_ref, vt_ref, o_ref, s_ref, p_ref, *, q_pos0, n_valid):
    tq, tk = q_ref.shape[1], k_ref.shape[1]
    query_chunk = (lax.broadcasted_iota(jnp.int32, (tq, tk), 0) + q_pos0) // CHUNK
    kpos = lax.broadcasted_iota(jnp.int32, (tq, tk), 1)
    mask = (kpos // CHUNK <= query_chunk) & (kpos < n_valid)
    slab = vt_ref.shape[3]
    for h in range(MLA_HEADS):
        s_ref[h] = lax.dot_general(q_ref[h], k_ref[h], NT_DIMS, preferred_element_type=F32)
    s = jnp.where(mask[None], s_ref[...], -jnp.inf)
    p = jnp.exp2(s - jnp.max(s, axis=2, keepdims=True))
    inv_l = 1.0 / jnp.sum(p, axis=2, keepdims=True)
    p_ref[...] = p.astype(BF16)
    for h in range(MLA_HEADS):
        o = jnp.zeros((tq, V_DIM), F32)
        for j in range(vt_ref.shape[0]):
            o = o + lax.dot_general(p_ref[h, :, j * slab:(j + 1) * slab],
                                    vt_ref[j, h, 0:V_DIM, :], NT_DIMS,
                                    preferred_element_type=F32)
        o_ref[:, h * V_DIM:(h + 1) * V_DIM] = (o * inv_l[h]).astype(o_ref.dtype)


def _attn_cached(q, k, vt, batch, q_pos0, n_valid):
    tq = q.shape[1] // batch
    tk = k.shape[1] // batch
    slabs = vt.shape[0] // batch
    slab = vt.shape[3]
    assert slabs * slab == tk
    vw = MLA_HEADS * V_DIM
    blocks = MLA_HEADS * ((tq + tk) * MLA_QK_PAD + V_EXT * tk) * 2 + tq * vw * 2
    resident = 4 * MLA_HEADS * tq * tk * 4
    return pl.pallas_call(
        functools.partial(_attn_cached_body, q_pos0=q_pos0, n_valid=n_valid),
        grid=(batch,),
        in_specs=[pl.BlockSpec((MLA_HEADS, tq, MLA_QK_PAD), lambda b: (0, b, 0)),
                  pl.BlockSpec((MLA_HEADS, tk, MLA_QK_PAD), lambda b: (0, b, 0)),
                  pl.BlockSpec((slabs, MLA_HEADS, V_EXT, slab), lambda b: (b, 0, 0, 0))],
        out_specs=pl.BlockSpec((tq, vw), lambda b: (b, 0)),
        out_shape=jax.ShapeDtypeStruct((q.shape[1], vw), BF16),
        scratch_shapes=[pltpu.VMEM((MLA_HEADS, tq, tk), F32), pltpu.VMEM((MLA_HEADS, tq, tk), BF16)],
        compiler_params=_params(("parallel",), blocks, resident),
        name="mla_attn_cached",
    )(q, k, vt)


def _rope_tables(pos):
    half = QK_ROPE // 2
    inv = ROPE_THETA ** (-jnp.arange(half, dtype=F32) / half)
    ang = pos.astype(F32)[:, None] * inv
    cos, sin = jnp.cos(ang), jnp.sin(ang)
    z = lambda n: jnp.zeros((pos.shape[0], n), F32)
    pad = V7X_LANES - QK_ROPE
    return (jnp.concatenate([cos, cos, z(pad)], axis=1),
            jnp.concatenate([-sin, z(half + pad)], axis=1),
            jnp.concatenate([z(half), sin, z(pad)], axis=1))


def _prep_mla(w_uq, w_dkv, w_ukv):
    qlora = w_uq.shape[0]
    lora = w_ukv.shape[0]
    wq = w_uq.reshape(qlora, MLA_HEADS, QK_NOPE + QK_ROPE)
    wq = jnp.pad(wq, ((0, 0), (0, 0), (0, MLA_QK_PAD - QK_NOPE - QK_ROPE)))
    w_uq_pad = wq.reshape(qlora, MLA_HEADS * MLA_QK_PAD).astype(BF16)
    w_dkv_pad = jnp.pad(w_dkv, ((0, 0), (0, V7X_LANES - QK_ROPE))).astype(BF16)
    wkv = w_ukv.reshape(lora, MLA_HEADS, QK_NOPE + V_DIM)
    w_uk = wkv[:, :, :QK_NOPE].reshape(lora, MLA_HEADS * QK_NOPE).astype(BF16)
    w_uvt = wkv[:, :, QK_NOPE:].reshape(lora, MLA_HEADS * V_DIM).T.astype(BF16)
    return w_uq_pad, w_dkv_pad, w_uk, w_uvt


def _trunk(x, batch, seq, past_len, mem_k, mem_v, s_hgrn, s_conv, ckv_past, kpe_past, p):
    depth = p["norm_g"].shape[0]
    rows = batch * seq
    pos = past_len + jnp.arange(seq, dtype=jnp.int32)
    cos, slo, shi = (jnp.tile(t, (batch, 1)) for t in _rope_tables(pos))
    hs, cs, ckvs, kpes = [], [], [], []
    for l in range(depth):
        g = lambda i: p["norm_g"][l, i][None, :]
        x = _ffn(x, g(0), p["ffn_w1"], p["ffn_w2"], g(1), l, 0)
        mem = (g(4), p["w_mem_q"][l], mem_k[l], mem_v[l], p["w_mem_o"][l], g(5))
        if l % 2 == 0:
            e = l // 2
            proj = _norm_matmul(x, g(2), p["w_in0"][e], BF16, tm_want=1024)
            oa, s_new = _hgrn(proj, p["lbs"][l][None, :], p["hgrn_gnorm"][e][None, :],
                              s_hgrn[e], batch, seq)
            x, c_new = _even_out(oa, proj, p["conv_wt"][e], s_conv[e], p["w_out0"][e],
                                 g(3), x, mem, batch, seq)
            hs.append(s_new)
            cs.append(c_new)
        else:
            o = l // 2
            q, ckv_new, kpe_new, kpad_new = _mla_proj(
                x, g(2), p["mla_w_dq"][o], p["mla_q_norm"][o][None, :], p["w_uq_pad"][o],
                p["w_dkv_pad"][o], p["mla_kv_norm"][o][None, :], cos, slo, shi)
            if past_len == 0:
                assert batch == 1
                tile = _row_tile(seq, ATTN_TILE)
                k, vt = _kv_up(ckv_new, kpad_new, p["w_uk"][o], p["w_uvt"][o], tile)
                att = _attn_causal(q, k, vt, tile)
            else:
                lora = ckv_new.shape[1]
                n_valid = past_len + seq
                n_keys = -(-n_valid // V7X_LANES) * V7X_LANES
                fill = n_keys - n_valid
                ckv_all = jnp.concatenate(
                    [ckv_past[o], ckv_new.reshape(batch, seq, lora),
                     jnp.zeros((batch, fill, lora), F32)], axis=1)
                kpad_past = jnp.pad(kpe_past[o], ((0, 0), (0, 0), (0, V7X_LANES - QK_ROPE)))
                kpad_all = jnp.concatenate(
                    [kpad_past.astype(BF16), kpad_new.reshape(batch, seq, V7X_LANES),
                     jnp.zeros((batch, fill, V7X_LANES), BF16)], axis=1)
                k, vt = _kv_up(ckv_all.reshape(batch * n_keys, lora),
                               kpad_all.reshape(batch * n_keys, V7X_LANES),
                               p["w_uk"][o], p["w_uvt"][o], _row_tile(n_keys, ATTN_TILE))
                att = _attn_cached(q, k, vt, batch, past_len, n_valid)
            x = _proj_residual(att, p["mla_w_o"][o], g(3), x, mem, batch, seq)
            ckvs.append(ckv_new.reshape(batch, seq, -1))
            kpes.append(kpe_new.reshape(batch, seq, -1))
        x = _ffn(x, g(6), p["ffn_w1"], p["ffn_w2"], g(7), l, 1)
    return (x.reshape(batch, seq, -1), jnp.stack(hs), jnp.stack(cs), jnp.stack(ckvs),
            jnp.stack(kpes))


def kernel(x_prompt, x_sample, mem_prompt, state_hgrn, state_conv, cache_ckv, cache_kpe, cache_mem_k, cache_mem_v, norm_g, ffn_w1, ffn_w2, w_in0, hgrn_lb, hgrn_gnorm, conv_w, w_out0, mla_w_dq, mla_q_norm, mla_w_uq, mla_w_dkv, mla_kv_norm, mla_w_ukv, mla_w_o, mem_norm, w_mem_q, w_mem_kv, w_mem_o):
    batch, seq, d = x_prompt.shape
    dec_batch, dec_seq, _ = x_sample.shape
    past_len = cache_ckv.shape[2]
    depth = norm_g.shape[0]
    n_mem = mem_prompt.shape[1]
    mem_width = MEM_HEADS * MEM_DIM

    prepped = [_prep_mla(mla_w_uq[o], mla_w_dkv[o], mla_w_ukv[o]) for o in range(mla_w_uq.shape[0])]
    p = dict(
        norm_g=norm_g,
        ffn_w1=ffn_w1.astype(BF16), ffn_w2=ffn_w2.astype(BF16),
        w_in0=w_in0.astype(BF16), w_out0=w_out0.astype(BF16),
        lbs=jnp.cumsum(jax.nn.softmax(hgrn_lb.astype(F32), axis=0), axis=0),
        hgrn_gnorm=hgrn_gnorm,
        conv_wt=jnp.swapaxes(conv_w, 1, 2),
        mla_w_dq=mla_w_dq.astype(BF16), mla_q_norm=mla_q_norm, mla_kv_norm=mla_kv_norm,
        w_uq_pad=[t[0] for t in prepped], w_dkv_pad=[t[1] for t in prepped],
        w_uk=[t[2] for t in prepped], w_uvt=[t[3] for t in prepped],
        mla_w_o=mla_w_o.astype(BF16),
        w_mem_q=w_mem_q.astype(BF16), w_mem_o=w_mem_o.astype(BF16),
    )

    mem_rows = mem_prompt.reshape(batch * n_mem, d)
    mks, mvs = [], []
    for l in range(depth):
        kv = _norm_matmul(mem_rows, mem_norm[l][None, :], w_mem_kv[l].astype(BF16), F32,
                          tn_want=2 * mem_width)
        kv = kv.reshape(batch, n_mem, 2, mem_width)
        mks.append(kv[:, :, 0])
        mvs.append(kv[:, :, 1])
    mem_k_p = jnp.stack(mks)
    mem_v_p = jnp.stack(mvs)

    n_even = state_hgrn.shape[0]
    n_odd = cache_ckv.shape[0]
    h0 = jnp.zeros((n_even, batch) + state_hgrn.shape[2:], F32)
    c0 = jnp.zeros((n_even, batch) + state_conv.shape[2:], F32)
    y_p, p_hgrn, p_conv, p_ckv, p_kpe = _trunk(
        x_prompt.reshape(batch * seq, d), batch, seq, 0, mem_k_p, mem_v_p, h0, c0, None, None, p)
    y_s, s_hgrn, s_conv, s_ckv, s_kpe = _trunk(
        x_sample.reshape(dec_batch * dec_seq, d), dec_batch, dec_seq, past_len,
        cache_mem_k.reshape(depth, dec_batch, n_mem, mem_width),
        cache_mem_v.reshape(depth, dec_batch, n_mem, mem_width),
        state_hgrn, state_conv, cache_ckv, cache_kpe, p)

    mem_shape = (depth, batch, n_mem, MEM_HEADS, MEM_DIM)
    return (y_p, y_s, p_hgrn, p_conv, p_ckv, p_kpe,
            mem_k_p.reshape(mem_shape), mem_v_p.reshape(mem_shape),
            s_hgrn, s_conv, s_ckv, s_kpe)
```

```python
import functools

import jax
import jax.numpy as jnp
import numpy as np
from jax import lax
from jax.experimental import pallas as pl
from jax.experimental.pallas import tpu as pltpu

F32 = jnp.float32
BF16 = jnp.bfloat16

V7X_LANES = 128
V7X_SUBLANES = 8
V7X_VMEM_BYTES = 64 * 1024 * 1024
MIB = 1024 * 1024
SPILL_ROOM_BYTES = 8 * MIB
VMEM_RESERVED_BYTES = 2 * MIB

EPS = 1e-6
CHUNK = 64
HGRN_HEADS = 8
HGRN_DK = 128
HGRN_SUB = 16
MLA_HEADS = 16
QK_NOPE = 128
QK_ROPE = 64
V_DIM = 128
V_EXT = V_DIM + 16
MLA_QK_PAD = 256
MLA_SCALE = (QK_NOPE + QK_ROPE) ** -0.5
ROPE_THETA = 10000.0
MEM_HEADS = 4
MEM_DIM = 128
CONV_W = 3
NEG_BIG = -1e30
LOG2E = 1.4426950408889634

NT_DIMS = (((1,), (1,)), ((), ()))
TN_DIMS = (((0,), (0,)), ((), ()))


def _vmem_limit(pipelined_bytes, resident_bytes):
    want = 2 * pipelined_bytes + resident_bytes + SPILL_ROOM_BYTES
    return int(min(want, V7X_VMEM_BYTES - VMEM_RESERVED_BYTES))


def _params(semantics, pipelined_bytes, resident_bytes):
    return pltpu.CompilerParams(
        dimension_semantics=semantics,
        vmem_limit_bytes=_vmem_limit(pipelined_bytes, resident_bytes))


def _row_tile(rows, want):
    if rows <= want:
        return rows
    t = want - want % V7X_LANES
    while t > 0 and rows % t:
        t -= V7X_LANES
    assert t > 0, (rows, want)
    return t


ROW_CHUNK = 256


def _row_chunks(rows):
    return [slice(r0, min(r0 + ROW_CHUNK, rows)) for r0 in range(0, rows, ROW_CHUNK)]


def _rms(x, g):
    ms = jnp.mean(x * x, axis=-1, keepdims=True)
    return x * lax.rsqrt(ms + EPS) * g


def _sigmoid(x):
    return 1.0 / (1.0 + jnp.exp(-x))


def _dot(a, b):
    return jnp.dot(a, b, preferred_element_type=F32)


FFN_COL_CHUNK = 512


def _ffn_hidden(xn, w1g_ref, w1u_ref):
    gate = _dot(xn, w1g_ref[...])
    up = _dot(xn, w1u_ref[...])
    return (gate * _sigmoid(gate) * up).astype(BF16)


def _ffn_body(x_ref, gpre_ref, w1g_ref, w1u_ref, w2_ref, gpost_ref, o_ref, xn_ref):
    k = pl.program_id(1)
    last = pl.num_programs(1) - 1
    tm, d = o_ref.shape

    @pl.when(k == 0)
    def _():
        gpre = gpre_ref[...]
        for rs in _row_chunks(tm):
            xn = _rms(x_ref[rs, :], gpre).astype(BF16)
            xn_ref[rs, :] = xn
            o_ref[rs, :] = _dot(_ffn_hidden(xn, w1g_ref, w1u_ref), w2_ref[...])

    @pl.when((k > 0) & (k < last))
    def _():
        o_ref[...] += _dot(_ffn_hidden(xn_ref[...], w1g_ref, w1u_ref), w2_ref[...])

    @pl.when(k == last)
    def _():
        h = _ffn_hidden(xn_ref[...], w1g_ref, w1u_ref)
        ss = jnp.zeros((tm, 1), F32)
        for c0 in range(0, d, FFN_COL_CHUNK):
            cs = slice(c0, c0 + FFN_COL_CHUNK)
            y = o_ref[:, cs] + _dot(h, w2_ref[:, cs])
            o_ref[:, cs] = y
            ss = ss + jnp.sum(y * y, axis=-1, keepdims=True)
        scale = 0.5 * lax.rsqrt(ss * (1.0 / d) + EPS)
        o_ref[...] = x_ref[...] + o_ref[...] * scale * gpost_ref[...]


def _ffn(x, g_pre, w1, w2, g_post, layer, half, tm_want=1024, tf=512):
    rows, d = x.shape
    dff = w2.shape[-2]
    tm = _row_tile(rows, tm_want)
    nk = dff // tf
    assert dff % tf == 0 and nk >= 2 and d % FFN_COL_CHUNK == 0
    blocks = 2 * tm * d * 4 + 3 * d * tf * 2
    resident = tm * d * (2 + 4) + 4 * tm * tf * 4
    return pl.pallas_call(
        _ffn_body,
        grid=(rows // tm, nk),
        in_specs=[
            pl.BlockSpec((tm, d), lambda i, k: (i, 0)),
            pl.BlockSpec((1, d), lambda i, k: (0, 0)),
            pl.BlockSpec((None, None, d, tf), lambda i, k: (layer, half, 0, k)),
            pl.BlockSpec((None, None, d, tf), lambda i, k: (layer, half, 0, nk + k)),
            pl.BlockSpec((None, None, tf, d), lambda i, k: (layer, half, k, 0)),
            pl.BlockSpec((1, d), lambda i, k: (0, 0)),
        ],
        out_specs=pl.BlockSpec((tm, d), lambda i, k: (i, 0)),
        out_shape=jax.ShapeDtypeStruct((rows, d), F32),
        scratch_shapes=[pltpu.VMEM((tm, d), BF16)],
        compiler_params=_params(("parallel", "arbitrary"), blocks, resident),
        name="ffn",
    )(x, g_pre, w1, w1, w2, g_post)


def _norm_matmul_body(x_ref, g_ref, w_ref, o_ref, xn_ref):
    j = pl.program_id(1)

    @pl.when(j == 0)
    def _():
        g = g_ref[...]
        for rs in _row_chunks(x_ref.shape[0]):
            xn = _rms(x_ref[rs, :], g).astype(BF16)
            xn_ref[rs, :] = xn
            o_ref[rs, :] = _dot(xn, w_ref[...]).astype(o_ref.dtype)

    @pl.when(j > 0)
    def _():
        o_ref[...] = _dot(xn_ref[...], w_ref[...]).astype(o_ref.dtype)


def _norm_matmul(x, g, w, out_dtype, tm_want=512, tn_want=1024):
    rows, d = x.shape
    n = w.shape[1]
    tm = _row_tile(rows, tm_want)
    tn = _row_tile(n, tn_want)
    blocks = tm * d * 4 + d * tn * 2 + tm * tn * 4
    resident = tm * d * 2 + tm * tn * 4
    return pl.pallas_call(
        _norm_matmul_body,
        grid=(rows // tm, n // tn),
        in_specs=[
            pl.BlockSpec((tm, d), lambda i, j: (i, 0)),
            pl.BlockSpec((1, d), lambda i, j: (0, 0)),
            pl.BlockSpec((d, tn), lambda i, j: (0, j)),
        ],
        out_specs=pl.BlockSpec((tm, tn), lambda i, j: (i, j)),
        out_shape=jax.ShapeDtypeStruct((rows, n), out_dtype),
        scratch_shapes=[pltpu.VMEM((tm, d), BF16)],
        compiler_params=_params(("parallel", "arbitrary"), blocks, resident),
        name="norm_matmul",
    )(x, g, w)


def _mem_sublayer(x, gpre_ref, wq_ref, mk_ref, mv_ref, wo_ref, gpost_ref):
    q = _dot(_rms(x, gpre_ref[...]).astype(BF16), wq_ref[...])
    outs = []
    for h in range(MEM_HEADS):
        hs = slice(h * MEM_DIM, (h + 1) * MEM_DIM)
        kh = mk_ref[:, hs].astype(BF16)
        vh = mv_ref[:, hs].astype(BF16)
        s = lax.dot_general(q[:, hs].astype(BF16), kh, NT_DIMS,
                            preferred_element_type=F32) * (MEM_DIM ** -0.5)
        p = jnp.exp(s - jnp.max(s, axis=1, keepdims=True))
        p = p * (1.0 / jnp.sum(p, axis=1, keepdims=True))
        outs.append(_dot(p.astype(BF16), vh).astype(BF16))
    o = jnp.concatenate(outs, axis=1)
    return x + _rms(_dot(o, wo_ref[...]), gpost_ref[...])


def _const_spec(a):
    return pl.BlockSpec(a.shape, lambda *_: (0,) * a.ndim, pipeline_mode=pl.Buffered(1))


def _mem_specs(mem_k, w_q, w_o, g_pre, g_post):
    n_mem, width = mem_k.shape[1], mem_k.shape[2]
    mem = pl.BlockSpec((None, n_mem, width), lambda b, t: (b, 0, 0))
    specs = [_const_spec(g_pre), _const_spec(w_q), mem, mem, _const_spec(w_o),
             _const_spec(g_post)]
    nbytes = (w_q.size + w_o.size) * 2 + 4 * n_mem * width * 4
    return specs, nbytes


def _proj_residual_body(a_ref, w_ref, g_ref, x_ref,
                        gpre_ref, wq_ref, mk_ref, mv_ref, wo_ref, gpost_ref, o_ref):
    x1 = x_ref[...] + _rms(_dot(a_ref[...], w_ref[...]), g_ref[...])
    o_ref[...] = _mem_sublayer(x1, gpre_ref, wq_ref, mk_ref, mv_ref, wo_ref, gpost_ref)


def _proj_residual(a, w, g, x, mem, batch, seq, tm_want=512):
    g_pre, w_q, mem_k, mem_v, w_o, g_post = mem
    kdim = a.shape[1]
    d = w.shape[1]
    tm = _row_tile(seq, tm_want)
    nt = seq // tm
    row = lambda n: pl.BlockSpec((tm, n), lambda b, t: (b * nt + t, 0))
    mem_specs, mem_bytes = _mem_specs(mem_k, w_q, w_o, g_pre, g_post)
    blocks = tm * kdim * 2 + 2 * tm * d * 4
    resident = kdim * d * 2 + mem_bytes + 4 * tm * d * 4
    return pl.pallas_call(
        _proj_residual_body,
        grid=(batch, nt),
        in_specs=[row(kdim), _const_spec(w), _const_spec(g), row(d)] + mem_specs,
        out_specs=row(d),
        out_shape=jax.ShapeDtypeStruct(x.shape, F32),
        compiler_params=_params(("parallel", "parallel"), blocks, resident),
        name="proj_residual_mem",
    )(a, w, g, x, g_pre, w_q, mem_k, mem_v, w_o, g_post)


def _hgrn_body(qa_ref, fa_ref, ia_ref, ga_ref, lb_ref, gn_ref, s0_ref,
               oa_ref, sout_ref, st_ref, o_ref):
    c = pl.program_id(1)
    rows = qa_ref.shape[0]
    width = qa_ref.shape[1]
    nsub = rows // HGRN_SUB

    @pl.when(c == 0)
    def _():
        for h in range(HGRN_HEADS):
            st_ref[h] = s0_ref[h].T

    lb = lb_ref[...]
    qa = qa_ref[...].astype(F32)
    q = qa * _sigmoid(qa) * (HGRN_DK ** -0.5)
    f = lb + (1.0 - lb) * _sigmoid(fa_ref[...].astype(F32))
    k = 1.0 - f
    g = jnp.log(f) * LOG2E
    v = ia_ref[...].astype(F32)

    r_i = lax.broadcasted_iota(jnp.int32, (rows, rows), 0)
    c_i = lax.broadcasted_iota(jnp.int32, (rows, rows), 1)
    tri = ((c_i <= r_i) & ((c_i // HGRN_SUB) == (r_i // HGRN_SUB))).astype(BF16)
    g1 = g.astype(BF16)
    rem = g - g1.astype(F32)
    g2 = rem.astype(BF16)
    g3 = (rem - g2.astype(F32)).astype(BF16)
    bl = _dot(tri, g1) + _dot(tri, g2) + _dot(tri, g3)

    qe = (q * jnp.exp2(bl)).astype(BF16)
    row8 = lax.broadcasted_iota(jnp.int32, (V7X_SUBLANES, width), 0)
    tiles = HGRN_SUB // V7X_SUBLANES

    for i in range(nsub):
        r0 = i * HGRN_SUB
        bl_i = bl[r0:r0 + HGRN_SUB]
        q_i = q[r0:r0 + HGRN_SUB]
        k_i = k[r0:r0 + HGRN_SUB]
        v_i = v[r0:r0 + HGRN_SUB]
        b_end = bl_i[HGRN_SUB - 1:HGRN_SUB]
        ke_i = (k_i * jnp.exp2(b_end - bl_i)).astype(BF16)
        dec_i = jnp.exp2(b_end)
        v_bf = v_i.astype(BF16)

        diag = [[jnp.zeros((V7X_SUBLANES, HGRN_DK), F32) for _ in range(tiles)]
                for _ in range(HGRN_HEADS)]
        for s in range(HGRN_SUB):
            ks, bs, vs = k_i[s:s + 1], bl_i[s:s + 1], v_i[s:s + 1]
            for rt in range(s // V7X_SUBLANES, tiles):
                rsl = slice(rt * V7X_SUBLANES, (rt + 1) * V7X_SUBLANES)
                rel = bl_i[rsl] - bs
                if rt == s // V7X_SUBLANES:
                    rel = jnp.where(row8 >= s % V7X_SUBLANES, rel, -jnp.inf)
                w = q_i[rsl] * ks * jnp.exp2(rel)
                for h in range(HGRN_HEADS):
                    hs = slice(h * HGRN_DK, (h + 1) * HGRN_DK)
                    col = jnp.sum(w[:, hs], axis=1, keepdims=True)
                    diag[h][rt] = diag[h][rt] + col * vs[:, hs]

        for h in range(HGRN_HEADS):
            hs = slice(h * HGRN_DK, (h + 1) * HGRN_DK)
            st = st_ref[h]
            inter = lax.dot_general(qe[r0:r0 + HGRN_SUB, hs], st.astype(BF16), NT_DIMS,
                                    preferred_element_type=F32)
            upd = lax.dot_general(v_bf[:, hs], ke_i[:, hs], TN_DIMS,
                                  preferred_element_type=F32)
            st_ref[h] = st * dec_i[:, hs] + upd
            o_ref[r0:r0 + HGRN_SUB, hs] = inter + jnp.concatenate(diag[h], axis=0)

    gn = gn_ref[...]
    for h in range(HGRN_HEADS):
        hs = slice(h * HGRN_DK, (h + 1) * HGRN_DK)
        ga = ga_ref[:, hs].astype(F32)
        oa_ref[:, hs] = (_rms(o_ref[:, hs], gn) * (ga * _sigmoid(ga))).astype(oa_ref.dtype)

    @pl.when(c == pl.num_programs(1) - 1)
    def _():
        for h in range(HGRN_HEADS):
            sout_ref[h] = st_ref[h].T


def _hgrn(proj, lb, gnorm, s0, batch, seq):
    width = HGRN_HEADS * HGRN_DK
    rows = min(CHUNK, seq)
    nc = seq // rows
    assert seq % rows == 0 and rows % HGRN_SUB == 0
    col = lambda j: pl.BlockSpec((rows, width), lambda b, c: (b * nc + c, j))
    state_spec = pl.BlockSpec((None, HGRN_HEADS, HGRN_DK, HGRN_DK), lambda b, c: (b, 0, 0, 0))
    blocks = 4 * rows * width * 4 + rows * width * 2 + 2 * HGRN_HEADS * HGRN_DK * HGRN_DK * 4
    resident = HGRN_HEADS * HGRN_DK * HGRN_DK * 4 + 16 * rows * width * 4
    return pl.pallas_call(
        _hgrn_body,
        grid=(batch, nc),
        in_specs=[col(0), col(1), col(2), col(3),
                  pl.BlockSpec((1, width), lambda b, c: (0, 0)),
                  pl.BlockSpec((1, HGRN_DK), lambda b, c: (0, 0)),
                  state_spec],
        out_specs=[pl.BlockSpec((rows, width), lambda b, c: (b * nc + c, 0)), state_spec],
        out_shape=[jax.ShapeDtypeStruct((batch * seq, width), BF16),
                   jax.ShapeDtypeStruct(s0.shape, F32)],
        scratch_shapes=[pltpu.VMEM((HGRN_HEADS, HGRN_DK, HGRN_DK), F32),
                        pltpu.VMEM((rows, width), F32)],
        compiler_params=_params(("parallel", "arbitrary"), blocks, resident),
        name="hgrn2",
    )(proj, proj, proj, proj, lb, gnorm, s0)


CONV_PAD = 8


def _even_out_body(oa_ref, bg_ref, cg_ref, hb_ref, cw_ref, cs_ref, w_ref, g_ref, x_ref,
                   gpre_ref, wq_ref, mk_ref, mv_ref, wo_ref, gpost_ref,
                   o_ref, cnew_ref, ubuf_ref):
    t = pl.program_id(1)
    tm = oa_ref.shape[0]
    half = oa_ref.shape[1]
    lo = CONV_PAD - (CONV_W - 1)

    @pl.when(t == 0)
    def _():
        ubuf_ref[lo:CONV_PAD, :] = cs_ref[...]

    u = cg_ref[...].astype(F32) * hb_ref[...].astype(F32)
    ubuf_ref[CONV_PAD:CONV_PAD + tm, :] = u
    yb = u * cw_ref[CONV_W - 1:CONV_W, :]
    for j in range(CONV_W - 1):
        yb = yb + ubuf_ref[lo + j:lo + j + tm, :] * cw_ref[j:j + 1, :]
    ob = (bg_ref[...].astype(F32) * yb).astype(BF16)
    y = _dot(oa_ref[...], w_ref[0:half, :]) + _dot(ob, w_ref[half:2 * half, :])
    x1 = x_ref[...] + _rms(y, g_ref[...])
    o_ref[...] = _mem_sublayer(x1, gpre_ref, wq_ref, mk_ref, mv_ref, wo_ref, gpost_ref)

    tail = ubuf_ref[lo + tm:CONV_PAD + tm, :]
    ubuf_ref[lo:CONV_PAD, :] = tail

    @pl.when(t == pl.num_programs(1) - 1)
    def _():
        cnew_ref[...] = tail


def _even_out(oa, proj, conv_wt, conv_state, w_out, g, x, mem, batch, seq, tm_want=512):
    g_pre, w_q, mem_k, mem_v, w_o, g_post = mem
    half = oa.shape[1]
    d = x.shape[1]
    tm = _row_tile(seq, tm_want)
    nt = seq // tm
    row = lambda b, t: (b * nt + t, 0)
    pcol = lambda j: pl.BlockSpec((tm, half), lambda b, t: (b * nt + t, j))
    cstate = pl.BlockSpec((None, CONV_W - 1, half), lambda b, t: (b, 0, 0))
    mem_specs, mem_bytes = _mem_specs(mem_k, w_q, w_o, g_pre, g_post)
    blocks = tm * half * 4 * proj.dtype.itemsize + 2 * tm * d * 4
    resident = (2 * half * d * 2 + mem_bytes + (tm + CONV_PAD) * half * 4 + 4 * tm * half * 4
                + 4 * tm * d * 4)
    return pl.pallas_call(
        _even_out_body,
        grid=(batch, nt),
        in_specs=[pl.BlockSpec((tm, half), row), pcol(4), pcol(5), pcol(6),
                  _const_spec(conv_wt), cstate, _const_spec(w_out), _const_spec(g),
                  pl.BlockSpec((tm, d), row)] + mem_specs,
        out_specs=[pl.BlockSpec((tm, d), row), cstate],
        out_shape=[jax.ShapeDtypeStruct(x.shape, F32),
                   jax.ShapeDtypeStruct(conv_state.shape, F32)],
        scratch_shapes=[pltpu.VMEM((tm + CONV_PAD, half), F32)],
        compiler_params=_params(("parallel", "arbitrary"), blocks, resident),
        name="even_out_mem",
    )(oa, proj, proj, proj, conv_wt, conv_state, w_out, g, x,
      g_pre, w_q, mem_k, mem_v, w_o, g_post)


Q_PRESCALE = MLA_SCALE * LOG2E


def _rope(x, cos, sin_lo, sin_hi):
    half = QK_ROPE // 2
    return (x * cos
            + pltpu.roll(x, V7X_LANES - half, axis=1) * sin_lo
            + pltpu.roll(x, half, axis=1) * sin_hi)


def _mla_proj_body(x_ref, g_ref, wdq_ref, qn_ref, wuq_ref, wdkv_ref, kvn_ref,
                   cos_ref, slo_ref, shi_ref, q_ref, ckv_ref, kpe_ref, kpad_ref):
    cos, slo, shi = cos_ref[...], slo_ref[...], shi_ref[...]
    lora = ckv_ref.shape[1]
    xn = _rms(x_ref[...], g_ref[...]).astype(BF16)
    cq = _rms(_dot(xn, wdq_ref[...]), qn_ref[...]).astype(BF16)
    ckr = _dot(xn, wdkv_ref[...])
    ckv_ref[...] = _rms(ckr[:, :lora], kvn_ref[...])
    kp = _rope(ckr[:, lora:lora + V7X_LANES], cos, slo, shi)
    kpe_ref[...] = kp[:, :QK_ROPE]
    kpad_ref[...] = kp.astype(BF16)
    q = _dot(cq, wuq_ref[...]) * Q_PRESCALE
    for h in range(MLA_HEADS):
        c0 = h * MLA_QK_PAD
        q_ref[h, :, 0:QK_NOPE] = q[:, c0:c0 + QK_NOPE].astype(BF16)
        q_ref[h, :, QK_NOPE:MLA_QK_PAD] = _rope(
            q[:, c0 + QK_NOPE:c0 + MLA_QK_PAD], cos, slo, shi).astype(BF16)


def _mla_proj(x, g, w_dq, q_norm, w_uq_pad, w_dkv_pad, kv_norm, cos, slo, shi, tm_want=512):
    rows, d = x.shape
    qlora = w_dq.shape[1]
    lora = kv_norm.shape[1]
    qw = w_uq_pad.shape[1]
    kw = w_dkv_pad.shape[1]
    tm = _row_tile(rows, tm_want)
    full = lambda a: pl.BlockSpec(a.shape, lambda i: (0,) * a.ndim)
    rowspec = lambda n: pl.BlockSpec((tm, n), lambda i: (i, 0))
    blocks = (tm * d * 4 + (d * qlora + qlora * qw + d * kw) * 2 + 3 * tm * V7X_LANES * 4
              + tm * (qw * 2 + lora * 4 + QK_ROPE * 4 + V7X_LANES * 2))
    resident = tm * (d * 6 + qw * 4 + kw * 4 + qlora * 8)
    return pl.pallas_call(
        _mla_proj_body,
        grid=(rows // tm,),
        in_specs=[rowspec(d), full(g), full(w_dq), full(q_norm), full(w_uq_pad),
                  full(w_dkv_pad), full(kv_norm),
                  rowspec(V7X_LANES), rowspec(V7X_LANES), rowspec(V7X_LANES)],
        out_specs=[pl.BlockSpec((MLA_HEADS, tm, MLA_QK_PAD), lambda i: (0, i, 0)),
                   rowspec(lora), rowspec(QK_ROPE), rowspec(V7X_LANES)],
        out_shape=[jax.ShapeDtypeStruct((MLA_HEADS, rows, MLA_QK_PAD), BF16),
                   jax.ShapeDtypeStruct((rows, lora), F32),
                   jax.ShapeDtypeStruct((rows, QK_ROPE), F32),
                   jax.ShapeDtypeStruct((rows, V7X_LANES), BF16)],
        compiler_params=_params(("parallel",), blocks, resident),
        name="mla_proj",
    )(x, g, w_dq, q_norm, w_uq_pad, w_dkv_pad, kv_norm, cos, slo, shi)


def _kv_up_body(ckv_ref, kpad_ref, wuk_ref, wuvt_ref, k_ref, vt_ref):
    c = ckv_ref[...].astype(BF16)
    kn = _dot(c, wuk_ref[...])
    vt = lax.dot_general(wuvt_ref[...], c, NT_DIMS, preferred_element_type=F32)
    ts = c.shape[0]
    vt_ref[:, 0:V_DIM, :] = vt.astype(BF16).reshape(MLA_HEADS, V_DIM, ts)
    ones_row = lax.broadcasted_iota(jnp.int32, (MLA_HEADS, V_EXT - V_DIM, ts), 1) == 0
    vt_ref[:, V_DIM:V_EXT, :] = ones_row.astype(BF16)
    kpad = kpad_ref[...]
    for h in range(MLA_HEADS):
        k_ref[h, :, 0:QK_NOPE] = kn[:, h * QK_NOPE:(h + 1) * QK_NOPE].astype(BF16)
        k_ref[h, :, QK_NOPE:MLA_QK_PAD] = kpad


def _kv_up(ckv, kpad, w_uk, w_uvt, ts):
    rows, lora = ckv.shape
    assert rows % ts == 0
    rowspec = lambda n: pl.BlockSpec((ts, n), lambda i: (i, 0))
    full = lambda a: pl.BlockSpec(a.shape, lambda i: (0,) * a.ndim)
    kw = MLA_HEADS * MLA_QK_PAD
    vw = MLA_HEADS * V_DIM
    blocks = ts * (lora * 4 + V7X_LANES * 2 + kw * 2 + vw * 2) + 2 * lora * vw * 2
    resident = ts * (kw + vw) * 4
    return pl.pallas_call(
        _kv_up_body,
        grid=(rows // ts,),
        in_specs=[rowspec(lora), rowspec(V7X_LANES), full(w_uk), full(w_uvt)],
        out_specs=[pl.BlockSpec((MLA_HEADS, ts, MLA_QK_PAD), lambda i: (0, i, 0)),
                   pl.BlockSpec((None, MLA_HEADS, V_EXT, ts), lambda i: (i, 0, 0, 0))],
        out_shape=[jax.ShapeDtypeStruct((MLA_HEADS, rows, MLA_QK_PAD), BF16),
                   jax.ShapeDtypeStruct((rows // ts, MLA_HEADS, V_EXT, ts), BF16)],
        compiler_params=_params(("parallel",), blocks, resident),
        name="kv_up",
    )(ckv, kpad, w_uk, w_uvt)


ATTN_TILE = 512
ATTN_Q_PER_K = 2
ATTN_SCORE_BUFFERS = 3


def _scores_t(h, q_ref, k_ref):
    return lax.dot_general(k_ref[h], q_ref[h], NT_DIMS, preferred_element_type=F32)


def _softmax_pv_t(h, st, vt_ref, m_ref, acc_ref, mask):
    if mask is not None:
        st = jnp.where(mask, st, -jnp.inf)
    m_prev = m_ref[h]
    m_next = jnp.maximum(m_prev, jnp.max(st, axis=0, keepdims=True))
    alpha = jnp.exp2(m_prev - m_next)
    p = jnp.exp2(st - m_next)
    m_ref[h] = m_next
    acc_ref[h] = alpha * acc_ref[h] + _dot(vt_ref[h], p.astype(BF16))


def _attn_block_t(q_ref, k_ref, vt_ref, m_ref, acc_ref, s_ref, mask):
    nbuf = s_ref.shape[0]
    for h in range(nbuf - 1):
        s_ref[h] = _scores_t(h, q_ref, k_ref)
    for h in range(MLA_HEADS):
        ahead = h + nbuf - 1
        if ahead < MLA_HEADS:
            s_ref[ahead % nbuf] = _scores_t(ahead, q_ref, k_ref)
        _softmax_pv_t(h, s_ref[h % nbuf], vt_ref, m_ref, acc_ref, mask)


def _attn_causal_body(qi_ref, kj_ref, q_ref, k_ref, vt_ref, o_ref, m_ref, acc_ref, s_ref):
    step = pl.program_id(0)
    qi = qi_ref[step]
    kj = kj_ref[step]
    tq, tk = q_ref.shape[1], k_ref.shape[1]
    per_q = tq // tk

    @pl.when(kj == 0)
    def _():
        m_ref[...] = jnp.full(m_ref.shape, NEG_BIG, F32)
        acc_ref[...] = jnp.zeros_like(acc_ref)

    @pl.when(kj < qi * per_q)
    def _():
        _attn_block_t(q_ref, k_ref, vt_ref, m_ref, acc_ref, s_ref, None)

    @pl.when(kj >= qi * per_q)
    def _():
        key_chunk = (lax.broadcasted_iota(jnp.int32, (tk, tq), 0) + kj * tk) // CHUNK
        query_chunk = (lax.broadcasted_iota(jnp.int32, (tk, tq), 1) + qi * tq) // CHUNK
        _attn_block_t(q_ref, k_ref, vt_ref, m_ref, acc_ref, s_ref, key_chunk <= query_chunk)

    @pl.when(kj == qi * per_q + per_q - 1)
    def _():
        for h in range(MLA_HEADS):
            out = acc_ref[h, 0:V_DIM, :] / acc_ref[h, V_DIM:V_DIM + 1, :]
            o_ref[:, h * V_DIM:(h + 1) * V_DIM] = out.T.astype(o_ref.dtype)


def _attn_causal(q, k, vt, tq, tk):
    rows = q.shape[1]
    assert tk % CHUNK == 0 and tq % tk == 0 and rows % tq == 0 and vt.shape[3] == tk
    per_q = tq // tk
    pairs = [(i, j) for i in range(rows // tq) for j in range(per_q * (i + 1))]
    qi = jnp.asarray(np.array([a for a, _ in pairs], np.int32))
    kj = jnp.asarray(np.array([b for _, b in pairs], np.int32))
    vw = MLA_HEADS * V_DIM
    blocks = MLA_HEADS * (tq * MLA_QK_PAD + tk * (MLA_QK_PAD + V_EXT)) * 2 + tq * vw * 2
    resident = (MLA_HEADS * (V_EXT + 8) * tq + (ATTN_SCORE_BUFFERS + 1) * tk * tq) * 4
    grid_spec = pltpu.PrefetchScalarGridSpec(
        num_scalar_prefetch=2,
        grid=(len(pairs),),
        in_specs=[pl.BlockSpec((MLA_HEADS, tq, MLA_QK_PAD), lambda p, qi, kj: (0, qi[p], 0)),
                  pl.BlockSpec((MLA_HEADS, tk, MLA_QK_PAD), lambda p, qi, kj: (0, kj[p], 0)),
                  pl.BlockSpec((None, MLA_HEADS, V_EXT, tk), lambda p, qi, kj: (kj[p], 0, 0, 0))],
        out_specs=pl.BlockSpec((tq, vw), lambda p, qi, kj: (qi[p], 0)),
        scratch_shapes=[pltpu.VMEM((MLA_HEADS, 1, tq), F32),
                        pltpu.VMEM((MLA_HEADS, V_EXT, tq), F32),
                        pltpu.VMEM((ATTN_SCORE_BUFFERS, tk, tq), F32)])
    return pl.pallas_call(
        _attn_causal_body,
        grid_spec=grid_spec,
        out_shape=jax.ShapeDtypeStruct((rows, vw), BF16),
        compiler_params=_params(("arbitrary",), blocks, resident),
        name="mla_attn_causal",
    )(qi, kj, q, k, vt)


def _attn_cached_body(q_ref, k_ref, vt_ref, o_ref, s_ref, p_ref, *, q_pos0, n_valid):
    tq, tk = q_ref.shape[1], k_ref.shape[1]
    query_chunk = (lax.broadcasted_iota(jnp.int32, (tq, tk), 0) + q_pos0) // CHUNK
    kpos = lax.broadcasted_iota(jnp.int32, (tq, tk), 1)
    mask = (kpos // CHUNK <= query_chunk) & (kpos < n_valid)
    slab = vt_ref.shape[3]
    for h in range(MLA_HEADS):
        s_ref[h] = lax.dot_general(q_ref[h], k_ref[h], NT_DIMS, preferred_element_type=F32)
    s = jnp.where(mask[None], s_ref[...], -jnp.inf)
    p = jnp.exp2(s - jnp.max(s, axis=2, keepdims=True))
    inv_l = 1.0 / jnp.sum(p, axis=2, keepdims=True)
    p_ref[...] = p.astype(BF16)
    for h in range(MLA_HEADS):
        o = jnp.zeros((tq, V_DIM), F32)
        for j in range(vt_ref.shape[0]):
            o = o + lax.dot_general(p_ref[h, :, j * slab:(j + 1) * slab],
                                    vt_ref[j, h, 0:V_DIM, :], NT_DIMS,
                                    preferred_element_type=F32)
        o_ref[:, h * V_DIM:(h + 1) * V_DIM] = (o * inv_l[h]).astype(o_ref.dtype)


def _attn_cached(q, k, vt, batch, q_pos0, n_valid):
    tq = q.shape[1] // batch
    tk = k.shape[1] // batch
    slabs = vt.shape[0] // batch
    slab = vt.shape[3]
    assert slabs * slab == tk
    vw = MLA_HEADS * V_DIM
    blocks = MLA_HEADS * ((tq + tk) * MLA_QK_PAD + V_EXT * tk) * 2 + tq * vw * 2
    resident = 4 * MLA_HEADS * tq * tk * 4
    return pl.pallas_call(
        functools.partial(_attn_cached_body, q_pos0=q_pos0, n_valid=n_valid),
        grid=(batch,),
        in_specs=[pl.BlockSpec((MLA_HEADS, tq, MLA_QK_PAD), lambda b: (0, b, 0)),
                  pl.BlockSpec((MLA_HEADS, tk, MLA_QK_PAD), lambda b: (0, b, 0)),
                  pl.BlockSpec((slabs, MLA_HEADS, V_EXT, slab), lambda b: (b, 0, 0, 0))],
        out_specs=pl.BlockSpec((tq, vw), lambda b: (b, 0)),
        out_shape=jax.ShapeDtypeStruct((q.shape[1], vw), BF16),
        scratch_shapes=[pltpu.VMEM((MLA_HEADS, tq, tk), F32), pltpu.VMEM((MLA_HEADS, tq, tk), BF16)],
        compiler_params=_params(("parallel",), blocks, resident),
        name="mla_attn_cached",
    )(q, k, vt)


def _rope_tables(pos):
    half = QK_ROPE // 2
    inv = ROPE_THETA ** (-jnp.arange(half, dtype=F32) / half)
    ang = pos.astype(F32)[:, None] * inv
    cos, sin = jnp.cos(ang), jnp.sin(ang)
    z = lambda n: jnp.zeros((pos.shape[0], n), F32)
    pad = V7X_LANES - QK_ROPE
    return (jnp.concatenate([cos, cos, z(pad)], axis=1),
            jnp.concatenate([-sin, z(half + pad)], axis=1),
            jnp.concatenate([z(half), sin, z(pad)], axis=1))


def _prep_mla(w_uq, w_dkv, w_ukv):
    qlora = w_uq.shape[0]
    lora = w_ukv.shape[0]
    wq = w_uq.reshape(qlora, MLA_HEADS, QK_NOPE + QK_ROPE)
    wq = jnp.pad(wq, ((0, 0), (0, 0), (0, MLA_QK_PAD - QK_NOPE - QK_ROPE)))
    w_uq_pad = wq.reshape(qlora, MLA_HEADS * MLA_QK_PAD).astype(BF16)
    w_dkv_pad = jnp.pad(w_dkv, ((0, 0), (0, V7X_LANES - QK_ROPE))).astype(BF16)
    wkv = w_ukv.reshape(lora, MLA_HEADS, QK_NOPE + V_DIM)
    w_uk = wkv[:, :, :QK_NOPE].reshape(lora, MLA_HEADS * QK_NOPE).astype(BF16)
    w_uvt = wkv[:, :, QK_NOPE:].reshape(lora, MLA_HEADS * V_DIM).T.astype(BF16)
    return w_uq_pad, w_dkv_pad, w_uk, w_uvt


def _trunk(x, batch, seq, past_len, mem_k, mem_v, s_hgrn, s_conv, ckv_past, kpe_past, p):
    depth = p["norm_g"].shape[0]
    rows = batch * seq
    pos = past_len + jnp.arange(seq, dtype=jnp.int32)
    cos, slo, shi = (jnp.tile(t, (batch, 1)) for t in _rope_tables(pos))
    hs, cs, ckvs, kpes = [], [], [], []
    for l in range(depth):
        g = lambda i: p["norm_g"][l, i][None, :]
        x = _ffn(x, g(0), p["ffn_w1"], p["ffn_w2"], g(1), l, 0)
        mem = (g(4), p["w_mem_q"][l], mem_k[l], mem_v[l], p["w_mem_o"][l], g(5))
        if l % 2 == 0:
            e = l // 2
            proj = _norm_matmul(x, g(2), p["w_in0"][e], BF16, tm_want=1024)
            oa, s_new = _hgrn(proj, p["lbs"][l][None, :], p["hgrn_gnorm"][e][None, :],
                              s_hgrn[e], batch, seq)
            x, c_new = _even_out(oa, proj, p["conv_wt"][e], s_conv[e], p["w_out0"][e],
                                 g(3), x, mem, batch, seq)
            hs.append(s_new)
            cs.append(c_new)
        else:
            o = l // 2
            q, ckv_new, kpe_new, kpad_new = _mla_proj(
                x, g(2), p["mla_w_dq"][o], p["mla_q_norm"][o][None, :], p["w_uq_pad"][o],
                p["w_dkv_pad"][o], p["mla_kv_norm"][o][None, :], cos, slo, shi)
            if past_len == 0:
                assert batch == 1
                tk = _row_tile(seq, ATTN_TILE)
                tq = tk * ATTN_Q_PER_K if seq % (tk * ATTN_Q_PER_K) == 0 else tk
                k, vt = _kv_up(ckv_new, kpad_new, p["w_uk"][o], p["w_uvt"][o], tk)
                att = _attn_causal(q, k, vt, tq, tk)
            else:
                lora = ckv_new.shape[1]
                n_valid = past_len + seq
                n_keys = -(-n_valid // V7X_LANES) * V7X_LANES
                fill = n_keys - n_valid
                ckv_all = jnp.concatenate(
                    [ckv_past[o], ckv_new.reshape(batch, seq, lora),
                     jnp.zeros((batch, fill, lora), F32)], axis=1)
                kpad_past = jnp.pad(kpe_past[o], ((0, 0), (0, 0), (0, V7X_LANES - QK_ROPE)))
                kpad_all = jnp.concatenate(
                    [kpad_past.astype(BF16), kpad_new.reshape(batch, seq, V7X_LANES),
                     jnp.zeros((batch, fill, V7X_LANES), BF16)], axis=1)
                k, vt = _kv_up(ckv_all.reshape(batch * n_keys, lora),
                               kpad_all.reshape(batch * n_keys, V7X_LANES),
                               p["w_uk"][o], p["w_uvt"][o], _row_tile(n_keys, ATTN_TILE))
                att = _attn_cached(q, k, vt, batch, past_len, n_valid)
            x = _proj_residual(att, p["mla_w_o"][o], g(3), x, mem, batch, seq)
            ckvs.append(ckv_new.reshape(batch, seq, -1))
            kpes.append(kpe_new.reshape(batch, seq, -1))
        x = _ffn(x, g(6), p["ffn_w1"], p["ffn_w2"], g(7), l, 1)
    return (x.reshape(batch, seq, -1), jnp.stack(hs), jnp.stack(cs), jnp.stack(ckvs),
            jnp.stack(kpes))


def kernel(x_prompt, x_sample, mem_prompt, state_hgrn, state_conv, cache_ckv, cache_kpe, cache_mem_k, cache_mem_v, norm_g, ffn_w1, ffn_w2, w_in0, hgrn_lb, hgrn_gnorm, conv_w, w_out0, mla_w_dq, mla_q_norm, mla_w_uq, mla_w_dkv, mla_kv_norm, mla_w_ukv, mla_w_o, mem_norm, w_mem_q, w_mem_kv, w_mem_o):
    batch, seq, d = x_prompt.shape
    dec_batch, dec_seq, _ = x_sample.shape
    past_len = cache_ckv.shape[2]
    depth = norm_g.shape[0]
    n_mem = mem_prompt.shape[1]
    mem_width = MEM_HEADS * MEM_DIM

    prepped = [_prep_mla(mla_w_uq[o], mla_w_dkv[o], mla_w_ukv[o]) for o in range(mla_w_uq.shape[0])]
    p = dict(
        norm_g=norm_g,
        ffn_w1=ffn_w1.astype(BF16), ffn_w2=ffn_w2.astype(BF16),
        w_in0=w_in0.astype(BF16), w_out0=w_out0.astype(BF16),
        lbs=jnp.cumsum(jax.nn.softmax(hgrn_lb.astype(F32), axis=0), axis=0),
        hgrn_gnorm=hgrn_gnorm,
        conv_wt=jnp.swapaxes(conv_w, 1, 2),
        mla_w_dq=mla_w_dq.astype(BF16), mla_q_norm=mla_q_norm, mla_kv_norm=mla_kv_norm,
        w_uq_pad=[t[0] for t in prepped], w_dkv_pad=[t[1] for t in prepped],
        w_uk=[t[2] for t in prepped], w_uvt=[t[3] for t in prepped],
        mla_w_o=mla_w_o.astype(BF16),
        w_mem_q=w_mem_q.astype(BF16), w_mem_o=w_mem_o.astype(BF16),
    )

    mem_rows = mem_prompt.reshape(batch * n_mem, d)
    mks, mvs = [], []
    for l in range(depth):
        kv = _norm_matmul(mem_rows, mem_norm[l][None, :], w_mem_kv[l].astype(BF16), F32,
                          tn_want=2 * mem_width)
        kv = kv.reshape(batch, n_mem, 2, mem_width)
        mks.append(kv[:, :, 0])
        mvs.append(kv[:, :, 1])
    mem_k_p = jnp.stack(mks)
    mem_v_p = jnp.stack(mvs)

    n_even = state_hgrn.shape[0]
    n_odd = cache_ckv.shape[0]
    h0 = jnp.zeros((n_even, batch) + state_hgrn.shape[2:], F32)
    c0 = jnp.zeros((n_even, batch) + state_conv.shape[2:], F32)
    y_p, p_hgrn, p_conv, p_ckv, p_kpe = _trunk(
        x_prompt.reshape(batch * seq, d), batch, seq, 0, mem_k_p, mem_v_p, h0, c0, None, None, p)
    y_s, s_hgrn, s_conv, s_ckv, s_kpe = _trunk(
        x_sample.reshape(dec_batch * dec_seq, d), dec_batch, dec_seq, past_len,
        cache_mem_k.reshape(depth, dec_batch, n_mem, mem_width),
        cache_mem_v.reshape(depth, dec_batch, n_mem, mem_width),
        state_hgrn, state_conv, cache_ckv, cache_kpe, p)

    mem_shape = (depth, batch, n_mem, MEM_HEADS, MEM_DIM)
    return (y_p, y_s, p_hgrn, p_conv, p_ckv, p_kpe,
            mem_k_p.reshape(mem_shape), mem_v_p.reshape(mem_shape),
            s_hgrn, s_conv, s_ckv, s_kpe)
```

```python
import functools

import jax
import jax.numpy as jnp
import numpy as np
from jax import lax
from jax.experimental import pallas as pl
from jax.experimental.pallas import tpu as pltpu

F32 = jnp.float32
BF16 = jnp.bfloat16

V7X_LANES = 128
V7X_SUBLANES = 8
V7X_VMEM_BYTES = 64 * 1024 * 1024
MIB = 1024 * 1024
SPILL_ROOM_BYTES = 8 * MIB
VMEM_RESERVED_BYTES = 2 * MIB

EPS = 1e-6
CHUNK = 64
HGRN_HEADS = 8
HGRN_DK = 128
HGRN_SUB = 16
MLA_HEADS = 16
QK_NOPE = 128
QK_ROPE = 64
V_DIM = 128
V_EXT = V_DIM + 16
MLA_QK_PAD = 256
MLA_SCALE = (QK_NOPE + QK_ROPE) ** -0.5
ROPE_THETA = 10000.0
MEM_HEADS = 4
MEM_DIM = 128
CONV_W = 3
NEG_BIG = -1e30
LOG2E = 1.4426950408889634

NT_DIMS = (((1,), (1,)), ((), ()))
TN_DIMS = (((0,), (0,)), ((), ()))


def _vmem_limit(pipelined_bytes, resident_bytes):
    want = 2 * pipelined_bytes + resident_bytes + SPILL_ROOM_BYTES
    return int(min(want, V7X_VMEM_BYTES - VMEM_RESERVED_BYTES))


def _params(semantics, pipelined_bytes, resident_bytes):
    return pltpu.CompilerParams(
        dimension_semantics=semantics,
        vmem_limit_bytes=_vmem_limit(pipelined_bytes, resident_bytes))


def _row_tile(rows, want):
    if rows <= want:
        return rows
    t = want - want % V7X_LANES
    while t > 0 and rows % t:
        t -= V7X_LANES
    assert t > 0, (rows, want)
    return t


ROW_CHUNK = 256


def _row_chunks(rows):
    return [slice(r0, min(r0 + ROW_CHUNK, rows)) for r0 in range(0, rows, ROW_CHUNK)]


def _rms(x, g):
    ms = jnp.mean(x * x, axis=-1, keepdims=True)
    return x * lax.rsqrt(ms + EPS) * g


def _sigmoid(x):
    return 1.0 / (1.0 + jnp.exp(-x))


def _dot(a, b):
    return jnp.dot(a, b, preferred_element_type=F32)


FFN_COL_CHUNK = 512


def _ffn_hidden(xn, w1g_ref, w1u_ref):
    gate = _dot(xn, w1g_ref[...])
    up = _dot(xn, w1u_ref[...])
    return (gate * _sigmoid(gate) * up).astype(BF16)


def _ffn_body(x_ref, gpre_ref, w1g_ref, w1u_ref, w2_ref, gpost_ref, o_ref, xn_ref):
    k = pl.program_id(1)
    last = pl.num_programs(1) - 1
    tm, d = o_ref.shape

    @pl.when(k == 0)
    def _():
        gpre = gpre_ref[...]
        for rs in _row_chunks(tm):
            xn = _rms(x_ref[rs, :], gpre).astype(BF16)
            xn_ref[rs, :] = xn
            o_ref[rs, :] = _dot(_ffn_hidden(xn, w1g_ref, w1u_ref), w2_ref[...])

    @pl.when((k > 0) & (k < last))
    def _():
        o_ref[...] += _dot(_ffn_hidden(xn_ref[...], w1g_ref, w1u_ref), w2_ref[...])

    @pl.when(k == last)
    def _():
        h = _ffn_hidden(xn_ref[...], w1g_ref, w1u_ref)
        ss = jnp.zeros((tm, 1), F32)
        for c0 in range(0, d, FFN_COL_CHUNK):
            cs = slice(c0, c0 + FFN_COL_CHUNK)
            y = o_ref[:, cs] + _dot(h, w2_ref[:, cs])
            o_ref[:, cs] = y
            ss = ss + jnp.sum(y * y, axis=-1, keepdims=True)
        scale = 0.5 * lax.rsqrt(ss * (1.0 / d) + EPS)
        o_ref[...] = x_ref[...] + o_ref[...] * scale * gpost_ref[...]


def _ffn(x, g_pre, w1, w2, g_post, layer, half, tm_want=1024, tf=512):
    rows, d = x.shape
    dff = w2.shape[-2]
    tm = _row_tile(rows, tm_want)
    nk = dff // tf
    assert dff % tf == 0 and nk >= 2 and d % FFN_COL_CHUNK == 0
    blocks = 2 * tm * d * 4 + 3 * d * tf * 2
    resident = tm * d * (2 + 4) + 4 * tm * tf * 4
    return pl.pallas_call(
        _ffn_body,
        grid=(rows // tm, nk),
        in_specs=[
            pl.BlockSpec((tm, d), lambda i, k: (i, 0)),
            pl.BlockSpec((1, d), lambda i, k: (0, 0)),
            pl.BlockSpec((None, None, d, tf), lambda i, k: (layer, half, 0, k)),
            pl.BlockSpec((None, None, d, tf), lambda i, k: (layer, half, 0, nk + k)),
            pl.BlockSpec((None, None, tf, d), lambda i, k: (layer, half, k, 0)),
            pl.BlockSpec((1, d), lambda i, k: (0, 0)),
        ],
        out_specs=pl.BlockSpec((tm, d), lambda i, k: (i, 0)),
        out_shape=jax.ShapeDtypeStruct((rows, d), F32),
        scratch_shapes=[pltpu.VMEM((tm, d), BF16)],
        compiler_params=_params(("parallel", "arbitrary"), blocks, resident),
        name="ffn",
    )(x, g_pre, w1, w1, w2, g_post)


def _norm_matmul_body(x_ref, g_ref, w_ref, o_ref, xn_ref):
    j = pl.program_id(1)

    @pl.when(j == 0)
    def _():
        g = g_ref[...]
        for rs in _row_chunks(x_ref.shape[0]):
            xn = _rms(x_ref[rs, :], g).astype(BF16)
            xn_ref[rs, :] = xn
            o_ref[rs, :] = _dot(xn, w_ref[...]).astype(o_ref.dtype)

    @pl.when(j > 0)
    def _():
        o_ref[...] = _dot(xn_ref[...], w_ref[...]).astype(o_ref.dtype)


def _norm_matmul(x, g, w, out_dtype, tm_want=512, tn_want=1024):
    rows, d = x.shape
    n = w.shape[1]
    tm = _row_tile(rows, tm_want)
    tn = _row_tile(n, tn_want)
    blocks = tm * d * 4 + d * tn * 2 + tm * tn * 4
    resident = tm * d * 2 + tm * tn * 4
    return pl.pallas_call(
        _norm_matmul_body,
        grid=(rows // tm, n // tn),
        in_specs=[
            pl.BlockSpec((tm, d), lambda i, j: (i, 0)),
            pl.BlockSpec((1, d), lambda i, j: (0, 0)),
            pl.BlockSpec((d, tn), lambda i, j: (0, j)),
        ],
        out_specs=pl.BlockSpec((tm, tn), lambda i, j: (i, j)),
        out_shape=jax.ShapeDtypeStruct((rows, n), out_dtype),
        scratch_shapes=[pltpu.VMEM((tm, d), BF16)],
        compiler_params=_params(("parallel", "arbitrary"), blocks, resident),
        name="norm_matmul",
    )(x, g, w)


def _mem_sublayer(x, gpre_ref, wq_ref, mk_ref, mv_ref, wo_ref, gpost_ref):
    q = _dot(_rms(x, gpre_ref[...]).astype(BF16), wq_ref[...])
    outs = []
    for h in range(MEM_HEADS):
        hs = slice(h * MEM_DIM, (h + 1) * MEM_DIM)
        kh = mk_ref[:, hs].astype(BF16)
        vh = mv_ref[:, hs].astype(BF16)
        s = lax.dot_general(q[:, hs].astype(BF16), kh, NT_DIMS,
                            preferred_element_type=F32) * (MEM_DIM ** -0.5)
        p = jnp.exp(s - jnp.max(s, axis=1, keepdims=True))
        p = p * (1.0 / jnp.sum(p, axis=1, keepdims=True))
        outs.append(_dot(p.astype(BF16), vh).astype(BF16))
    o = jnp.concatenate(outs, axis=1)
    return x + _rms(_dot(o, wo_ref[...]), gpost_ref[...])


def _const_spec(a):
    return pl.BlockSpec(a.shape, lambda *_: (0,) * a.ndim, pipeline_mode=pl.Buffered(1))


def _mem_specs(mem_k, w_q, w_o, g_pre, g_post, layer):
    n_mem, width = mem_k.shape[2], mem_k.shape[3]
    mem = pl.BlockSpec((None, None, n_mem, width), lambda b, t: (layer, b, 0, 0))
    specs = [_const_spec(g_pre), _const_spec(w_q), mem, mem, _const_spec(w_o),
             _const_spec(g_post)]
    nbytes = (w_q.size + w_o.size) * 2 + 4 * n_mem * width * 4
    return specs, nbytes


def _proj_residual_body(a_ref, w_ref, g_ref, x_ref,
                        gpre_ref, wq_ref, mk_ref, mv_ref, wo_ref, gpost_ref, o_ref):
    x1 = x_ref[...] + _rms(_dot(a_ref[...], w_ref[...]), g_ref[...])
    o_ref[...] = _mem_sublayer(x1, gpre_ref, wq_ref, mk_ref, mv_ref, wo_ref, gpost_ref)


def _proj_residual(a, w, g, x, mem, batch, seq, tm_want=512):
    g_pre, w_q, mem_k, mem_v, w_o, g_post, layer = mem
    kdim = a.shape[1]
    d = w.shape[1]
    tm = _row_tile(seq, tm_want)
    nt = seq // tm
    row = lambda n: pl.BlockSpec((tm, n), lambda b, t: (b * nt + t, 0))
    mem_specs, mem_bytes = _mem_specs(mem_k, w_q, w_o, g_pre, g_post, layer)
    blocks = tm * kdim * 2 + 2 * tm * d * 4
    resident = kdim * d * 2 + mem_bytes + 4 * tm * d * 4
    return pl.pallas_call(
        _proj_residual_body,
        grid=(batch, nt),
        in_specs=[row(kdim), _const_spec(w), _const_spec(g), row(d)] + mem_specs,
        out_specs=row(d),
        out_shape=jax.ShapeDtypeStruct(x.shape, F32),
        compiler_params=_params(("parallel", "parallel"), blocks, resident),
        name="proj_residual_mem",
    )(a, w, g, x, g_pre, w_q, mem_k, mem_v, w_o, g_post)


def _hgrn_body(qa_ref, fa_ref, ia_ref, ga_ref, lb_ref, gn_ref, s0_ref,
               oa_ref, sout_ref, st_ref, o_ref):
    c = pl.program_id(1)
    rows = qa_ref.shape[0]
    width = qa_ref.shape[1]
    nsub = rows // HGRN_SUB

    @pl.when(c == 0)
    def _():
        for h in range(HGRN_HEADS):
            st_ref[h] = s0_ref[h].T

    lb = lb_ref[...]
    qa = qa_ref[...].astype(F32)
    q = qa * _sigmoid(qa) * (HGRN_DK ** -0.5)
    f = lb + (1.0 - lb) * _sigmoid(fa_ref[...].astype(F32))
    k = 1.0 - f
    g = jnp.log(f) * LOG2E
    v = ia_ref[...].astype(F32)

    r_i = lax.broadcasted_iota(jnp.int32, (rows, rows), 0)
    c_i = lax.broadcasted_iota(jnp.int32, (rows, rows), 1)
    tri = ((c_i <= r_i) & ((c_i // HGRN_SUB) == (r_i // HGRN_SUB))).astype(BF16)
    g1 = g.astype(BF16)
    rem = g - g1.astype(F32)
    g2 = rem.astype(BF16)
    g3 = (rem - g2.astype(F32)).astype(BF16)
    bl = _dot(tri, g1) + _dot(tri, g2) + _dot(tri, g3)

    qe = (q * jnp.exp2(bl)).astype(BF16)
    row8 = lax.broadcasted_iota(jnp.int32, (V7X_SUBLANES, width), 0)
    tiles = HGRN_SUB // V7X_SUBLANES

    for i in range(nsub):
        r0 = i * HGRN_SUB
        bl_i = bl[r0:r0 + HGRN_SUB]
        q_i = q[r0:r0 + HGRN_SUB]
        k_i = k[r0:r0 + HGRN_SUB]
        v_i = v[r0:r0 + HGRN_SUB]
        b_end = bl_i[HGRN_SUB - 1:HGRN_SUB]
        ke_i = (k_i * jnp.exp2(b_end - bl_i)).astype(BF16)
        dec_i = jnp.exp2(b_end)
        v_bf = v_i.astype(BF16)

        diag = [[jnp.zeros((V7X_SUBLANES, HGRN_DK), F32) for _ in range(tiles)]
                for _ in range(HGRN_HEADS)]
        for s in range(HGRN_SUB):
            ks, bs, vs = k_i[s:s + 1], bl_i[s:s + 1], v_i[s:s + 1]
            for rt in range(s // V7X_SUBLANES, tiles):
                rsl = slice(rt * V7X_SUBLANES, (rt + 1) * V7X_SUBLANES)
                rel = bl_i[rsl] - bs
                if rt == s // V7X_SUBLANES:
                    rel = jnp.where(row8 >= s % V7X_SUBLANES, rel, -jnp.inf)
                w = q_i[rsl] * ks * jnp.exp2(rel)
                for h in range(HGRN_HEADS):
                    hs = slice(h * HGRN_DK, (h + 1) * HGRN_DK)
                    col = jnp.sum(w[:, hs], axis=1, keepdims=True)
                    diag[h][rt] = diag[h][rt] + col * vs[:, hs]

        for h in range(HGRN_HEADS):
            hs = slice(h * HGRN_DK, (h + 1) * HGRN_DK)
            st = st_ref[h]
            inter = lax.dot_general(qe[r0:r0 + HGRN_SUB, hs], st.astype(BF16), NT_DIMS,
                                    preferred_element_type=F32)
            upd = lax.dot_general(v_bf[:, hs], ke_i[:, hs], TN_DIMS,
                                  preferred_element_type=F32)
            st_ref[h] = st * dec_i[:, hs] + upd
            o_ref[r0:r0 + HGRN_SUB, hs] = inter + jnp.concatenate(diag[h], axis=0)

    gn = gn_ref[...]
    for h in range(HGRN_HEADS):
        hs = slice(h * HGRN_DK, (h + 1) * HGRN_DK)
        ga = ga_ref[:, hs].astype(F32)
        oa_ref[:, hs] = (_rms(o_ref[:, hs], gn) * (ga * _sigmoid(ga))).astype(oa_ref.dtype)

    @pl.when(c == pl.num_programs(1) - 1)
    def _():
        for h in range(HGRN_HEADS):
            sout_ref[h] = st_ref[h].T


def _hgrn(proj, lb, gnorm, s0, batch, seq):
    width = HGRN_HEADS * HGRN_DK
    rows = min(CHUNK, seq)
    nc = seq // rows
    assert seq % rows == 0 and rows % HGRN_SUB == 0
    col = lambda j: pl.BlockSpec((rows, width), lambda b, c: (b * nc + c, j))
    state_spec = pl.BlockSpec((None, HGRN_HEADS, HGRN_DK, HGRN_DK), lambda b, c: (b, 0, 0, 0))
    blocks = 4 * rows * width * 4 + rows * width * 2 + 2 * HGRN_HEADS * HGRN_DK * HGRN_DK * 4
    resident = HGRN_HEADS * HGRN_DK * HGRN_DK * 4 + 16 * rows * width * 4
    return pl.pallas_call(
        _hgrn_body,
        grid=(batch, nc),
        in_specs=[col(0), col(1), col(2), col(3),
                  pl.BlockSpec((1, width), lambda b, c: (0, 0)),
                  pl.BlockSpec((1, HGRN_DK), lambda b, c: (0, 0)),
                  state_spec],
        out_specs=[pl.BlockSpec((rows, width), lambda b, c: (b * nc + c, 0)), state_spec],
        out_shape=[jax.ShapeDtypeStruct((batch * seq, width), BF16),
                   jax.ShapeDtypeStruct(s0.shape, F32)],
        scratch_shapes=[pltpu.VMEM((HGRN_HEADS, HGRN_DK, HGRN_DK), F32),
                        pltpu.VMEM((rows, width), F32)],
        compiler_params=_params(("parallel", "arbitrary"), blocks, resident),
        name="hgrn2",
    )(proj, proj, proj, proj, lb, gnorm, s0)


CONV_PAD = 8


def _even_out_body(oa_ref, bg_ref, cg_ref, hb_ref, cw_ref, cs_ref, w_ref, g_ref, x_ref,
                   gpre_ref, wq_ref, mk_ref, mv_ref, wo_ref, gpost_ref,
                   o_ref, cnew_ref, ubuf_ref):
    t = pl.program_id(1)
    tm = oa_ref.shape[0]
    half = oa_ref.shape[1]
    lo = CONV_PAD - (CONV_W - 1)

    @pl.when(t == 0)
    def _():
        ubuf_ref[lo:CONV_PAD, :] = cs_ref[...]

    u = cg_ref[...].astype(F32) * hb_ref[...].astype(F32)
    ubuf_ref[CONV_PAD:CONV_PAD + tm, :] = u
    yb = u * cw_ref[CONV_W - 1:CONV_W, :]
    for j in range(CONV_W - 1):
        yb = yb + ubuf_ref[lo + j:lo + j + tm, :] * cw_ref[j:j + 1, :]
    ob = (bg_ref[...].astype(F32) * yb).astype(BF16)
    y = _dot(oa_ref[...], w_ref[0:half, :]) + _dot(ob, w_ref[half:2 * half, :])
    x1 = x_ref[...] + _rms(y, g_ref[...])
    o_ref[...] = _mem_sublayer(x1, gpre_ref, wq_ref, mk_ref, mv_ref, wo_ref, gpost_ref)

    tail = ubuf_ref[lo + tm:CONV_PAD + tm, :]
    ubuf_ref[lo:CONV_PAD, :] = tail

    @pl.when(t == pl.num_programs(1) - 1)
    def _():
        cnew_ref[...] = tail


def _even_out(oa, proj, conv_wt, conv_state, w_out, g, x, mem, batch, seq, tm_want=512):
    g_pre, w_q, mem_k, mem_v, w_o, g_post, layer = mem
    half = oa.shape[1]
    d = x.shape[1]
    tm = _row_tile(seq, tm_want)
    nt = seq // tm
    row = lambda b, t: (b * nt + t, 0)
    pcol = lambda j: pl.BlockSpec((tm, half), lambda b, t: (b * nt + t, j))
    cstate = pl.BlockSpec((None, CONV_W - 1, half), lambda b, t: (b, 0, 0))
    mem_specs, mem_bytes = _mem_specs(mem_k, w_q, w_o, g_pre, g_post, layer)
    blocks = tm * half * 4 * proj.dtype.itemsize + 2 * tm * d * 4
    resident = (2 * half * d * 2 + mem_bytes + (tm + CONV_PAD) * half * 4 + 4 * tm * half * 4
                + 4 * tm * d * 4)
    return pl.pallas_call(
        _even_out_body,
        grid=(batch, nt),
        in_specs=[pl.BlockSpec((tm, half), row), pcol(4), pcol(5), pcol(6),
                  _const_spec(conv_wt), cstate, _const_spec(w_out), _const_spec(g),
                  pl.BlockSpec((tm, d), row)] + mem_specs,
        out_specs=[pl.BlockSpec((tm, d), row), cstate],
        out_shape=[jax.ShapeDtypeStruct(x.shape, F32),
                   jax.ShapeDtypeStruct(conv_state.shape, F32)],
        scratch_shapes=[pltpu.VMEM((tm + CONV_PAD, half), F32)],
        compiler_params=_params(("parallel", "arbitrary"), blocks, resident),
        name="even_out_mem",
    )(oa, proj, proj, proj, conv_wt, conv_state, w_out, g, x,
      g_pre, w_q, mem_k, mem_v, w_o, g_post)


Q_PRESCALE = MLA_SCALE * LOG2E


def _rope(x, cos, sin_lo, sin_hi):
    half = QK_ROPE // 2
    return (x * cos
            + pltpu.roll(x, V7X_LANES - half, axis=1) * sin_lo
            + pltpu.roll(x, half, axis=1) * sin_hi)


def _mla_proj_body(x_ref, g_ref, wdq_ref, qn_ref, wuq_ref, wdkv_ref, kvn_ref,
                   cos_ref, slo_ref, shi_ref, q_ref, ckv_ref, kpe_ref, kpad_ref):
    cos, slo, shi = cos_ref[...], slo_ref[...], shi_ref[...]
    lora = ckv_ref.shape[1]
    xn = _rms(x_ref[...], g_ref[...]).astype(BF16)
    cq = _rms(_dot(xn, wdq_ref[...]), qn_ref[...]).astype(BF16)
    ckr = _dot(xn, wdkv_ref[...])
    ckv_ref[...] = _rms(ckr[:, :lora], kvn_ref[...])
    kp = _rope(ckr[:, lora:lora + V7X_LANES], cos, slo, shi)
    kpe_ref[...] = kp[:, :QK_ROPE]
    kpad_ref[...] = kp.astype(BF16)
    q = _dot(cq, wuq_ref[...]) * Q_PRESCALE
    for h in range(MLA_HEADS):
        c0 = h * MLA_QK_PAD
        q_ref[h, :, 0:QK_NOPE] = q[:, c0:c0 + QK_NOPE].astype(BF16)
        q_ref[h, :, QK_NOPE:MLA_QK_PAD] = _rope(
            q[:, c0 + QK_NOPE:c0 + MLA_QK_PAD], cos, slo, shi).astype(BF16)


def _mla_proj(x, g, w_dq, q_norm, w_uq_pad, w_dkv_pad, kv_norm, cos, slo, shi, tm_want=512):
    rows, d = x.shape
    qlora = w_dq.shape[1]
    lora = kv_norm.shape[1]
    qw = w_uq_pad.shape[1]
    kw = w_dkv_pad.shape[1]
    tm = _row_tile(rows, tm_want)
    full = lambda a: pl.BlockSpec(a.shape, lambda i: (0,) * a.ndim)
    rowspec = lambda n: pl.BlockSpec((tm, n), lambda i: (i, 0))
    blocks = (tm * d * 4 + (d * qlora + qlora * qw + d * kw) * 2 + 3 * tm * V7X_LANES * 4
              + tm * (qw * 2 + lora * 4 + QK_ROPE * 4 + V7X_LANES * 2))
    resident = tm * (d * 6 + qw * 4 + kw * 4 + qlora * 8)
    return pl.pallas_call(
        _mla_proj_body,
        grid=(rows // tm,),
        in_specs=[rowspec(d), full(g), full(w_dq), full(q_norm), full(w_uq_pad),
                  full(w_dkv_pad), full(kv_norm),
                  rowspec(V7X_LANES), rowspec(V7X_LANES), rowspec(V7X_LANES)],
        out_specs=[pl.BlockSpec((MLA_HEADS, tm, MLA_QK_PAD), lambda i: (0, i, 0)),
                   rowspec(lora), rowspec(QK_ROPE), rowspec(V7X_LANES)],
        out_shape=[jax.ShapeDtypeStruct((MLA_HEADS, rows, MLA_QK_PAD), BF16),
                   jax.ShapeDtypeStruct((rows, lora), F32),
                   jax.ShapeDtypeStruct((rows, QK_ROPE), F32),
                   jax.ShapeDtypeStruct((rows, V7X_LANES), BF16)],
        compiler_params=_params(("parallel",), blocks, resident),
        name="mla_proj",
    )(x, g, w_dq, q_norm, w_uq_pad, w_dkv_pad, kv_norm, cos, slo, shi)


def _kv_up_body(ckv_ref, kpad_ref, wuk_ref, wuvt_ref, k_ref, vt_ref):
    c = ckv_ref[...].astype(BF16)
    kn = _dot(c, wuk_ref[...])
    vt = lax.dot_general(wuvt_ref[...], c, NT_DIMS, preferred_element_type=F32)
    ts = c.shape[0]
    vt_ref[:, 0:V_DIM, :] = vt.astype(BF16).reshape(MLA_HEADS, V_DIM, ts)
    ones_row = lax.broadcasted_iota(jnp.int32, (MLA_HEADS, V_EXT - V_DIM, ts), 1) == 0
    vt_ref[:, V_DIM:V_EXT, :] = ones_row.astype(BF16)
    kpad = kpad_ref[...]
    for h in range(MLA_HEADS):
        k_ref[h, :, 0:QK_NOPE] = kn[:, h * QK_NOPE:(h + 1) * QK_NOPE].astype(BF16)
        k_ref[h, :, QK_NOPE:MLA_QK_PAD] = kpad


def _kv_up(ckv, kpad, w_uk, w_uvt, ts):
    rows, lora = ckv.shape
    assert rows % ts == 0
    rowspec = lambda n: pl.BlockSpec((ts, n), lambda i: (i, 0))
    full = lambda a: pl.BlockSpec(a.shape, lambda i: (0,) * a.ndim)
    kw = MLA_HEADS * MLA_QK_PAD
    vw = MLA_HEADS * V_DIM
    blocks = ts * (lora * 4 + V7X_LANES * 2 + kw * 2 + vw * 2) + 2 * lora * vw * 2
    resident = ts * (kw + vw) * 4
    return pl.pallas_call(
        _kv_up_body,
        grid=(rows // ts,),
        in_specs=[rowspec(lora), rowspec(V7X_LANES), full(w_uk), full(w_uvt)],
        out_specs=[pl.BlockSpec((MLA_HEADS, ts, MLA_QK_PAD), lambda i: (0, i, 0)),
                   pl.BlockSpec((None, MLA_HEADS, V_EXT, ts), lambda i: (i, 0, 0, 0))],
        out_shape=[jax.ShapeDtypeStruct((MLA_HEADS, rows, MLA_QK_PAD), BF16),
                   jax.ShapeDtypeStruct((rows // ts, MLA_HEADS, V_EXT, ts), BF16)],
        compiler_params=_params(("parallel",), blocks, resident),
        name="kv_up",
    )(ckv, kpad, w_uk, w_uvt)


ATTN_TILE = 512
ATTN_Q_PER_K = 1
ATTN_SCORE_BUFFERS = 3


def _scores_t(h, q_ref, k_ref, s_ref, smax_ref, mask):
    st = lax.dot_general(k_ref[h], q_ref[h], NT_DIMS, preferred_element_type=F32)
    if mask is not None:
        st = jnp.where(mask, st, -jnp.inf)
    s_ref[h % s_ref.shape[0]] = st
    smax_ref[h] = jnp.max(st, axis=0, keepdims=True)


def _softmax_pv_t(h, vt_ref, m_ref, acc_ref, s_ref, smax_ref):
    m_prev = m_ref[h]
    m_next = jnp.maximum(m_prev, smax_ref[h])
    alpha = jnp.exp2(m_prev - m_next)
    p = jnp.exp2(s_ref[h % s_ref.shape[0]] - m_next)
    m_ref[h] = m_next
    acc_ref[h] = alpha * acc_ref[h] + _dot(vt_ref[h], p.astype(BF16))


def _attn_block_t(q_ref, k_ref, vt_ref, m_ref, acc_ref, s_ref, smax_ref, mask):
    ahead = s_ref.shape[0] - 1
    for h in range(ahead):
        _scores_t(h, q_ref, k_ref, s_ref, smax_ref, mask)
    for h in range(MLA_HEADS):
        if h + ahead < MLA_HEADS:
            _scores_t(h + ahead, q_ref, k_ref, s_ref, smax_ref, mask)
        _softmax_pv_t(h, vt_ref, m_ref, acc_ref, s_ref, smax_ref)


def _attn_causal_body(qi_ref, kj_ref, q_ref, k_ref, vt_ref, o_ref, m_ref, acc_ref, s_ref,
                      smax_ref):
    step = pl.program_id(0)
    qi = qi_ref[step]
    kj = kj_ref[step]
    tq, tk = q_ref.shape[1], k_ref.shape[1]
    per_q = tq // tk

    @pl.when(kj == 0)
    def _():
        m_ref[...] = jnp.full(m_ref.shape, NEG_BIG, F32)
        acc_ref[...] = jnp.zeros_like(acc_ref)

    @pl.when(kj < qi * per_q)
    def _():
        _attn_block_t(q_ref, k_ref, vt_ref, m_ref, acc_ref, s_ref, smax_ref, None)

    @pl.when(kj >= qi * per_q)
    def _():
        key_chunk = (lax.broadcasted_iota(jnp.int32, (tk, tq), 0) + kj * tk) // CHUNK
        query_chunk = (lax.broadcasted_iota(jnp.int32, (tk, tq), 1) + qi * tq) // CHUNK
        _attn_block_t(q_ref, k_ref, vt_ref, m_ref, acc_ref, s_ref, smax_ref,
                      key_chunk <= query_chunk)

    @pl.when(kj == qi * per_q + per_q - 1)
    def _():
        for h in range(MLA_HEADS):
            out = acc_ref[h, 0:V_DIM, :] / acc_ref[h, V_DIM:V_DIM + 1, :]
            o_ref[:, h * V_DIM:(h + 1) * V_DIM] = out.T.astype(o_ref.dtype)


def _attn_causal(q, k, vt, tq, tk):
    rows = q.shape[1]
    assert tk % CHUNK == 0 and tq % tk == 0 and rows % tq == 0 and vt.shape[3] == tk
    per_q = tq // tk
    pairs = [(i, j) for i in range(rows // tq) for j in range(per_q * (i + 1))]
    qi = jnp.asarray(np.array([a for a, _ in pairs], np.int32))
    kj = jnp.asarray(np.array([b for _, b in pairs], np.int32))
    vw = MLA_HEADS * V_DIM
    blocks = MLA_HEADS * (tq * MLA_QK_PAD + tk * (MLA_QK_PAD + V_EXT)) * 2 + tq * vw * 2
    resident = (MLA_HEADS * (V_EXT + 8) * tq + (ATTN_SCORE_BUFFERS + 1) * tk * tq) * 4
    grid_spec = pltpu.PrefetchScalarGridSpec(
        num_scalar_prefetch=2,
        grid=(len(pairs),),
        in_specs=[pl.BlockSpec((MLA_HEADS, tq, MLA_QK_PAD), lambda p, qi, kj: (0, qi[p], 0)),
                  pl.BlockSpec((MLA_HEADS, tk, MLA_QK_PAD), lambda p, qi, kj: (0, kj[p], 0)),
                  pl.BlockSpec((None, MLA_HEADS, V_EXT, tk), lambda p, qi, kj: (kj[p], 0, 0, 0))],
        out_specs=pl.BlockSpec((tq, vw), lambda p, qi, kj: (qi[p], 0)),
        scratch_shapes=[pltpu.VMEM((MLA_HEADS, 1, tq), F32),
                        pltpu.VMEM((MLA_HEADS, V_EXT, tq), F32),
                        pltpu.VMEM((ATTN_SCORE_BUFFERS, tk, tq), F32),
                        pltpu.VMEM((MLA_HEADS, 1, tq), F32)])
    return pl.pallas_call(
        _attn_causal_body,
        grid_spec=grid_spec,
        out_shape=jax.ShapeDtypeStruct((rows, vw), BF16),
        compiler_params=_params(("arbitrary",), blocks, resident),
        name="mla_attn_causal",
    )(qi, kj, q, k, vt)


def _attn_cached_body(q_ref, k_ref, vt_ref, o_ref, s_ref, p_ref, *, q_pos0, n_valid):
    tq, tk = q_ref.shape[1], k_ref.shape[1]
    query_chunk = (lax.broadcasted_iota(jnp.int32, (tq, tk), 0) + q_pos0) // CHUNK
    kpos = lax.broadcasted_iota(jnp.int32, (tq, tk), 1)
    mask = (kpos // CHUNK <= query_chunk) & (kpos < n_valid)
    slab = vt_ref.shape[3]
    for h in range(MLA_HEADS):
        s_ref[h] = lax.dot_general(q_ref[h], k_ref[h], NT_DIMS, preferred_element_type=F32)
    s = jnp.where(mask[None], s_ref[...], -jnp.inf)
    p = jnp.exp2(s - jnp.max(s, axis=2, keepdims=True))
    inv_l = 1.0 / jnp.sum(p, axis=2, keepdims=True)
    p_ref[...] = p.astype(BF16)
    for h in range(MLA_HEADS):
        o = jnp.zeros((tq, V_DIM), F32)
        for j in range(vt_ref.shape[0]):
            o = o + lax.dot_general(p_ref[h, :, j * slab:(j + 1) * slab],
                                    vt_ref[j, h, 0:V_DIM, :], NT_DIMS,
                                    preferred_element_type=F32)
        o_ref[:, h * V_DIM:(h + 1) * V_DIM] = (o * inv_l[h]).astype(o_ref.dtype)


def _attn_cached(q, k, vt, batch, q_pos0, n_valid):
    tq = q.shape[1] // batch
    tk = k.shape[1] // batch
    slabs = vt.shape[0] // batch
    slab = vt.shape[3]
    assert slabs * slab == tk
    vw = MLA_HEADS * V_DIM
    blocks = MLA_HEADS * ((tq + tk) * MLA_QK_PAD + V_EXT * tk) * 2 + tq * vw * 2
    resident = 4 * MLA_HEADS * tq * tk * 4
    return pl.pallas_call(
        functools.partial(_attn_cached_body, q_pos0=q_pos0, n_valid=n_valid),
        grid=(batch,),
        in_specs=[pl.BlockSpec((MLA_HEADS, tq, MLA_QK_PAD), lambda b: (0, b, 0)),
                  pl.BlockSpec((MLA_HEADS, tk, MLA_QK_PAD), lambda b: (0, b, 0)),
                  pl.BlockSpec((slabs, MLA_HEADS, V_EXT, slab), lambda b: (b, 0, 0, 0))],
        out_specs=pl.BlockSpec((tq, vw), lambda b: (b, 0)),
        out_shape=jax.ShapeDtypeStruct((q.shape[1], vw), BF16),
        scratch_shapes=[pltpu.VMEM((MLA_HEADS, tq, tk), F32), pltpu.VMEM((MLA_HEADS, tq, tk), BF16)],
        compiler_params=_params(("parallel",), blocks, resident),
        name="mla_attn_cached",
    )(q, k, vt)


def _rope_tables(pos):
    half = QK_ROPE // 2
    inv = ROPE_THETA ** (-jnp.arange(half, dtype=F32) / half)
    ang = pos.astype(F32)[:, None] * inv
    cos, sin = jnp.cos(ang), jnp.sin(ang)
    z = lambda n: jnp.zeros((pos.shape[0], n), F32)
    pad = V7X_LANES - QK_ROPE
    return (jnp.concatenate([cos, cos, z(pad)], axis=1),
            jnp.concatenate([-sin, z(half + pad)], axis=1),
            jnp.concatenate([z(half), sin, z(pad)], axis=1))


def _prep_mla(w_uq, w_dkv, w_ukv):
    qlora = w_uq.shape[0]
    lora = w_ukv.shape[0]
    wq = w_uq.reshape(qlora, MLA_HEADS, QK_NOPE + QK_ROPE)
    wq = jnp.pad(wq, ((0, 0), (0, 0), (0, MLA_QK_PAD - QK_NOPE - QK_ROPE)))
    w_uq_pad = wq.reshape(qlora, MLA_HEADS * MLA_QK_PAD).astype(BF16)
    w_dkv_pad = jnp.pad(w_dkv, ((0, 0), (0, V7X_LANES - QK_ROPE))).astype(BF16)
    wkv = w_ukv.reshape(lora, MLA_HEADS, QK_NOPE + V_DIM)
    w_uk = wkv[:, :, :QK_NOPE].reshape(lora, MLA_HEADS * QK_NOPE).astype(BF16)
    w_uvt = wkv[:, :, QK_NOPE:].reshape(lora, MLA_HEADS * V_DIM).T.astype(BF16)
    return w_uq_pad, w_dkv_pad, w_uk, w_uvt


def _stack(arrays):
    return arrays[0][None] if len(arrays) == 1 else jnp.stack(arrays)


def _trunk(x, batch, seq, past_len, mem_k, mem_v, s_hgrn, s_conv, ckv_past, kpe_past, p):
    depth = p["norm_g"].shape[0]
    rows = batch * seq
    pos = past_len + jnp.arange(seq, dtype=jnp.int32)
    cos, slo, shi = (jnp.tile(t, (batch, 1)) for t in _rope_tables(pos))
    hs, cs, ckvs, kpes = [], [], [], []
    for l in range(depth):
        g = lambda i: p["norm_g"][l, i][None, :]
        x = _ffn(x, g(0), p["ffn_w1"], p["ffn_w2"], g(1), l, 0)
        mem = (g(4), p["w_mem_q"][l], mem_k, mem_v, p["w_mem_o"][l], g(5), l)
        if l % 2 == 0:
            e = l // 2
            proj = _norm_matmul(x, g(2), p["w_in0"][e], BF16, tm_want=1024)
            oa, s_new = _hgrn(proj, p["lbs"][l][None, :], p["hgrn_gnorm"][e][None, :],
                              s_hgrn[e], batch, seq)
            x, c_new = _even_out(oa, proj, p["conv_wt"][e], s_conv[e], p["w_out0"][e],
                                 g(3), x, mem, batch, seq)
            hs.append(s_new)
            cs.append(c_new)
        else:
            o = l // 2
            q, ckv_new, kpe_new, kpad_new = _mla_proj(
                x, g(2), p["mla_w_dq"][o], p["mla_q_norm"][o][None, :], p["w_uq_pad"][o],
                p["w_dkv_pad"][o], p["mla_kv_norm"][o][None, :], cos, slo, shi)
            if past_len == 0:
                assert batch == 1
                tk = _row_tile(seq, ATTN_TILE)
                tq = tk * ATTN_Q_PER_K if seq % (tk * ATTN_Q_PER_K) == 0 else tk
                k, vt = _kv_up(ckv_new, kpad_new, p["w_uk"][o], p["w_uvt"][o], tk)
                att = _attn_causal(q, k, vt, tq, tk)
            else:
                lora = ckv_new.shape[1]
                n_valid = past_len + seq
                n_keys = -(-n_valid // V7X_LANES) * V7X_LANES
                fill = n_keys - n_valid
                ckv_all = jnp.concatenate(
                    [ckv_past[o], ckv_new.reshape(batch, seq, lora),
                     jnp.zeros((batch, fill, lora), F32)], axis=1)
                kpad_past = jnp.pad(kpe_past[o], ((0, 0), (0, 0), (0, V7X_LANES - QK_ROPE)))
                kpad_all = jnp.concatenate(
                    [kpad_past.astype(BF16), kpad_new.reshape(batch, seq, V7X_LANES),
                     jnp.zeros((batch, fill, V7X_LANES), BF16)], axis=1)
                k, vt = _kv_up(ckv_all.reshape(batch * n_keys, lora),
                               kpad_all.reshape(batch * n_keys, V7X_LANES),
                               p["w_uk"][o], p["w_uvt"][o], _row_tile(n_keys, ATTN_TILE))
                att = _attn_cached(q, k, vt, batch, past_len, n_valid)
            x = _proj_residual(att, p["mla_w_o"][o], g(3), x, mem, batch, seq)
            ckvs.append(ckv_new.reshape(batch, seq, -1))
            kpes.append(kpe_new.reshape(batch, seq, -1))
        x = _ffn(x, g(6), p["ffn_w1"], p["ffn_w2"], g(7), l, 1)
    return (x.reshape(batch, seq, -1), _stack(hs), _stack(cs), _stack(ckvs), _stack(kpes))


def kernel(x_prompt, x_sample, mem_prompt, state_hgrn, state_conv, cache_ckv, cache_kpe, cache_mem_k, cache_mem_v, norm_g, ffn_w1, ffn_w2, w_in0, hgrn_lb, hgrn_gnorm, conv_w, w_out0, mla_w_dq, mla_q_norm, mla_w_uq, mla_w_dkv, mla_kv_norm, mla_w_ukv, mla_w_o, mem_norm, w_mem_q, w_mem_kv, w_mem_o):
    batch, seq, d = x_prompt.shape
    dec_batch, dec_seq, _ = x_sample.shape
    past_len = cache_ckv.shape[2]
    depth = norm_g.shape[0]
    n_mem = mem_prompt.shape[1]
    mem_width = MEM_HEADS * MEM_DIM

    prepped = [_prep_mla(mla_w_uq[o], mla_w_dkv[o], mla_w_ukv[o]) for o in range(mla_w_uq.shape[0])]
    p = dict(
        norm_g=norm_g,
        ffn_w1=ffn_w1.astype(BF16), ffn_w2=ffn_w2.astype(BF16),
        w_in0=w_in0.astype(BF16), w_out0=w_out0.astype(BF16),
        lbs=jnp.cumsum(jax.nn.softmax(hgrn_lb.astype(F32), axis=0), axis=0),
        hgrn_gnorm=hgrn_gnorm,
        conv_wt=jnp.swapaxes(conv_w, 1, 2),
        mla_w_dq=mla_w_dq.astype(BF16), mla_q_norm=mla_q_norm, mla_kv_norm=mla_kv_norm,
        w_uq_pad=[t[0] for t in prepped], w_dkv_pad=[t[1] for t in prepped],
        w_uk=[t[2] for t in prepped], w_uvt=[t[3] for t in prepped],
        mla_w_o=mla_w_o.astype(BF16),
        w_mem_q=w_mem_q.astype(BF16), w_mem_o=w_mem_o.astype(BF16),
    )

    mem_rows = mem_prompt.reshape(batch * n_mem, d)
    mks, mvs = [], []
    for l in range(depth):
        kv = _norm_matmul(mem_rows, mem_norm[l][None, :], w_mem_kv[l].astype(BF16), F32,
                          tn_want=2 * mem_width)
        kv = kv.reshape(batch, n_mem, 2, mem_width)
        mks.append(kv[:, :, 0])
        mvs.append(kv[:, :, 1])
    mem_k_p = jnp.stack(mks)
    mem_v_p = jnp.stack(mvs)

    n_even = state_hgrn.shape[0]
    n_odd = cache_ckv.shape[0]
    h0 = jnp.zeros((n_even, batch) + state_hgrn.shape[2:], F32)
    c0 = jnp.zeros((n_even, batch) + state_conv.shape[2:], F32)
    y_p, p_hgrn, p_conv, p_ckv, p_kpe = _trunk(
        x_prompt.reshape(batch * seq, d), batch, seq, 0, mem_k_p, mem_v_p, h0, c0, None, None, p)
    y_s, s_hgrn, s_conv, s_ckv, s_kpe = _trunk(
        x_sample.reshape(dec_batch * dec_seq, d), dec_batch, dec_seq, past_len,
        cache_mem_k.reshape(depth, dec_batch, n_mem, mem_width),
        cache_mem_v.reshape(depth, dec_batch, n_mem, mem_width),
        state_hgrn, state_conv, cache_ckv, cache_kpe, p)

    mem_shape = (depth, batch, n_mem, MEM_HEADS, MEM_DIM)
    return (y_p, y_s, p_hgrn, p_conv, p_ckv, p_kpe,
            mem_k_p.reshape(mem_shape), mem_v_p.reshape(mem_shape),
            s_hgrn, s_conv, s_ckv, s_kpe)
```

```python
import functools

import jax
import jax.numpy as jnp
import numpy as np
from jax import lax
from jax.experimental import pallas as pl
from jax.experimental.pallas import tpu as pltpu

F32 = jnp.float32
BF16 = jnp.bfloat16

V7X_LANES = 128
V7X_SUBLANES = 8
V7X_VMEM_BYTES = 64 * 1024 * 1024
MIB = 1024 * 1024
SPILL_ROOM_BYTES = 8 * MIB
VMEM_RESERVED_BYTES = 2 * MIB

EPS = 1e-6
CHUNK = 64
HGRN_HEADS = 8
HGRN_DK = 128
HGRN_SUB = 16
MLA_HEADS = 16
QK_NOPE = 128
QK_ROPE = 64
V_DIM = 128
V_EXT = V_DIM + 16
MLA_QK_PAD = 256
MLA_SCALE = (QK_NOPE + QK_ROPE) ** -0.5
ROPE_THETA = 10000.0
MEM_HEADS = 4
MEM_DIM = 128
CONV_W = 3
NEG_BIG = -1e30
LOG2E = 1.4426950408889634

NT_DIMS = (((1,), (1,)), ((), ()))
TN_DIMS = (((0,), (0,)), ((), ()))


def _vmem_limit(pipelined_bytes, resident_bytes):
    want = 2 * pipelined_bytes + resident_bytes + SPILL_ROOM_BYTES
    return int(min(want, V7X_VMEM_BYTES - VMEM_RESERVED_BYTES))


def _params(semantics, pipelined_bytes, resident_bytes):
    return pltpu.CompilerParams(
        dimension_semantics=semantics,
        vmem_limit_bytes=_vmem_limit(pipelined_bytes, resident_bytes))


def _row_tile(rows, want):
    if rows <= want:
        return rows
    t = want - want % V7X_LANES
    while t > 0 and rows % t:
        t -= V7X_LANES
    assert t > 0, (rows, want)
    return t


ROW_CHUNK = 256


def _row_chunks(rows):
    return [slice(r0, min(r0 + ROW_CHUNK, rows)) for r0 in range(0, rows, ROW_CHUNK)]


def _rms(x, g):
    ms = jnp.mean(x * x, axis=-1, keepdims=True)
    return x * lax.rsqrt(ms + EPS) * g


def _sigmoid(x):
    return 1.0 / (1.0 + jnp.exp(-x))


def _dot(a, b):
    return jnp.dot(a, b, preferred_element_type=F32)


FFN_COL_CHUNK = 512


def _ffn_hidden(xn, w1g_ref, w1u_ref):
    gate = _dot(xn, w1g_ref[...])
    up = _dot(xn, w1u_ref[...])
    return (gate * _sigmoid(gate) * up).astype(BF16)


def _ffn_body(x_ref, gpre_ref, w1g_ref, w1u_ref, w2_ref, gpost_ref, o_ref, xn_ref):
    k = pl.program_id(1)
    last = pl.num_programs(1) - 1
    tm, d = o_ref.shape

    @pl.when(k == 0)
    def _():
        gpre = gpre_ref[...]
        for rs in _row_chunks(tm):
            xn = _rms(x_ref[rs, :], gpre).astype(BF16)
            xn_ref[rs, :] = xn
            o_ref[rs, :] = _dot(_ffn_hidden(xn, w1g_ref, w1u_ref), w2_ref[...])

    @pl.when((k > 0) & (k < last))
    def _():
        o_ref[...] += _dot(_ffn_hidden(xn_ref[...], w1g_ref, w1u_ref), w2_ref[...])

    @pl.when(k == last)
    def _():
        h = _ffn_hidden(xn_ref[...], w1g_ref, w1u_ref)
        ss = jnp.zeros((tm, 1), F32)
        for c0 in range(0, d, FFN_COL_CHUNK):
            cs = slice(c0, c0 + FFN_COL_CHUNK)
            y = o_ref[:, cs] + _dot(h, w2_ref[:, cs])
            o_ref[:, cs] = y
            ss = ss + jnp.sum(y * y, axis=-1, keepdims=True)
        scale = 0.5 * lax.rsqrt(ss * (1.0 / d) + EPS)
        o_ref[...] = x_ref[...] + o_ref[...] * scale * gpost_ref[...]


def _ffn(x, g_pre, w1, w2, g_post, layer, half, tm_want=1024, tf=512):
    rows, d = x.shape
    dff = w2.shape[-2]
    tm = _row_tile(rows, tm_want)
    nk = dff // tf
    assert dff % tf == 0 and nk >= 2 and d % FFN_COL_CHUNK == 0
    blocks = 2 * tm * d * 4 + 3 * d * tf * 2
    resident = tm * d * (2 + 4) + 4 * tm * tf * 4
    return pl.pallas_call(
        _ffn_body,
        grid=(rows // tm, nk),
        in_specs=[
            pl.BlockSpec((tm, d), lambda i, k: (i, 0)),
            pl.BlockSpec((1, d), lambda i, k: (0, 0)),
            pl.BlockSpec((None, None, d, tf), lambda i, k: (layer, half, 0, k)),
            pl.BlockSpec((None, None, d, tf), lambda i, k: (layer, half, 0, nk + k)),
            pl.BlockSpec((None, None, tf, d), lambda i, k: (layer, half, k, 0)),
            pl.BlockSpec((1, d), lambda i, k: (0, 0)),
        ],
        out_specs=pl.BlockSpec((tm, d), lambda i, k: (i, 0)),
        out_shape=jax.ShapeDtypeStruct((rows, d), F32),
        scratch_shapes=[pltpu.VMEM((tm, d), BF16)],
        compiler_params=_params(("parallel", "arbitrary"), blocks, resident),
        name="ffn",
    )(x, g_pre, w1, w1, w2, g_post)


def _norm_matmul_body(x_ref, g_ref, w_ref, o_ref, xn_ref):
    j = pl.program_id(1)

    @pl.when(j == 0)
    def _():
        g = g_ref[...]
        for rs in _row_chunks(x_ref.shape[0]):
            xn = _rms(x_ref[rs, :], g).astype(BF16)
            xn_ref[rs, :] = xn
            o_ref[rs, :] = _dot(xn, w_ref[...]).astype(o_ref.dtype)

    @pl.when(j > 0)
    def _():
        o_ref[...] = _dot(xn_ref[...], w_ref[...]).astype(o_ref.dtype)


def _norm_matmul(x, g, w, out_dtype, tm_want=512, tn_want=1024):
    rows, d = x.shape
    n = w.shape[1]
    tm = _row_tile(rows, tm_want)
    tn = _row_tile(n, tn_want)
    blocks = tm * d * 4 + d * tn * 2 + tm * tn * 4
    resident = tm * d * 2 + tm * tn * 4
    return pl.pallas_call(
        _norm_matmul_body,
        grid=(rows // tm, n // tn),
        in_specs=[
            pl.BlockSpec((tm, d), lambda i, j: (i, 0)),
            pl.BlockSpec((1, d), lambda i, j: (0, 0)),
            pl.BlockSpec((d, tn), lambda i, j: (0, j)),
        ],
        out_specs=pl.BlockSpec((tm, tn), lambda i, j: (i, j)),
        out_shape=jax.ShapeDtypeStruct((rows, n), out_dtype),
        scratch_shapes=[pltpu.VMEM((tm, d), BF16)],
        compiler_params=_params(("parallel", "arbitrary"), blocks, resident),
        name="norm_matmul",
    )(x, g, w)


def _mem_sublayer(x, gpre_ref, wq_ref, mk_ref, mv_ref, wo_ref, gpost_ref):
    q = _dot(_rms(x, gpre_ref[...]).astype(BF16), wq_ref[...])
    outs = []
    for h in range(MEM_HEADS):
        hs = slice(h * MEM_DIM, (h + 1) * MEM_DIM)
        kh = mk_ref[:, hs].astype(BF16)
        vh = mv_ref[:, hs].astype(BF16)
        s = lax.dot_general(q[:, hs].astype(BF16), kh, NT_DIMS,
                            preferred_element_type=F32) * (MEM_DIM ** -0.5)
        p = jnp.exp(s - jnp.max(s, axis=1, keepdims=True))
        p = p * (1.0 / jnp.sum(p, axis=1, keepdims=True))
        outs.append(_dot(p.astype(BF16), vh).astype(BF16))
    o = jnp.concatenate(outs, axis=1)
    return x + _rms(_dot(o, wo_ref[...]), gpost_ref[...])


def _const_spec(a):
    return pl.BlockSpec(a.shape, lambda *_: (0,) * a.ndim, pipeline_mode=pl.Buffered(1))


def _mem_specs(mem_k, w_q, w_o, g_pre, g_post, layer):
    n_mem, width = mem_k.shape[2], mem_k.shape[3]
    mem = pl.BlockSpec((None, None, n_mem, width), lambda b, t: (layer, b, 0, 0))
    specs = [_const_spec(g_pre), _const_spec(w_q), mem, mem, _const_spec(w_o),
             _const_spec(g_post)]
    nbytes = (w_q.size + w_o.size) * 2 + 4 * n_mem * width * 4
    return specs, nbytes


def _proj_residual_body(a_ref, w_ref, g_ref, x_ref,
                        gpre_ref, wq_ref, mk_ref, mv_ref, wo_ref, gpost_ref, o_ref):
    x1 = x_ref[...] + _rms(_dot(a_ref[...], w_ref[...]), g_ref[...])
    o_ref[...] = _mem_sublayer(x1, gpre_ref, wq_ref, mk_ref, mv_ref, wo_ref, gpost_ref)


def _proj_residual(a, w, g, x, mem, batch, seq, tm_want=512):
    g_pre, w_q, mem_k, mem_v, w_o, g_post, layer = mem
    kdim = a.shape[1]
    d = w.shape[1]
    tm = _row_tile(seq, tm_want)
    nt = seq // tm
    row = lambda n: pl.BlockSpec((tm, n), lambda b, t: (b * nt + t, 0))
    mem_specs, mem_bytes = _mem_specs(mem_k, w_q, w_o, g_pre, g_post, layer)
    blocks = tm * kdim * 2 + 2 * tm * d * 4
    resident = kdim * d * 2 + mem_bytes + 4 * tm * d * 4
    return pl.pallas_call(
        _proj_residual_body,
        grid=(batch, nt),
        in_specs=[row(kdim), _const_spec(w), _const_spec(g), row(d)] + mem_specs,
        out_specs=row(d),
        out_shape=jax.ShapeDtypeStruct(x.shape, F32),
        compiler_params=_params(("parallel", "parallel"), blocks, resident),
        name="proj_residual_mem",
    )(a, w, g, x, g_pre, w_q, mem_k, mem_v, w_o, g_post)


def _hgrn_body(qa_ref, fa_ref, ia_ref, ga_ref, lb_ref, gn_ref, s0_ref,
               oa_ref, sout_ref, st_ref, o_ref):
    c = pl.program_id(1)
    rows = qa_ref.shape[0]
    width = qa_ref.shape[1]
    nsub = rows // HGRN_SUB

    @pl.when(c == 0)
    def _():
        for h in range(HGRN_HEADS):
            st_ref[h] = s0_ref[h].T

    lb = lb_ref[...]
    qa = qa_ref[...].astype(F32)
    q = qa * _sigmoid(qa) * (HGRN_DK ** -0.5)
    f = lb + (1.0 - lb) * _sigmoid(fa_ref[...].astype(F32))
    k = 1.0 - f
    g = jnp.log(f) * LOG2E
    v = ia_ref[...].astype(F32)

    r_i = lax.broadcasted_iota(jnp.int32, (rows, rows), 0)
    c_i = lax.broadcasted_iota(jnp.int32, (rows, rows), 1)
    tri = ((c_i <= r_i) & ((c_i // HGRN_SUB) == (r_i // HGRN_SUB))).astype(BF16)
    g1 = g.astype(BF16)
    rem = g - g1.astype(F32)
    g2 = rem.astype(BF16)
    g3 = (rem - g2.astype(F32)).astype(BF16)
    bl = _dot(tri, g1) + _dot(tri, g2) + _dot(tri, g3)

    qe = (q * jnp.exp2(bl)).astype(BF16)
    row8 = lax.broadcasted_iota(jnp.int32, (V7X_SUBLANES, width), 0)
    tiles = HGRN_SUB // V7X_SUBLANES

    for i in range(nsub):
        r0 = i * HGRN_SUB
        bl_i = bl[r0:r0 + HGRN_SUB]
        q_i = q[r0:r0 + HGRN_SUB]
        k_i = k[r0:r0 + HGRN_SUB]
        v_i = v[r0:r0 + HGRN_SUB]
        b_end = bl_i[HGRN_SUB - 1:HGRN_SUB]
        ke_i = (k_i * jnp.exp2(b_end - bl_i)).astype(BF16)
        dec_i = jnp.exp2(b_end)
        v_bf = v_i.astype(BF16)

        diag = [[jnp.zeros((V7X_SUBLANES, HGRN_DK), F32) for _ in range(tiles)]
                for _ in range(HGRN_HEADS)]
        for s in range(HGRN_SUB):
            ks, bs, vs = k_i[s:s + 1], bl_i[s:s + 1], v_i[s:s + 1]
            for rt in range(s // V7X_SUBLANES, tiles):
                rsl = slice(rt * V7X_SUBLANES, (rt + 1) * V7X_SUBLANES)
                rel = bl_i[rsl] - bs
                if rt == s // V7X_SUBLANES:
                    rel = jnp.where(row8 >= s % V7X_SUBLANES, rel, -jnp.inf)
                w = q_i[rsl] * ks * jnp.exp2(rel)
                for h in range(HGRN_HEADS):
                    hs = slice(h * HGRN_DK, (h + 1) * HGRN_DK)
                    col = jnp.sum(w[:, hs], axis=1, keepdims=True)
                    diag[h][rt] = diag[h][rt] + col * vs[:, hs]

        for h in range(HGRN_HEADS):
            hs = slice(h * HGRN_DK, (h + 1) * HGRN_DK)
            st = st_ref[h]
            inter = lax.dot_general(qe[r0:r0 + HGRN_SUB, hs], st.astype(BF16), NT_DIMS,
                                    preferred_element_type=F32)
            upd = lax.dot_general(v_bf[:, hs], ke_i[:, hs], TN_DIMS,
                                  preferred_element_type=F32)
            st_ref[h] = st * dec_i[:, hs] + upd
            o_ref[r0:r0 + HGRN_SUB, hs] = inter + jnp.concatenate(diag[h], axis=0)

    gn = gn_ref[...]
    for h in range(HGRN_HEADS):
        hs = slice(h * HGRN_DK, (h + 1) * HGRN_DK)
        ga = ga_ref[:, hs].astype(F32)
        oa_ref[:, hs] = (_rms(o_ref[:, hs], gn) * (ga * _sigmoid(ga))).astype(oa_ref.dtype)

    @pl.when(c == pl.num_programs(1) - 1)
    def _():
        for h in range(HGRN_HEADS):
            sout_ref[h] = st_ref[h].T


def _hgrn(proj, lb, gnorm, s0, batch, seq):
    width = HGRN_HEADS * HGRN_DK
    rows = min(CHUNK, seq)
    nc = seq // rows
    assert seq % rows == 0 and rows % HGRN_SUB == 0
    col = lambda j: pl.BlockSpec((rows, width), lambda b, c: (b * nc + c, j))
    state_spec = pl.BlockSpec((None, HGRN_HEADS, HGRN_DK, HGRN_DK), lambda b, c: (b, 0, 0, 0))
    blocks = 4 * rows * width * 4 + rows * width * 2 + 2 * HGRN_HEADS * HGRN_DK * HGRN_DK * 4
    resident = HGRN_HEADS * HGRN_DK * HGRN_DK * 4 + 16 * rows * width * 4
    return pl.pallas_call(
        _hgrn_body,
        grid=(batch, nc),
        in_specs=[col(0), col(1), col(2), col(3),
                  pl.BlockSpec((1, width), lambda b, c: (0, 0)),
                  pl.BlockSpec((1, HGRN_DK), lambda b, c: (0, 0)),
                  state_spec],
        out_specs=[pl.BlockSpec((rows, width), lambda b, c: (b * nc + c, 0)), state_spec],
        out_shape=[jax.ShapeDtypeStruct((batch * seq, width), BF16),
                   jax.ShapeDtypeStruct(s0.shape, F32)],
        scratch_shapes=[pltpu.VMEM((HGRN_HEADS, HGRN_DK, HGRN_DK), F32),
                        pltpu.VMEM((rows, width), F32)],
        compiler_params=_params(("parallel", "arbitrary"), blocks, resident),
        name="hgrn2",
    )(proj, proj, proj, proj, lb, gnorm, s0)


CONV_PAD = 8


def _even_out_body(oa_ref, bg_ref, cg_ref, hb_ref, cw_ref, cs_ref, w_ref, g_ref, x_ref,
                   gpre_ref, wq_ref, mk_ref, mv_ref, wo_ref, gpost_ref,
                   o_ref, cnew_ref, ubuf_ref):
    t = pl.program_id(1)
    tm = oa_ref.shape[0]
    half = oa_ref.shape[1]
    lo = CONV_PAD - (CONV_W - 1)

    @pl.when(t == 0)
    def _():
        ubuf_ref[lo:CONV_PAD, :] = cs_ref[...]

    u = cg_ref[...].astype(F32) * hb_ref[...].astype(F32)
    ubuf_ref[CONV_PAD:CONV_PAD + tm, :] = u
    yb = u * cw_ref[CONV_W - 1:CONV_W, :]
    for j in range(CONV_W - 1):
        yb = yb + ubuf_ref[lo + j:lo + j + tm, :] * cw_ref[j:j + 1, :]
    ob = (bg_ref[...].astype(F32) * yb).astype(BF16)
    y = _dot(oa_ref[...], w_ref[0:half, :]) + _dot(ob, w_ref[half:2 * half, :])
    x1 = x_ref[...] + _rms(y, g_ref[...])
    o_ref[...] = _mem_sublayer(x1, gpre_ref, wq_ref, mk_ref, mv_ref, wo_ref, gpost_ref)

    tail = ubuf_ref[lo + tm:CONV_PAD + tm, :]
    ubuf_ref[lo:CONV_PAD, :] = tail

    @pl.when(t == pl.num_programs(1) - 1)
    def _():
        cnew_ref[...] = tail


def _even_out(oa, proj, conv_wt, conv_state, w_out, g, x, mem, batch, seq, tm_want=512):
    g_pre, w_q, mem_k, mem_v, w_o, g_post, layer = mem
    half = oa.shape[1]
    d = x.shape[1]
    tm = _row_tile(seq, tm_want)
    nt = seq // tm
    row = lambda b, t: (b * nt + t, 0)
    pcol = lambda j: pl.BlockSpec((tm, half), lambda b, t: (b * nt + t, j))
    cstate = pl.BlockSpec((None, CONV_W - 1, half), lambda b, t: (b, 0, 0))
    mem_specs, mem_bytes = _mem_specs(mem_k, w_q, w_o, g_pre, g_post, layer)
    blocks = tm * half * 4 * proj.dtype.itemsize + 2 * tm * d * 4
    resident = (2 * half * d * 2 + mem_bytes + (tm + CONV_PAD) * half * 4 + 4 * tm * half * 4
                + 4 * tm * d * 4)
    return pl.pallas_call(
        _even_out_body,
        grid=(batch, nt),
        in_specs=[pl.BlockSpec((tm, half), row), pcol(4), pcol(5), pcol(6),
                  _const_spec(conv_wt), cstate, _const_spec(w_out), _const_spec(g),
                  pl.BlockSpec((tm, d), row)] + mem_specs,
        out_specs=[pl.BlockSpec((tm, d), row), cstate],
        out_shape=[jax.ShapeDtypeStruct(x.shape, F32),
                   jax.ShapeDtypeStruct(conv_state.shape, F32)],
        scratch_shapes=[pltpu.VMEM((tm + CONV_PAD, half), F32)],
        compiler_params=_params(("parallel", "arbitrary"), blocks, resident),
        name="even_out_mem",
    )(oa, proj, proj, proj, conv_wt, conv_state, w_out, g, x,
      g_pre, w_q, mem_k, mem_v, w_o, g_post)


Q_PRESCALE = MLA_SCALE * LOG2E


def _rope(x, cos, sin_lo, sin_hi):
    half = QK_ROPE // 2
    return (x * cos
            + pltpu.roll(x, V7X_LANES - half, axis=1) * sin_lo
            + pltpu.roll(x, half, axis=1) * sin_hi)


def _mla_proj_body(x_ref, g_ref, wdq_ref, qn_ref, wuq_ref, wdkv_ref, kvn_ref,
                   cos_ref, slo_ref, shi_ref, q_ref, ckv_ref, kpe_ref, kpad_ref):
    cos, slo, shi = cos_ref[...], slo_ref[...], shi_ref[...]
    lora = ckv_ref.shape[1]
    xn = _rms(x_ref[...], g_ref[...]).astype(BF16)
    cq = _rms(_dot(xn, wdq_ref[...]), qn_ref[...]).astype(BF16)
    ckr = _dot(xn, wdkv_ref[...])
    ckv_ref[...] = _rms(ckr[:, :lora], kvn_ref[...])
    kp = _rope(ckr[:, lora:lora + V7X_LANES], cos, slo, shi)
    kpe_ref[...] = kp[:, :QK_ROPE]
    kpad_ref[...] = kp.astype(BF16)
    q = _dot(cq, wuq_ref[...]) * Q_PRESCALE
    for h in range(MLA_HEADS):
        c0 = h * MLA_QK_PAD
        q_ref[h, :, 0:QK_NOPE] = q[:, c0:c0 + QK_NOPE].astype(BF16)
        q_ref[h, :, QK_NOPE:MLA_QK_PAD] = _rope(
            q[:, c0 + QK_NOPE:c0 + MLA_QK_PAD], cos, slo, shi).astype(BF16)


def _mla_proj(x, g, w_dq, q_norm, w_uq_pad, w_dkv_pad, kv_norm, cos, slo, shi, tm_want=512):
    rows, d = x.shape
    qlora = w_dq.shape[1]
    lora = kv_norm.shape[1]
    qw = w_uq_pad.shape[1]
    kw = w_dkv_pad.shape[1]
    tm = _row_tile(rows, tm_want)
    full = lambda a: pl.BlockSpec(a.shape, lambda i: (0,) * a.ndim)
    rowspec = lambda n: pl.BlockSpec((tm, n), lambda i: (i, 0))
    blocks = (tm * d * 4 + (d * qlora + qlora * qw + d * kw) * 2 + 3 * tm * V7X_LANES * 4
              + tm * (qw * 2 + lora * 4 + QK_ROPE * 4 + V7X_LANES * 2))
    resident = tm * (d * 6 + qw * 4 + kw * 4 + qlora * 8)
    return pl.pallas_call(
        _mla_proj_body,
        grid=(rows // tm,),
        in_specs=[rowspec(d), full(g), full(w_dq), full(q_norm), full(w_uq_pad),
                  full(w_dkv_pad), full(kv_norm),
                  rowspec(V7X_LANES), rowspec(V7X_LANES), rowspec(V7X_LANES)],
        out_specs=[pl.BlockSpec((MLA_HEADS, tm, MLA_QK_PAD), lambda i: (0, i, 0)),
                   rowspec(lora), rowspec(QK_ROPE), rowspec(V7X_LANES)],
        out_shape=[jax.ShapeDtypeStruct((MLA_HEADS, rows, MLA_QK_PAD), BF16),
                   jax.ShapeDtypeStruct((rows, lora), F32),
                   jax.ShapeDtypeStruct((rows, QK_ROPE), F32),
                   jax.ShapeDtypeStruct((rows, V7X_LANES), BF16)],
        compiler_params=_params(("parallel",), blocks, resident),
        name="mla_proj",
    )(x, g, w_dq, q_norm, w_uq_pad, w_dkv_pad, kv_norm, cos, slo, shi)


def _kv_up_body(ckv_ref, kpad_ref, wuk_ref, wuvt_ref, k_ref, vt_ref):
    c = ckv_ref[...].astype(BF16)
    kn = _dot(c, wuk_ref[...])
    vt = lax.dot_general(wuvt_ref[...], c, NT_DIMS, preferred_element_type=F32)
    ts = c.shape[0]
    vt_ref[:, 0:V_DIM, :] = vt.astype(BF16).reshape(MLA_HEADS, V_DIM, ts)
    ones_row = lax.broadcasted_iota(jnp.int32, (MLA_HEADS, V_EXT - V_DIM, ts), 1) == 0
    vt_ref[:, V_DIM:V_EXT, :] = ones_row.astype(BF16)
    kpad = kpad_ref[...]
    for h in range(MLA_HEADS):
        k_ref[h, :, 0:QK_NOPE] = kn[:, h * QK_NOPE:(h + 1) * QK_NOPE].astype(BF16)
        k_ref[h, :, QK_NOPE:MLA_QK_PAD] = kpad


def _kv_up(ckv, kpad, w_uk, w_uvt, ts):
    rows, lora = ckv.shape
    assert rows % ts == 0
    rowspec = lambda n: pl.BlockSpec((ts, n), lambda i: (i, 0))
    full = lambda a: pl.BlockSpec(a.shape, lambda i: (0,) * a.ndim)
    kw = MLA_HEADS * MLA_QK_PAD
    vw = MLA_HEADS * V_DIM
    blocks = ts * (lora * 4 + V7X_LANES * 2 + kw * 2 + vw * 2) + 2 * lora * vw * 2
    resident = ts * (kw + vw) * 4
    return pl.pallas_call(
        _kv_up_body,
        grid=(rows // ts,),
        in_specs=[rowspec(lora), rowspec(V7X_LANES), full(w_uk), full(w_uvt)],
        out_specs=[pl.BlockSpec((MLA_HEADS, ts, MLA_QK_PAD), lambda i: (0, i, 0)),
                   pl.BlockSpec((None, MLA_HEADS, V_EXT, ts), lambda i: (i, 0, 0, 0))],
        out_shape=[jax.ShapeDtypeStruct((MLA_HEADS, rows, MLA_QK_PAD), BF16),
                   jax.ShapeDtypeStruct((rows // ts, MLA_HEADS, V_EXT, ts), BF16)],
        compiler_params=_params(("parallel",), blocks, resident),
        name="kv_up",
    )(ckv, kpad, w_uk, w_uvt)


ATTN_TILE = 512
ATTN_Q_PER_K = 1
ATTN_SCORE_BUFFERS = 3


def _scores_t(h, q_ref, k_ref):
    return lax.dot_general(k_ref[h], q_ref[h], NT_DIMS, preferred_element_type=F32)


def _softmax_pv_t(h, st, vt_ref, m_ref, acc_ref, mask):
    if mask is not None:
        st = jnp.where(mask, st, -jnp.inf)
    m_prev = m_ref[h]
    m_next = jnp.maximum(m_prev, jnp.max(st, axis=0, keepdims=True))
    alpha = jnp.exp2(m_prev - m_next)
    p = jnp.exp2(st - m_next)
    m_ref[h] = m_next
    acc_ref[h] = alpha * acc_ref[h] + _dot(vt_ref[h], p.astype(BF16))


def _attn_block_t(q_ref, k_ref, vt_ref, m_ref, acc_ref, s_ref, mask):
    nbuf = s_ref.shape[0]
    for h in range(nbuf - 1):
        s_ref[h] = _scores_t(h, q_ref, k_ref)
    for h in range(MLA_HEADS):
        ahead = h + nbuf - 1
        if ahead < MLA_HEADS:
            s_ref[ahead % nbuf] = _scores_t(ahead, q_ref, k_ref)
        _softmax_pv_t(h, s_ref[h % nbuf], vt_ref, m_ref, acc_ref, mask)


def _attn_causal_body(qi_ref, kj_ref, q_ref, k_ref, vt_ref, o_ref, m_ref, acc_ref, s_ref):
    step = pl.program_id(0)
    qi = qi_ref[step]
    kj = kj_ref[step]
    tq, tk = q_ref.shape[1], k_ref.shape[1]
    per_q = tq // tk

    @pl.when(kj == 0)
    def _():
        m_ref[...] = jnp.full(m_ref.shape, NEG_BIG, F32)
        acc_ref[...] = jnp.zeros_like(acc_ref)

    @pl.when(kj < qi * per_q)
    def _():
        _attn_block_t(q_ref, k_ref, vt_ref, m_ref, acc_ref, s_ref, None)

    @pl.when(kj >= qi * per_q)
    def _():
        key_chunk = (lax.broadcasted_iota(jnp.int32, (tk, tq), 0) + kj * tk) // CHUNK
        query_chunk = (lax.broadcasted_iota(jnp.int32, (tk, tq), 1) + qi * tq) // CHUNK
        _attn_block_t(q_ref, k_ref, vt_ref, m_ref, acc_ref, s_ref, key_chunk <= query_chunk)

    @pl.when(kj == qi * per_q + per_q - 1)
    def _():
        for h in range(MLA_HEADS):
            out = acc_ref[h, 0:V_DIM, :] / acc_ref[h, V_DIM:V_DIM + 1, :]
            o_ref[:, h * V_DIM:(h + 1) * V_DIM] = out.T.astype(o_ref.dtype)


def _attn_causal(q, k, vt, tq, tk):
    rows = q.shape[1]
    assert tk % CHUNK == 0 and tq % tk == 0 and rows % tq == 0 and vt.shape[3] == tk
    per_q = tq // tk
    pairs = [(i, j) for i in range(rows // tq) for j in range(per_q * (i + 1))]
    qi = jnp.asarray(np.array([a for a, _ in pairs], np.int32))
    kj = jnp.asarray(np.array([b for _, b in pairs], np.int32))
    vw = MLA_HEADS * V_DIM
    blocks = MLA_HEADS * (tq * MLA_QK_PAD + tk * (MLA_QK_PAD + V_EXT)) * 2 + tq * vw * 2
    resident = (MLA_HEADS * (V_EXT + 8) * tq + (ATTN_SCORE_BUFFERS + 1) * tk * tq) * 4
    grid_spec = pltpu.PrefetchScalarGridSpec(
        num_scalar_prefetch=2,
        grid=(len(pairs),),
        in_specs=[pl.BlockSpec((MLA_HEADS, tq, MLA_QK_PAD), lambda p, qi, kj: (0, qi[p], 0)),
                  pl.BlockSpec((MLA_HEADS, tk, MLA_QK_PAD), lambda p, qi, kj: (0, kj[p], 0)),
                  pl.BlockSpec((None, MLA_HEADS, V_EXT, tk), lambda p, qi, kj: (kj[p], 0, 0, 0))],
        out_specs=pl.BlockSpec((tq, vw), lambda p, qi, kj: (qi[p], 0)),
        scratch_shapes=[pltpu.VMEM((MLA_HEADS, 1, tq), F32),
                        pltpu.VMEM((MLA_HEADS, V_EXT, tq), F32),
                        pltpu.VMEM((ATTN_SCORE_BUFFERS, tk, tq), F32)])
    return pl.pallas_call(
        _attn_causal_body,
        grid_spec=grid_spec,
        out_shape=jax.ShapeDtypeStruct((rows, vw), BF16),
        compiler_params=_params(("arbitrary",), blocks, resident),
        name="mla_attn_causal",
    )(qi, kj, q, k, vt)


def _attn_latent_body(q_ref, ckvp_ref, kpep_ref, ckvn_ref, kpadn_ref, wukt_ref, wuv_ref, o_ref,
                      ckv_ref, kpe_ref, qa_ref, qp_ref, *, q_pos0):
    tq = q_ref.shape[1]
    past = ckvp_ref.shape[0]
    n_keys = ckv_ref.shape[0]
    n_valid = past + tq

    ckv_ref[0:past, :] = ckvp_ref[...].astype(BF16)
    ckv_ref[past:n_valid, :] = ckvn_ref[...].astype(BF16)
    ckv_ref[n_valid:n_keys, :] = jnp.zeros((n_keys - n_valid, ckv_ref.shape[1]), BF16)
    kpe_ref[...] = jnp.zeros_like(kpe_ref)
    kpe_ref[0:past, 0:QK_ROPE] = kpep_ref[...].astype(BF16)
    kpe_ref[past:n_valid, :] = kpadn_ref[...]

    for h in range(MLA_HEADS):
        rows = slice(h * tq, (h + 1) * tq)
        qa_ref[rows, :] = _dot(q_ref[h, :, 0:QK_NOPE], wukt_ref[h]).astype(BF16)
        qp_ref[rows, :] = q_ref[h, :, QK_NOPE:MLA_QK_PAD]

    s = (lax.dot_general(qa_ref[...], ckv_ref[...], NT_DIMS, preferred_element_type=F32)
         + lax.dot_general(qp_ref[...], kpe_ref[...], NT_DIMS, preferred_element_type=F32))
    s = s.reshape(MLA_HEADS, tq, n_keys)
    query_chunk = (lax.broadcasted_iota(jnp.int32, (tq, n_keys), 0) + q_pos0) // CHUNK
    kpos = lax.broadcasted_iota(jnp.int32, (tq, n_keys), 1)
    mask = (kpos // CHUNK <= query_chunk) & (kpos < n_valid)
    s = jnp.where(mask[None], s, -jnp.inf)
    p = jnp.exp2(s - jnp.max(s, axis=2, keepdims=True))
    inv_l = 1.0 / jnp.sum(p, axis=2, keepdims=True)
    lat = _dot(p.astype(BF16).reshape(MLA_HEADS * tq, n_keys), ckv_ref[...])
    lat = (lat.reshape(MLA_HEADS, tq, -1) * inv_l).astype(BF16)
    for h in range(MLA_HEADS):
        o_ref[:, h * V_DIM:(h + 1) * V_DIM] = _dot(lat[h], wuv_ref[h]).astype(o_ref.dtype)


def _attn_latent(q, ckv_past, kpe_past, ckv_new, kpad_new, w_ukt, w_uv, layer, batch, q_pos0):
    tq = q.shape[1] // batch
    past, lora = ckv_past.shape[2], ckv_past.shape[3]
    n_keys = -(-(past + tq) // V7X_LANES) * V7X_LANES
    vw = MLA_HEADS * V_DIM
    rows = MLA_HEADS * tq
    blocks = (MLA_HEADS * tq * MLA_QK_PAD * 2 + past * (lora + V7X_LANES) * 4
              + tq * (lora * 4 + V7X_LANES * 2) + tq * vw * 2)
    resident = (2 * MLA_HEADS * lora * QK_NOPE * 2 + n_keys * (lora + V7X_LANES) * 2
                + rows * (lora + V7X_LANES) * 2 + 4 * rows * n_keys * 4 + 2 * rows * lora * 4)
    return pl.pallas_call(
        functools.partial(_attn_latent_body, q_pos0=q_pos0),
        grid=(batch,),
        in_specs=[pl.BlockSpec((MLA_HEADS, tq, MLA_QK_PAD), lambda b: (0, b, 0)),
                  pl.BlockSpec((None, None, past, lora), lambda b: (layer, b, 0, 0)),
                  pl.BlockSpec((None, None, past, QK_ROPE), lambda b: (layer, b, 0, 0)),
                  pl.BlockSpec((tq, lora), lambda b: (b, 0)),
                  pl.BlockSpec((tq, V7X_LANES), lambda b: (b, 0)),
                  _const_spec(w_ukt), _const_spec(w_uv)],
        out_specs=pl.BlockSpec((tq, vw), lambda b: (b, 0)),
        out_shape=jax.ShapeDtypeStruct((q.shape[1], vw), BF16),
        scratch_shapes=[pltpu.VMEM((n_keys, lora), BF16), pltpu.VMEM((n_keys, V7X_LANES), BF16),
                        pltpu.VMEM((rows, lora), BF16), pltpu.VMEM((rows, V7X_LANES), BF16)],
        compiler_params=_params(("parallel",), blocks, resident),
        name="mla_attn_latent",
    )(q, ckv_past, kpe_past, ckv_new, kpad_new, w_ukt, w_uv)


def _rope_tables(pos):
    half = QK_ROPE // 2
    inv = ROPE_THETA ** (-jnp.arange(half, dtype=F32) / half)
    ang = pos.astype(F32)[:, None] * inv
    cos, sin = jnp.cos(ang), jnp.sin(ang)
    z = lambda n: jnp.zeros((pos.shape[0], n), F32)
    pad = V7X_LANES - QK_ROPE
    return (jnp.concatenate([cos, cos, z(pad)], axis=1),
            jnp.concatenate([-sin, z(half + pad)], axis=1),
            jnp.concatenate([z(half), sin, z(pad)], axis=1))


def _prep_mla(w_uq, w_dkv, w_ukv):
    qlora = w_uq.shape[0]
    lora = w_ukv.shape[0]
    wq = w_uq.reshape(qlora, MLA_HEADS, QK_NOPE + QK_ROPE)
    wq = jnp.pad(wq, ((0, 0), (0, 0), (0, MLA_QK_PAD - QK_NOPE - QK_ROPE)))
    w_uq_pad = wq.reshape(qlora, MLA_HEADS * MLA_QK_PAD).astype(BF16)
    w_dkv_pad = jnp.pad(w_dkv, ((0, 0), (0, V7X_LANES - QK_ROPE))).astype(BF16)
    wkv = w_ukv.reshape(lora, MLA_HEADS, QK_NOPE + V_DIM)
    w_uk = wkv[:, :, :QK_NOPE].reshape(lora, MLA_HEADS * QK_NOPE).astype(BF16)
    w_uvt = wkv[:, :, QK_NOPE:].reshape(lora, MLA_HEADS * V_DIM).T.astype(BF16)
    w_ukt_heads = jnp.transpose(wkv[:, :, :QK_NOPE], (1, 2, 0)).astype(BF16)
    w_uv_heads = jnp.transpose(wkv[:, :, QK_NOPE:], (1, 0, 2)).astype(BF16)
    return w_uq_pad, w_dkv_pad, w_uk, w_uvt, w_ukt_heads, w_uv_heads


def _stack(arrays):
    return arrays[0][None] if len(arrays) == 1 else jnp.stack(arrays)


def _trunk(x, batch, seq, past_len, mem_k, mem_v, s_hgrn, s_conv, ckv_past, kpe_past, p):
    depth = p["norm_g"].shape[0]
    rows = batch * seq
    pos = past_len + jnp.arange(seq, dtype=jnp.int32)
    cos, slo, shi = (jnp.tile(t, (batch, 1)) for t in _rope_tables(pos))
    hs, cs, ckvs, kpes = [], [], [], []
    for l in range(depth):
        g = lambda i: p["norm_g"][l, i][None, :]
        x = _ffn(x, g(0), p["ffn_w1"], p["ffn_w2"], g(1), l, 0)
        mem = (g(4), p["w_mem_q"][l], mem_k, mem_v, p["w_mem_o"][l], g(5), l)
        if l % 2 == 0:
            e = l // 2
            proj = _norm_matmul(x, g(2), p["w_in0"][e], BF16, tm_want=1024)
            oa, s_new = _hgrn(proj, p["lbs"][l][None, :], p["hgrn_gnorm"][e][None, :],
                              s_hgrn[e], batch, seq)
            x, c_new = _even_out(oa, proj, p["conv_wt"][e], s_conv[e], p["w_out0"][e],
                                 g(3), x, mem, batch, seq)
            hs.append(s_new)
            cs.append(c_new)
        else:
            o = l // 2
            q, ckv_new, kpe_new, kpad_new = _mla_proj(
                x, g(2), p["mla_w_dq"][o], p["mla_q_norm"][o][None, :], p["w_uq_pad"][o],
                p["w_dkv_pad"][o], p["mla_kv_norm"][o][None, :], cos, slo, shi)
            if past_len == 0:
                assert batch == 1
                tk = _row_tile(seq, ATTN_TILE)
                tq = tk * ATTN_Q_PER_K if seq % (tk * ATTN_Q_PER_K) == 0 else tk
                k, vt = _kv_up(ckv_new, kpad_new, p["w_uk"][o], p["w_uvt"][o], tk)
                att = _attn_causal(q, k, vt, tq, tk)
            else:
                att = _attn_latent(q, ckv_past, kpe_past, ckv_new, kpad_new,
                                   p["w_ukt_heads"][o], p["w_uv_heads"][o], o, batch, past_len)
            x = _proj_residual(att, p["mla_w_o"][o], g(3), x, mem, batch, seq)
            ckvs.append(ckv_new.reshape(batch, seq, -1))
            kpes.append(kpe_new.reshape(batch, seq, -1))
        x = _ffn(x, g(6), p["ffn_w1"], p["ffn_w2"], g(7), l, 1)
    return (x.reshape(batch, seq, -1), _stack(hs), _stack(cs), _stack(ckvs), _stack(kpes))


def kernel(x_prompt, x_sample, mem_prompt, state_hgrn, state_conv, cache_ckv, cache_kpe, cache_mem_k, cache_mem_v, norm_g, ffn_w1, ffn_w2, w_in0, hgrn_lb, hgrn_gnorm, conv_w, w_out0, mla_w_dq, mla_q_norm, mla_w_uq, mla_w_dkv, mla_kv_norm, mla_w_ukv, mla_w_o, mem_norm, w_mem_q, w_mem_kv, w_mem_o):
    batch, seq, d = x_prompt.shape
    dec_batch, dec_seq, _ = x_sample.shape
    past_len = cache_ckv.shape[2]
    depth = norm_g.shape[0]
    n_mem = mem_prompt.shape[1]
    mem_width = MEM_HEADS * MEM_DIM

    prepped = [_prep_mla(mla_w_uq[o], mla_w_dkv[o], mla_w_ukv[o]) for o in range(mla_w_uq.shape[0])]
    p = dict(
        norm_g=norm_g,
        ffn_w1=ffn_w1.astype(BF16), ffn_w2=ffn_w2.astype(BF16),
        w_in0=w_in0.astype(BF16), w_out0=w_out0.astype(BF16),
        lbs=jnp.cumsum(jax.nn.softmax(hgrn_lb.astype(F32), axis=0), axis=0),
        hgrn_gnorm=hgrn_gnorm,
        conv_wt=jnp.swapaxes(conv_w, 1, 2),
        mla_w_dq=mla_w_dq.astype(BF16), mla_q_norm=mla_q_norm, mla_kv_norm=mla_kv_norm,
        w_uq_pad=[t[0] for t in prepped], w_dkv_pad=[t[1] for t in prepped],
        w_uk=[t[2] for t in prepped], w_uvt=[t[3] for t in prepped],
        w_ukt_heads=[t[4] for t in prepped], w_uv_heads=[t[5] for t in prepped],
        mla_w_o=mla_w_o.astype(BF16),
        w_mem_q=w_mem_q.astype(BF16), w_mem_o=w_mem_o.astype(BF16),
    )

    mem_rows = mem_prompt.reshape(batch * n_mem, d)
    mks, mvs = [], []
    for l in range(depth):
        kv = _norm_matmul(mem_rows, mem_norm[l][None, :], w_mem_kv[l].astype(BF16), F32,
                          tn_want=2 * mem_width)
        kv = kv.reshape(batch, n_mem, 2, mem_width)
        mks.append(kv[:, :, 0])
        mvs.append(kv[:, :, 1])
    mem_k_p = jnp.stack(mks)
    mem_v_p = jnp.stack(mvs)

    n_even = state_hgrn.shape[0]
    n_odd = cache_ckv.shape[0]
    h0 = jnp.zeros((n_even, batch) + state_hgrn.shape[2:], F32)
    c0 = jnp.zeros((n_even, batch) + state_conv.shape[2:], F32)
    y_p, p_hgrn, p_conv, p_ckv, p_kpe = _trunk(
        x_prompt.reshape(batch * seq, d), batch, seq, 0, mem_k_p, mem_v_p, h0, c0, None, None, p)
    y_s, s_hgrn, s_conv, s_ckv, s_kpe = _trunk(
        x_sample.reshape(dec_batch * dec_seq, d), dec_batch, dec_seq, past_len,
        cache_mem_k.reshape(depth, dec_batch, n_mem, mem_width),
        cache_mem_v.reshape(depth, dec_batch, n_mem, mem_width),
        state_hgrn, state_conv, cache_ckv, cache_kpe, p)

    mem_shape = (depth, batch, n_mem, MEM_HEADS, MEM_DIM)
    return (y_p, y_s, p_hgrn, p_conv, p_ckv, p_kpe,
            mem_k_p.reshape(mem_shape), mem_v_p.reshape(mem_shape),
            s_hgrn, s_conv, s_ckv, s_kpe)
```

```python
import functools

import jax
import jax.numpy as jnp
import numpy as np
from jax import lax
from jax.experimental import pallas as pl
from jax.experimental.pallas import tpu as pltpu

F32 = jnp.float32
BF16 = jnp.bfloat16

V7X_LANES = 128
V7X_SUBLANES = 8
V7X_VMEM_BYTES = 64 * 1024 * 1024
MIB = 1024 * 1024
SPILL_ROOM_BYTES = 8 * MIB
VMEM_RESERVED_BYTES = 2 * MIB

EPS = 1e-6
CHUNK = 64
HGRN_HEADS = 8
HGRN_DK = 128
HGRN_SUB = 16
HGRN_BLOCK_ROWS = 256
MLA_HEADS = 16
QK_NOPE = 128
QK_ROPE = 64
V_DIM = 128
V_EXT = V_DIM + 16
MLA_QK_PAD = 256
MLA_SCALE = (QK_NOPE + QK_ROPE) ** -0.5
ROPE_THETA = 10000.0
MEM_HEADS = 4
MEM_DIM = 128
CONV_W = 3
NEG_BIG = -1e30
LOG2E = 1.4426950408889634

NT_DIMS = (((1,), (1,)), ((), ()))
TN_DIMS = (((0,), (0,)), ((), ()))


def _vmem_limit(pipelined_bytes, resident_bytes):
    want = 2 * pipelined_bytes + resident_bytes + SPILL_ROOM_BYTES
    return int(min(want, V7X_VMEM_BYTES - VMEM_RESERVED_BYTES))


def _params(semantics, pipelined_bytes, resident_bytes):
    return pltpu.CompilerParams(
        dimension_semantics=semantics,
        vmem_limit_bytes=_vmem_limit(pipelined_bytes, resident_bytes))


def _row_tile(rows, want):
    if rows <= want:
        return rows
    t = want - want % V7X_LANES
    while t > 0 and rows % t:
        t -= V7X_LANES
    assert t > 0, (rows, want)
    return t


ROW_CHUNK = 256


def _row_chunks(rows):
    return [slice(r0, min(r0 + ROW_CHUNK, rows)) for r0 in range(0, rows, ROW_CHUNK)]


def _rms(x, g):
    ms = jnp.mean(x * x, axis=-1, keepdims=True)
    return x * lax.rsqrt(ms + EPS) * g


def _sigmoid(x):
    return 1.0 / (1.0 + jnp.exp(-x))


def _dot(a, b):
    return jnp.dot(a, b, preferred_element_type=F32)


FFN_COL_CHUNK = 512


def _ffn_hidden(xn, w1g_ref, w1u_ref):
    gate = _dot(xn, w1g_ref[...])
    up = _dot(xn, w1u_ref[...])
    return (gate * _sigmoid(gate) * up).astype(BF16)


def _ffn_body(x_ref, gpre_ref, w1g_ref, w1u_ref, w2_ref, gpost_ref, o_ref, xn_ref):
    k = pl.program_id(1)
    last = pl.num_programs(1) - 1
    tm, d = o_ref.shape

    @pl.when(k == 0)
    def _():
        gpre = gpre_ref[...]
        for rs in _row_chunks(tm):
            xn = _rms(x_ref[rs, :], gpre).astype(BF16)
            xn_ref[rs, :] = xn
            o_ref[rs, :] = _dot(_ffn_hidden(xn, w1g_ref, w1u_ref), w2_ref[...])

    @pl.when((k > 0) & (k < last))
    def _():
        o_ref[...] += _dot(_ffn_hidden(xn_ref[...], w1g_ref, w1u_ref), w2_ref[...])

    @pl.when(k == last)
    def _():
        h = _ffn_hidden(xn_ref[...], w1g_ref, w1u_ref)
        ss = jnp.zeros((tm, 1), F32)
        for c0 in range(0, d, FFN_COL_CHUNK):
            cs = slice(c0, c0 + FFN_COL_CHUNK)
            y = o_ref[:, cs] + _dot(h, w2_ref[:, cs])
            o_ref[:, cs] = y
            ss = ss + jnp.sum(y * y, axis=-1, keepdims=True)
        scale = 0.5 * lax.rsqrt(ss * (1.0 / d) + EPS)
        o_ref[...] = x_ref[...] + o_ref[...] * scale * gpost_ref[...]


def _ffn(x, g_pre, w1, w2, g_post, layer, half, tm_want=1024, tf=512):
    rows, d = x.shape
    dff = w2.shape[-2]
    tm = _row_tile(rows, tm_want)
    nk = dff // tf
    assert dff % tf == 0 and nk >= 2 and d % FFN_COL_CHUNK == 0
    blocks = 2 * tm * d * 4 + 3 * d * tf * 2
    resident = tm * d * (2 + 4) + 4 * tm * tf * 4
    return pl.pallas_call(
        _ffn_body,
        grid=(rows // tm, nk),
        in_specs=[
            pl.BlockSpec((tm, d), lambda i, k: (i, 0)),
            pl.BlockSpec((1, d), lambda i, k: (0, 0)),
            pl.BlockSpec((None, None, d, tf), lambda i, k: (layer, half, 0, k)),
            pl.BlockSpec((None, None, d, tf), lambda i, k: (layer, half, 0, nk + k)),
            pl.BlockSpec((None, None, tf, d), lambda i, k: (layer, half, k, 0)),
            pl.BlockSpec((1, d), lambda i, k: (0, 0)),
        ],
        out_specs=pl.BlockSpec((tm, d), lambda i, k: (i, 0)),
        out_shape=jax.ShapeDtypeStruct((rows, d), F32),
        scratch_shapes=[pltpu.VMEM((tm, d), BF16)],
        compiler_params=_params(("parallel", "arbitrary"), blocks, resident),
        name="ffn",
    )(x, g_pre, w1, w1, w2, g_post)


def _norm_matmul_body(x_ref, g_ref, w_ref, o_ref, xn_ref):
    j = pl.program_id(1)

    @pl.when(j == 0)
    def _():
        g = g_ref[...]
        for rs in _row_chunks(x_ref.shape[0]):
            xn = _rms(x_ref[rs, :], g).astype(BF16)
            xn_ref[rs, :] = xn
            o_ref[rs, :] = _dot(xn, w_ref[...]).astype(o_ref.dtype)

    @pl.when(j > 0)
    def _():
        o_ref[...] = _dot(xn_ref[...], w_ref[...]).astype(o_ref.dtype)


def _norm_matmul(x, g, w, out_dtype, tm_want=512, tn_want=1024):
    rows, d = x.shape
    n = w.shape[1]
    tm = _row_tile(rows, tm_want)
    tn = _row_tile(n, tn_want)
    blocks = tm * d * 4 + d * tn * 2 + tm * tn * 4
    resident = tm * d * 2 + tm * tn * 4
    return pl.pallas_call(
        _norm_matmul_body,
        grid=(rows // tm, n // tn),
        in_specs=[
            pl.BlockSpec((tm, d), lambda i, j: (i, 0)),
            pl.BlockSpec((1, d), lambda i, j: (0, 0)),
            pl.BlockSpec((d, tn), lambda i, j: (0, j)),
        ],
        out_specs=pl.BlockSpec((tm, tn), lambda i, j: (i, j)),
        out_shape=jax.ShapeDtypeStruct((rows, n), out_dtype),
        scratch_shapes=[pltpu.VMEM((tm, d), BF16)],
        compiler_params=_params(("parallel", "arbitrary"), blocks, resident),
        name="norm_matmul",
    )(x, g, w)


def _mem_sublayer(x, gpre_ref, wq_ref, mk_ref, mv_ref, wo_ref, gpost_ref):
    q = _dot(_rms(x, gpre_ref[...]).astype(BF16), wq_ref[...])
    outs = []
    for h in range(MEM_HEADS):
        hs = slice(h * MEM_DIM, (h + 1) * MEM_DIM)
        kh = mk_ref[:, hs].astype(BF16)
        vh = mv_ref[:, hs].astype(BF16)
        s = lax.dot_general(q[:, hs].astype(BF16), kh, NT_DIMS,
                            preferred_element_type=F32) * (MEM_DIM ** -0.5)
        p = jnp.exp(s - jnp.max(s, axis=1, keepdims=True))
        p = p * (1.0 / jnp.sum(p, axis=1, keepdims=True))
        outs.append(_dot(p.astype(BF16), vh).astype(BF16))
    o = jnp.concatenate(outs, axis=1)
    return x + _rms(_dot(o, wo_ref[...]), gpost_ref[...])


def _const_spec(a):
    return pl.BlockSpec(a.shape, lambda *_: (0,) * a.ndim, pipeline_mode=pl.Buffered(1))


def _mem_specs(mem_k, w_q, w_o, g_pre, g_post, layer):
    n_mem, width = mem_k.shape[2], mem_k.shape[3]
    mem = pl.BlockSpec((None, None, n_mem, width), lambda b, t: (layer, b, 0, 0))
    specs = [_const_spec(g_pre), _const_spec(w_q), mem, mem, _const_spec(w_o),
             _const_spec(g_post)]
    nbytes = (w_q.size + w_o.size) * 2 + 4 * n_mem * width * 4
    return specs, nbytes


def _proj_residual_body(a_ref, w_ref, g_ref, x_ref,
                        gpre_ref, wq_ref, mk_ref, mv_ref, wo_ref, gpost_ref, o_ref):
    x1 = x_ref[...] + _rms(_dot(a_ref[...], w_ref[...]), g_ref[...])
    o_ref[...] = _mem_sublayer(x1, gpre_ref, wq_ref, mk_ref, mv_ref, wo_ref, gpost_ref)


def _proj_residual(a, w, g, x, mem, batch, seq, tm_want=512):
    g_pre, w_q, mem_k, mem_v, w_o, g_post, layer = mem
    kdim = a.shape[1]
    d = w.shape[1]
    tm = _row_tile(seq, tm_want)
    nt = seq // tm
    row = lambda n: pl.BlockSpec((tm, n), lambda b, t: (b * nt + t, 0))
    mem_specs, mem_bytes = _mem_specs(mem_k, w_q, w_o, g_pre, g_post, layer)
    blocks = tm * kdim * 2 + 2 * tm * d * 4
    resident = kdim * d * 2 + mem_bytes + 4 * tm * d * 4
    return pl.pallas_call(
        _proj_residual_body,
        grid=(batch, nt),
        in_specs=[row(kdim), _const_spec(w), _const_spec(g), row(d)] + mem_specs,
        out_specs=row(d),
        out_shape=jax.ShapeDtypeStruct(x.shape, F32),
        compiler_params=_params(("parallel", "parallel"), blocks, resident),
        name="proj_residual_mem",
    )(a, w, g, x, g_pre, w_q, mem_k, mem_v, w_o, g_post)


def _hgrn_body(qa_ref, fa_ref, ia_ref, ga_ref, lb_ref, gn_ref, s0_ref,
               oa_ref, sout_ref, st_ref, o_ref):
    c = pl.program_id(1)
    rows = qa_ref.shape[0]
    width = qa_ref.shape[1]
    nsub = rows // HGRN_SUB

    @pl.when(c == 0)
    def _():
        for h in range(HGRN_HEADS):
            st_ref[h] = s0_ref[h].T

    lb = lb_ref[...]
    qa = qa_ref[...].astype(F32)
    q = qa * _sigmoid(qa) * (HGRN_DK ** -0.5)
    f = lb + (1.0 - lb) * _sigmoid(fa_ref[...].astype(F32))
    k = 1.0 - f
    g = jnp.log(f) * LOG2E
    v = ia_ref[...].astype(F32)

    r_i = lax.broadcasted_iota(jnp.int32, (rows, rows), 0)
    c_i = lax.broadcasted_iota(jnp.int32, (rows, rows), 1)
    tri = ((c_i <= r_i) & ((c_i // HGRN_SUB) == (r_i // HGRN_SUB))).astype(BF16)
    g1 = g.astype(BF16)
    rem = g - g1.astype(F32)
    g2 = rem.astype(BF16)
    g3 = (rem - g2.astype(F32)).astype(BF16)
    bl = _dot(tri, g1) + _dot(tri, g2) + _dot(tri, g3)

    qe = (q * jnp.exp2(bl)).astype(BF16)
    row8 = lax.broadcasted_iota(jnp.int32, (V7X_SUBLANES, width), 0)
    tiles = HGRN_SUB // V7X_SUBLANES

    for i in range(nsub):
        r0 = i * HGRN_SUB
        bl_i = bl[r0:r0 + HGRN_SUB]
        q_i = q[r0:r0 + HGRN_SUB]
        k_i = k[r0:r0 + HGRN_SUB]
        v_i = v[r0:r0 + HGRN_SUB]
        b_end = bl_i[HGRN_SUB - 1:HGRN_SUB]
        ke_i = (k_i * jnp.exp2(b_end - bl_i)).astype(BF16)
        dec_i = jnp.exp2(b_end)
        v_bf = v_i.astype(BF16)

        diag = [[jnp.zeros((V7X_SUBLANES, HGRN_DK), F32) for _ in range(tiles)]
                for _ in range(HGRN_HEADS)]
        for s in range(HGRN_SUB):
            ks, bs, vs = k_i[s:s + 1], bl_i[s:s + 1], v_i[s:s + 1]
            for rt in range(s // V7X_SUBLANES, tiles):
                rsl = slice(rt * V7X_SUBLANES, (rt + 1) * V7X_SUBLANES)
                rel = bl_i[rsl] - bs
                if rt == s // V7X_SUBLANES:
                    rel = jnp.where(row8 >= s % V7X_SUBLANES, rel, -jnp.inf)
                w = q_i[rsl] * ks * jnp.exp2(rel)
                for h in range(HGRN_HEADS):
                    hs = slice(h * HGRN_DK, (h + 1) * HGRN_DK)
                    col = jnp.sum(w[:, hs], axis=1, keepdims=True)
                    diag[h][rt] = diag[h][rt] + col * vs[:, hs]

        for h in range(HGRN_HEADS):
            hs = slice(h * HGRN_DK, (h + 1) * HGRN_DK)
            st = st_ref[h]
            inter = lax.dot_general(qe[r0:r0 + HGRN_SUB, hs], st.astype(BF16), NT_DIMS,
                                    preferred_element_type=F32)
            upd = lax.dot_general(v_bf[:, hs], ke_i[:, hs], TN_DIMS,
                                  preferred_element_type=F32)
            st_ref[h] = st * dec_i[:, hs] + upd
            o_ref[r0:r0 + HGRN_SUB, hs] = inter + jnp.concatenate(diag[h], axis=0)

    gn = gn_ref[...]
    for h in range(HGRN_HEADS):
        hs = slice(h * HGRN_DK, (h + 1) * HGRN_DK)
        ga = ga_ref[:, hs].astype(F32)
        oa_ref[:, hs] = (_rms(o_ref[:, hs], gn) * (ga * _sigmoid(ga))).astype(oa_ref.dtype)

    @pl.when(c == pl.num_programs(1) - 1)
    def _():
        for h in range(HGRN_HEADS):
            sout_ref[h] = st_ref[h].T


def _hgrn(proj, lb, gnorm, s0, batch, seq):
    width = HGRN_HEADS * HGRN_DK
    rows = min(HGRN_BLOCK_ROWS, seq)
    nc = seq // rows
    assert seq % rows == 0 and rows % HGRN_SUB == 0
    col = lambda j: pl.BlockSpec((rows, width), lambda b, c: (b * nc + c, j))
    state_spec = pl.BlockSpec((None, HGRN_HEADS, HGRN_DK, HGRN_DK), lambda b, c: (b, 0, 0, 0))
    blocks = 4 * rows * width * 4 + rows * width * 2 + 2 * HGRN_HEADS * HGRN_DK * HGRN_DK * 4
    resident = HGRN_HEADS * HGRN_DK * HGRN_DK * 4 + 16 * rows * width * 4
    return pl.pallas_call(
        _hgrn_body,
        grid=(batch, nc),
        in_specs=[col(0), col(1), col(2), col(3),
                  pl.BlockSpec((1, width), lambda b, c: (0, 0)),
                  pl.BlockSpec((1, HGRN_DK), lambda b, c: (0, 0)),
                  state_spec],
        out_specs=[pl.BlockSpec((rows, width), lambda b, c: (b * nc + c, 0)), state_spec],
        out_shape=[jax.ShapeDtypeStruct((batch * seq, width), BF16),
                   jax.ShapeDtypeStruct(s0.shape, F32)],
        scratch_shapes=[pltpu.VMEM((HGRN_HEADS, HGRN_DK, HGRN_DK), F32),
                        pltpu.VMEM((rows, width), F32)],
        compiler_params=_params(("parallel", "arbitrary"), blocks, resident),
        name="hgrn2",
    )(proj, proj, proj, proj, lb, gnorm, s0)


CONV_PAD = 8


def _even_out_body(oa_ref, bg_ref, cg_ref, hb_ref, cw_ref, cs_ref, w_ref, g_ref, x_ref,
                   gpre_ref, wq_ref, mk_ref, mv_ref, wo_ref, gpost_ref,
                   o_ref, cnew_ref, ubuf_ref):
    t = pl.program_id(1)
    tm = oa_ref.shape[0]
    half = oa_ref.shape[1]
    lo = CONV_PAD - (CONV_W - 1)

    @pl.when(t == 0)
    def _():
        ubuf_ref[lo:CONV_PAD, :] = cs_ref[...]

    u = cg_ref[...].astype(F32) * hb_ref[...].astype(F32)
    ubuf_ref[CONV_PAD:CONV_PAD + tm, :] = u
    yb = u * cw_ref[CONV_W - 1:CONV_W, :]
    for j in range(CONV_W - 1):
        yb = yb + ubuf_ref[lo + j:lo + j + tm, :] * cw_ref[j:j + 1, :]
    ob = (bg_ref[...].astype(F32) * yb).astype(BF16)
    y = _dot(oa_ref[...], w_ref[0:half, :]) + _dot(ob, w_ref[half:2 * half, :])
    x1 = x_ref[...] + _rms(y, g_ref[...])
    o_ref[...] = _mem_sublayer(x1, gpre_ref, wq_ref, mk_ref, mv_ref, wo_ref, gpost_ref)

    tail = ubuf_ref[lo + tm:CONV_PAD + tm, :]
    ubuf_ref[lo:CONV_PAD, :] = tail

    @pl.when(t == pl.num_programs(1) - 1)
    def _():
        cnew_ref[...] = tail


def _even_out(oa, proj, conv_wt, conv_state, w_out, g, x, mem, batch, seq, tm_want=512):
    g_pre, w_q, mem_k, mem_v, w_o, g_post, layer = mem
    half = oa.shape[1]
    d = x.shape[1]
    tm = _row_tile(seq, tm_want)
    nt = seq // tm
    row = lambda b, t: (b * nt + t, 0)
    pcol = lambda j: pl.BlockSpec((tm, half), lambda b, t: (b * nt + t, j))
    cstate = pl.BlockSpec((None, CONV_W - 1, half), lambda b, t: (b, 0, 0))
    mem_specs, mem_bytes = _mem_specs(mem_k, w_q, w_o, g_pre, g_post, layer)
    blocks = tm * half * 4 * proj.dtype.itemsize + 2 * tm * d * 4
    resident = (2 * half * d * 2 + mem_bytes + (tm + CONV_PAD) * half * 4 + 4 * tm * half * 4
                + 4 * tm * d * 4)
    return pl.pallas_call(
        _even_out_body,
        grid=(batch, nt),
        in_specs=[pl.BlockSpec((tm, half), row), pcol(4), pcol(5), pcol(6),
                  _const_spec(conv_wt), cstate, _const_spec(w_out), _const_spec(g),
                  pl.BlockSpec((tm, d), row)] + mem_specs,
        out_specs=[pl.BlockSpec((tm, d), row), cstate],
        out_shape=[jax.ShapeDtypeStruct(x.shape, F32),
                   jax.ShapeDtypeStruct(conv_state.shape, F32)],
        scratch_shapes=[pltpu.VMEM((tm + CONV_PAD, half), F32)],
        compiler_params=_params(("parallel", "arbitrary"), blocks, resident),
        name="even_out_mem",
    )(oa, proj, proj, proj, conv_wt, conv_state, w_out, g, x,
      g_pre, w_q, mem_k, mem_v, w_o, g_post)


Q_PRESCALE = MLA_SCALE * LOG2E


def _rope(x, cos, sin_lo, sin_hi):
    half = QK_ROPE // 2
    return (x * cos
            + pltpu.roll(x, V7X_LANES - half, axis=1) * sin_lo
            + pltpu.roll(x, half, axis=1) * sin_hi)


def _mla_proj_body(x_ref, g_ref, wdq_ref, qn_ref, wuq_ref, wdkv_ref, kvn_ref,
                   cos_ref, slo_ref, shi_ref, q_ref, ckv_ref, kpe_ref, kpad_ref):
    cos, slo, shi = cos_ref[...], slo_ref[...], shi_ref[...]
    lora = ckv_ref.shape[1]
    xn = _rms(x_ref[...], g_ref[...]).astype(BF16)
    cq = _rms(_dot(xn, wdq_ref[...]), qn_ref[...]).astype(BF16)
    ckr = _dot(xn, wdkv_ref[...])
    ckv_ref[...] = _rms(ckr[:, :lora], kvn_ref[...])
    kp = _rope(ckr[:, lora:lora + V7X_LANES], cos, slo, shi)
    kpe_ref[...] = kp[:, :QK_ROPE]
    kpad_ref[...] = kp.astype(BF16)
    q = _dot(cq, wuq_ref[...]) * Q_PRESCALE
    for h in range(MLA_HEADS):
        c0 = h * MLA_QK_PAD
        q_ref[h, :, 0:QK_NOPE] = q[:, c0:c0 + QK_NOPE].astype(BF16)
        q_ref[h, :, QK_NOPE:MLA_QK_PAD] = _rope(
            q[:, c0 + QK_NOPE:c0 + MLA_QK_PAD], cos, slo, shi).astype(BF16)


def _mla_proj(x, g, w_dq, q_norm, w_uq_pad, w_dkv_pad, kv_norm, cos, slo, shi, tm_want=512):
    rows, d = x.shape
    qlora = w_dq.shape[1]
    lora = kv_norm.shape[1]
    qw = w_uq_pad.shape[1]
    kw = w_dkv_pad.shape[1]
    tm = _row_tile(rows, tm_want)
    full = lambda a: pl.BlockSpec(a.shape, lambda i: (0,) * a.ndim)
    rowspec = lambda n: pl.BlockSpec((tm, n), lambda i: (i, 0))
    blocks = (tm * d * 4 + (d * qlora + qlora * qw + d * kw) * 2 + 3 * tm * V7X_LANES * 4
              + tm * (qw * 2 + lora * 4 + QK_ROPE * 4 + V7X_LANES * 2))
    resident = tm * (d * 6 + qw * 4 + kw * 4 + qlora * 8)
    return pl.pallas_call(
        _mla_proj_body,
        grid=(rows // tm,),
        in_specs=[rowspec(d), full(g), full(w_dq), full(q_norm), full(w_uq_pad),
                  full(w_dkv_pad), full(kv_norm),
                  rowspec(V7X_LANES), rowspec(V7X_LANES), rowspec(V7X_LANES)],
        out_specs=[pl.BlockSpec((MLA_HEADS, tm, MLA_QK_PAD), lambda i: (0, i, 0)),
                   rowspec(lora), rowspec(QK_ROPE), rowspec(V7X_LANES)],
        out_shape=[jax.ShapeDtypeStruct((MLA_HEADS, rows, MLA_QK_PAD), BF16),
                   jax.ShapeDtypeStruct((rows, lora), F32),
                   jax.ShapeDtypeStruct((rows, QK_ROPE), F32),
                   jax.ShapeDtypeStruct((rows, V7X_LANES), BF16)],
        compiler_params=_params(("parallel",), blocks, resident),
        name="mla_proj",
    )(x, g, w_dq, q_norm, w_uq_pad, w_dkv_pad, kv_norm, cos, slo, shi)


def _kv_up_body(ckv_ref, kpad_ref, wuk_ref, wuvt_ref, k_ref, vt_ref):
    c = ckv_ref[...].astype(BF16)
    kn = _dot(c, wuk_ref[...])
    vt = lax.dot_general(wuvt_ref[...], c, NT_DIMS, preferred_element_type=F32)
    ts = c.shape[0]
    vt_ref[:, 0:V_DIM, :] = vt.astype(BF16).reshape(MLA_HEADS, V_DIM, ts)
    ones_row = lax.broadcasted_iota(jnp.int32, (MLA_HEADS, V_EXT - V_DIM, ts), 1) == 0
    vt_ref[:, V_DIM:V_EXT, :] = ones_row.astype(BF16)
    kpad = kpad_ref[...]
    for h in range(MLA_HEADS):
        k_ref[h, :, 0:QK_NOPE] = kn[:, h * QK_NOPE:(h + 1) * QK_NOPE].astype(BF16)
        k_ref[h, :, QK_NOPE:MLA_QK_PAD] = kpad


def _kv_up(ckv, kpad, w_uk, w_uvt, ts):
    rows, lora = ckv.shape
    assert rows % ts == 0
    rowspec = lambda n: pl.BlockSpec((ts, n), lambda i: (i, 0))
    full = lambda a: pl.BlockSpec(a.shape, lambda i: (0,) * a.ndim)
    kw = MLA_HEADS * MLA_QK_PAD
    vw = MLA_HEADS * V_DIM
    blocks = ts * (lora * 4 + V7X_LANES * 2 + kw * 2 + vw * 2) + 2 * lora * vw * 2
    resident = ts * (kw + vw) * 4
    return pl.pallas_call(
        _kv_up_body,
        grid=(rows // ts,),
        in_specs=[rowspec(lora), rowspec(V7X_LANES), full(w_uk), full(w_uvt)],
        out_specs=[pl.BlockSpec((MLA_HEADS, ts, MLA_QK_PAD), lambda i: (0, i, 0)),
                   pl.BlockSpec((None, MLA_HEADS, V_EXT, ts), lambda i: (i, 0, 0, 0))],
        out_shape=[jax.ShapeDtypeStruct((MLA_HEADS, rows, MLA_QK_PAD), BF16),
                   jax.ShapeDtypeStruct((rows // ts, MLA_HEADS, V_EXT, ts), BF16)],
        compiler_params=_params(("parallel",), blocks, resident),
        name="kv_up",
    )(ckv, kpad, w_uk, w_uvt)


ATTN_TILE = 512
ATTN_Q_PER_K = 1
ATTN_SCORE_BUFFERS = 3


def _scores_t(h, q_ref, k_ref):
    return lax.dot_general(k_ref[h], q_ref[h], NT_DIMS, preferred_element_type=F32)


def _softmax_pv_t(h, st, vt_ref, m_ref, acc_ref, mask):
    if mask is not None:
        st = jnp.where(mask, st, -jnp.inf)
    m_prev = m_ref[h]
    m_next = jnp.maximum(m_prev, jnp.max(st, axis=0, keepdims=True))
    alpha = jnp.exp2(m_prev - m_next)
    p = jnp.exp2(st - m_next)
    m_ref[h] = m_next
    acc_ref[h] = alpha * acc_ref[h] + _dot(vt_ref[h], p.astype(BF16))


def _attn_block_t(q_ref, k_ref, vt_ref, m_ref, acc_ref, s_ref, mask):
    nbuf = s_ref.shape[0]
    for h in range(nbuf - 1):
        s_ref[h] = _scores_t(h, q_ref, k_ref)
    for h in range(MLA_HEADS):
        ahead = h + nbuf - 1
        if ahead < MLA_HEADS:
            s_ref[ahead % nbuf] = _scores_t(ahead, q_ref, k_ref)
        _softmax_pv_t(h, s_ref[h % nbuf], vt_ref, m_ref, acc_ref, mask)


def _attn_causal_body(qi_ref, kj_ref, q_ref, k_ref, vt_ref, o_ref, m_ref, acc_ref, s_ref):
    step = pl.program_id(0)
    qi = qi_ref[step]
    kj = kj_ref[step]
    tq, tk = q_ref.shape[1], k_ref.shape[1]
    per_q = tq // tk

    @pl.when(kj == 0)
    def _():
        m_ref[...] = jnp.full(m_ref.shape, NEG_BIG, F32)
        acc_ref[...] = jnp.zeros_like(acc_ref)

    @pl.when(kj < qi * per_q)
    def _():
        _attn_block_t(q_ref, k_ref, vt_ref, m_ref, acc_ref, s_ref, None)

    @pl.when(kj >= qi * per_q)
    def _():
        key_off = 0 if per_q == 1 else (kj - qi * per_q) * tk
        key_chunk = (lax.broadcasted_iota(jnp.int32, (tk, tq), 0) + key_off) // CHUNK
        query_chunk = lax.broadcasted_iota(jnp.int32, (tk, tq), 1) // CHUNK
        _attn_block_t(q_ref, k_ref, vt_ref, m_ref, acc_ref, s_ref, key_chunk <= query_chunk)

    @pl.when(kj == qi * per_q + per_q - 1)
    def _():
        for h in range(MLA_HEADS):
            out = acc_ref[h, 0:V_DIM, :] / acc_ref[h, V_DIM:V_DIM + 1, :]
            o_ref[:, h * V_DIM:(h + 1) * V_DIM] = out.T.astype(o_ref.dtype)


def _attn_causal(q, k, vt, tq, tk):
    rows = q.shape[1]
    assert tk % CHUNK == 0 and tq % tk == 0 and rows % tq == 0 and vt.shape[3] == tk
    per_q = tq // tk
    pairs = [(i, j) for i in range(rows // tq) for j in range(per_q * (i + 1))]
    qi = jnp.asarray(np.array([a for a, _ in pairs], np.int32))
    kj = jnp.asarray(np.array([b for _, b in pairs], np.int32))
    vw = MLA_HEADS * V_DIM
    blocks = MLA_HEADS * (tq * MLA_QK_PAD + tk * (MLA_QK_PAD + V_EXT)) * 2 + tq * vw * 2
    resident = (MLA_HEADS * (V_EXT + 8) * tq + (ATTN_SCORE_BUFFERS + 1) * tk * tq) * 4
    grid_spec = pltpu.PrefetchScalarGridSpec(
        num_scalar_prefetch=2,
        grid=(len(pairs),),
        in_specs=[pl.BlockSpec((MLA_HEADS, tq, MLA_QK_PAD), lambda p, qi, kj: (0, qi[p], 0)),
                  pl.BlockSpec((MLA_HEADS, tk, MLA_QK_PAD), lambda p, qi, kj: (0, kj[p], 0)),
                  pl.BlockSpec((None, MLA_HEADS, V_EXT, tk), lambda p, qi, kj: (kj[p], 0, 0, 0))],
        out_specs=pl.BlockSpec((tq, vw), lambda p, qi, kj: (qi[p], 0)),
        scratch_shapes=[pltpu.VMEM((MLA_HEADS, 1, tq), F32),
                        pltpu.VMEM((MLA_HEADS, V_EXT, tq), F32),
                        pltpu.VMEM((ATTN_SCORE_BUFFERS, tk, tq), F32)])
    return pl.pallas_call(
        _attn_causal_body,
        grid_spec=grid_spec,
        out_shape=jax.ShapeDtypeStruct((rows, vw), BF16),
        compiler_params=_params(("arbitrary",), blocks, resident),
        name="mla_attn_causal",
    )(qi, kj, q, k, vt)


def _attn_latent_body(q_ref, ckvp_ref, kpep_ref, ckvn_ref, kpadn_ref, wukt_ref, wuv_ref, o_ref,
                      ckv_ref, kpe_ref, qa_ref, qp_ref, *, q_pos0):
    tq = q_ref.shape[1]
    past = ckvp_ref.shape[0]
    n_keys = ckv_ref.shape[0]
    n_valid = past + tq

    ckv_ref[0:past, :] = ckvp_ref[...].astype(BF16)
    ckv_ref[past:n_valid, :] = ckvn_ref[...].astype(BF16)
    ckv_ref[n_valid:n_keys, :] = jnp.zeros((n_keys - n_valid, ckv_ref.shape[1]), BF16)
    kpe_ref[...] = jnp.zeros_like(kpe_ref)
    kpe_ref[0:past, 0:QK_ROPE] = kpep_ref[...].astype(BF16)
    kpe_ref[past:n_valid, :] = kpadn_ref[...]

    for h in range(MLA_HEADS):
        rows = slice(h * tq, (h + 1) * tq)
        qa_ref[rows, :] = _dot(q_ref[h, :, 0:QK_NOPE], wukt_ref[h]).astype(BF16)
        qp_ref[rows, :] = q_ref[h, :, QK_NOPE:MLA_QK_PAD]

    s = (lax.dot_general(qa_ref[...], ckv_ref[...], NT_DIMS, preferred_element_type=F32)
         + lax.dot_general(qp_ref[...], kpe_ref[...], NT_DIMS, preferred_element_type=F32))
    s = s.reshape(MLA_HEADS, tq, n_keys)
    query_chunk = (lax.broadcasted_iota(jnp.int32, (tq, n_keys), 0) + q_pos0) // CHUNK
    kpos = lax.broadcasted_iota(jnp.int32, (tq, n_keys), 1)
    mask = (kpos // CHUNK <= query_chunk) & (kpos < n_valid)
    s = jnp.where(mask[None], s, -jnp.inf)
    p = jnp.exp2(s - jnp.max(s, axis=2, keepdims=True))
    inv_l = 1.0 / jnp.sum(p, axis=2, keepdims=True)
    lat = _dot(p.astype(BF16).reshape(MLA_HEADS * tq, n_keys), ckv_ref[...])
    lat = (lat.reshape(MLA_HEADS, tq, -1) * inv_l).astype(BF16)
    for h in range(MLA_HEADS):
        o_ref[:, h * V_DIM:(h + 1) * V_DIM] = _dot(lat[h], wuv_ref[h]).astype(o_ref.dtype)


def _attn_latent(q, ckv_past, kpe_past, ckv_new, kpad_new, w_ukt, w_uv, layer, batch, q_pos0):
    tq = q.shape[1] // batch
    past, lora = ckv_past.shape[2], ckv_past.shape[3]
    n_keys = -(-(past + tq) // V7X_LANES) * V7X_LANES
    vw = MLA_HEADS * V_DIM
    rows = MLA_HEADS * tq
    blocks = (MLA_HEADS * tq * MLA_QK_PAD * 2 + past * (lora + V7X_LANES) * 4
              + tq * (lora * 4 + V7X_LANES * 2) + tq * vw * 2)
    resident = (2 * MLA_HEADS * lora * QK_NOPE * 2 + n_keys * (lora + V7X_LANES) * 2
                + rows * (lora + V7X_LANES) * 2 + 4 * rows * n_keys * 4 + 2 * rows * lora * 4)
    return pl.pallas_call(
        functools.partial(_attn_latent_body, q_pos0=q_pos0),
        grid=(batch,),
        in_specs=[pl.BlockSpec((MLA_HEADS, tq, MLA_QK_PAD), lambda b: (0, b, 0)),
                  pl.BlockSpec((None, None, past, lora), lambda b: (layer, b, 0, 0)),
                  pl.BlockSpec((None, None, past, QK_ROPE), lambda b: (layer, b, 0, 0)),
                  pl.BlockSpec((tq, lora), lambda b: (b, 0)),
                  pl.BlockSpec((tq, V7X_LANES), lambda b: (b, 0)),
                  _const_spec(w_ukt), _const_spec(w_uv)],
        out_specs=pl.BlockSpec((tq, vw), lambda b: (b, 0)),
        out_shape=jax.ShapeDtypeStruct((q.shape[1], vw), BF16),
        scratch_shapes=[pltpu.VMEM((n_keys, lora), BF16), pltpu.VMEM((n_keys, V7X_LANES), BF16),
                        pltpu.VMEM((rows, lora), BF16), pltpu.VMEM((rows, V7X_LANES), BF16)],
        compiler_params=_params(("parallel",), blocks, resident),
        name="mla_attn_latent",
    )(q, ckv_past, kpe_past, ckv_new, kpad_new, w_ukt, w_uv)


def _rope_tables(pos):
    half = QK_ROPE // 2
    inv = ROPE_THETA ** (-jnp.arange(half, dtype=F32) / half)
    ang = pos.astype(F32)[:, None] * inv
    cos, sin = jnp.cos(ang), jnp.sin(ang)
    z = lambda n: jnp.zeros((pos.shape[0], n), F32)
    pad = V7X_LANES - QK_ROPE
    return (jnp.concatenate([cos, cos, z(pad)], axis=1),
            jnp.concatenate([-sin, z(half + pad)], axis=1),
            jnp.concatenate([z(half), sin, z(pad)], axis=1))


def _prep_mla(w_uq, w_dkv, w_ukv):
    qlora = w_uq.shape[0]
    lora = w_ukv.shape[0]
    wq = w_uq.reshape(qlora, MLA_HEADS, QK_NOPE + QK_ROPE)
    wq = jnp.pad(wq, ((0, 0), (0, 0), (0, MLA_QK_PAD - QK_NOPE - QK_ROPE)))
    w_uq_pad = wq.reshape(qlora, MLA_HEADS * MLA_QK_PAD).astype(BF16)
    w_dkv_pad = jnp.pad(w_dkv, ((0, 0), (0, V7X_LANES - QK_ROPE))).astype(BF16)
    wkv = w_ukv.reshape(lora, MLA_HEADS, QK_NOPE + V_DIM)
    w_uk = wkv[:, :, :QK_NOPE].reshape(lora, MLA_HEADS * QK_NOPE).astype(BF16)
    w_uvt = wkv[:, :, QK_NOPE:].reshape(lora, MLA_HEADS * V_DIM).T.astype(BF16)
    w_ukt_heads = jnp.transpose(wkv[:, :, :QK_NOPE], (1, 2, 0)).astype(BF16)
    w_uv_heads = jnp.transpose(wkv[:, :, QK_NOPE:], (1, 0, 2)).astype(BF16)
    return w_uq_pad, w_dkv_pad, w_uk, w_uvt, w_ukt_heads, w_uv_heads


def _stack(arrays):
    return arrays[0][None] if len(arrays) == 1 else jnp.stack(arrays)


def _trunk(x, batch, seq, past_len, mem_k, mem_v, s_hgrn, s_conv, ckv_past, kpe_past, p):
    depth = p["norm_g"].shape[0]
    rows = batch * seq
    pos = past_len + jnp.arange(seq, dtype=jnp.int32)
    cos, slo, shi = (jnp.tile(t, (batch, 1)) for t in _rope_tables(pos))
    hs, cs, ckvs, kpes = [], [], [], []
    for l in range(depth):
        g = lambda i: p["norm_g"][l, i][None, :]
        x = _ffn(x, g(0), p["ffn_w1"], p["ffn_w2"], g(1), l, 0)
        mem = (g(4), p["w_mem_q"][l], mem_k, mem_v, p["w_mem_o"][l], g(5), l)
        if l % 2 == 0:
            e = l // 2
            proj = _norm_matmul(x, g(2), p["w_in0"][e], BF16, tm_want=1024)
            oa, s_new = _hgrn(proj, p["lbs"][l][None, :], p["hgrn_gnorm"][e][None, :],
                              s_hgrn[e], batch, seq)
            x, c_new = _even_out(oa, proj, p["conv_wt"][e], s_conv[e], p["w_out0"][e],
                                 g(3), x, mem, batch, seq)
            hs.append(s_new)
            cs.append(c_new)
        else:
            o = l // 2
            q, ckv_new, kpe_new, kpad_new = _mla_proj(
                x, g(2), p["mla_w_dq"][o], p["mla_q_norm"][o][None, :], p["w_uq_pad"][o],
                p["w_dkv_pad"][o], p["mla_kv_norm"][o][None, :], cos, slo, shi)
            if past_len == 0:
                assert batch == 1
                tk = _row_tile(seq, ATTN_TILE)
                tq = tk * ATTN_Q_PER_K if seq % (tk * ATTN_Q_PER_K) == 0 else tk
                k, vt = _kv_up(ckv_new, kpad_new, p["w_uk"][o], p["w_uvt"][o], tk)
                att = _attn_causal(q, k, vt, tq, tk)
            else:
                att = _attn_latent(q, ckv_past, kpe_past, ckv_new, kpad_new,
                                   p["w_ukt_heads"][o], p["w_uv_heads"][o], o, batch, past_len)
            x = _proj_residual(att, p["mla_w_o"][o], g(3), x, mem, batch, seq)
            ckvs.append(ckv_new.reshape(batch, seq, -1))
            kpes.append(kpe_new.reshape(batch, seq, -1))
        x = _ffn(x, g(6), p["ffn_w1"], p["ffn_w2"], g(7), l, 1)
    return (x.reshape(batch, seq, -1), _stack(hs), _stack(cs), _stack(ckvs), _stack(kpes))


def kernel(x_prompt, x_sample, mem_prompt, state_hgrn, state_conv, cache_ckv, cache_kpe, cache_mem_k, cache_mem_v, norm_g, ffn_w1, ffn_w2, w_in0, hgrn_lb, hgrn_gnorm, conv_w, w_out0, mla_w_dq, mla_q_norm, mla_w_uq, mla_w_dkv, mla_kv_norm, mla_w_ukv, mla_w_o, mem_norm, w_mem_q, w_mem_kv, w_mem_o):
    batch, seq, d = x_prompt.shape
    dec_batch, dec_seq, _ = x_sample.shape
    past_len = cache_ckv.shape[2]
    depth = norm_g.shape[0]
    n_mem = mem_prompt.shape[1]
    mem_width = MEM_HEADS * MEM_DIM

    prepped = [_prep_mla(mla_w_uq[o], mla_w_dkv[o], mla_w_ukv[o]) for o in range(mla_w_uq.shape[0])]
    p = dict(
        norm_g=norm_g,
        ffn_w1=ffn_w1.astype(BF16), ffn_w2=ffn_w2.astype(BF16),
        w_in0=w_in0.astype(BF16), w_out0=w_out0.astype(BF16),
        lbs=jnp.cumsum(jax.nn.softmax(hgrn_lb.astype(F32), axis=0), axis=0),
        hgrn_gnorm=hgrn_gnorm,
        conv_wt=jnp.swapaxes(conv_w, 1, 2),
        mla_w_dq=mla_w_dq.astype(BF16), mla_q_norm=mla_q_norm, mla_kv_norm=mla_kv_norm,
        w_uq_pad=[t[0] for t in prepped], w_dkv_pad=[t[1] for t in prepped],
        w_uk=[t[2] for t in prepped], w_uvt=[t[3] for t in prepped],
        w_ukt_heads=[t[4] for t in prepped], w_uv_heads=[t[5] for t in prepped],
        mla_w_o=mla_w_o.astype(BF16),
        w_mem_q=w_mem_q.astype(BF16), w_mem_o=w_mem_o.astype(BF16),
    )

    mem_rows = mem_prompt.reshape(batch * n_mem, d)
    mks, mvs = [], []
    for l in range(depth):
        kv = _norm_matmul(mem_rows, mem_norm[l][None, :], w_mem_kv[l].astype(BF16), F32,
                          tn_want=2 * mem_width)
        kv = kv.reshape(batch, n_mem, 2, mem_width)
        mks.append(kv[:, :, 0])
        mvs.append(kv[:, :, 1])
    mem_k_p = jnp.stack(mks)
    mem_v_p = jnp.stack(mvs)

    n_even = state_hgrn.shape[0]
    h0 = jnp.zeros((n_even, batch) + state_hgrn.shape[2:], F32)
    c0 = jnp.zeros((n_even, batch) + state_conv.shape[2:], F32)
    y_p, p_hgrn, p_conv, p_ckv, p_kpe = _trunk(
        x_prompt.reshape(batch * seq, d), batch, seq, 0, mem_k_p, mem_v_p, h0, c0, None, None, p)
    y_s, s_hgrn, s_conv, s_ckv, s_kpe = _trunk(
        x_sample.reshape(dec_batch * dec_seq, d), dec_batch, dec_seq, past_len,
        cache_mem_k.reshape(depth, dec_batch, n_mem, mem_width),
        cache_mem_v.reshape(depth, dec_batch, n_mem, mem_width),
        state_hgrn, state_conv, cache_ckv, cache_kpe, p)

    mem_shape = (depth, batch, n_mem, MEM_HEADS, MEM_DIM)
    return (y_p, y_s, p_hgrn, p_conv, p_ckv, p_kpe,
            mem_k_p.reshape(mem_shape), mem_v_p.reshape(mem_shape),
            s_hgrn, s_conv, s_ckv, s_kpe)
```

```python
import functools

import jax
import jax.numpy as jnp
import numpy as np
from jax import lax
from jax.experimental import pallas as pl
from jax.experimental.pallas import tpu as pltpu

F32 = jnp.float32
BF16 = jnp.bfloat16

V7X_LANES = 128
V7X_SUBLANES = 8
V7X_VMEM_BYTES = 64 * 1024 * 1024
MIB = 1024 * 1024
SPILL_ROOM_BYTES = 8 * MIB
VMEM_RESERVED_BYTES = 2 * MIB

EPS = 1e-6
CHUNK = 64
HGRN_HEADS = 8
HGRN_DK = 128
HGRN_SUB = 16
HGRN_BLOCK_ROWS = 256
MLA_HEADS = 16
QK_NOPE = 128
QK_ROPE = 64
V_DIM = 128
V_EXT = V_DIM + 16
MLA_QK_PAD = 256
MLA_SCALE = (QK_NOPE + QK_ROPE) ** -0.5
ROPE_THETA = 10000.0
MEM_HEADS = 4
MEM_DIM = 128
CONV_W = 3
NEG_BIG = -1e30
LOG2E = 1.4426950408889634

NT_DIMS = (((1,), (1,)), ((), ()))
TN_DIMS = (((0,), (0,)), ((), ()))


def _vmem_limit(pipelined_bytes, resident_bytes):
    want = 2 * pipelined_bytes + resident_bytes + SPILL_ROOM_BYTES
    return int(min(want, V7X_VMEM_BYTES - VMEM_RESERVED_BYTES))


def _params(semantics, pipelined_bytes, resident_bytes):
    return pltpu.CompilerParams(
        dimension_semantics=semantics,
        vmem_limit_bytes=_vmem_limit(pipelined_bytes, resident_bytes))


def _row_tile(rows, want):
    if rows <= want:
        return rows
    t = want - want % V7X_LANES
    while t > 0 and rows % t:
        t -= V7X_LANES
    assert t > 0, (rows, want)
    return t


ROW_CHUNK = 256


def _row_chunks(rows):
    return [slice(r0, min(r0 + ROW_CHUNK, rows)) for r0 in range(0, rows, ROW_CHUNK)]


def _rms(x, g):
    ms = jnp.mean(x * x, axis=-1, keepdims=True)
    return x * lax.rsqrt(ms + EPS) * g


def _sigmoid(x):
    return 1.0 / (1.0 + jnp.exp(-x))


def _dot(a, b):
    return jnp.dot(a, b, preferred_element_type=F32)


FFN_COL_CHUNK = 512


def _ffn_hidden(xn, w1g_ref, w1u_ref):
    gate = _dot(xn, w1g_ref[...])
    up = _dot(xn, w1u_ref[...])
    return (gate * _sigmoid(gate) * up).astype(BF16)


def _ffn_body(x_ref, gpre_ref, w1g_ref, w1u_ref, w2_ref, gpost_ref, o_ref, xn_ref):
    k = pl.program_id(1)
    last = pl.num_programs(1) - 1
    tm, d = o_ref.shape

    @pl.when(k == 0)
    def _():
        gpre = gpre_ref[...]
        for rs in _row_chunks(tm):
            xn = _rms(x_ref[rs, :], gpre).astype(BF16)
            xn_ref[rs, :] = xn
            o_ref[rs, :] = _dot(_ffn_hidden(xn, w1g_ref, w1u_ref), w2_ref[...])

    @pl.when((k > 0) & (k < last))
    def _():
        o_ref[...] += _dot(_ffn_hidden(xn_ref[...], w1g_ref, w1u_ref), w2_ref[...])

    @pl.when(k == last)
    def _():
        h = _ffn_hidden(xn_ref[...], w1g_ref, w1u_ref)
        ss = jnp.zeros((tm, 1), F32)
        for c0 in range(0, d, FFN_COL_CHUNK):
            cs = slice(c0, c0 + FFN_COL_CHUNK)
            y = o_ref[:, cs] + _dot(h, w2_ref[:, cs])
            o_ref[:, cs] = y
            ss = ss + jnp.sum(y * y, axis=-1, keepdims=True)
        scale = 0.5 * lax.rsqrt(ss * (1.0 / d) + EPS)
        o_ref[...] = x_ref[...] + o_ref[...] * scale * gpost_ref[...]


def _ffn(x, g_pre, w1, w2, g_post, layer, half, tm_want=1024, tf=512):
    rows, d = x.shape
    dff = w2.shape[-2]
    tm = _row_tile(rows, tm_want)
    nk = dff // tf
    assert dff % tf == 0 and nk >= 2 and d % FFN_COL_CHUNK == 0
    blocks = 2 * tm * d * 4 + 3 * d * tf * 2
    resident = tm * d * (2 + 4) + 4 * tm * tf * 4
    return pl.pallas_call(
        _ffn_body,
        grid=(rows // tm, nk),
        in_specs=[
            pl.BlockSpec((tm, d), lambda i, k: (i, 0)),
            pl.BlockSpec((1, d), lambda i, k: (0, 0)),
            pl.BlockSpec((None, None, d, tf), lambda i, k: (layer, half, 0, k)),
            pl.BlockSpec((None, None, d, tf), lambda i, k: (layer, half, 0, nk + k)),
            pl.BlockSpec((None, None, tf, d), lambda i, k: (layer, half, k, 0)),
            pl.BlockSpec((1, d), lambda i, k: (0, 0)),
        ],
        out_specs=pl.BlockSpec((tm, d), lambda i, k: (i, 0)),
        out_shape=jax.ShapeDtypeStruct((rows, d), F32),
        scratch_shapes=[pltpu.VMEM((tm, d), BF16)],
        compiler_params=_params(("parallel", "arbitrary"), blocks, resident),
        name="ffn",
    )(x, g_pre, w1, w1, w2, g_post)


def _norm_matmul_body(x_ref, g_ref, w_ref, o_ref, xn_ref):
    j = pl.program_id(1)

    @pl.when(j == 0)
    def _():
        g = g_ref[...]
        for rs in _row_chunks(x_ref.shape[0]):
            xn = _rms(x_ref[rs, :], g).astype(BF16)
            xn_ref[rs, :] = xn
            o_ref[rs, :] = _dot(xn, w_ref[...]).astype(o_ref.dtype)

    @pl.when(j > 0)
    def _():
        o_ref[...] = _dot(xn_ref[...], w_ref[...]).astype(o_ref.dtype)


def _norm_matmul(x, g, w, out_dtype, tm_want=512, tn_want=1024):
    rows, d = x.shape
    n = w.shape[1]
    tm = _row_tile(rows, tm_want)
    tn = _row_tile(n, tn_want)
    blocks = tm * d * 4 + d * tn * 2 + tm * tn * 4
    resident = tm * d * 2 + tm * tn * 4
    return pl.pallas_call(
        _norm_matmul_body,
        grid=(rows // tm, n // tn),
        in_specs=[
            pl.BlockSpec((tm, d), lambda i, j: (i, 0)),
            pl.BlockSpec((1, d), lambda i, j: (0, 0)),
            pl.BlockSpec((d, tn), lambda i, j: (0, j)),
        ],
        out_specs=pl.BlockSpec((tm, tn), lambda i, j: (i, j)),
        out_shape=jax.ShapeDtypeStruct((rows, n), out_dtype),
        scratch_shapes=[pltpu.VMEM((tm, d), BF16)],
        compiler_params=_params(("parallel", "arbitrary"), blocks, resident),
        name="norm_matmul",
    )(x, g, w)


def _mem_sublayer(x, gpre_ref, wq_ref, mk_ref, mv_ref, wo_ref, gpost_ref):
    q = _dot(_rms(x, gpre_ref[...]).astype(BF16), wq_ref[...])
    outs = []
    for h in range(MEM_HEADS):
        hs = slice(h * MEM_DIM, (h + 1) * MEM_DIM)
        kh = mk_ref[:, hs].astype(BF16)
        vh = mv_ref[:, hs].astype(BF16)
        s = lax.dot_general(q[:, hs].astype(BF16), kh, NT_DIMS,
                            preferred_element_type=F32) * (MEM_DIM ** -0.5)
        p = jnp.exp(s - jnp.max(s, axis=1, keepdims=True))
        p = p * (1.0 / jnp.sum(p, axis=1, keepdims=True))
        outs.append(_dot(p.astype(BF16), vh).astype(BF16))
    o = jnp.concatenate(outs, axis=1)
    return x + _rms(_dot(o, wo_ref[...]), gpost_ref[...])


def _const_spec(a):
    return pl.BlockSpec(a.shape, lambda *_: (0,) * a.ndim, pipeline_mode=pl.Buffered(1))


def _mem_specs(mem_k, w_q, w_o, g_pre, g_post, layer):
    n_mem, width = mem_k.shape[2], mem_k.shape[3]
    mem = pl.BlockSpec((None, None, n_mem, width), lambda b, t: (layer, b, 0, 0))
    specs = [_const_spec(g_pre), _const_spec(w_q), mem, mem, _const_spec(w_o),
             _const_spec(g_post)]
    nbytes = (w_q.size + w_o.size) * 2 + 4 * n_mem * width * 4
    return specs, nbytes


def _proj_residual_body(a_ref, w_ref, g_ref, x_ref,
                        gpre_ref, wq_ref, mk_ref, mv_ref, wo_ref, gpost_ref, o_ref):
    x1 = x_ref[...] + _rms(_dot(a_ref[...], w_ref[...]), g_ref[...])
    o_ref[...] = _mem_sublayer(x1, gpre_ref, wq_ref, mk_ref, mv_ref, wo_ref, gpost_ref)


def _proj_residual(a, w, g, x, mem, batch, seq, tm_want=512):
    g_pre, w_q, mem_k, mem_v, w_o, g_post, layer = mem
    kdim = a.shape[1]
    d = w.shape[1]
    tm = _row_tile(seq, tm_want)
    nt = seq // tm
    row = lambda n: pl.BlockSpec((tm, n), lambda b, t: (b * nt + t, 0))
    mem_specs, mem_bytes = _mem_specs(mem_k, w_q, w_o, g_pre, g_post, layer)
    blocks = tm * kdim * 2 + 2 * tm * d * 4
    resident = kdim * d * 2 + mem_bytes + 4 * tm * d * 4
    return pl.pallas_call(
        _proj_residual_body,
        grid=(batch, nt),
        in_specs=[row(kdim), _const_spec(w), _const_spec(g), row(d)] + mem_specs,
        out_specs=row(d),
        out_shape=jax.ShapeDtypeStruct(x.shape, F32),
        compiler_params=_params(("parallel", "parallel"), blocks, resident),
        name="proj_residual_mem",
    )(a, w, g, x, g_pre, w_q, mem_k, mem_v, w_o, g_post)


def _hgrn_body(qa_ref, fa_ref, ia_ref, ga_ref, lb_ref, gn_ref, s0_ref,
               oa_ref, sout_ref, st_ref, o_ref):
    c = pl.program_id(1)
    rows = qa_ref.shape[0]
    width = qa_ref.shape[1]
    nsub = rows // HGRN_SUB

    @pl.when(c == 0)
    def _():
        for h in range(HGRN_HEADS):
            st_ref[h] = s0_ref[h].T

    lb = lb_ref[...]
    qa = qa_ref[...].astype(F32)
    q = qa * _sigmoid(qa) * (HGRN_DK ** -0.5)
    f = lb + (1.0 - lb) * _sigmoid(fa_ref[...].astype(F32))
    k = 1.0 - f
    g = jnp.log(f) * LOG2E
    v = ia_ref[...].astype(F32)

    r_i = lax.broadcasted_iota(jnp.int32, (rows, rows), 0)
    c_i = lax.broadcasted_iota(jnp.int32, (rows, rows), 1)
    tri = ((c_i <= r_i) & ((c_i // HGRN_SUB) == (r_i // HGRN_SUB))).astype(BF16)
    g1 = g.astype(BF16)
    rem = g - g1.astype(F32)
    g2 = rem.astype(BF16)
    g3 = (rem - g2.astype(F32)).astype(BF16)
    bl = _dot(tri, g1) + _dot(tri, g2) + _dot(tri, g3)

    qe = (q * jnp.exp2(bl)).astype(BF16)
    row8 = lax.broadcasted_iota(jnp.int32, (V7X_SUBLANES, width), 0)
    tiles = HGRN_SUB // V7X_SUBLANES

    for i in range(nsub):
        r0 = i * HGRN_SUB
        bl_i = bl[r0:r0 + HGRN_SUB]
        q_i = q[r0:r0 + HGRN_SUB]
        k_i = k[r0:r0 + HGRN_SUB]
        v_i = v[r0:r0 + HGRN_SUB]
        b_end = bl_i[HGRN_SUB - 1:HGRN_SUB]
        ke_i = (k_i * jnp.exp2(b_end - bl_i)).astype(BF16)
        dec_i = jnp.exp2(b_end)
        v_bf = v_i.astype(BF16)

        diag = [[jnp.zeros((V7X_SUBLANES, HGRN_DK), F32) for _ in range(tiles)]
                for _ in range(HGRN_HEADS)]
        for s in range(HGRN_SUB):
            ks, bs, vs = k_i[s:s + 1], bl_i[s:s + 1], v_i[s:s + 1]
            for rt in range(s // V7X_SUBLANES, tiles):
                rsl = slice(rt * V7X_SUBLANES, (rt + 1) * V7X_SUBLANES)
                rel = bl_i[rsl] - bs
                if rt == s // V7X_SUBLANES:
                    rel = jnp.where(row8 >= s % V7X_SUBLANES, rel, -jnp.inf)
                w = q_i[rsl] * ks * jnp.exp2(rel)
                for h in range(HGRN_HEADS):
                    hs = slice(h * HGRN_DK, (h + 1) * HGRN_DK)
                    col = jnp.sum(w[:, hs], axis=1, keepdims=True)
                    diag[h][rt] = diag[h][rt] + col * vs[:, hs]

        for h in range(HGRN_HEADS):
            hs = slice(h * HGRN_DK, (h + 1) * HGRN_DK)
            st = st_ref[h]
            inter = lax.dot_general(qe[r0:r0 + HGRN_SUB, hs], st.astype(BF16), NT_DIMS,
                                    preferred_element_type=F32)
            upd = lax.dot_general(v_bf[:, hs], ke_i[:, hs], TN_DIMS,
                                  preferred_element_type=F32)
            st_ref[h] = st * dec_i[:, hs] + upd
            o_ref[r0:r0 + HGRN_SUB, hs] = inter + jnp.concatenate(diag[h], axis=0)

    gn = gn_ref[...]
    for h in range(HGRN_HEADS):
        hs = slice(h * HGRN_DK, (h + 1) * HGRN_DK)
        ga = ga_ref[:, hs].astype(F32)
        oa_ref[:, hs] = (_rms(o_ref[:, hs], gn) * (ga * _sigmoid(ga))).astype(oa_ref.dtype)

    @pl.when(c == pl.num_programs(1) - 1)
    def _():
        for h in range(HGRN_HEADS):
            sout_ref[h] = st_ref[h].T


def _hgrn(proj, lb, gnorm, s0, batch, seq):
    width = HGRN_HEADS * HGRN_DK
    rows = min(HGRN_BLOCK_ROWS, seq)
    nc = seq // rows
    assert seq % rows == 0 and rows % HGRN_SUB == 0
    col = lambda j: pl.BlockSpec((rows, width), lambda b, c: (b * nc + c, j))
    state_spec = pl.BlockSpec((None, HGRN_HEADS, HGRN_DK, HGRN_DK), lambda b, c: (b, 0, 0, 0))
    blocks = 4 * rows * width * 4 + rows * width * 2 + 2 * HGRN_HEADS * HGRN_DK * HGRN_DK * 4
    resident = HGRN_HEADS * HGRN_DK * HGRN_DK * 4 + 16 * rows * width * 4
    return pl.pallas_call(
        _hgrn_body,
        grid=(batch, nc),
        in_specs=[col(0), col(1), col(2), col(3),
                  pl.BlockSpec((1, width), lambda b, c: (0, 0)),
                  pl.BlockSpec((1, HGRN_DK), lambda b, c: (0, 0)),
                  state_spec],
        out_specs=[pl.BlockSpec((rows, width), lambda b, c: (b * nc + c, 0)), state_spec],
        out_shape=[jax.ShapeDtypeStruct((batch * seq, width), BF16),
                   jax.ShapeDtypeStruct(s0.shape, F32)],
        scratch_shapes=[pltpu.VMEM((HGRN_HEADS, HGRN_DK, HGRN_DK), F32),
                        pltpu.VMEM((rows, width), F32)],
        compiler_params=_params(("parallel", "arbitrary"), blocks, resident),
        name="hgrn2",
    )(proj, proj, proj, proj, lb, gnorm, s0)


CONV_PAD = 8


def _even_out_body(oa_ref, bg_ref, cg_ref, hb_ref, cw_ref, cs_ref, w_ref, g_ref, x_ref,
                   gpre_ref, wq_ref, mk_ref, mv_ref, wo_ref, gpost_ref,
                   o_ref, cnew_ref, ubuf_ref):
    t = pl.program_id(1)
    tm = oa_ref.shape[0]
    half = oa_ref.shape[1]
    lo = CONV_PAD - (CONV_W - 1)

    @pl.when(t == 0)
    def _():
        ubuf_ref[lo:CONV_PAD, :] = cs_ref[...]

    u = cg_ref[...].astype(F32) * hb_ref[...].astype(F32)
    ubuf_ref[CONV_PAD:CONV_PAD + tm, :] = u
    yb = u * cw_ref[CONV_W - 1:CONV_W, :]
    for j in range(CONV_W - 1):
        yb = yb + ubuf_ref[lo + j:lo + j + tm, :] * cw_ref[j:j + 1, :]
    ob = (bg_ref[...].astype(F32) * yb).astype(BF16)
    y = _dot(oa_ref[...], w_ref[0:half, :]) + _dot(ob, w_ref[half:2 * half, :])
    x1 = x_ref[...] + _rms(y, g_ref[...])
    o_ref[...] = _mem_sublayer(x1, gpre_ref, wq_ref, mk_ref, mv_ref, wo_ref, gpost_ref)

    tail = ubuf_ref[lo + tm:CONV_PAD + tm, :]
    ubuf_ref[lo:CONV_PAD, :] = tail

    @pl.when(t == pl.num_programs(1) - 1)
    def _():
        cnew_ref[...] = tail


def _even_out(oa, proj, conv_wt, conv_state, w_out, g, x, mem, batch, seq, tm_want=512):
    g_pre, w_q, mem_k, mem_v, w_o, g_post, layer = mem
    half = oa.shape[1]
    d = x.shape[1]
    tm = _row_tile(seq, tm_want)
    nt = seq // tm
    row = lambda b, t: (b * nt + t, 0)
    pcol = lambda j: pl.BlockSpec((tm, half), lambda b, t: (b * nt + t, j))
    cstate = pl.BlockSpec((None, CONV_W - 1, half), lambda b, t: (b, 0, 0))
    mem_specs, mem_bytes = _mem_specs(mem_k, w_q, w_o, g_pre, g_post, layer)
    blocks = tm * half * 4 * proj.dtype.itemsize + 2 * tm * d * 4
    resident = (2 * half * d * 2 + mem_bytes + (tm + CONV_PAD) * half * 4 + 4 * tm * half * 4
                + 4 * tm * d * 4)
    return pl.pallas_call(
        _even_out_body,
        grid=(batch, nt),
        in_specs=[pl.BlockSpec((tm, half), row), pcol(4), pcol(5), pcol(6),
                  _const_spec(conv_wt), cstate, _const_spec(w_out), _const_spec(g),
                  pl.BlockSpec((tm, d), row)] + mem_specs,
        out_specs=[pl.BlockSpec((tm, d), row), cstate],
        out_shape=[jax.ShapeDtypeStruct(x.shape, F32),
                   jax.ShapeDtypeStruct(conv_state.shape, F32)],
        scratch_shapes=[pltpu.VMEM((tm + CONV_PAD, half), F32)],
        compiler_params=_params(("parallel", "arbitrary"), blocks, resident),
        name="even_out_mem",
    )(oa, proj, proj, proj, conv_wt, conv_state, w_out, g, x,
      g_pre, w_q, mem_k, mem_v, w_o, g_post)


Q_PRESCALE = MLA_SCALE * LOG2E


def _rope(x, cos, sin_lo, sin_hi):
    half = QK_ROPE // 2
    return (x * cos
            + pltpu.roll(x, V7X_LANES - half, axis=1) * sin_lo
            + pltpu.roll(x, half, axis=1) * sin_hi)


def _mla_proj_body(x_ref, g_ref, wdq_ref, qn_ref, wuq_ref, wdkv_ref, kvn_ref,
                   cos_ref, slo_ref, shi_ref, *rest, q_transposed):
    if q_transposed:
        cost_ref, sint_ref, q_ref, ckv_ref, kpe_ref, kpad_ref = rest
    else:
        q_ref, ckv_ref, kpe_ref, kpad_ref = rest
    cos, slo, shi = cos_ref[...], slo_ref[...], shi_ref[...]
    lora = ckv_ref.shape[1]
    xn = _rms(x_ref[...], g_ref[...]).astype(BF16)
    cq = _rms(_dot(xn, wdq_ref[...]), qn_ref[...]).astype(BF16)
    ckr = _dot(xn, wdkv_ref[...])
    ckv_ref[...] = _rms(ckr[:, :lora], kvn_ref[...])
    kp = _rope(ckr[:, lora:lora + V7X_LANES], cos, slo, shi)
    kpe_ref[...] = kp[:, :QK_ROPE]
    kpad_ref[...] = kp.astype(BF16)
    if q_transposed:
        half = QK_ROPE // 2
        cost, sint = cost_ref[...], sint_ref[...]
        qt = lax.dot_general(wuq_ref[...], cq, NT_DIMS, preferred_element_type=F32) * Q_PRESCALE
        for h in range(MLA_HEADS):
            r0 = h * MLA_QK_PAD
            x1 = qt[r0 + QK_NOPE:r0 + QK_NOPE + half]
            x2 = qt[r0 + QK_NOPE + half:r0 + QK_NOPE + QK_ROPE]
            q_ref[h, 0:QK_NOPE, :] = qt[r0:r0 + QK_NOPE].astype(BF16)
            q_ref[h, QK_NOPE:QK_NOPE + half, :] = (x1 * cost - x2 * sint).astype(BF16)
            q_ref[h, QK_NOPE + half:QK_NOPE + QK_ROPE, :] = (x1 * sint + x2 * cost).astype(BF16)
            q_ref[h, QK_NOPE + QK_ROPE:MLA_QK_PAD, :] = qt[
                r0 + QK_NOPE + QK_ROPE:r0 + MLA_QK_PAD].astype(BF16)
    else:
        q = _dot(cq, wuq_ref[...]) * Q_PRESCALE
        for h in range(MLA_HEADS):
            c0 = h * MLA_QK_PAD
            q_ref[h, :, 0:QK_NOPE] = q[:, c0:c0 + QK_NOPE].astype(BF16)
            q_ref[h, :, QK_NOPE:MLA_QK_PAD] = _rope(
                q[:, c0 + QK_NOPE:c0 + MLA_QK_PAD], cos, slo, shi).astype(BF16)


def _mla_proj(x, g, w_dq, q_norm, w_uq, w_dkv_pad, kv_norm, tables, tables_t=None, tm_want=512):
    q_transposed = tables_t is not None
    rows, d = x.shape
    qlora = w_dq.shape[1]
    lora = kv_norm.shape[1]
    qw = MLA_HEADS * MLA_QK_PAD
    kw = w_dkv_pad.shape[1]
    tm = _row_tile(rows, tm_want)
    full = lambda a: pl.BlockSpec(a.shape, lambda i: (0,) * a.ndim)
    rowspec = lambda n: pl.BlockSpec((tm, n), lambda i: (i, 0))
    blocks = (tm * d * 4 + (d * qlora + qlora * qw + d * kw) * 2 + 4 * tm * V7X_LANES * 4
              + tm * (qw * 2 + lora * 4 + QK_ROPE * 4 + V7X_LANES * 2))
    resident = tm * (d * 6 + qw * 4 + kw * 4 + qlora * 8)
    in_specs = [rowspec(d), full(g), full(w_dq), full(q_norm), full(w_uq),
                full(w_dkv_pad), full(kv_norm),
                rowspec(V7X_LANES), rowspec(V7X_LANES), rowspec(V7X_LANES)]
    operands = [x, g, w_dq, q_norm, w_uq, w_dkv_pad, kv_norm, *tables]
    if q_transposed:
        in_specs += [pl.BlockSpec((QK_ROPE // 2, tm), lambda i: (0, i))] * 2
        operands += list(tables_t)
        q_spec = pl.BlockSpec((MLA_HEADS, MLA_QK_PAD, tm), lambda i: (0, 0, i))
        q_shape = (MLA_HEADS, MLA_QK_PAD, rows)
    else:
        q_spec = pl.BlockSpec((MLA_HEADS, tm, MLA_QK_PAD), lambda i: (0, i, 0))
        q_shape = (MLA_HEADS, rows, MLA_QK_PAD)
    return pl.pallas_call(
        functools.partial(_mla_proj_body, q_transposed=q_transposed),
        grid=(rows // tm,),
        in_specs=in_specs,
        out_specs=[q_spec, rowspec(lora), rowspec(QK_ROPE), rowspec(V7X_LANES)],
        out_shape=[jax.ShapeDtypeStruct(q_shape, BF16),
                   jax.ShapeDtypeStruct((rows, lora), F32),
                   jax.ShapeDtypeStruct((rows, QK_ROPE), F32),
                   jax.ShapeDtypeStruct((rows, V7X_LANES), BF16)],
        compiler_params=_params(("parallel",), blocks, resident),
        name="mla_proj",
    )(*operands)


def _kv_up_body(ckv_ref, kpad_ref, wuk_ref, wuvt_ref, k_ref, vt_ref):
    c = ckv_ref[...].astype(BF16)
    kn = _dot(c, wuk_ref[...])
    vt = lax.dot_general(wuvt_ref[...], c, NT_DIMS, preferred_element_type=F32)
    ts = c.shape[0]
    vt_ref[:, 0:V_DIM, :] = vt.astype(BF16).reshape(MLA_HEADS, V_DIM, ts)
    ones_row = lax.broadcasted_iota(jnp.int32, (MLA_HEADS, V_EXT - V_DIM, ts), 1) == 0
    vt_ref[:, V_DIM:V_EXT, :] = ones_row.astype(BF16)
    kpad = kpad_ref[...]
    for h in range(MLA_HEADS):
        k_ref[h, :, 0:QK_NOPE] = kn[:, h * QK_NOPE:(h + 1) * QK_NOPE].astype(BF16)
        k_ref[h, :, QK_NOPE:MLA_QK_PAD] = kpad


def _kv_up(ckv, kpad, w_uk, w_uvt, ts):
    rows, lora = ckv.shape
    assert rows % ts == 0
    rowspec = lambda n: pl.BlockSpec((ts, n), lambda i: (i, 0))
    full = lambda a: pl.BlockSpec(a.shape, lambda i: (0,) * a.ndim)
    kw = MLA_HEADS * MLA_QK_PAD
    vw = MLA_HEADS * V_DIM
    blocks = ts * (lora * 4 + V7X_LANES * 2 + kw * 2 + vw * 2) + 2 * lora * vw * 2
    resident = ts * (kw + vw) * 4
    return pl.pallas_call(
        _kv_up_body,
        grid=(rows // ts,),
        in_specs=[rowspec(lora), rowspec(V7X_LANES), full(w_uk), full(w_uvt)],
        out_specs=[pl.BlockSpec((MLA_HEADS, ts, MLA_QK_PAD), lambda i: (0, i, 0)),
                   pl.BlockSpec((None, MLA_HEADS, V_EXT, ts), lambda i: (i, 0, 0, 0))],
        out_shape=[jax.ShapeDtypeStruct((MLA_HEADS, rows, MLA_QK_PAD), BF16),
                   jax.ShapeDtypeStruct((rows // ts, MLA_HEADS, V_EXT, ts), BF16)],
        compiler_params=_params(("parallel",), blocks, resident),
        name="kv_up",
    )(ckv, kpad, w_uk, w_uvt)


ATTN_TILE = 512
ATTN_Q_PER_K = 1
ATTN_SCORE_BUFFERS = 3


def _scores_t(h, qt_ref, k_ref):
    return _dot(k_ref[h], qt_ref[h])


def _softmax_pv_t(h, st, vt_ref, m_ref, acc_ref, mask):
    if mask is not None:
        st = jnp.where(mask, st, -jnp.inf)
    m_prev = m_ref[h]
    m_next = jnp.maximum(m_prev, jnp.max(st, axis=0, keepdims=True))
    alpha = jnp.exp2(m_prev - m_next)
    p = jnp.exp2(st - m_next)
    m_ref[h] = m_next
    acc_ref[h] = alpha * acc_ref[h] + _dot(vt_ref[h], p.astype(BF16))


def _attn_block_t(q_ref, k_ref, vt_ref, m_ref, acc_ref, s_ref, mask):
    nbuf = s_ref.shape[0]
    for h in range(nbuf - 1):
        s_ref[h] = _scores_t(h, q_ref, k_ref)
    for h in range(MLA_HEADS):
        ahead = h + nbuf - 1
        if ahead < MLA_HEADS:
            s_ref[ahead % nbuf] = _scores_t(ahead, q_ref, k_ref)
        _softmax_pv_t(h, s_ref[h % nbuf], vt_ref, m_ref, acc_ref, mask)


def _attn_causal_body(qi_ref, kj_ref, q_ref, k_ref, vt_ref, o_ref, m_ref, acc_ref, s_ref):
    step = pl.program_id(0)
    qi = qi_ref[step]
    kj = kj_ref[step]
    tq, tk = q_ref.shape[2], k_ref.shape[1]
    per_q = tq // tk

    @pl.when(kj == 0)
    def _():
        m_ref[...] = jnp.full(m_ref.shape, NEG_BIG, F32)
        acc_ref[...] = jnp.zeros_like(acc_ref)

    @pl.when(kj < qi * per_q)
    def _():
        _attn_block_t(q_ref, k_ref, vt_ref, m_ref, acc_ref, s_ref, None)

    @pl.when(kj >= qi * per_q)
    def _():
        key_off = 0 if per_q == 1 else (kj - qi * per_q) * tk
        key_chunk = (lax.broadcasted_iota(jnp.int32, (tk, tq), 0) + key_off) // CHUNK
        query_chunk = lax.broadcasted_iota(jnp.int32, (tk, tq), 1) // CHUNK
        _attn_block_t(q_ref, k_ref, vt_ref, m_ref, acc_ref, s_ref, key_chunk <= query_chunk)

    @pl.when(kj == qi * per_q + per_q - 1)
    def _():
        for h in range(MLA_HEADS):
            out = acc_ref[h, 0:V_DIM, :] / acc_ref[h, V_DIM:V_DIM + 1, :]
            o_ref[:, h * V_DIM:(h + 1) * V_DIM] = out.T.astype(o_ref.dtype)


def _attn_causal(q, k, vt, tq, tk):
    rows = q.shape[2]
    assert tk % CHUNK == 0 and tq % tk == 0 and rows % tq == 0 and vt.shape[3] == tk
    per_q = tq // tk
    pairs = [(i, j) for i in range(rows // tq) for j in range(per_q * (i + 1))]
    qi = jnp.asarray(np.array([a for a, _ in pairs], np.int32))
    kj = jnp.asarray(np.array([b for _, b in pairs], np.int32))
    vw = MLA_HEADS * V_DIM
    blocks = MLA_HEADS * (tq * MLA_QK_PAD + tk * (MLA_QK_PAD + V_EXT)) * 2 + tq * vw * 2
    resident = (MLA_HEADS * (V_EXT + 8) * tq + (ATTN_SCORE_BUFFERS + 1) * tk * tq) * 4
    grid_spec = pltpu.PrefetchScalarGridSpec(
        num_scalar_prefetch=2,
        grid=(len(pairs),),
        in_specs=[pl.BlockSpec((MLA_HEADS, MLA_QK_PAD, tq), lambda p, qi, kj: (0, 0, qi[p])),
                  pl.BlockSpec((MLA_HEADS, tk, MLA_QK_PAD), lambda p, qi, kj: (0, kj[p], 0)),
                  pl.BlockSpec((None, MLA_HEADS, V_EXT, tk), lambda p, qi, kj: (kj[p], 0, 0, 0))],
        out_specs=pl.BlockSpec((tq, vw), lambda p, qi, kj: (qi[p], 0)),
        scratch_shapes=[pltpu.VMEM((MLA_HEADS, 1, tq), F32),
                        pltpu.VMEM((MLA_HEADS, V_EXT, tq), F32),
                        pltpu.VMEM((ATTN_SCORE_BUFFERS, tk, tq), F32)])
    return pl.pallas_call(
        _attn_causal_body,
        grid_spec=grid_spec,
        out_shape=jax.ShapeDtypeStruct((rows, vw), BF16),
        compiler_params=_params(("arbitrary",), blocks, resident),
        name="mla_attn_causal",
    )(qi, kj, q, k, vt)


def _attn_latent_body(q_ref, ckvp_ref, kpep_ref, ckvn_ref, kpadn_ref, wukt_ref, wuv_ref, o_ref,
                      ckv_ref, kpe_ref, qa_ref, qp_ref, *, q_pos0):
    tq = q_ref.shape[1]
    past = ckvp_ref.shape[0]
    n_keys = ckv_ref.shape[0]
    n_valid = past + tq

    ckv_ref[0:past, :] = ckvp_ref[...].astype(BF16)
    ckv_ref[past:n_valid, :] = ckvn_ref[...].astype(BF16)
    ckv_ref[n_valid:n_keys, :] = jnp.zeros((n_keys - n_valid, ckv_ref.shape[1]), BF16)
    kpe_ref[...] = jnp.zeros_like(kpe_ref)
    kpe_ref[0:past, 0:QK_ROPE] = kpep_ref[...].astype(BF16)
    kpe_ref[past:n_valid, :] = kpadn_ref[...]

    for h in range(MLA_HEADS):
        rows = slice(h * tq, (h + 1) * tq)
        qa_ref[rows, :] = _dot(q_ref[h, :, 0:QK_NOPE], wukt_ref[h]).astype(BF16)
        qp_ref[rows, :] = q_ref[h, :, QK_NOPE:MLA_QK_PAD]

    s = (lax.dot_general(qa_ref[...], ckv_ref[...], NT_DIMS, preferred_element_type=F32)
         + lax.dot_general(qp_ref[...], kpe_ref[...], NT_DIMS, preferred_element_type=F32))
    s = s.reshape(MLA_HEADS, tq, n_keys)
    query_chunk = (lax.broadcasted_iota(jnp.int32, (tq, n_keys), 0) + q_pos0) // CHUNK
    kpos = lax.broadcasted_iota(jnp.int32, (tq, n_keys), 1)
    mask = (kpos // CHUNK <= query_chunk) & (kpos < n_valid)
    s = jnp.where(mask[None], s, -jnp.inf)
    p = jnp.exp2(s - jnp.max(s, axis=2, keepdims=True))
    inv_l = 1.0 / jnp.sum(p, axis=2, keepdims=True)
    lat = _dot(p.astype(BF16).reshape(MLA_HEADS * tq, n_keys), ckv_ref[...])
    lat = (lat.reshape(MLA_HEADS, tq, -1) * inv_l).astype(BF16)
    for h in range(MLA_HEADS):
        o_ref[:, h * V_DIM:(h + 1) * V_DIM] = _dot(lat[h], wuv_ref[h]).astype(o_ref.dtype)


def _attn_latent(q, ckv_past, kpe_past, ckv_new, kpad_new, w_ukt, w_uv, layer, batch, q_pos0):
    tq = q.shape[1] // batch
    past, lora = ckv_past.shape[2], ckv_past.shape[3]
    n_keys = -(-(past + tq) // V7X_LANES) * V7X_LANES
    vw = MLA_HEADS * V_DIM
    rows = MLA_HEADS * tq
    blocks = (MLA_HEADS * tq * MLA_QK_PAD * 2 + past * (lora + V7X_LANES) * 4
              + tq * (lora * 4 + V7X_LANES * 2) + tq * vw * 2)
    resident = (2 * MLA_HEADS * lora * QK_NOPE * 2 + n_keys * (lora + V7X_LANES) * 2
                + rows * (lora + V7X_LANES) * 2 + 4 * rows * n_keys * 4 + 2 * rows * lora * 4)
    return pl.pallas_call(
        functools.partial(_attn_latent_body, q_pos0=q_pos0),
        grid=(batch,),
        in_specs=[pl.BlockSpec((MLA_HEADS, tq, MLA_QK_PAD), lambda b: (0, b, 0)),
                  pl.BlockSpec((None, None, past, lora), lambda b: (layer, b, 0, 0)),
                  pl.BlockSpec((None, None, past, QK_ROPE), lambda b: (layer, b, 0, 0)),
                  pl.BlockSpec((tq, lora), lambda b: (b, 0)),
                  pl.BlockSpec((tq, V7X_LANES), lambda b: (b, 0)),
                  _const_spec(w_ukt), _const_spec(w_uv)],
        out_specs=pl.BlockSpec((tq, vw), lambda b: (b, 0)),
        out_shape=jax.ShapeDtypeStruct((q.shape[1], vw), BF16),
        scratch_shapes=[pltpu.VMEM((n_keys, lora), BF16), pltpu.VMEM((n_keys, V7X_LANES), BF16),
                        pltpu.VMEM((rows, lora), BF16), pltpu.VMEM((rows, V7X_LANES), BF16)],
        compiler_params=_params(("parallel",), blocks, resident),
        name="mla_attn_latent",
    )(q, ckv_past, kpe_past, ckv_new, kpad_new, w_ukt, w_uv)


def _rope_tables(pos):
    half = QK_ROPE // 2
    inv = ROPE_THETA ** (-jnp.arange(half, dtype=F32) / half)
    ang = pos.astype(F32)[:, None] * inv
    cos, sin = jnp.cos(ang), jnp.sin(ang)
    z = lambda n: jnp.zeros((pos.shape[0], n), F32)
    pad = V7X_LANES - QK_ROPE
    tables = (jnp.concatenate([cos, cos, z(pad)], axis=1),
              jnp.concatenate([-sin, z(half + pad)], axis=1),
              jnp.concatenate([z(half), sin, z(pad)], axis=1))
    return tables, (cos.T, sin.T)


def _prep_mla(w_uq, w_dkv, w_ukv):
    qlora = w_uq.shape[0]
    lora = w_ukv.shape[0]
    wq = w_uq.reshape(qlora, MLA_HEADS, QK_NOPE + QK_ROPE)
    wq = jnp.pad(wq, ((0, 0), (0, 0), (0, MLA_QK_PAD - QK_NOPE - QK_ROPE)))
    w_uq_pad = wq.reshape(qlora, MLA_HEADS * MLA_QK_PAD).astype(BF16)
    w_dkv_pad = jnp.pad(w_dkv, ((0, 0), (0, V7X_LANES - QK_ROPE))).astype(BF16)
    wkv = w_ukv.reshape(lora, MLA_HEADS, QK_NOPE + V_DIM)
    w_uk = wkv[:, :, :QK_NOPE].reshape(lora, MLA_HEADS * QK_NOPE).astype(BF16)
    w_uvt = wkv[:, :, QK_NOPE:].reshape(lora, MLA_HEADS * V_DIM).T.astype(BF16)
    w_ukt_heads = jnp.transpose(wkv[:, :, :QK_NOPE], (1, 2, 0)).astype(BF16)
    w_uv_heads = jnp.transpose(wkv[:, :, QK_NOPE:], (1, 0, 2)).astype(BF16)
    return w_uq_pad, w_dkv_pad, w_uk, w_uvt, w_ukt_heads, w_uv_heads


def _stack(arrays):
    return arrays[0][None] if len(arrays) == 1 else jnp.stack(arrays)


def _trunk(x, batch, seq, past_len, mem_k, mem_v, s_hgrn, s_conv, ckv_past, kpe_past, p):
    depth = p["norm_g"].shape[0]
    rows = batch * seq
    pos = past_len + jnp.arange(seq, dtype=jnp.int32)
    tables, tables_t = _rope_tables(pos)
    tables = tuple(jnp.tile(t, (batch, 1)) for t in tables)
    causal = past_len == 0
    hs, cs, ckvs, kpes = [], [], [], []
    for l in range(depth):
        g = lambda i: p["norm_g"][l, i][None, :]
        x = _ffn(x, g(0), p["ffn_w1"], p["ffn_w2"], g(1), l, 0)
        mem = (g(4), p["w_mem_q"][l], mem_k, mem_v, p["w_mem_o"][l], g(5), l)
        if l % 2 == 0:
            e = l // 2
            proj = _norm_matmul(x, g(2), p["w_in0"][e], BF16, tm_want=1024)
            oa, s_new = _hgrn(proj, p["lbs"][l][None, :], p["hgrn_gnorm"][e][None, :],
                              s_hgrn[e], batch, seq)
            x, c_new = _even_out(oa, proj, p["conv_wt"][e], s_conv[e], p["w_out0"][e],
                                 g(3), x, mem, batch, seq)
            hs.append(s_new)
            cs.append(c_new)
        else:
            o = l // 2
            q, ckv_new, kpe_new, kpad_new = _mla_proj(
                x, g(2), p["mla_w_dq"][o], p["mla_q_norm"][o][None, :],
                p["w_uq_pad_t"][o] if causal else p["w_uq_pad"][o],
                p["w_dkv_pad"][o], p["mla_kv_norm"][o][None, :], tables,
                tables_t if causal else None)
            if causal:
                assert batch == 1
                tk = _row_tile(seq, ATTN_TILE)
                tq = tk * ATTN_Q_PER_K if seq % (tk * ATTN_Q_PER_K) == 0 else tk
                k, vt = _kv_up(ckv_new, kpad_new, p["w_uk"][o], p["w_uvt"][o], tk)
                att = _attn_causal(q, k, vt, tq, tk)
            else:
                att = _attn_latent(q, ckv_past, kpe_past, ckv_new, kpad_new,
                                   p["w_ukt_heads"][o], p["w_uv_heads"][o], o, batch, past_len)
            x = _proj_residual(att, p["mla_w_o"][o], g(3), x, mem, batch, seq)
            ckvs.append(ckv_new.reshape(batch, seq, -1))
            kpes.append(kpe_new.reshape(batch, seq, -1))
        x = _ffn(x, g(6), p["ffn_w1"], p["ffn_w2"], g(7), l, 1)
    return (x.reshape(batch, seq, -1), _stack(hs), _stack(cs), _stack(ckvs), _stack(kpes))


def kernel(x_prompt, x_sample, mem_prompt, state_hgrn, state_conv, cache_ckv, cache_kpe, cache_mem_k, cache_mem_v, norm_g, ffn_w1, ffn_w2, w_in0, hgrn_lb, hgrn_gnorm, conv_w, w_out0, mla_w_dq, mla_q_norm, mla_w_uq, mla_w_dkv, mla_kv_norm, mla_w_ukv, mla_w_o, mem_norm, w_mem_q, w_mem_kv, w_mem_o):
    batch, seq, d = x_prompt.shape
    dec_batch, dec_seq, _ = x_sample.shape
    past_len = cache_ckv.shape[2]
    depth = norm_g.shape[0]
    n_mem = mem_prompt.shape[1]
    mem_width = MEM_HEADS * MEM_DIM

    prepped = [_prep_mla(mla_w_uq[o], mla_w_dkv[o], mla_w_ukv[o]) for o in range(mla_w_uq.shape[0])]
    p = dict(
        norm_g=norm_g,
        ffn_w1=ffn_w1.astype(BF16), ffn_w2=ffn_w2.astype(BF16),
        w_in0=w_in0.astype(BF16), w_out0=w_out0.astype(BF16),
        lbs=jnp.cumsum(jax.nn.softmax(hgrn_lb.astype(F32), axis=0), axis=0),
        hgrn_gnorm=hgrn_gnorm,
        conv_wt=jnp.swapaxes(conv_w, 1, 2),
        mla_w_dq=mla_w_dq.astype(BF16), mla_q_norm=mla_q_norm, mla_kv_norm=mla_kv_norm,
        w_uq_pad=[t[0] for t in prepped], w_uq_pad_t=[t[0].T for t in prepped],
        w_dkv_pad=[t[1] for t in prepped],
        w_uk=[t[2] for t in prepped], w_uvt=[t[3] for t in prepped],
        w_ukt_heads=[t[4] for t in prepped], w_uv_heads=[t[5] for t in prepped],
        mla_w_o=mla_w_o.astype(BF16),
        w_mem_q=w_mem_q.astype(BF16), w_mem_o=w_mem_o.astype(BF16),
    )

    mem_rows = mem_prompt.reshape(batch * n_mem, d)
    mks, mvs = [], []
    for l in range(depth):
        kv = _norm_matmul(mem_rows, mem_norm[l][None, :], w_mem_kv[l].astype(BF16), F32,
                          tn_want=2 * mem_width)
        kv = kv.reshape(batch, n_mem, 2, mem_width)
        mks.append(kv[:, :, 0])
        mvs.append(kv[:, :, 1])
    mem_k_p = jnp.stack(mks)
    mem_v_p = jnp.stack(mvs)

    n_even = state_hgrn.shape[0]
    h0 = jnp.zeros((n_even, batch) + state_hgrn.shape[2:], F32)
    c0 = jnp.zeros((n_even, batch) + state_conv.shape[2:], F32)
    y_p, p_hgrn, p_conv, p_ckv, p_kpe = _trunk(
        x_prompt.reshape(batch * seq, d), batch, seq, 0, mem_k_p, mem_v_p, h0, c0, None, None, p)
    y_s, s_hgrn, s_conv, s_ckv, s_kpe = _trunk(
        x_sample.reshape(dec_batch * dec_seq, d), dec_batch, dec_seq, past_len,
        cache_mem_k.reshape(depth, dec_batch, n_mem, mem_width),
        cache_mem_v.reshape(depth, dec_batch, n_mem, mem_width),
        state_hgrn, state_conv, cache_ckv, cache_kpe, p)

    mem_shape = (depth, batch, n_mem, MEM_HEADS, MEM_DIM)
    return (y_p, y_s, p_hgrn, p_conv, p_ckv, p_kpe,
            mem_k_p.reshape(mem_shape), mem_v_p.reshape(mem_shape),
            s_hgrn, s_conv, s_ckv, s_kpe)
```

```python
import functools

import jax
import jax.numpy as jnp
import numpy as np
from jax import lax
from jax.experimental import pallas as pl
from jax.experimental.pallas import tpu as pltpu

F32 = jnp.float32
BF16 = jnp.bfloat16

V7X_LANES = 128
V7X_SUBLANES = 8
V7X_VMEM_BYTES = 64 * 1024 * 1024
MIB = 1024 * 1024
SPILL_ROOM_BYTES = 8 * MIB
VMEM_RESERVED_BYTES = 2 * MIB

EPS = 1e-6
CHUNK = 64
HGRN_HEADS = 8
HGRN_DK = 128
HGRN_SUB = 16
HGRN_BLOCK_ROWS = 256
MLA_HEADS = 16
QK_NOPE = 128
QK_ROPE = 64
V_DIM = 128
V_EXT = V_DIM + 16
MLA_QK_PAD = 256
MLA_SCALE = (QK_NOPE + QK_ROPE) ** -0.5
ROPE_THETA = 10000.0
MEM_HEADS = 4
MEM_DIM = 128
CONV_W = 3
NEG_BIG = -1e30
LOG2E = 1.4426950408889634

NT_DIMS = (((1,), (1,)), ((), ()))
TN_DIMS = (((0,), (0,)), ((), ()))


def _vmem_limit(pipelined_bytes, resident_bytes):
    want = 2 * pipelined_bytes + resident_bytes + SPILL_ROOM_BYTES
    return int(min(want, V7X_VMEM_BYTES - VMEM_RESERVED_BYTES))


def _params(semantics, pipelined_bytes, resident_bytes):
    return pltpu.CompilerParams(
        dimension_semantics=semantics,
        vmem_limit_bytes=_vmem_limit(pipelined_bytes, resident_bytes))


def _row_tile(rows, want):
    if rows <= want:
        return rows
    t = want - want % V7X_LANES
    while t > 0 and rows % t:
        t -= V7X_LANES
    assert t > 0, (rows, want)
    return t


ROW_CHUNK = 256


def _row_chunks(rows):
    return [slice(r0, min(r0 + ROW_CHUNK, rows)) for r0 in range(0, rows, ROW_CHUNK)]


def _rms(x, g):
    ms = jnp.mean(x * x, axis=-1, keepdims=True)
    return x * lax.rsqrt(ms + EPS) * g


def _sigmoid(x):
    return 1.0 / (1.0 + jnp.exp(-x))


def _dot(a, b):
    return jnp.dot(a, b, preferred_element_type=F32)


FFN_COL_CHUNK = 512


def _ffn_hidden(xn, w1g_ref, w1u_ref):
    gate = _dot(xn, w1g_ref[...])
    up = _dot(xn, w1u_ref[...])
    return (gate * _sigmoid(gate) * up).astype(BF16)


def _ffn_body(x_ref, gpre_ref, w1g_ref, w1u_ref, w2_ref, gpost_ref, o_ref, xn_ref):
    k = pl.program_id(1)
    last = pl.num_programs(1) - 1
    tm, d = o_ref.shape

    @pl.when(k == 0)
    def _():
        gpre = gpre_ref[...]
        for rs in _row_chunks(tm):
            xn = _rms(x_ref[rs, :], gpre).astype(BF16)
            xn_ref[rs, :] = xn
            o_ref[rs, :] = _dot(_ffn_hidden(xn, w1g_ref, w1u_ref), w2_ref[...])

    @pl.when((k > 0) & (k < last))
    def _():
        o_ref[...] += _dot(_ffn_hidden(xn_ref[...], w1g_ref, w1u_ref), w2_ref[...])

    @pl.when(k == last)
    def _():
        h = _ffn_hidden(xn_ref[...], w1g_ref, w1u_ref)
        ss = jnp.zeros((tm, 1), F32)
        for c0 in range(0, d, FFN_COL_CHUNK):
            cs = slice(c0, c0 + FFN_COL_CHUNK)
            y = o_ref[:, cs] + _dot(h, w2_ref[:, cs])
            o_ref[:, cs] = y
            ss = ss + jnp.sum(y * y, axis=-1, keepdims=True)
        scale = 0.5 * lax.rsqrt(ss * (1.0 / d) + EPS)
        o_ref[...] = x_ref[...] + o_ref[...] * scale * gpost_ref[...]


def _ffn(x, g_pre, w1, w2, g_post, layer, half, tm_want=1024, tf=512):
    rows, d = x.shape
    dff = w2.shape[-2]
    tm = _row_tile(rows, tm_want)
    nk = dff // tf
    assert dff % tf == 0 and nk >= 2 and d % FFN_COL_CHUNK == 0
    blocks = 2 * tm * d * 4 + 3 * d * tf * 2
    resident = tm * d * (2 + 4) + 4 * tm * tf * 4
    return pl.pallas_call(
        _ffn_body,
        grid=(rows // tm, nk),
        in_specs=[
            pl.BlockSpec((tm, d), lambda i, k: (i, 0)),
            pl.BlockSpec((1, d), lambda i, k: (0, 0)),
            pl.BlockSpec((None, None, d, tf), lambda i, k: (layer, half, 0, k)),
            pl.BlockSpec((None, None, d, tf), lambda i, k: (layer, half, 0, nk + k)),
            pl.BlockSpec((None, None, tf, d), lambda i, k: (layer, half, k, 0)),
            pl.BlockSpec((1, d), lambda i, k: (0, 0)),
        ],
        out_specs=pl.BlockSpec((tm, d), lambda i, k: (i, 0)),
        out_shape=jax.ShapeDtypeStruct((rows, d), F32),
        scratch_shapes=[pltpu.VMEM((tm, d), BF16)],
        compiler_params=_params(("parallel", "arbitrary"), blocks, resident),
        name="ffn",
    )(x, g_pre, w1, w1, w2, g_post)


def _norm_matmul_body(x_ref, g_ref, w_ref, o_ref, xn_ref):
    j = pl.program_id(1)

    @pl.when(j == 0)
    def _():
        g = g_ref[...]
        for rs in _row_chunks(x_ref.shape[0]):
            xn = _rms(x_ref[rs, :], g).astype(BF16)
            xn_ref[rs, :] = xn
            o_ref[rs, :] = _dot(xn, w_ref[...]).astype(o_ref.dtype)

    @pl.when(j > 0)
    def _():
        o_ref[...] = _dot(xn_ref[...], w_ref[...]).astype(o_ref.dtype)


def _norm_matmul(x, g, w, out_dtype, tm_want=512, tn_want=1024):
    rows, d = x.shape
    n = w.shape[1]
    tm = _row_tile(rows, tm_want)
    tn = _row_tile(n, tn_want)
    blocks = tm * d * 4 + d * tn * 2 + tm * tn * 4
    resident = tm * d * 2 + tm * tn * 4
    return pl.pallas_call(
        _norm_matmul_body,
        grid=(rows // tm, n // tn),
        in_specs=[
            pl.BlockSpec((tm, d), lambda i, j: (i, 0)),
            pl.BlockSpec((1, d), lambda i, j: (0, 0)),
            pl.BlockSpec((d, tn), lambda i, j: (0, j)),
        ],
        out_specs=pl.BlockSpec((tm, tn), lambda i, j: (i, j)),
        out_shape=jax.ShapeDtypeStruct((rows, n), out_dtype),
        scratch_shapes=[pltpu.VMEM((tm, d), BF16)],
        compiler_params=_params(("parallel", "arbitrary"), blocks, resident),
        name="norm_matmul",
    )(x, g, w)


def _mem_sublayer(x, gpre_ref, wq_ref, mkt_ref, mv_ref, wo_ref, gpost_ref):
    q = _dot(_rms(x, gpre_ref[...]).astype(BF16), wq_ref[...])
    outs = []
    for h in range(MEM_HEADS):
        hs = slice(h * MEM_DIM, (h + 1) * MEM_DIM)
        kht = mkt_ref[hs, :].astype(BF16)
        vh = mv_ref[:, hs].astype(BF16)
        s = _dot(q[:, hs].astype(BF16), kht) * (MEM_DIM ** -0.5)
        p = jnp.exp(s - jnp.max(s, axis=1, keepdims=True))
        p = p * (1.0 / jnp.sum(p, axis=1, keepdims=True))
        outs.append(_dot(p.astype(BF16), vh).astype(BF16))
    o = jnp.concatenate(outs, axis=1)
    return x + _rms(_dot(o, wo_ref[...]), gpost_ref[...])


def _const_spec(a):
    return pl.BlockSpec(a.shape, lambda *_: (0,) * a.ndim, pipeline_mode=pl.Buffered(1))


def _mem_specs(mem_kt, w_q, w_o, g_pre, g_post, layer):
    width, n_mem = mem_kt.shape[2], mem_kt.shape[3]
    pick = lambda b, t: (layer, b, 0, 0)
    specs = [_const_spec(g_pre), _const_spec(w_q),
             pl.BlockSpec((None, None, width, n_mem), pick),
             pl.BlockSpec((None, None, n_mem, width), pick),
             _const_spec(w_o), _const_spec(g_post)]
    nbytes = (w_q.size + w_o.size) * 2 + 4 * n_mem * width * 4
    return specs, nbytes


def _proj_residual_body(a_ref, w_ref, g_ref, x_ref,
                        gpre_ref, wq_ref, mk_ref, mv_ref, wo_ref, gpost_ref, o_ref):
    x1 = x_ref[...] + _rms(_dot(a_ref[...], w_ref[...]), g_ref[...])
    o_ref[...] = _mem_sublayer(x1, gpre_ref, wq_ref, mk_ref, mv_ref, wo_ref, gpost_ref)


def _proj_residual(a, w, g, x, mem, batch, seq, tm_want=512):
    g_pre, w_q, mem_k, mem_v, w_o, g_post, layer = mem
    kdim = a.shape[1]
    d = w.shape[1]
    tm = _row_tile(seq, tm_want)
    nt = seq // tm
    row = lambda n: pl.BlockSpec((tm, n), lambda b, t: (b * nt + t, 0))
    mem_specs, mem_bytes = _mem_specs(mem_k, w_q, w_o, g_pre, g_post, layer)
    blocks = tm * kdim * 2 + 2 * tm * d * 4
    resident = kdim * d * 2 + mem_bytes + 4 * tm * d * 4
    return pl.pallas_call(
        _proj_residual_body,
        grid=(batch, nt),
        in_specs=[row(kdim), _const_spec(w), _const_spec(g), row(d)] + mem_specs,
        out_specs=row(d),
        out_shape=jax.ShapeDtypeStruct(x.shape, F32),
        compiler_params=_params(("parallel", "parallel"), blocks, resident),
        name="proj_residual_mem",
    )(a, w, g, x, g_pre, w_q, mem_k, mem_v, w_o, g_post)


def _hgrn_body(qa_ref, fa_ref, ia_ref, ga_ref, lb_ref, gn_ref, s0_ref,
               oa_ref, sout_ref, st_ref, o_ref):
    c = pl.program_id(1)
    rows = qa_ref.shape[0]
    width = qa_ref.shape[1]
    nsub = rows // HGRN_SUB

    @pl.when(c == 0)
    def _():
        for h in range(HGRN_HEADS):
            st_ref[h] = s0_ref[h].T

    lb = lb_ref[...]
    qa = qa_ref[...].astype(F32)
    q = qa * _sigmoid(qa) * (HGRN_DK ** -0.5)
    f = lb + (1.0 - lb) * _sigmoid(fa_ref[...].astype(F32))
    k = 1.0 - f
    g = jnp.log(f) * LOG2E
    v = ia_ref[...].astype(F32)

    r_i = lax.broadcasted_iota(jnp.int32, (rows, rows), 0)
    c_i = lax.broadcasted_iota(jnp.int32, (rows, rows), 1)
    tri = ((c_i <= r_i) & ((c_i // HGRN_SUB) == (r_i // HGRN_SUB))).astype(BF16)
    g1 = g.astype(BF16)
    rem = g - g1.astype(F32)
    g2 = rem.astype(BF16)
    g3 = (rem - g2.astype(F32)).astype(BF16)
    bl = _dot(tri, g1) + _dot(tri, g2) + _dot(tri, g3)

    qe = (q * jnp.exp2(bl)).astype(BF16)
    row8 = lax.broadcasted_iota(jnp.int32, (V7X_SUBLANES, width), 0)
    tiles = HGRN_SUB // V7X_SUBLANES

    for i in range(nsub):
        r0 = i * HGRN_SUB
        bl_i = bl[r0:r0 + HGRN_SUB]
        q_i = q[r0:r0 + HGRN_SUB]
        k_i = k[r0:r0 + HGRN_SUB]
        v_i = v[r0:r0 + HGRN_SUB]
        b_end = bl_i[HGRN_SUB - 1:HGRN_SUB]
        ke_i = (k_i * jnp.exp2(b_end - bl_i)).astype(BF16)
        dec_i = jnp.exp2(b_end)
        v_bf = v_i.astype(BF16)

        diag = [[jnp.zeros((V7X_SUBLANES, HGRN_DK), F32) for _ in range(tiles)]
                for _ in range(HGRN_HEADS)]
        for s in range(HGRN_SUB):
            ks, bs, vs = k_i[s:s + 1], bl_i[s:s + 1], v_i[s:s + 1]
            for rt in range(s // V7X_SUBLANES, tiles):
                rsl = slice(rt * V7X_SUBLANES, (rt + 1) * V7X_SUBLANES)
                rel = bl_i[rsl] - bs
                if rt == s // V7X_SUBLANES:
                    rel = jnp.where(row8 >= s % V7X_SUBLANES, rel, -jnp.inf)
                w = q_i[rsl] * ks * jnp.exp2(rel)
                for h in range(HGRN_HEADS):
                    hs = slice(h * HGRN_DK, (h + 1) * HGRN_DK)
                    col = jnp.sum(w[:, hs], axis=1, keepdims=True)
                    diag[h][rt] = diag[h][rt] + col * vs[:, hs]

        for h in range(HGRN_HEADS):
            hs = slice(h * HGRN_DK, (h + 1) * HGRN_DK)
            st = st_ref[h]
            inter = lax.dot_general(qe[r0:r0 + HGRN_SUB, hs], st.astype(BF16), NT_DIMS,
                                    preferred_element_type=F32)
            upd = lax.dot_general(v_bf[:, hs], ke_i[:, hs], TN_DIMS,
                                  preferred_element_type=F32)
            st_ref[h] = st * dec_i[:, hs] + upd
            o_ref[r0:r0 + HGRN_SUB, hs] = inter + jnp.concatenate(diag[h], axis=0)

    gn = gn_ref[...]
    for h in range(HGRN_HEADS):
        hs = slice(h * HGRN_DK, (h + 1) * HGRN_DK)
        ga = ga_ref[:, hs].astype(F32)
        oa_ref[:, hs] = (_rms(o_ref[:, hs], gn) * (ga * _sigmoid(ga))).astype(oa_ref.dtype)

    @pl.when(c == pl.num_programs(1) - 1)
    def _():
        for h in range(HGRN_HEADS):
            sout_ref[h] = st_ref[h].T


def _hgrn(proj, lb, gnorm, s0, batch, seq):
    width = HGRN_HEADS * HGRN_DK
    rows = min(HGRN_BLOCK_ROWS, seq)
    nc = seq // rows
    assert seq % rows == 0 and rows % HGRN_SUB == 0
    col = lambda j: pl.BlockSpec((rows, width), lambda b, c: (b * nc + c, j))
    state_spec = pl.BlockSpec((None, HGRN_HEADS, HGRN_DK, HGRN_DK), lambda b, c: (b, 0, 0, 0))
    blocks = 4 * rows * width * 4 + rows * width * 2 + 2 * HGRN_HEADS * HGRN_DK * HGRN_DK * 4
    resident = HGRN_HEADS * HGRN_DK * HGRN_DK * 4 + 16 * rows * width * 4
    return pl.pallas_call(
        _hgrn_body,
        grid=(batch, nc),
        in_specs=[col(0), col(1), col(2), col(3),
                  pl.BlockSpec((1, width), lambda b, c: (0, 0)),
                  pl.BlockSpec((1, HGRN_DK), lambda b, c: (0, 0)),
                  state_spec],
        out_specs=[pl.BlockSpec((rows, width), lambda b, c: (b * nc + c, 0)), state_spec],
        out_shape=[jax.ShapeDtypeStruct((batch * seq, width), BF16),
                   jax.ShapeDtypeStruct(s0.shape, F32)],
        scratch_shapes=[pltpu.VMEM((HGRN_HEADS, HGRN_DK, HGRN_DK), F32),
                        pltpu.VMEM((rows, width), F32)],
        compiler_params=_params(("parallel", "arbitrary"), blocks, resident),
        name="hgrn2",
    )(proj, proj, proj, proj, lb, gnorm, s0)


CONV_PAD = 8


def _even_out_body(oa_ref, bg_ref, cg_ref, hb_ref, cw_ref, cs_ref, w_ref, g_ref, x_ref,
                   gpre_ref, wq_ref, mk_ref, mv_ref, wo_ref, gpost_ref,
                   o_ref, cnew_ref, ubuf_ref):
    t = pl.program_id(1)
    tm = oa_ref.shape[0]
    half = oa_ref.shape[1]
    lo = CONV_PAD - (CONV_W - 1)

    @pl.when(t == 0)
    def _():
        ubuf_ref[lo:CONV_PAD, :] = cs_ref[...]

    u = cg_ref[...].astype(F32) * hb_ref[...].astype(F32)
    ubuf_ref[CONV_PAD:CONV_PAD + tm, :] = u
    yb = u * cw_ref[CONV_W - 1:CONV_W, :]
    for j in range(CONV_W - 1):
        yb = yb + ubuf_ref[lo + j:lo + j + tm, :] * cw_ref[j:j + 1, :]
    ob = (bg_ref[...].astype(F32) * yb).astype(BF16)
    y = _dot(oa_ref[...], w_ref[0:half, :]) + _dot(ob, w_ref[half:2 * half, :])
    x1 = x_ref[...] + _rms(y, g_ref[...])
    o_ref[...] = _mem_sublayer(x1, gpre_ref, wq_ref, mk_ref, mv_ref, wo_ref, gpost_ref)

    tail = ubuf_ref[lo + tm:CONV_PAD + tm, :]
    ubuf_ref[lo:CONV_PAD, :] = tail

    @pl.when(t == pl.num_programs(1) - 1)
    def _():
        cnew_ref[...] = tail


def _even_out(oa, proj, conv_wt, conv_state, w_out, g, x, mem, batch, seq, tm_want=512):
    g_pre, w_q, mem_k, mem_v, w_o, g_post, layer = mem
    half = oa.shape[1]
    d = x.shape[1]
    tm = _row_tile(seq, tm_want)
    nt = seq // tm
    row = lambda b, t: (b * nt + t, 0)
    pcol = lambda j: pl.BlockSpec((tm, half), lambda b, t: (b * nt + t, j))
    cstate = pl.BlockSpec((None, CONV_W - 1, half), lambda b, t: (b, 0, 0))
    mem_specs, mem_bytes = _mem_specs(mem_k, w_q, w_o, g_pre, g_post, layer)
    blocks = tm * half * 4 * proj.dtype.itemsize + 2 * tm * d * 4
    resident = (2 * half * d * 2 + mem_bytes + (tm + CONV_PAD) * half * 4 + 4 * tm * half * 4
                + 4 * tm * d * 4)
    return pl.pallas_call(
        _even_out_body,
        grid=(batch, nt),
        in_specs=[pl.BlockSpec((tm, half), row), pcol(4), pcol(5), pcol(6),
                  _const_spec(conv_wt), cstate, _const_spec(w_out), _const_spec(g),
                  pl.BlockSpec((tm, d), row)] + mem_specs,
        out_specs=[pl.BlockSpec((tm, d), row), cstate],
        out_shape=[jax.ShapeDtypeStruct(x.shape, F32),
                   jax.ShapeDtypeStruct(conv_state.shape, F32)],
        scratch_shapes=[pltpu.VMEM((tm + CONV_PAD, half), F32)],
        compiler_params=_params(("parallel", "arbitrary"), blocks, resident),
        name="even_out_mem",
    )(oa, proj, proj, proj, conv_wt, conv_state, w_out, g, x,
      g_pre, w_q, mem_k, mem_v, w_o, g_post)


Q_PRESCALE = MLA_SCALE * LOG2E


def _rope(x, cos, sin_lo, sin_hi):
    half = QK_ROPE // 2
    return (x * cos
            + pltpu.roll(x, V7X_LANES - half, axis=1) * sin_lo
            + pltpu.roll(x, half, axis=1) * sin_hi)


def _mla_proj_body(x_ref, g_ref, wdq_ref, qn_ref, wuq_ref, wdkv_ref, kvn_ref,
                   cos_ref, slo_ref, shi_ref, *rest, q_transposed):
    if q_transposed:
        cost_ref, sint_ref, q_ref, ckv_ref, kpe_ref, kpad_ref = rest
    else:
        q_ref, ckv_ref, kpe_ref, kpad_ref = rest
    cos, slo, shi = cos_ref[...], slo_ref[...], shi_ref[...]
    lora = ckv_ref.shape[1]
    xn = _rms(x_ref[...], g_ref[...]).astype(BF16)
    cq = _rms(_dot(xn, wdq_ref[...]), qn_ref[...]).astype(BF16)
    ckr = _dot(xn, wdkv_ref[...])
    ckv_ref[...] = _rms(ckr[:, :lora], kvn_ref[...])
    kp = _rope(ckr[:, lora:lora + V7X_LANES], cos, slo, shi)
    kpe_ref[...] = kp[:, :QK_ROPE]
    kpad_ref[...] = kp.astype(BF16)
    if q_transposed:
        half = QK_ROPE // 2
        cost, sint = cost_ref[...], sint_ref[...]
        qt = lax.dot_general(wuq_ref[...], cq, NT_DIMS, preferred_element_type=F32) * Q_PRESCALE
        for h in range(MLA_HEADS):
            r0 = h * MLA_QK_PAD
            x1 = qt[r0 + QK_NOPE:r0 + QK_NOPE + half]
            x2 = qt[r0 + QK_NOPE + half:r0 + QK_NOPE + QK_ROPE]
            q_ref[h, 0:QK_NOPE, :] = qt[r0:r0 + QK_NOPE].astype(BF16)
            q_ref[h, QK_NOPE:QK_NOPE + half, :] = (x1 * cost - x2 * sint).astype(BF16)
            q_ref[h, QK_NOPE + half:QK_NOPE + QK_ROPE, :] = (x1 * sint + x2 * cost).astype(BF16)
            q_ref[h, QK_NOPE + QK_ROPE:MLA_QK_PAD, :] = qt[
                r0 + QK_NOPE + QK_ROPE:r0 + MLA_QK_PAD].astype(BF16)
    else:
        q = _dot(cq, wuq_ref[...]) * Q_PRESCALE
        for h in range(MLA_HEADS):
            c0 = h * MLA_QK_PAD
            q_ref[h, :, 0:QK_NOPE] = q[:, c0:c0 + QK_NOPE].astype(BF16)
            q_ref[h, :, QK_NOPE:MLA_QK_PAD] = _rope(
                q[:, c0 + QK_NOPE:c0 + MLA_QK_PAD], cos, slo, shi).astype(BF16)


def _mla_proj(x, g, w_dq, q_norm, w_uq, w_dkv_pad, kv_norm, tables, tables_t=None, tm_want=512):
    q_transposed = tables_t is not None
    rows, d = x.shape
    qlora = w_dq.shape[1]
    lora = kv_norm.shape[1]
    qw = MLA_HEADS * MLA_QK_PAD
    kw = w_dkv_pad.shape[1]
    tm = _row_tile(rows, tm_want)
    full = lambda a: pl.BlockSpec(a.shape, lambda i: (0,) * a.ndim)
    rowspec = lambda n: pl.BlockSpec((tm, n), lambda i: (i, 0))
    blocks = (tm * d * 4 + (d * qlora + qlora * qw + d * kw) * 2 + 4 * tm * V7X_LANES * 4
              + tm * (qw * 2 + lora * 4 + QK_ROPE * 4 + V7X_LANES * 2))
    resident = tm * (d * 6 + qw * 4 + kw * 4 + qlora * 8)
    in_specs = [rowspec(d), full(g), full(w_dq), full(q_norm), full(w_uq),
                full(w_dkv_pad), full(kv_norm),
                rowspec(V7X_LANES), rowspec(V7X_LANES), rowspec(V7X_LANES)]
    operands = [x, g, w_dq, q_norm, w_uq, w_dkv_pad, kv_norm, *tables]
    if q_transposed:
        in_specs += [pl.BlockSpec((QK_ROPE // 2, tm), lambda i: (0, i))] * 2
        operands += list(tables_t)
        q_spec = pl.BlockSpec((MLA_HEADS, MLA_QK_PAD, tm), lambda i: (0, 0, i))
        q_shape = (MLA_HEADS, MLA_QK_PAD, rows)
    else:
        q_spec = pl.BlockSpec((MLA_HEADS, tm, MLA_QK_PAD), lambda i: (0, i, 0))
        q_shape = (MLA_HEADS, rows, MLA_QK_PAD)
    return pl.pallas_call(
        functools.partial(_mla_proj_body, q_transposed=q_transposed),
        grid=(rows // tm,),
        in_specs=in_specs,
        out_specs=[q_spec, rowspec(lora), rowspec(QK_ROPE), rowspec(V7X_LANES)],
        out_shape=[jax.ShapeDtypeStruct(q_shape, BF16),
                   jax.ShapeDtypeStruct((rows, lora), F32),
                   jax.ShapeDtypeStruct((rows, QK_ROPE), F32),
                   jax.ShapeDtypeStruct((rows, V7X_LANES), BF16)],
        compiler_params=_params(("parallel",), blocks, resident),
        name="mla_proj",
    )(*operands)


def _kv_up_body(ckv_ref, kpad_ref, wuk_ref, wuvt_ref, k_ref, vt_ref):
    c = ckv_ref[...].astype(BF16)
    kn = _dot(c, wuk_ref[...])
    vt = lax.dot_general(wuvt_ref[...], c, NT_DIMS, preferred_element_type=F32)
    ts = c.shape[0]
    vt_ref[:, 0:V_DIM, :] = vt.astype(BF16).reshape(MLA_HEADS, V_DIM, ts)
    ones_row = lax.broadcasted_iota(jnp.int32, (MLA_HEADS, V_EXT - V_DIM, ts), 1) == 0
    vt_ref[:, V_DIM:V_EXT, :] = ones_row.astype(BF16)
    kpad = kpad_ref[...]
    for h in range(MLA_HEADS):
        k_ref[h, :, 0:QK_NOPE] = kn[:, h * QK_NOPE:(h + 1) * QK_NOPE].astype(BF16)
        k_ref[h, :, QK_NOPE:MLA_QK_PAD] = kpad


def _kv_up(ckv, kpad, w_uk, w_uvt, ts):
    rows, lora = ckv.shape
    assert rows % ts == 0
    rowspec = lambda n: pl.BlockSpec((ts, n), lambda i: (i, 0))
    full = lambda a: pl.BlockSpec(a.shape, lambda i: (0,) * a.ndim)
    kw = MLA_HEADS * MLA_QK_PAD
    vw = MLA_HEADS * V_DIM
    blocks = ts * (lora * 4 + V7X_LANES * 2 + kw * 2 + vw * 2) + 2 * lora * vw * 2
    resident = ts * (kw + vw) * 4
    return pl.pallas_call(
        _kv_up_body,
        grid=(rows // ts,),
        in_specs=[rowspec(lora), rowspec(V7X_LANES), full(w_uk), full(w_uvt)],
        out_specs=[pl.BlockSpec((MLA_HEADS, ts, MLA_QK_PAD), lambda i: (0, i, 0)),
                   pl.BlockSpec((None, MLA_HEADS, V_EXT, ts), lambda i: (i, 0, 0, 0))],
        out_shape=[jax.ShapeDtypeStruct((MLA_HEADS, rows, MLA_QK_PAD), BF16),
                   jax.ShapeDtypeStruct((rows // ts, MLA_HEADS, V_EXT, ts), BF16)],
        compiler_params=_params(("parallel",), blocks, resident),
        name="kv_up",
    )(ckv, kpad, w_uk, w_uvt)


ATTN_TILE = 512
ATTN_Q_PER_K = 1
ATTN_SCORE_BUFFERS = 3


def _scores_t(h, qt_ref, k_ref):
    return _dot(k_ref[h], qt_ref[h])


def _softmax_pv_t(h, st, vt_ref, m_ref, acc_ref, mask):
    if mask is not None:
        st = jnp.where(mask, st, -jnp.inf)
    m_prev = m_ref[h]
    m_next = jnp.maximum(m_prev, jnp.max(st, axis=0, keepdims=True))
    alpha = jnp.exp2(m_prev - m_next)
    p = jnp.exp2(st - m_next)
    m_ref[h] = m_next
    acc_ref[h] = alpha * acc_ref[h] + _dot(vt_ref[h], p.astype(BF16))


def _attn_block_t(q_ref, k_ref, vt_ref, m_ref, acc_ref, s_ref, mask):
    nbuf = s_ref.shape[0]
    for h in range(nbuf - 1):
        s_ref[h] = _scores_t(h, q_ref, k_ref)
    for h in range(MLA_HEADS):
        ahead = h + nbuf - 1
        if ahead < MLA_HEADS:
            s_ref[ahead % nbuf] = _scores_t(ahead, q_ref, k_ref)
        _softmax_pv_t(h, s_ref[h % nbuf], vt_ref, m_ref, acc_ref, mask)


def _attn_causal_body(qi_ref, kj_ref, q_ref, k_ref, vt_ref, o_ref, m_ref, acc_ref, s_ref):
    step = pl.program_id(0)
    qi = qi_ref[step]
    kj = kj_ref[step]
    tq, tk = q_ref.shape[2], k_ref.shape[1]
    per_q = tq // tk

    @pl.when(kj == 0)
    def _():
        m_ref[...] = jnp.full(m_ref.shape, NEG_BIG, F32)
        acc_ref[...] = jnp.zeros_like(acc_ref)

    @pl.when(kj < qi * per_q)
    def _():
        _attn_block_t(q_ref, k_ref, vt_ref, m_ref, acc_ref, s_ref, None)

    @pl.when(kj >= qi * per_q)
    def _():
        key_off = 0 if per_q == 1 else (kj - qi * per_q) * tk
        key_chunk = (lax.broadcasted_iota(jnp.int32, (tk, tq), 0) + key_off) // CHUNK
        query_chunk = lax.broadcasted_iota(jnp.int32, (tk, tq), 1) // CHUNK
        _attn_block_t(q_ref, k_ref, vt_ref, m_ref, acc_ref, s_ref, key_chunk <= query_chunk)

    @pl.when(kj == qi * per_q + per_q - 1)
    def _():
        for h in range(MLA_HEADS):
            out = acc_ref[h, 0:V_DIM, :] / acc_ref[h, V_DIM:V_DIM + 1, :]
            o_ref[:, h * V_DIM:(h + 1) * V_DIM] = out.T.astype(o_ref.dtype)


def _attn_causal(q, k, vt, tq, tk):
    rows = q.shape[2]
    assert tk % CHUNK == 0 and tq % tk == 0 and rows % tq == 0 and vt.shape[3] == tk
    per_q = tq // tk
    pairs = [(i, j) for i in range(rows // tq) for j in range(per_q * (i + 1))]
    qi = jnp.asarray(np.array([a for a, _ in pairs], np.int32))
    kj = jnp.asarray(np.array([b for _, b in pairs], np.int32))
    vw = MLA_HEADS * V_DIM
    blocks = MLA_HEADS * (tq * MLA_QK_PAD + tk * (MLA_QK_PAD + V_EXT)) * 2 + tq * vw * 2
    resident = (MLA_HEADS * (V_EXT + 8) * tq + (ATTN_SCORE_BUFFERS + 1) * tk * tq) * 4
    grid_spec = pltpu.PrefetchScalarGridSpec(
        num_scalar_prefetch=2,
        grid=(len(pairs),),
        in_specs=[pl.BlockSpec((MLA_HEADS, MLA_QK_PAD, tq), lambda p, qi, kj: (0, 0, qi[p])),
                  pl.BlockSpec((MLA_HEADS, tk, MLA_QK_PAD), lambda p, qi, kj: (0, kj[p], 0)),
                  pl.BlockSpec((None, MLA_HEADS, V_EXT, tk), lambda p, qi, kj: (kj[p], 0, 0, 0))],
        out_specs=pl.BlockSpec((tq, vw), lambda p, qi, kj: (qi[p], 0)),
        scratch_shapes=[pltpu.VMEM((MLA_HEADS, 1, tq), F32),
                        pltpu.VMEM((MLA_HEADS, V_EXT, tq), F32),
                        pltpu.VMEM((ATTN_SCORE_BUFFERS, tk, tq), F32)])
    return pl.pallas_call(
        _attn_causal_body,
        grid_spec=grid_spec,
        out_shape=jax.ShapeDtypeStruct((rows, vw), BF16),
        compiler_params=_params(("arbitrary",), blocks, resident),
        name="mla_attn_causal",
    )(qi, kj, q, k, vt)


def _attn_latent_body(q_ref, ckvp_ref, kpep_ref, ckvn_ref, kpadn_ref, wukt_ref, wuv_ref, o_ref,
                      ckv_ref, kpe_ref, qa_ref, qp_ref, *, q_pos0):
    tq = q_ref.shape[1]
    past = ckvp_ref.shape[0]
    n_keys = ckv_ref.shape[0]
    n_valid = past + tq

    ckv_ref[0:past, :] = ckvp_ref[...].astype(BF16)
    ckv_ref[past:n_valid, :] = ckvn_ref[...].astype(BF16)
    ckv_ref[n_valid:n_keys, :] = jnp.zeros((n_keys - n_valid, ckv_ref.shape[1]), BF16)
    kpe_ref[...] = jnp.zeros_like(kpe_ref)
    kpe_ref[0:past, 0:QK_ROPE] = kpep_ref[...].astype(BF16)
    kpe_ref[past:n_valid, :] = kpadn_ref[...]

    for h in range(MLA_HEADS):
        rows = slice(h * tq, (h + 1) * tq)
        qa_ref[rows, :] = _dot(q_ref[h, :, 0:QK_NOPE], wukt_ref[h]).astype(BF16)
        qp_ref[rows, :] = q_ref[h, :, QK_NOPE:MLA_QK_PAD]

    s = (lax.dot_general(qa_ref[...], ckv_ref[...], NT_DIMS, preferred_element_type=F32)
         + lax.dot_general(qp_ref[...], kpe_ref[...], NT_DIMS, preferred_element_type=F32))
    s = s.reshape(MLA_HEADS, tq, n_keys)
    query_chunk = (lax.broadcasted_iota(jnp.int32, (tq, n_keys), 0) + q_pos0) // CHUNK
    kpos = lax.broadcasted_iota(jnp.int32, (tq, n_keys), 1)
    mask = (kpos // CHUNK <= query_chunk) & (kpos < n_valid)
    s = jnp.where(mask[None], s, -jnp.inf)
    p = jnp.exp2(s - jnp.max(s, axis=2, keepdims=True))
    inv_l = 1.0 / jnp.sum(p, axis=2, keepdims=True)
    lat = _dot(p.astype(BF16).reshape(MLA_HEADS * tq, n_keys), ckv_ref[...])
    lat = (lat.reshape(MLA_HEADS, tq, -1) * inv_l).astype(BF16)
    for h in range(MLA_HEADS):
        o_ref[:, h * V_DIM:(h + 1) * V_DIM] = _dot(lat[h], wuv_ref[h]).astype(o_ref.dtype)


def _attn_latent(q, ckv_past, kpe_past, ckv_new, kpad_new, w_ukt, w_uv, layer, batch, q_pos0):
    tq = q.shape[1] // batch
    past, lora = ckv_past.shape[2], ckv_past.shape[3]
    n_keys = -(-(past + tq) // V7X_LANES) * V7X_LANES
    vw = MLA_HEADS * V_DIM
    rows = MLA_HEADS * tq
    blocks = (MLA_HEADS * tq * MLA_QK_PAD * 2 + past * (lora + V7X_LANES) * 4
              + tq * (lora * 4 + V7X_LANES * 2) + tq * vw * 2)
    resident = (2 * MLA_HEADS * lora * QK_NOPE * 2 + n_keys * (lora + V7X_LANES) * 2
                + rows * (lora + V7X_LANES) * 2 + 4 * rows * n_keys * 4 + 2 * rows * lora * 4)
    return pl.pallas_call(
        functools.partial(_attn_latent_body, q_pos0=q_pos0),
        grid=(batch,),
        in_specs=[pl.BlockSpec((MLA_HEADS, tq, MLA_QK_PAD), lambda b: (0, b, 0)),
                  pl.BlockSpec((None, None, past, lora), lambda b: (layer, b, 0, 0)),
                  pl.BlockSpec((None, None, past, QK_ROPE), lambda b: (layer, b, 0, 0)),
                  pl.BlockSpec((tq, lora), lambda b: (b, 0)),
                  pl.BlockSpec((tq, V7X_LANES), lambda b: (b, 0)),
                  _const_spec(w_ukt), _const_spec(w_uv)],
        out_specs=pl.BlockSpec((tq, vw), lambda b: (b, 0)),
        out_shape=jax.ShapeDtypeStruct((q.shape[1], vw), BF16),
        scratch_shapes=[pltpu.VMEM((n_keys, lora), BF16), pltpu.VMEM((n_keys, V7X_LANES), BF16),
                        pltpu.VMEM((rows, lora), BF16), pltpu.VMEM((rows, V7X_LANES), BF16)],
        compiler_params=_params(("parallel",), blocks, resident),
        name="mla_attn_latent",
    )(q, ckv_past, kpe_past, ckv_new, kpad_new, w_ukt, w_uv)


def _rope_tables(pos):
    half = QK_ROPE // 2
    inv = ROPE_THETA ** (-jnp.arange(half, dtype=F32) / half)
    ang = pos.astype(F32)[:, None] * inv
    cos, sin = jnp.cos(ang), jnp.sin(ang)
    z = lambda n: jnp.zeros((pos.shape[0], n), F32)
    pad = V7X_LANES - QK_ROPE
    tables = (jnp.concatenate([cos, cos, z(pad)], axis=1),
              jnp.concatenate([-sin, z(half + pad)], axis=1),
              jnp.concatenate([z(half), sin, z(pad)], axis=1))
    return tables, (cos.T, sin.T)


def _prep_mla(w_uq, w_dkv, w_ukv):
    qlora = w_uq.shape[0]
    lora = w_ukv.shape[0]
    wq = w_uq.reshape(qlora, MLA_HEADS, QK_NOPE + QK_ROPE)
    wq = jnp.pad(wq, ((0, 0), (0, 0), (0, MLA_QK_PAD - QK_NOPE - QK_ROPE)))
    w_uq_pad = wq.reshape(qlora, MLA_HEADS * MLA_QK_PAD).astype(BF16)
    w_dkv_pad = jnp.pad(w_dkv, ((0, 0), (0, V7X_LANES - QK_ROPE))).astype(BF16)
    wkv = w_ukv.reshape(lora, MLA_HEADS, QK_NOPE + V_DIM)
    w_uk = wkv[:, :, :QK_NOPE].reshape(lora, MLA_HEADS * QK_NOPE).astype(BF16)
    w_uvt = wkv[:, :, QK_NOPE:].reshape(lora, MLA_HEADS * V_DIM).T.astype(BF16)
    w_ukt_heads = jnp.transpose(wkv[:, :, :QK_NOPE], (1, 2, 0)).astype(BF16)
    w_uv_heads = jnp.transpose(wkv[:, :, QK_NOPE:], (1, 0, 2)).astype(BF16)
    return w_uq_pad, w_dkv_pad, w_uk, w_uvt, w_ukt_heads, w_uv_heads


def _stack(arrays):
    return arrays[0][None] if len(arrays) == 1 else jnp.stack(arrays)


def _trunk(x, batch, seq, past_len, mem_k, mem_v, s_hgrn, s_conv, ckv_past, kpe_past, p):
    depth = p["norm_g"].shape[0]
    pos = past_len + jnp.arange(seq, dtype=jnp.int32)
    tables, tables_t = _rope_tables(pos)
    tables = tuple(jnp.tile(t, (batch, 1)) for t in tables)
    causal = past_len == 0
    hs, cs, ckvs, kpes = [], [], [], []
    for l in range(depth):
        g = lambda i: p["norm_g"][l, i][None, :]
        x = _ffn(x, g(0), p["ffn_w1"], p["ffn_w2"], g(1), l, 0)
        mem = (g(4), p["w_mem_q"][l], mem_k, mem_v, p["w_mem_o"][l], g(5), l)
        if l % 2 == 0:
            e = l // 2
            proj = _norm_matmul(x, g(2), p["w_in0"][e], BF16, tm_want=1024)
            oa, s_new = _hgrn(proj, p["lbs"][l][None, :], p["hgrn_gnorm"][e][None, :],
                              s_hgrn[e], batch, seq)
            x, c_new = _even_out(oa, proj, p["conv_wt"][e], s_conv[e], p["w_out0"][e],
                                 g(3), x, mem, batch, seq)
            hs.append(s_new)
            cs.append(c_new)
        else:
            o = l // 2
            q, ckv_new, kpe_new, kpad_new = _mla_proj(
                x, g(2), p["mla_w_dq"][o], p["mla_q_norm"][o][None, :],
                p["w_uq_pad_t"][o] if causal else p["w_uq_pad"][o],
                p["w_dkv_pad"][o], p["mla_kv_norm"][o][None, :], tables,
                tables_t if causal else None)
            if causal:
                assert batch == 1
                tk = _row_tile(seq, ATTN_TILE)
                tq = tk * ATTN_Q_PER_K if seq % (tk * ATTN_Q_PER_K) == 0 else tk
                k, vt = _kv_up(ckv_new, kpad_new, p["w_uk"][o], p["w_uvt"][o], tk)
                att = _attn_causal(q, k, vt, tq, tk)
            else:
                att = _attn_latent(q, ckv_past, kpe_past, ckv_new, kpad_new,
                                   p["w_ukt_heads"][o], p["w_uv_heads"][o], o, batch, past_len)
            x = _proj_residual(att, p["mla_w_o"][o], g(3), x, mem, batch, seq)
            ckvs.append(ckv_new.reshape(batch, seq, -1))
            kpes.append(kpe_new.reshape(batch, seq, -1))
        x = _ffn(x, g(6), p["ffn_w1"], p["ffn_w2"], g(7), l, 1)
    return (x.reshape(batch, seq, -1), _stack(hs), _stack(cs), _stack(ckvs), _stack(kpes))


def kernel(x_prompt, x_sample, mem_prompt, state_hgrn, state_conv, cache_ckv, cache_kpe, cache_mem_k, cache_mem_v, norm_g, ffn_w1, ffn_w2, w_in0, hgrn_lb, hgrn_gnorm, conv_w, w_out0, mla_w_dq, mla_q_norm, mla_w_uq, mla_w_dkv, mla_kv_norm, mla_w_ukv, mla_w_o, mem_norm, w_mem_q, w_mem_kv, w_mem_o):
    batch, seq, d = x_prompt.shape
    dec_batch, dec_seq, _ = x_sample.shape
    past_len = cache_ckv.shape[2]
    depth = norm_g.shape[0]
    n_mem = mem_prompt.shape[1]
    mem_width = MEM_HEADS * MEM_DIM

    prepped = [_prep_mla(mla_w_uq[o], mla_w_dkv[o], mla_w_ukv[o]) for o in range(mla_w_uq.shape[0])]
    p = dict(
        norm_g=norm_g,
        ffn_w1=ffn_w1.astype(BF16), ffn_w2=ffn_w2.astype(BF16),
        w_in0=w_in0.astype(BF16), w_out0=w_out0.astype(BF16),
        lbs=jnp.cumsum(jax.nn.softmax(hgrn_lb.astype(F32), axis=0), axis=0),
        hgrn_gnorm=hgrn_gnorm,
        conv_wt=jnp.swapaxes(conv_w, 1, 2),
        mla_w_dq=mla_w_dq.astype(BF16), mla_q_norm=mla_q_norm, mla_kv_norm=mla_kv_norm,
        w_uq_pad=[t[0] for t in prepped], w_uq_pad_t=[t[0].T for t in prepped],
        w_dkv_pad=[t[1] for t in prepped],
        w_uk=[t[2] for t in prepped], w_uvt=[t[3] for t in prepped],
        w_ukt_heads=[t[4] for t in prepped], w_uv_heads=[t[5] for t in prepped],
        mla_w_o=mla_w_o.astype(BF16),
        w_mem_q=w_mem_q.astype(BF16), w_mem_o=w_mem_o.astype(BF16),
    )

    mem_rows = mem_prompt.reshape(batch * n_mem, d)
    mks, mvs = [], []
    for l in range(depth):
        kv = _norm_matmul(mem_rows, mem_norm[l][None, :], w_mem_kv[l].astype(BF16), F32,
                          tn_want=2 * mem_width)
        kv = kv.reshape(batch, n_mem, 2, mem_width)
        mks.append(kv[:, :, 0])
        mvs.append(kv[:, :, 1])
    mem_k_p = jnp.stack(mks)
    mem_v_p = jnp.stack(mvs)

    n_even = state_hgrn.shape[0]
    h0 = jnp.zeros((n_even, batch) + state_hgrn.shape[2:], F32)
    c0 = jnp.zeros((n_even, batch) + state_conv.shape[2:], F32)
    y_p, p_hgrn, p_conv, p_ckv, p_kpe = _trunk(
        x_prompt.reshape(batch * seq, d), batch, seq, 0, jnp.swapaxes(mem_k_p, 2, 3), mem_v_p,
        h0, c0, None, None, p)
    y_s, s_hgrn, s_conv, s_ckv, s_kpe = _trunk(
        x_sample.reshape(dec_batch * dec_seq, d), dec_batch, dec_seq, past_len,
        jnp.transpose(cache_mem_k, (0, 1, 3, 4, 2)).reshape(depth, dec_batch, mem_width, n_mem),
        cache_mem_v.reshape(depth, dec_batch, n_mem, mem_width),
        state_hgrn, state_conv, cache_ckv, cache_kpe, p)

    mem_shape = (depth, batch, n_mem, MEM_HEADS, MEM_DIM)
    return (y_p, y_s, p_hgrn, p_conv, p_ckv, p_kpe,
            mem_k_p.reshape(mem_shape), mem_v_p.reshape(mem_shape),
            s_hgrn, s_conv, s_ckv, s_kpe)
```

```python
import functools

import jax
import jax.numpy as jnp
import numpy as np
from jax import lax
from jax.experimental import pallas as pl
from jax.experimental.pallas import tpu as pltpu

F32 = jnp.float32
BF16 = jnp.bfloat16

V7X_LANES = 128
V7X_SUBLANES = 8
V7X_VMEM_BYTES = 64 * 1024 * 1024
MIB = 1024 * 1024
SPILL_ROOM_BYTES = 8 * MIB
VMEM_RESERVED_BYTES = 2 * MIB

EPS = 1e-6
CHUNK = 64
HGRN_HEADS = 8
HGRN_DK = 128
HGRN_SUB = 16
HGRN_BLOCK_ROWS = 256
MLA_HEADS = 16
QK_NOPE = 128
QK_ROPE = 64
V_DIM = 128
V_EXT = V_DIM + 16
MLA_QK_PAD = 256
MLA_SCALE = (QK_NOPE + QK_ROPE) ** -0.5
ROPE_THETA = 10000.0
MEM_HEADS = 4
MEM_DIM = 128
CONV_W = 3
NEG_BIG = -1e30
LOG2E = 1.4426950408889634

NT_DIMS = (((1,), (1,)), ((), ()))
TN_DIMS = (((0,), (0,)), ((), ()))


def _vmem_limit(pipelined_bytes, resident_bytes):
    want = 2 * pipelined_bytes + resident_bytes + SPILL_ROOM_BYTES
    return int(min(want, V7X_VMEM_BYTES - VMEM_RESERVED_BYTES))


def _params(semantics, pipelined_bytes, resident_bytes):
    return pltpu.CompilerParams(
        dimension_semantics=semantics,
        vmem_limit_bytes=_vmem_limit(pipelined_bytes, resident_bytes))


def _row_tile(rows, want):
    if rows <= want:
        return rows
    t = want - want % V7X_LANES
    while t > 0 and rows % t:
        t -= V7X_LANES
    assert t > 0, (rows, want)
    return t


ROW_CHUNK = 256


def _row_chunks(rows):
    return [slice(r0, min(r0 + ROW_CHUNK, rows)) for r0 in range(0, rows, ROW_CHUNK)]


def _rms(x, g):
    ms = jnp.mean(x * x, axis=-1, keepdims=True)
    return x * lax.rsqrt(ms + EPS) * g


def _sigmoid(x):
    return 1.0 / (1.0 + jnp.exp(-x))


def _dot(a, b):
    return jnp.dot(a, b, preferred_element_type=F32)


FFN_COL_CHUNK = 512


def _ffn_hidden(xn, w1g_ref, w1u_ref):
    gate = _dot(xn, w1g_ref[...])
    up = _dot(xn, w1u_ref[...])
    return (gate * _sigmoid(gate) * up).astype(BF16)


def _ffn_body(x_ref, gpre_ref, w1g_ref, w1u_ref, w2_ref, gpost_ref, o_ref, xn_ref):
    k = pl.program_id(1)
    last = pl.num_programs(1) - 1
    tm, d = o_ref.shape

    @pl.when(k == 0)
    def _():
        gpre = gpre_ref[...]
        for rs in _row_chunks(tm):
            xn = _rms(x_ref[rs, :], gpre).astype(BF16)
            xn_ref[rs, :] = xn
            o_ref[rs, :] = _dot(_ffn_hidden(xn, w1g_ref, w1u_ref), w2_ref[...])

    @pl.when((k > 0) & (k < last))
    def _():
        o_ref[...] += _dot(_ffn_hidden(xn_ref[...], w1g_ref, w1u_ref), w2_ref[...])

    @pl.when(k == last)
    def _():
        h = _ffn_hidden(xn_ref[...], w1g_ref, w1u_ref)
        ss = jnp.zeros((tm, 1), F32)
        for c0 in range(0, d, FFN_COL_CHUNK):
            cs = slice(c0, c0 + FFN_COL_CHUNK)
            y = o_ref[:, cs] + _dot(h, w2_ref[:, cs])
            o_ref[:, cs] = y
            ss = ss + jnp.sum(y * y, axis=-1, keepdims=True)
        scale = 0.5 * lax.rsqrt(ss * (1.0 / d) + EPS)
        o_ref[...] = x_ref[...] + o_ref[...] * scale * gpost_ref[...]


def _ffn(x, g_pre, w1, w2, g_post, layer, half, tm_want=1024, tf=512):
    rows, d = x.shape
    dff = w2.shape[-2]
    tm = _row_tile(rows, tm_want)
    nk = dff // tf
    assert dff % tf == 0 and nk >= 2 and d % FFN_COL_CHUNK == 0
    blocks = 2 * tm * d * 4 + 3 * d * tf * 2
    resident = tm * d * (2 + 4) + 4 * tm * tf * 4
    return pl.pallas_call(
        _ffn_body,
        grid=(rows // tm, nk),
        in_specs=[
            pl.BlockSpec((tm, d), lambda i, k: (i, 0)),
            pl.BlockSpec((1, d), lambda i, k: (0, 0)),
            pl.BlockSpec((None, None, d, tf), lambda i, k: (layer, half, 0, k)),
            pl.BlockSpec((None, None, d, tf), lambda i, k: (layer, half, 0, nk + k)),
            pl.BlockSpec((None, None, tf, d), lambda i, k: (layer, half, k, 0)),
            pl.BlockSpec((1, d), lambda i, k: (0, 0)),
        ],
        out_specs=pl.BlockSpec((tm, d), lambda i, k: (i, 0)),
        out_shape=jax.ShapeDtypeStruct((rows, d), F32),
        scratch_shapes=[pltpu.VMEM((tm, d), BF16)],
        compiler_params=_params(("parallel", "arbitrary"), blocks, resident),
        name="ffn",
    )(x, g_pre, w1, w1, w2, g_post)


def _norm_matmul_body(x_ref, g_ref, w_ref, o_ref, xn_ref):
    j = pl.program_id(1)

    @pl.when(j == 0)
    def _():
        g = g_ref[...]
        for rs in _row_chunks(x_ref.shape[0]):
            xn = _rms(x_ref[rs, :], g).astype(BF16)
            xn_ref[rs, :] = xn
            o_ref[rs, :] = _dot(xn, w_ref[...]).astype(o_ref.dtype)

    @pl.when(j > 0)
    def _():
        o_ref[...] = _dot(xn_ref[...], w_ref[...]).astype(o_ref.dtype)


def _norm_matmul(x, g, w, out_dtype, tm_want=512, tn_want=1024):
    rows, d = x.shape
    n = w.shape[1]
    tm = _row_tile(rows, tm_want)
    tn = _row_tile(n, tn_want)
    blocks = tm * d * 4 + d * tn * 2 + tm * tn * 4
    resident = tm * d * 2 + tm * tn * 4
    return pl.pallas_call(
        _norm_matmul_body,
        grid=(rows // tm, n // tn),
        in_specs=[
            pl.BlockSpec((tm, d), lambda i, j: (i, 0)),
            pl.BlockSpec((1, d), lambda i, j: (0, 0)),
            pl.BlockSpec((d, tn), lambda i, j: (0, j)),
        ],
        out_specs=pl.BlockSpec((tm, tn), lambda i, j: (i, j)),
        out_shape=jax.ShapeDtypeStruct((rows, n), out_dtype),
        scratch_shapes=[pltpu.VMEM((tm, d), BF16)],
        compiler_params=_params(("parallel", "arbitrary"), blocks, resident),
        name="norm_matmul",
    )(x, g, w)


def _mem_sublayer(x, gpre_ref, wq_ref, mk_ref, mv_ref, wo_ref, gpost_ref):
    q = _dot(_rms(x, gpre_ref[...]).astype(BF16), wq_ref[...])
    outs = []
    for h in range(MEM_HEADS):
        hs = slice(h * MEM_DIM, (h + 1) * MEM_DIM)
        kh = mk_ref[:, hs].astype(BF16)
        vh = mv_ref[:, hs].astype(BF16)
        s = lax.dot_general(q[:, hs].astype(BF16), kh, NT_DIMS,
                            preferred_element_type=F32) * (MEM_DIM ** -0.5)
        p = jnp.exp(s - jnp.max(s, axis=1, keepdims=True))
        p = p * (1.0 / jnp.sum(p, axis=1, keepdims=True))
        outs.append(_dot(p.astype(BF16), vh).astype(BF16))
    o = jnp.concatenate(outs, axis=1)
    return x + _rms(_dot(o, wo_ref[...]), gpost_ref[...])


def _const_spec(a):
    return pl.BlockSpec(a.shape, lambda *_: (0,) * a.ndim, pipeline_mode=pl.Buffered(1))


def _mem_specs(mem_k, w_q, w_o, g_pre, g_post, layer):
    n_mem, width = mem_k.shape[2], mem_k.shape[3]
    mem = pl.BlockSpec((None, None, n_mem, width), lambda b, t: (layer, b, 0, 0))
    specs = [_const_spec(g_pre), _const_spec(w_q), mem, mem, _const_spec(w_o),
             _const_spec(g_post)]
    nbytes = (w_q.size + w_o.size) * 2 + 4 * n_mem * width * 4
    return specs, nbytes


def _proj_residual_body(a_ref, w_ref, g_ref, x_ref,
                        gpre_ref, wq_ref, mk_ref, mv_ref, wo_ref, gpost_ref, o_ref):
    x1 = x_ref[...] + _rms(_dot(a_ref[...], w_ref[...]), g_ref[...])
    o_ref[...] = _mem_sublayer(x1, gpre_ref, wq_ref, mk_ref, mv_ref, wo_ref, gpost_ref)


def _proj_residual(a, w, g, x, mem, batch, seq, tm_want=512):
    g_pre, w_q, mem_k, mem_v, w_o, g_post, layer = mem
    kdim = a.shape[1]
    d = w.shape[1]
    tm = _row_tile(seq, tm_want)
    nt = seq // tm
    row = lambda n: pl.BlockSpec((tm, n), lambda b, t: (b * nt + t, 0))
    mem_specs, mem_bytes = _mem_specs(mem_k, w_q, w_o, g_pre, g_post, layer)
    blocks = tm * kdim * 2 + 2 * tm * d * 4
    resident = kdim * d * 2 + mem_bytes + 4 * tm * d * 4
    return pl.pallas_call(
        _proj_residual_body,
        grid=(batch, nt),
        in_specs=[row(kdim), _const_spec(w), _const_spec(g), row(d)] + mem_specs,
        out_specs=row(d),
        out_shape=jax.ShapeDtypeStruct(x.shape, F32),
        compiler_params=_params(("parallel", "parallel"), blocks, resident),
        name="proj_residual_mem",
    )(a, w, g, x, g_pre, w_q, mem_k, mem_v, w_o, g_post)


def _hgrn_body(qa_ref, fa_ref, ia_ref, ga_ref, lb_ref, gn_ref, s0_ref,
               oa_ref, sout_ref, st_ref, o_ref):
    c = pl.program_id(1)
    rows = qa_ref.shape[0]
    width = qa_ref.shape[1]
    nsub = rows // HGRN_SUB

    @pl.when(c == 0)
    def _():
        for h in range(HGRN_HEADS):
            st_ref[h] = s0_ref[h].T

    lb = lb_ref[...]
    qa = qa_ref[...].astype(F32)
    q = qa * _sigmoid(qa) * (HGRN_DK ** -0.5)
    f = lb + (1.0 - lb) * _sigmoid(fa_ref[...].astype(F32))
    k = 1.0 - f
    g = jnp.log(f) * LOG2E
    v = ia_ref[...].astype(F32)

    r_i = lax.broadcasted_iota(jnp.int32, (rows, rows), 0)
    c_i = lax.broadcasted_iota(jnp.int32, (rows, rows), 1)
    tri = ((c_i <= r_i) & ((c_i // HGRN_SUB) == (r_i // HGRN_SUB))).astype(BF16)
    g1 = g.astype(BF16)
    rem = g - g1.astype(F32)
    g2 = rem.astype(BF16)
    g3 = (rem - g2.astype(F32)).astype(BF16)
    bl = _dot(tri, g1) + _dot(tri, g2) + _dot(tri, g3)

    qe = (q * jnp.exp2(bl)).astype(BF16)
    row8 = lax.broadcasted_iota(jnp.int32, (V7X_SUBLANES, width), 0)
    tiles = HGRN_SUB // V7X_SUBLANES

    for i in range(nsub):
        r0 = i * HGRN_SUB
        bl_i = bl[r0:r0 + HGRN_SUB]
        q_i = q[r0:r0 + HGRN_SUB]
        k_i = k[r0:r0 + HGRN_SUB]
        v_i = v[r0:r0 + HGRN_SUB]
        b_end = bl_i[HGRN_SUB - 1:HGRN_SUB]
        ke_i = (k_i * jnp.exp2(b_end - bl_i)).astype(BF16)
        dec_i = jnp.exp2(b_end)
        v_bf = v_i.astype(BF16)

        diag = [[jnp.zeros((V7X_SUBLANES, HGRN_DK), F32) for _ in range(tiles)]
                for _ in range(HGRN_HEADS)]
        for s in range(HGRN_SUB):
            ks, bs, vs = k_i[s:s + 1], bl_i[s:s + 1], v_i[s:s + 1]
            for rt in range(s // V7X_SUBLANES, tiles):
                rsl = slice(rt * V7X_SUBLANES, (rt + 1) * V7X_SUBLANES)
                rel = bl_i[rsl] - bs
                if rt == s // V7X_SUBLANES:
                    rel = jnp.where(row8 >= s % V7X_SUBLANES, rel, -jnp.inf)
                w = q_i[rsl] * ks * jnp.exp2(rel)
                for h in range(HGRN_HEADS):
                    hs = slice(h * HGRN_DK, (h + 1) * HGRN_DK)
                    col = jnp.sum(w[:, hs], axis=1, keepdims=True)
                    diag[h][rt] = diag[h][rt] + col * vs[:, hs]

        for h in range(HGRN_HEADS):
            hs = slice(h * HGRN_DK, (h + 1) * HGRN_DK)
            st = st_ref[h]
            inter = lax.dot_general(qe[r0:r0 + HGRN_SUB, hs], st.astype(BF16), NT_DIMS,
                                    preferred_element_type=F32)
            upd = lax.dot_general(v_bf[:, hs], ke_i[:, hs], TN_DIMS,
                                  preferred_element_type=F32)
            st_ref[h] = st * dec_i[:, hs] + upd
            o_ref[r0:r0 + HGRN_SUB, hs] = inter + jnp.concatenate(diag[h], axis=0)

    gn = gn_ref[...]
    for h in range(HGRN_HEADS):
        hs = slice(h * HGRN_DK, (h + 1) * HGRN_DK)
        ga = ga_ref[:, hs].astype(F32)
        oa_ref[:, hs] = (_rms(o_ref[:, hs], gn) * (ga * _sigmoid(ga))).astype(oa_ref.dtype)

    @pl.when(c == pl.num_programs(1) - 1)
    def _():
        for h in range(HGRN_HEADS):
            sout_ref[h] = st_ref[h].T


def _hgrn(proj, lb, gnorm, s0, batch, seq):
    width = HGRN_HEADS * HGRN_DK
    rows = min(HGRN_BLOCK_ROWS, seq)
    nc = seq // rows
    assert seq % rows == 0 and rows % HGRN_SUB == 0
    col = lambda j: pl.BlockSpec((rows, width), lambda b, c: (b * nc + c, j))
    state_spec = pl.BlockSpec((None, HGRN_HEADS, HGRN_DK, HGRN_DK), lambda b, c: (b, 0, 0, 0))
    blocks = 4 * rows * width * 4 + rows * width * 2 + 2 * HGRN_HEADS * HGRN_DK * HGRN_DK * 4
    resident = HGRN_HEADS * HGRN_DK * HGRN_DK * 4 + 16 * rows * width * 4
    return pl.pallas_call(
        _hgrn_body,
        grid=(batch, nc),
        in_specs=[col(0), col(1), col(2), col(3),
                  pl.BlockSpec((1, width), lambda b, c: (0, 0)),
                  pl.BlockSpec((1, HGRN_DK), lambda b, c: (0, 0)),
                  state_spec],
        out_specs=[pl.BlockSpec((rows, width), lambda b, c: (b * nc + c, 0)), state_spec],
        out_shape=[jax.ShapeDtypeStruct((batch * seq, width), BF16),
                   jax.ShapeDtypeStruct(s0.shape, F32)],
        scratch_shapes=[pltpu.VMEM((HGRN_HEADS, HGRN_DK, HGRN_DK), F32),
                        pltpu.VMEM((rows, width), F32)],
        compiler_params=_params(("parallel", "arbitrary"), blocks, resident),
        name="hgrn2",
    )(proj, proj, proj, proj, lb, gnorm, s0)


CONV_PAD = 8


def _even_out_body(oa_ref, bg_ref, cg_ref, hb_ref, cw_ref, cs_ref, w_ref, g_ref, x_ref,
                   gpre_ref, wq_ref, mk_ref, mv_ref, wo_ref, gpost_ref,
                   o_ref, cnew_ref, ubuf_ref):
    t = pl.program_id(1)
    tm = oa_ref.shape[0]
    half = oa_ref.shape[1]
    lo = CONV_PAD - (CONV_W - 1)

    @pl.when(t == 0)
    def _():
        ubuf_ref[lo:CONV_PAD, :] = cs_ref[...]

    u = cg_ref[...].astype(F32) * hb_ref[...].astype(F32)
    ubuf_ref[CONV_PAD:CONV_PAD + tm, :] = u
    yb = u * cw_ref[CONV_W - 1:CONV_W, :]
    for j in range(CONV_W - 1):
        yb = yb + ubuf_ref[lo + j:lo + j + tm, :] * cw_ref[j:j + 1, :]
    ob = (bg_ref[...].astype(F32) * yb).astype(BF16)
    y = _dot(oa_ref[...], w_ref[0:half, :]) + _dot(ob, w_ref[half:2 * half, :])
    x1 = x_ref[...] + _rms(y, g_ref[...])
    o_ref[...] = _mem_sublayer(x1, gpre_ref, wq_ref, mk_ref, mv_ref, wo_ref, gpost_ref)

    tail = ubuf_ref[lo + tm:CONV_PAD + tm, :]
    ubuf_ref[lo:CONV_PAD, :] = tail

    @pl.when(t == pl.num_programs(1) - 1)
    def _():
        cnew_ref[...] = tail


def _even_out(oa, proj, conv_wt, conv_state, w_out, g, x, mem, batch, seq, tm_want=512):
    g_pre, w_q, mem_k, mem_v, w_o, g_post, layer = mem
    half = oa.shape[1]
    d = x.shape[1]
    tm = _row_tile(seq, tm_want)
    nt = seq // tm
    row = lambda b, t: (b * nt + t, 0)
    pcol = lambda j: pl.BlockSpec((tm, half), lambda b, t: (b * nt + t, j))
    cstate = pl.BlockSpec((None, CONV_W - 1, half), lambda b, t: (b, 0, 0))
    mem_specs, mem_bytes = _mem_specs(mem_k, w_q, w_o, g_pre, g_post, layer)
    blocks = tm * half * 4 * proj.dtype.itemsize + 2 * tm * d * 4
    resident = (2 * half * d * 2 + mem_bytes + (tm + CONV_PAD) * half * 4 + 4 * tm * half * 4
                + 4 * tm * d * 4)
    return pl.pallas_call(
        _even_out_body,
        grid=(batch, nt),
        in_specs=[pl.BlockSpec((tm, half), row), pcol(4), pcol(5), pcol(6),
                  _const_spec(conv_wt), cstate, _const_spec(w_out), _const_spec(g),
                  pl.BlockSpec((tm, d), row)] + mem_specs,
        out_specs=[pl.BlockSpec((tm, d), row), cstate],
        out_shape=[jax.ShapeDtypeStruct(x.shape, F32),
                   jax.ShapeDtypeStruct(conv_state.shape, F32)],
        scratch_shapes=[pltpu.VMEM((tm + CONV_PAD, half), F32)],
        compiler_params=_params(("parallel", "arbitrary"), blocks, resident),
        name="even_out_mem",
    )(oa, proj, proj, proj, conv_wt, conv_state, w_out, g, x,
      g_pre, w_q, mem_k, mem_v, w_o, g_post)


Q_PRESCALE = MLA_SCALE * LOG2E


def _rope(x, cos, sin_lo, sin_hi):
    half = QK_ROPE // 2
    return (x * cos
            + pltpu.roll(x, V7X_LANES - half, axis=1) * sin_lo
            + pltpu.roll(x, half, axis=1) * sin_hi)


def _mla_proj_body(x_ref, g_ref, wdq_ref, qn_ref, wuq_ref, wdkv_ref, kvn_ref,
                   cos_ref, slo_ref, shi_ref, *rest, q_transposed):
    if q_transposed:
        cost_ref, sint_ref, q_ref, ckv_ref, kpe_ref, kpad_ref = rest
    else:
        q_ref, ckv_ref, kpe_ref, kpad_ref = rest
    cos, slo, shi = cos_ref[...], slo_ref[...], shi_ref[...]
    lora = ckv_ref.shape[1]
    xn = _rms(x_ref[...], g_ref[...]).astype(BF16)
    cq = _rms(_dot(xn, wdq_ref[...]), qn_ref[...]).astype(BF16)
    ckr = _dot(xn, wdkv_ref[...])
    ckv_ref[...] = _rms(ckr[:, :lora], kvn_ref[...])
    kp = _rope(ckr[:, lora:lora + V7X_LANES], cos, slo, shi)
    kpe_ref[...] = kp[:, :QK_ROPE]
    kpad_ref[...] = kp.astype(BF16)
    if q_transposed:
        half = QK_ROPE // 2
        cost, sint = cost_ref[...], sint_ref[...]
        qt = lax.dot_general(wuq_ref[...], cq, NT_DIMS, preferred_element_type=F32) * Q_PRESCALE
        for h in range(MLA_HEADS):
            r0 = h * MLA_QK_PAD
            x1 = qt[r0 + QK_NOPE:r0 + QK_NOPE + half]
            x2 = qt[r0 + QK_NOPE + half:r0 + QK_NOPE + QK_ROPE]
            q_ref[h, 0:QK_NOPE, :] = qt[r0:r0 + QK_NOPE].astype(BF16)
            q_ref[h, QK_NOPE:QK_NOPE + half, :] = (x1 * cost - x2 * sint).astype(BF16)
            q_ref[h, QK_NOPE + half:QK_NOPE + QK_ROPE, :] = (x1 * sint + x2 * cost).astype(BF16)
            q_ref[h, QK_NOPE + QK_ROPE:MLA_QK_PAD, :] = qt[
                r0 + QK_NOPE + QK_ROPE:r0 + MLA_QK_PAD].astype(BF16)
    else:
        q = _dot(cq, wuq_ref[...]) * Q_PRESCALE
        for h in range(MLA_HEADS):
            c0 = h * MLA_QK_PAD
            q_ref[h, :, 0:QK_NOPE] = q[:, c0:c0 + QK_NOPE].astype(BF16)
            q_ref[h, :, QK_NOPE:MLA_QK_PAD] = _rope(
                q[:, c0 + QK_NOPE:c0 + MLA_QK_PAD], cos, slo, shi).astype(BF16)


def _mla_proj(x, g, w_dq, q_norm, w_uq, w_dkv_pad, kv_norm, tables, tables_t=None, tm_want=512):
    q_transposed = tables_t is not None
    rows, d = x.shape
    qlora = w_dq.shape[1]
    lora = kv_norm.shape[1]
    qw = MLA_HEADS * MLA_QK_PAD
    kw = w_dkv_pad.shape[1]
    tm = _row_tile(rows, tm_want)
    full = lambda a: pl.BlockSpec(a.shape, lambda i: (0,) * a.ndim)
    rowspec = lambda n: pl.BlockSpec((tm, n), lambda i: (i, 0))
    blocks = (tm * d * 4 + (d * qlora + qlora * qw + d * kw) * 2 + 4 * tm * V7X_LANES * 4
              + tm * (qw * 2 + lora * 4 + QK_ROPE * 4 + V7X_LANES * 2))
    resident = tm * (d * 6 + qw * 4 + kw * 4 + qlora * 8)
    in_specs = [rowspec(d), full(g), full(w_dq), full(q_norm), full(w_uq),
                full(w_dkv_pad), full(kv_norm),
                rowspec(V7X_LANES), rowspec(V7X_LANES), rowspec(V7X_LANES)]
    operands = [x, g, w_dq, q_norm, w_uq, w_dkv_pad, kv_norm, *tables]
    if q_transposed:
        in_specs += [pl.BlockSpec((QK_ROPE // 2, tm), lambda i: (0, i))] * 2
        operands += list(tables_t)
        q_spec = pl.BlockSpec((MLA_HEADS, MLA_QK_PAD, tm), lambda i: (0, 0, i))
        q_shape = (MLA_HEADS, MLA_QK_PAD, rows)
    else:
        q_spec = pl.BlockSpec((MLA_HEADS, tm, MLA_QK_PAD), lambda i: (0, i, 0))
        q_shape = (MLA_HEADS, rows, MLA_QK_PAD)
    return pl.pallas_call(
        functools.partial(_mla_proj_body, q_transposed=q_transposed),
        grid=(rows // tm,),
        in_specs=in_specs,
        out_specs=[q_spec, rowspec(lora), rowspec(QK_ROPE), rowspec(V7X_LANES)],
        out_shape=[jax.ShapeDtypeStruct(q_shape, BF16),
                   jax.ShapeDtypeStruct((rows, lora), F32),
                   jax.ShapeDtypeStruct((rows, QK_ROPE), F32),
                   jax.ShapeDtypeStruct((rows, V7X_LANES), BF16)],
        compiler_params=_params(("parallel",), blocks, resident),
        name="mla_proj",
    )(*operands)


def _kv_up_body(ckv_ref, kpad_ref, wuk_ref, wuvt_ref, k_ref, vt_ref):
    c = ckv_ref[...].astype(BF16)
    kn = _dot(c, wuk_ref[...])
    vt = lax.dot_general(wuvt_ref[...], c, NT_DIMS, preferred_element_type=F32)
    ts = c.shape[0]
    vt_ref[:, 0:V_DIM, :] = vt.astype(BF16).reshape(MLA_HEADS, V_DIM, ts)
    ones_row = lax.broadcasted_iota(jnp.int32, (MLA_HEADS, V_EXT - V_DIM, ts), 1) == 0
    vt_ref[:, V_DIM:V_EXT, :] = ones_row.astype(BF16)
    kpad = kpad_ref[...]
    for h in range(MLA_HEADS):
        k_ref[h, :, 0:QK_NOPE] = kn[:, h * QK_NOPE:(h + 1) * QK_NOPE].astype(BF16)
        k_ref[h, :, QK_NOPE:MLA_QK_PAD] = kpad


def _kv_up(ckv, kpad, w_uk, w_uvt, ts):
    rows, lora = ckv.shape
    assert rows % ts == 0
    rowspec = lambda n: pl.BlockSpec((ts, n), lambda i: (i, 0))
    full = lambda a: pl.BlockSpec(a.shape, lambda i: (0,) * a.ndim)
    kw = MLA_HEADS * MLA_QK_PAD
    vw = MLA_HEADS * V_DIM
    blocks = ts * (lora * 4 + V7X_LANES * 2 + kw * 2 + vw * 2) + 2 * lora * vw * 2
    resident = ts * (kw + vw) * 4
    return pl.pallas_call(
        _kv_up_body,
        grid=(rows // ts,),
        in_specs=[rowspec(lora), rowspec(V7X_LANES), full(w_uk), full(w_uvt)],
        out_specs=[pl.BlockSpec((MLA_HEADS, ts, MLA_QK_PAD), lambda i: (0, i, 0)),
                   pl.BlockSpec((None, MLA_HEADS, V_EXT, ts), lambda i: (i, 0, 0, 0))],
        out_shape=[jax.ShapeDtypeStruct((MLA_HEADS, rows, MLA_QK_PAD), BF16),
                   jax.ShapeDtypeStruct((rows // ts, MLA_HEADS, V_EXT, ts), BF16)],
        compiler_params=_params(("parallel",), blocks, resident),
        name="kv_up",
    )(ckv, kpad, w_uk, w_uvt)


ATTN_TILE = 512
ATTN_Q_PER_K = 1
ATTN_SCORE_BUFFERS = 3


def _scores_t(h, qt_ref, k_ref):
    return _dot(k_ref[h], qt_ref[h])


def _softmax_pv_t(h, st, vt_ref, m_ref, acc_ref, mask):
    if mask is not None:
        st = jnp.where(mask, st, -jnp.inf)
    m_prev = m_ref[h]
    m_next = jnp.maximum(m_prev, jnp.max(st, axis=0, keepdims=True))
    alpha = jnp.exp2(m_prev - m_next)
    p = jnp.exp2(st - m_next)
    m_ref[h] = m_next
    acc_ref[h] = alpha * acc_ref[h] + _dot(vt_ref[h], p.astype(BF16))


def _attn_block_t(q_ref, k_ref, vt_ref, m_ref, acc_ref, s_ref, mask):
    nbuf = s_ref.shape[0]
    for h in range(nbuf - 1):
        s_ref[h] = _scores_t(h, q_ref, k_ref)
    for h in range(MLA_HEADS):
        ahead = h + nbuf - 1
        if ahead < MLA_HEADS:
            s_ref[ahead % nbuf] = _scores_t(ahead, q_ref, k_ref)
        _softmax_pv_t(h, s_ref[h % nbuf], vt_ref, m_ref, acc_ref, mask)


def _attn_causal_body(qi_ref, kj_ref, q_ref, k_ref, vt_ref, o_ref, m_ref, acc_ref, s_ref):
    step = pl.program_id(0)
    qi = qi_ref[step]
    kj = kj_ref[step]
    tq, tk = q_ref.shape[2], k_ref.shape[1]
    per_q = tq // tk

    @pl.when(kj == 0)
    def _():
        m_ref[...] = jnp.full(m_ref.shape, NEG_BIG, F32)
        acc_ref[...] = jnp.zeros_like(acc_ref)

    @pl.when(kj < qi * per_q)
    def _():
        _attn_block_t(q_ref, k_ref, vt_ref, m_ref, acc_ref, s_ref, None)

    @pl.when(kj >= qi * per_q)
    def _():
        key_off = 0 if per_q == 1 else (kj - qi * per_q) * tk
        key_chunk = (lax.broadcasted_iota(jnp.int32, (tk, tq), 0) + key_off) // CHUNK
        query_chunk = lax.broadcasted_iota(jnp.int32, (tk, tq), 1) // CHUNK
        _attn_block_t(q_ref, k_ref, vt_ref, m_ref, acc_ref, s_ref, key_chunk <= query_chunk)

    @pl.when(kj == qi * per_q + per_q - 1)
    def _():
        for h in range(MLA_HEADS):
            out = acc_ref[h, 0:V_DIM, :] / acc_ref[h, V_DIM:V_DIM + 1, :]
            o_ref[:, h * V_DIM:(h + 1) * V_DIM] = out.T.astype(o_ref.dtype)


def _attn_causal(q, k, vt, tq, tk):
    rows = q.shape[2]
    assert tk % CHUNK == 0 and tq % tk == 0 and rows % tq == 0 and vt.shape[3] == tk
    per_q = tq // tk
    pairs = [(i, j) for i in range(rows // tq) for j in range(per_q * (i + 1))]
    qi = jnp.asarray(np.array([a for a, _ in pairs], np.int32))
    kj = jnp.asarray(np.array([b for _, b in pairs], np.int32))
    vw = MLA_HEADS * V_DIM
    blocks = MLA_HEADS * (tq * MLA_QK_PAD + tk * (MLA_QK_PAD + V_EXT)) * 2 + tq * vw * 2
    resident = (MLA_HEADS * (V_EXT + 8) * tq + (ATTN_SCORE_BUFFERS + 1) * tk * tq) * 4
    grid_spec = pltpu.PrefetchScalarGridSpec(
        num_scalar_prefetch=2,
        grid=(len(pairs),),
        in_specs=[pl.BlockSpec((MLA_HEADS, MLA_QK_PAD, tq), lambda p, qi, kj: (0, 0, qi[p])),
                  pl.BlockSpec((MLA_HEADS, tk, MLA_QK_PAD), lambda p, qi, kj: (0, kj[p], 0)),
                  pl.BlockSpec((None, MLA_HEADS, V_EXT, tk), lambda p, qi, kj: (kj[p], 0, 0, 0))],
        out_specs=pl.BlockSpec((tq, vw), lambda p, qi, kj: (qi[p], 0)),
        scratch_shapes=[pltpu.VMEM((MLA_HEADS, 1, tq), F32),
                        pltpu.VMEM((MLA_HEADS, V_EXT, tq), F32),
                        pltpu.VMEM((ATTN_SCORE_BUFFERS, tk, tq), F32)])
    return pl.pallas_call(
        _attn_causal_body,
        grid_spec=grid_spec,
        out_shape=jax.ShapeDtypeStruct((rows, vw), BF16),
        compiler_params=_params(("arbitrary",), blocks, resident),
        name="mla_attn_causal",
    )(qi, kj, q, k, vt)


def _attn_latent_body(q_ref, ckvp_ref, kpep_ref, ckvn_ref, kpadn_ref, wukt_ref, wuv_ref, o_ref,
                      ckv_ref, kpe_ref, qa_ref, qp_ref, *, q_pos0):
    tq = q_ref.shape[1]
    past = ckvp_ref.shape[0]
    n_keys = ckv_ref.shape[0]
    n_valid = past + tq

    ckv_ref[0:past, :] = ckvp_ref[...].astype(BF16)
    ckv_ref[past:n_valid, :] = ckvn_ref[...].astype(BF16)
    ckv_ref[n_valid:n_keys, :] = jnp.zeros((n_keys - n_valid, ckv_ref.shape[1]), BF16)
    kpe_ref[...] = jnp.zeros_like(kpe_ref)
    kpe_ref[0:past, 0:QK_ROPE] = kpep_ref[...].astype(BF16)
    kpe_ref[past:n_valid, :] = kpadn_ref[...]

    for h in range(MLA_HEADS):
        rows = slice(h * tq, (h + 1) * tq)
        qa_ref[rows, :] = _dot(q_ref[h, :, 0:QK_NOPE], wukt_ref[h]).astype(BF16)
        qp_ref[rows, :] = q_ref[h, :, QK_NOPE:MLA_QK_PAD]

    s = (lax.dot_general(qa_ref[...], ckv_ref[...], NT_DIMS, preferred_element_type=F32)
         + lax.dot_general(qp_ref[...], kpe_ref[...], NT_DIMS, preferred_element_type=F32))
    s = s.reshape(MLA_HEADS, tq, n_keys)
    query_chunk = (lax.broadcasted_iota(jnp.int32, (tq, n_keys), 0) + q_pos0) // CHUNK
    kpos = lax.broadcasted_iota(jnp.int32, (tq, n_keys), 1)
    mask = (kpos // CHUNK <= query_chunk) & (kpos < n_valid)
    s = jnp.where(mask[None], s, -jnp.inf)
    p = jnp.exp2(s - jnp.max(s, axis=2, keepdims=True))
    inv_l = 1.0 / jnp.sum(p, axis=2, keepdims=True)
    lat = _dot(p.astype(BF16).reshape(MLA_HEADS * tq, n_keys), ckv_ref[...])
    lat = (lat.reshape(MLA_HEADS, tq, -1) * inv_l).astype(BF16)
    for h in range(MLA_HEADS):
        o_ref[:, h * V_DIM:(h + 1) * V_DIM] = _dot(lat[h], wuv_ref[h]).astype(o_ref.dtype)


def _attn_latent(q, ckv_past, kpe_past, ckv_new, kpad_new, w_ukt, w_uv, layer, batch, q_pos0):
    tq = q.shape[1] // batch
    past, lora = ckv_past.shape[2], ckv_past.shape[3]
    n_keys = -(-(past + tq) // V7X_LANES) * V7X_LANES
    vw = MLA_HEADS * V_DIM
    rows = MLA_HEADS * tq
    blocks = (MLA_HEADS * tq * MLA_QK_PAD * 2 + past * (lora + V7X_LANES) * 4
              + tq * (lora * 4 + V7X_LANES * 2) + tq * vw * 2)
    resident = (2 * MLA_HEADS * lora * QK_NOPE * 2 + n_keys * (lora + V7X_LANES) * 2
                + rows * (lora + V7X_LANES) * 2 + 4 * rows * n_keys * 4 + 2 * rows * lora * 4)
    return pl.pallas_call(
        functools.partial(_attn_latent_body, q_pos0=q_pos0),
        grid=(batch,),
        in_specs=[pl.BlockSpec((MLA_HEADS, tq, MLA_QK_PAD), lambda b: (0, b, 0)),
                  pl.BlockSpec((None, None, past, lora), lambda b: (layer, b, 0, 0)),
                  pl.BlockSpec((None, None, past, QK_ROPE), lambda b: (layer, b, 0, 0)),
                  pl.BlockSpec((tq, lora), lambda b: (b, 0)),
                  pl.BlockSpec((tq, V7X_LANES), lambda b: (b, 0)),
                  _const_spec(w_ukt), _const_spec(w_uv)],
        out_specs=pl.BlockSpec((tq, vw), lambda b: (b, 0)),
        out_shape=jax.ShapeDtypeStruct((q.shape[1], vw), BF16),
        scratch_shapes=[pltpu.VMEM((n_keys, lora), BF16), pltpu.VMEM((n_keys, V7X_LANES), BF16),
                        pltpu.VMEM((rows, lora), BF16), pltpu.VMEM((rows, V7X_LANES), BF16)],
        compiler_params=_params(("parallel",), blocks, resident),
        name="mla_attn_latent",
    )(q, ckv_past, kpe_past, ckv_new, kpad_new, w_ukt, w_uv)


def _rope_tables(pos):
    half = QK_ROPE // 2
    inv = ROPE_THETA ** (-jnp.arange(half, dtype=F32) / half)
    ang = pos.astype(F32)[:, None] * inv
    cos, sin = jnp.cos(ang), jnp.sin(ang)
    z = lambda n: jnp.zeros((pos.shape[0], n), F32)
    pad = V7X_LANES - QK_ROPE
    tables = (jnp.concatenate([cos, cos, z(pad)], axis=1),
              jnp.concatenate([-sin, z(half + pad)], axis=1),
              jnp.concatenate([z(half), sin, z(pad)], axis=1))
    return tables, (cos.T, sin.T)


def _prep_mla(w_uq, w_dkv, w_ukv):
    qlora = w_uq.shape[0]
    lora = w_ukv.shape[0]
    wq = w_uq.reshape(qlora, MLA_HEADS, QK_NOPE + QK_ROPE)
    wq = jnp.pad(wq, ((0, 0), (0, 0), (0, MLA_QK_PAD - QK_NOPE - QK_ROPE)))
    w_uq_pad = wq.reshape(qlora, MLA_HEADS * MLA_QK_PAD).astype(BF16)
    w_dkv_pad = jnp.pad(w_dkv, ((0, 0), (0, V7X_LANES - QK_ROPE))).astype(BF16)
    wkv = w_ukv.reshape(lora, MLA_HEADS, QK_NOPE + V_DIM)
    w_uk = wkv[:, :, :QK_NOPE].reshape(lora, MLA_HEADS * QK_NOPE).astype(BF16)
    w_uvt = wkv[:, :, QK_NOPE:].reshape(lora, MLA_HEADS * V_DIM).T.astype(BF16)
    w_ukt_heads = jnp.transpose(wkv[:, :, :QK_NOPE], (1, 2, 0)).astype(BF16)
    w_uv_heads = jnp.transpose(wkv[:, :, QK_NOPE:], (1, 0, 2)).astype(BF16)
    return w_uq_pad, w_dkv_pad, w_uk, w_uvt, w_ukt_heads, w_uv_heads


def _stack(arrays):
    return arrays[0][None] if len(arrays) == 1 else jnp.stack(arrays)


def _trunk(x, batch, seq, past_len, mem_k, mem_v, s_hgrn, s_conv, ckv_past, kpe_past, p):
    depth = p["norm_g"].shape[0]
    pos = past_len + jnp.arange(seq, dtype=jnp.int32)
    tables, tables_t = _rope_tables(pos)
    tables = tuple(jnp.tile(t, (batch, 1)) for t in tables)
    causal = past_len == 0
    hs, cs, ckvs, kpes = [], [], [], []
    for l in range(depth):
        g = lambda i: p["norm_g"][l, i][None, :]
        x = _ffn(x, g(0), p["ffn_w1"], p["ffn_w2"], g(1), l, 0)
        mem = (g(4), p["w_mem_q"][l], mem_k, mem_v, p["w_mem_o"][l], g(5), l)
        if l % 2 == 0:
            e = l // 2
            proj = _norm_matmul(x, g(2), p["w_in0"][e], BF16, tm_want=1024, tn_want=1792)
            oa, s_new = _hgrn(proj, p["lbs"][l][None, :], p["hgrn_gnorm"][e][None, :],
                              s_hgrn[e], batch, seq)
            x, c_new = _even_out(oa, proj, p["conv_wt"][e], s_conv[e], p["w_out0"][e],
                                 g(3), x, mem, batch, seq)
            hs.append(s_new)
            cs.append(c_new)
        else:
            o = l // 2
            q, ckv_new, kpe_new, kpad_new = _mla_proj(
                x, g(2), p["mla_w_dq"][o], p["mla_q_norm"][o][None, :],
                p["w_uq_pad_t"][o] if causal else p["w_uq_pad"][o],
                p["w_dkv_pad"][o], p["mla_kv_norm"][o][None, :], tables,
                tables_t if causal else None)
            if causal:
                assert batch == 1
                tk = _row_tile(seq, ATTN_TILE)
                tq = tk * ATTN_Q_PER_K if seq % (tk * ATTN_Q_PER_K) == 0 else tk
                k, vt = _kv_up(ckv_new, kpad_new, p["w_uk"][o], p["w_uvt"][o], tk)
                att = _attn_causal(q, k, vt, tq, tk)
            else:
                att = _attn_latent(q, ckv_past, kpe_past, ckv_new, kpad_new,
                                   p["w_ukt_heads"][o], p["w_uv_heads"][o], o, batch, past_len)
            x = _proj_residual(att, p["mla_w_o"][o], g(3), x, mem, batch, seq)
            ckvs.append(ckv_new.reshape(batch, seq, -1))
            kpes.append(kpe_new.reshape(batch, seq, -1))
        x = _ffn(x, g(6), p["ffn_w1"], p["ffn_w2"], g(7), l, 1)
    return (x.reshape(batch, seq, -1), _stack(hs), _stack(cs), _stack(ckvs), _stack(kpes))


def kernel(x_prompt, x_sample, mem_prompt, state_hgrn, state_conv, cache_ckv, cache_kpe, cache_mem_k, cache_mem_v, norm_g, ffn_w1, ffn_w2, w_in0, hgrn_lb, hgrn_gnorm, conv_w, w_out0, mla_w_dq, mla_q_norm, mla_w_uq, mla_w_dkv, mla_kv_norm, mla_w_ukv, mla_w_o, mem_norm, w_mem_q, w_mem_kv, w_mem_o):
    batch, seq, d = x_prompt.shape
    dec_batch, dec_seq, _ = x_sample.shape
    past_len = cache_ckv.shape[2]
    depth = norm_g.shape[0]
    n_mem = mem_prompt.shape[1]
    mem_width = MEM_HEADS * MEM_DIM

    prepped = [_prep_mla(mla_w_uq[o], mla_w_dkv[o], mla_w_ukv[o]) for o in range(mla_w_uq.shape[0])]
    p = dict(
        norm_g=norm_g,
        ffn_w1=ffn_w1.astype(BF16), ffn_w2=ffn_w2.astype(BF16),
        w_in0=w_in0.astype(BF16), w_out0=w_out0.astype(BF16),
        lbs=jnp.cumsum(jax.nn.softmax(hgrn_lb.astype(F32), axis=0), axis=0),
        hgrn_gnorm=hgrn_gnorm,
        conv_wt=jnp.swapaxes(conv_w, 1, 2),
        mla_w_dq=mla_w_dq.astype(BF16), mla_q_norm=mla_q_norm, mla_kv_norm=mla_kv_norm,
        w_uq_pad=[t[0] for t in prepped], w_uq_pad_t=[t[0].T for t in prepped],
        w_dkv_pad=[t[1] for t in prepped],
        w_uk=[t[2] for t in prepped], w_uvt=[t[3] for t in prepped],
        w_ukt_heads=[t[4] for t in prepped], w_uv_heads=[t[5] for t in prepped],
        mla_w_o=mla_w_o.astype(BF16),
        w_mem_q=w_mem_q.astype(BF16), w_mem_o=w_mem_o.astype(BF16),
    )

    mem_rows = mem_prompt.reshape(batch * n_mem, d)
    mks, mvs = [], []
    for l in range(depth):
        kv = _norm_matmul(mem_rows, mem_norm[l][None, :], w_mem_kv[l].astype(BF16), F32,
                          tn_want=2 * mem_width)
        kv = kv.reshape(batch, n_mem, 2, mem_width)
        mks.append(kv[:, :, 0])
        mvs.append(kv[:, :, 1])
    mem_k_p = jnp.stack(mks)
    mem_v_p = jnp.stack(mvs)

    n_even = state_hgrn.shape[0]
    h0 = jnp.zeros((n_even, batch) + state_hgrn.shape[2:], F32)
    c0 = jnp.zeros((n_even, batch) + state_conv.shape[2:], F32)
    y_p, p_hgrn, p_conv, p_ckv, p_kpe = _trunk(
        x_prompt.reshape(batch * seq, d), batch, seq, 0, mem_k_p, mem_v_p, h0, c0, None, None, p)
    y_s, s_hgrn, s_conv, s_ckv, s_kpe = _trunk(
        x_sample.reshape(dec_batch * dec_seq, d), dec_batch, dec_seq, past_len,
        cache_mem_k.reshape(depth, dec_batch, n_mem, mem_width),
        cache_mem_v.reshape(depth, dec_batch, n_mem, mem_width),
        state_hgrn, state_conv, cache_ckv, cache_kpe, p)

    mem_shape = (depth, batch, n_mem, MEM_HEADS, MEM_DIM)
    return (y_p, y_s, p_hgrn, p_conv, p_ckv, p_kpe,
            mem_k_p.reshape(mem_shape), mem_v_p.reshape(mem_shape),
            s_hgrn, s_conv, s_ckv, s_kpe)
```

```python
import functools

import jax
import jax.numpy as jnp
import numpy as np
from jax import lax
from jax.experimental import pallas as pl
from jax.experimental.pallas import tpu as pltpu

F32 = jnp.float32
BF16 = jnp.bfloat16

V7X_LANES = 128
V7X_SUBLANES = 8
V7X_VMEM_BYTES = 64 * 1024 * 1024
MIB = 1024 * 1024
SPILL_ROOM_BYTES = 8 * MIB
VMEM_RESERVED_BYTES = 2 * MIB

EPS = 1e-6
CHUNK = 64
HGRN_HEADS = 8
HGRN_DK = 128
HGRN_SUB = 16
HGRN_BLOCK_ROWS = 256
MLA_HEADS = 16
QK_NOPE = 128
QK_ROPE = 64
V_DIM = 128
V_EXT = V_DIM + 16
MLA_QK_PAD = 256
MLA_SCALE = (QK_NOPE + QK_ROPE) ** -0.5
ROPE_THETA = 10000.0
MEM_HEADS = 4
MEM_DIM = 128
CONV_W = 3
NEG_BIG = -1e30
LOG2E = 1.4426950408889634

NT_DIMS = (((1,), (1,)), ((), ()))
TN_DIMS = (((0,), (0,)), ((), ()))


def _vmem_limit(pipelined_bytes, resident_bytes):
    want = 2 * pipelined_bytes + resident_bytes + SPILL_ROOM_BYTES
    return int(min(want, V7X_VMEM_BYTES - VMEM_RESERVED_BYTES))


def _params(semantics, pipelined_bytes, resident_bytes):
    return pltpu.CompilerParams(
        dimension_semantics=semantics,
        vmem_limit_bytes=_vmem_limit(pipelined_bytes, resident_bytes))


def _row_tile(rows, want):
    if rows <= want:
        return rows
    t = want - want % V7X_LANES
    while t > 0 and rows % t:
        t -= V7X_LANES
    assert t > 0, (rows, want)
    return t


ROW_CHUNK = 256


def _row_chunks(rows):
    return [slice(r0, min(r0 + ROW_CHUNK, rows)) for r0 in range(0, rows, ROW_CHUNK)]


def _rms(x, g):
    ms = jnp.mean(x * x, axis=-1, keepdims=True)
    return x * lax.rsqrt(ms + EPS) * g


def _sigmoid(x):
    return 1.0 / (1.0 + jnp.exp(-x))


def _dot(a, b):
    return jnp.dot(a, b, preferred_element_type=F32)


FFN_COL_CHUNK = 512


def _ffn_hidden(xn, w1g_ref, w1u_ref):
    gate = _dot(xn, w1g_ref[...])
    up = _dot(xn, w1u_ref[...])
    return (gate * _sigmoid(gate) * up).astype(BF16)


def _ffn_body(x_ref, gpre_ref, w1g_ref, w1u_ref, w2_ref, gpost_ref, o_ref, xn_ref):
    k = pl.program_id(1)
    last = pl.num_programs(1) - 1
    tm, d = o_ref.shape

    @pl.when(k == 0)
    def _():
        gpre = gpre_ref[...]
        for rs in _row_chunks(tm):
            xn = _rms(x_ref[rs, :], gpre).astype(BF16)
            xn_ref[rs, :] = xn
            o_ref[rs, :] = _dot(_ffn_hidden(xn, w1g_ref, w1u_ref), w2_ref[...])

    @pl.when((k > 0) & (k < last))
    def _():
        o_ref[...] += _dot(_ffn_hidden(xn_ref[...], w1g_ref, w1u_ref), w2_ref[...])

    @pl.when(k == last)
    def _():
        h = _ffn_hidden(xn_ref[...], w1g_ref, w1u_ref)
        ss = jnp.zeros((tm, 1), F32)
        for c0 in range(0, d, FFN_COL_CHUNK):
            cs = slice(c0, c0 + FFN_COL_CHUNK)
            y = o_ref[:, cs] + _dot(h, w2_ref[:, cs])
            o_ref[:, cs] = y
            ss = ss + jnp.sum(y * y, axis=-1, keepdims=True)
        scale = 0.5 * lax.rsqrt(ss * (1.0 / d) + EPS)
        o_ref[...] = x_ref[...] + o_ref[...] * scale * gpost_ref[...]


def _ffn(x, g_pre, w1, w2, g_post, layer, half, tm_want=1024, tf=512):
    rows, d = x.shape
    dff = w2.shape[-2]
    tm = _row_tile(rows, tm_want)
    nk = dff // tf
    assert dff % tf == 0 and nk >= 2 and d % FFN_COL_CHUNK == 0
    blocks = 2 * tm * d * 4 + 3 * d * tf * 2
    resident = tm * d * (2 + 4) + 4 * tm * tf * 4
    return pl.pallas_call(
        _ffn_body,
        grid=(rows // tm, nk),
        in_specs=[
            pl.BlockSpec((tm, d), lambda i, k: (i, 0)),
            pl.BlockSpec((1, d), lambda i, k: (0, 0)),
            pl.BlockSpec((None, None, d, tf), lambda i, k: (layer, half, 0, k)),
            pl.BlockSpec((None, None, d, tf), lambda i, k: (layer, half, 0, nk + k)),
            pl.BlockSpec((None, None, tf, d), lambda i, k: (layer, half, k, 0)),
            pl.BlockSpec((1, d), lambda i, k: (0, 0)),
        ],
        out_specs=pl.BlockSpec((tm, d), lambda i, k: (i, 0)),
        out_shape=jax.ShapeDtypeStruct((rows, d), F32),
        scratch_shapes=[pltpu.VMEM((tm, d), BF16)],
        compiler_params=_params(("parallel", "arbitrary"), blocks, resident),
        name="ffn",
    )(x, g_pre, w1, w1, w2, g_post)


def _norm_matmul_body(x_ref, g_ref, w_ref, o_ref, xn_ref):
    j = pl.program_id(1)

    @pl.when(j == 0)
    def _():
        g = g_ref[...]
        for rs in _row_chunks(x_ref.shape[0]):
            xn = _rms(x_ref[rs, :], g).astype(BF16)
            xn_ref[rs, :] = xn
            o_ref[rs, :] = _dot(xn, w_ref[...]).astype(o_ref.dtype)

    @pl.when(j > 0)
    def _():
        o_ref[...] = _dot(xn_ref[...], w_ref[...]).astype(o_ref.dtype)


def _norm_matmul(x, g, w, out_dtype, tm_want=512, tn_want=1024):
    rows, d = x.shape
    n = w.shape[1]
    tm = _row_tile(rows, tm_want)
    tn = _row_tile(n, tn_want)
    blocks = tm * d * 4 + d * tn * 2 + tm * tn * 4
    resident = tm * d * 2 + tm * tn * 4
    return pl.pallas_call(
        _norm_matmul_body,
        grid=(rows // tm, n // tn),
        in_specs=[
            pl.BlockSpec((tm, d), lambda i, j: (i, 0)),
            pl.BlockSpec((1, d), lambda i, j: (0, 0)),
            pl.BlockSpec((d, tn), lambda i, j: (0, j)),
        ],
        out_specs=pl.BlockSpec((tm, tn), lambda i, j: (i, j)),
        out_shape=jax.ShapeDtypeStruct((rows, n), out_dtype),
        scratch_shapes=[pltpu.VMEM((tm, d), BF16)],
        compiler_params=_params(("parallel", "arbitrary"), blocks, resident),
        name="norm_matmul",
    )(x, g, w)


def _mem_sublayer(x, gpre_ref, wq_ref, mkt_ref, mv_ref, wo_ref, gpost_ref):
    q = _dot(_rms(x, gpre_ref[...]).astype(BF16), wq_ref[...])
    outs = []
    for h in range(MEM_HEADS):
        hs = slice(h * MEM_DIM, (h + 1) * MEM_DIM)
        kht = mkt_ref[hs, :].astype(BF16)
        vh = mv_ref[:, hs].astype(BF16)
        s = _dot(q[:, hs].astype(BF16), kht) * (MEM_DIM ** -0.5)
        p = jnp.exp(s - jnp.max(s, axis=1, keepdims=True))
        p = p * (1.0 / jnp.sum(p, axis=1, keepdims=True))
        outs.append(_dot(p.astype(BF16), vh).astype(BF16))
    o = jnp.concatenate(outs, axis=1)
    return x + _rms(_dot(o, wo_ref[...]), gpost_ref[...])


def _const_spec(a):
    return pl.BlockSpec(a.shape, lambda *_: (0,) * a.ndim, pipeline_mode=pl.Buffered(1))


def _mem_specs(mem_kt, w_q, w_o, g_pre, g_post, layer):
    width, n_mem = mem_kt.shape[2], mem_kt.shape[3]
    pick = lambda b, t: (layer, b, 0, 0)
    specs = [_const_spec(g_pre), _const_spec(w_q),
             pl.BlockSpec((None, None, width, n_mem), pick),
             pl.BlockSpec((None, None, n_mem, width), pick),
             _const_spec(w_o), _const_spec(g_post)]
    nbytes = (w_q.size + w_o.size) * 2 + 4 * n_mem * width * 4
    return specs, nbytes


def _proj_residual_body(a_ref, w_ref, g_ref, x_ref,
                        gpre_ref, wq_ref, mk_ref, mv_ref, wo_ref, gpost_ref, o_ref):
    x1 = x_ref[...] + _rms(_dot(a_ref[...], w_ref[...]), g_ref[...])
    o_ref[...] = _mem_sublayer(x1, gpre_ref, wq_ref, mk_ref, mv_ref, wo_ref, gpost_ref)


def _proj_residual(a, w, g, x, mem, batch, seq, tm_want=512):
    g_pre, w_q, mem_k, mem_v, w_o, g_post, layer = mem
    kdim = a.shape[1]
    d = w.shape[1]
    tm = _row_tile(seq, tm_want)
    nt = seq // tm
    row = lambda n: pl.BlockSpec((tm, n), lambda b, t: (b * nt + t, 0))
    mem_specs, mem_bytes = _mem_specs(mem_k, w_q, w_o, g_pre, g_post, layer)
    blocks = tm * kdim * 2 + 2 * tm * d * 4
    resident = kdim * d * 2 + mem_bytes + 4 * tm * d * 4
    return pl.pallas_call(
        _proj_residual_body,
        grid=(batch, nt),
        in_specs=[row(kdim), _const_spec(w), _const_spec(g), row(d)] + mem_specs,
        out_specs=row(d),
        out_shape=jax.ShapeDtypeStruct(x.shape, F32),
        compiler_params=_params(("parallel", "parallel"), blocks, resident),
        name="proj_residual_mem",
    )(a, w, g, x, g_pre, w_q, mem_k, mem_v, w_o, g_post)


def _hgrn_body(qa_ref, fa_ref, ia_ref, ga_ref, lb_ref, gn_ref, s0_ref,
               oa_ref, sout_ref, st_ref, o_ref):
    c = pl.program_id(1)
    rows = qa_ref.shape[0]
    width = qa_ref.shape[1]
    nsub = rows // HGRN_SUB

    @pl.when(c == 0)
    def _():
        for h in range(HGRN_HEADS):
            st_ref[h] = s0_ref[h].T

    lb = lb_ref[...]
    qa = qa_ref[...].astype(F32)
    q = qa * _sigmoid(qa) * (HGRN_DK ** -0.5)
    f = lb + (1.0 - lb) * _sigmoid(fa_ref[...].astype(F32))
    k = 1.0 - f
    g = jnp.log(f) * LOG2E
    v = ia_ref[...].astype(F32)

    r_i = lax.broadcasted_iota(jnp.int32, (rows, rows), 0)
    c_i = lax.broadcasted_iota(jnp.int32, (rows, rows), 1)
    tri = ((c_i <= r_i) & ((c_i // HGRN_SUB) == (r_i // HGRN_SUB))).astype(BF16)
    g1 = g.astype(BF16)
    rem = g - g1.astype(F32)
    g2 = rem.astype(BF16)
    g3 = (rem - g2.astype(F32)).astype(BF16)
    bl = _dot(tri, g1) + _dot(tri, g2) + _dot(tri, g3)

    qe = (q * jnp.exp2(bl)).astype(BF16)
    row8 = lax.broadcasted_iota(jnp.int32, (V7X_SUBLANES, width), 0)
    tiles = HGRN_SUB // V7X_SUBLANES

    for i in range(nsub):
        r0 = i * HGRN_SUB
        bl_i = bl[r0:r0 + HGRN_SUB]
        q_i = q[r0:r0 + HGRN_SUB]
        k_i = k[r0:r0 + HGRN_SUB]
        v_i = v[r0:r0 + HGRN_SUB]
        b_end = bl_i[HGRN_SUB - 1:HGRN_SUB]
        ke_i = (k_i * jnp.exp2(b_end - bl_i)).astype(BF16)
        dec_i = jnp.exp2(b_end)
        v_bf = v_i.astype(BF16)

        diag = [[jnp.zeros((V7X_SUBLANES, HGRN_DK), F32) for _ in range(tiles)]
                for _ in range(HGRN_HEADS)]
        for s in range(HGRN_SUB):
            ks, bs, vs = k_i[s:s + 1], bl_i[s:s + 1], v_i[s:s + 1]
            for rt in range(s // V7X_SUBLANES, tiles):
                rsl = slice(rt * V7X_SUBLANES, (rt + 1) * V7X_SUBLANES)
                rel = bl_i[rsl] - bs
                if rt == s // V7X_SUBLANES:
                    rel = jnp.where(row8 >= s % V7X_SUBLANES, rel, -jnp.inf)
                w = q_i[rsl] * ks * jnp.exp2(rel)
                for h in range(HGRN_HEADS):
                    hs = slice(h * HGRN_DK, (h + 1) * HGRN_DK)
                    col = jnp.sum(w[:, hs], axis=1, keepdims=True)
                    diag[h][rt] = diag[h][rt] + col * vs[:, hs]

        for h in range(HGRN_HEADS):
            hs = slice(h * HGRN_DK, (h + 1) * HGRN_DK)
            st = st_ref[h]
            inter = lax.dot_general(qe[r0:r0 + HGRN_SUB, hs], st.astype(BF16), NT_DIMS,
                                    preferred_element_type=F32)
            upd = lax.dot_general(v_bf[:, hs], ke_i[:, hs], TN_DIMS,
                                  preferred_element_type=F32)
            st_ref[h] = st * dec_i[:, hs] + upd
            o_ref[r0:r0 + HGRN_SUB, hs] = inter + jnp.concatenate(diag[h], axis=0)

    gn = gn_ref[...]
    for h in range(HGRN_HEADS):
        hs = slice(h * HGRN_DK, (h + 1) * HGRN_DK)
        ga = ga_ref[:, hs].astype(F32)
        oa_ref[:, hs] = (_rms(o_ref[:, hs], gn) * (ga * _sigmoid(ga))).astype(oa_ref.dtype)

    @pl.when(c == pl.num_programs(1) - 1)
    def _():
        for h in range(HGRN_HEADS):
            sout_ref[h] = st_ref[h].T


def _hgrn(proj, lb, gnorm, s0, batch, seq):
    width = HGRN_HEADS * HGRN_DK
    rows = min(HGRN_BLOCK_ROWS, seq)
    nc = seq // rows
    assert seq % rows == 0 and rows % HGRN_SUB == 0
    col = lambda j: pl.BlockSpec((rows, width), lambda b, c: (b * nc + c, j))
    state_spec = pl.BlockSpec((None, HGRN_HEADS, HGRN_DK, HGRN_DK), lambda b, c: (b, 0, 0, 0))
    blocks = 4 * rows * width * 4 + rows * width * 2 + 2 * HGRN_HEADS * HGRN_DK * HGRN_DK * 4
    resident = HGRN_HEADS * HGRN_DK * HGRN_DK * 4 + 16 * rows * width * 4
    return pl.pallas_call(
        _hgrn_body,
        grid=(batch, nc),
        in_specs=[col(0), col(1), col(2), col(3),
                  pl.BlockSpec((1, width), lambda b, c: (0, 0)),
                  pl.BlockSpec((1, HGRN_DK), lambda b, c: (0, 0)),
                  state_spec],
        out_specs=[pl.BlockSpec((rows, width), lambda b, c: (b * nc + c, 0)), state_spec],
        out_shape=[jax.ShapeDtypeStruct((batch * seq, width), BF16),
                   jax.ShapeDtypeStruct(s0.shape, F32)],
        scratch_shapes=[pltpu.VMEM((HGRN_HEADS, HGRN_DK, HGRN_DK), F32),
                        pltpu.VMEM((rows, width), F32)],
        compiler_params=_params(("parallel", "arbitrary"), blocks, resident),
        name="hgrn2",
    )(proj, proj, proj, proj, lb, gnorm, s0)


CONV_PAD = 8


def _even_out_body(oa_ref, bg_ref, cg_ref, hb_ref, cw_ref, cs_ref, w_ref, g_ref, x_ref,
                   gpre_ref, wq_ref, mk_ref, mv_ref, wo_ref, gpost_ref,
                   o_ref, cnew_ref, ubuf_ref):
    t = pl.program_id(1)
    tm = oa_ref.shape[0]
    half = oa_ref.shape[1]
    lo = CONV_PAD - (CONV_W - 1)

    @pl.when(t == 0)
    def _():
        ubuf_ref[lo:CONV_PAD, :] = cs_ref[...]

    u = cg_ref[...].astype(F32) * hb_ref[...].astype(F32)
    ubuf_ref[CONV_PAD:CONV_PAD + tm, :] = u
    yb = u * cw_ref[CONV_W - 1:CONV_W, :]
    for j in range(CONV_W - 1):
        yb = yb + ubuf_ref[lo + j:lo + j + tm, :] * cw_ref[j:j + 1, :]
    ob = (bg_ref[...].astype(F32) * yb).astype(BF16)
    y = _dot(oa_ref[...], w_ref[0:half, :]) + _dot(ob, w_ref[half:2 * half, :])
    x1 = x_ref[...] + _rms(y, g_ref[...])
    o_ref[...] = _mem_sublayer(x1, gpre_ref, wq_ref, mk_ref, mv_ref, wo_ref, gpost_ref)

    tail = ubuf_ref[lo + tm:CONV_PAD + tm, :]
    ubuf_ref[lo:CONV_PAD, :] = tail

    @pl.when(t == pl.num_programs(1) - 1)
    def _():
        cnew_ref[...] = tail


def _even_out(oa, proj, conv_wt, conv_state, w_out, g, x, mem, batch, seq, tm_want=512):
    g_pre, w_q, mem_k, mem_v, w_o, g_post, layer = mem
    half = oa.shape[1]
    d = x.shape[1]
    tm = _row_tile(seq, tm_want)
    nt = seq // tm
    row = lambda b, t: (b * nt + t, 0)
    pcol = lambda j: pl.BlockSpec((tm, half), lambda b, t: (b * nt + t, j))
    cstate = pl.BlockSpec((None, CONV_W - 1, half), lambda b, t: (b, 0, 0))
    mem_specs, mem_bytes = _mem_specs(mem_k, w_q, w_o, g_pre, g_post, layer)
    blocks = tm * half * 4 * proj.dtype.itemsize + 2 * tm * d * 4
    resident = (2 * half * d * 2 + mem_bytes + (tm + CONV_PAD) * half * 4 + 4 * tm * half * 4
                + 4 * tm * d * 4)
    return pl.pallas_call(
        _even_out_body,
        grid=(batch, nt),
        in_specs=[pl.BlockSpec((tm, half), row), pcol(4), pcol(5), pcol(6),
                  _const_spec(conv_wt), cstate, _const_spec(w_out), _const_spec(g),
                  pl.BlockSpec((tm, d), row)] + mem_specs,
        out_specs=[pl.BlockSpec((tm, d), row), cstate],
        out_shape=[jax.ShapeDtypeStruct(x.shape, F32),
                   jax.ShapeDtypeStruct(conv_state.shape, F32)],
        scratch_shapes=[pltpu.VMEM((tm + CONV_PAD, half), F32)],
        compiler_params=_params(("parallel", "arbitrary"), blocks, resident),
        name="even_out_mem",
    )(oa, proj, proj, proj, conv_wt, conv_state, w_out, g, x,
      g_pre, w_q, mem_k, mem_v, w_o, g_post)


Q_PRESCALE = MLA_SCALE * LOG2E


def _rope(x, cos, sin_lo, sin_hi):
    half = QK_ROPE // 2
    return (x * cos
            + pltpu.roll(x, V7X_LANES - half, axis=1) * sin_lo
            + pltpu.roll(x, half, axis=1) * sin_hi)


def _mla_proj_body(x_ref, g_ref, wdq_ref, qn_ref, wuq_ref, wdkv_ref, kvn_ref,
                   cos_ref, slo_ref, shi_ref, *rest, q_transposed):
    if q_transposed:
        cost_ref, sint_ref, q_ref, ckv_ref, kpe_ref, kpad_ref = rest
    else:
        q_ref, ckv_ref, kpe_ref, kpad_ref = rest
    cos, slo, shi = cos_ref[...], slo_ref[...], shi_ref[...]
    lora = ckv_ref.shape[1]
    xn = _rms(x_ref[...], g_ref[...]).astype(BF16)
    cq = _rms(_dot(xn, wdq_ref[...]), qn_ref[...]).astype(BF16)
    ckr = _dot(xn, wdkv_ref[...])
    ckv_ref[...] = _rms(ckr[:, :lora], kvn_ref[...])
    kp = _rope(ckr[:, lora:lora + V7X_LANES], cos, slo, shi)
    kpe_ref[...] = kp[:, :QK_ROPE]
    kpad_ref[...] = kp.astype(BF16)
    if q_transposed:
        half = QK_ROPE // 2
        cost, sint = cost_ref[...], sint_ref[...]
        qt = lax.dot_general(wuq_ref[...], cq, NT_DIMS, preferred_element_type=F32) * Q_PRESCALE
        for h in range(MLA_HEADS):
            r0 = h * MLA_QK_PAD
            x1 = qt[r0 + QK_NOPE:r0 + QK_NOPE + half]
            x2 = qt[r0 + QK_NOPE + half:r0 + QK_NOPE + QK_ROPE]
            q_ref[h, 0:QK_NOPE, :] = qt[r0:r0 + QK_NOPE].astype(BF16)
            q_ref[h, QK_NOPE:QK_NOPE + half, :] = (x1 * cost - x2 * sint).astype(BF16)
            q_ref[h, QK_NOPE + half:QK_NOPE + QK_ROPE, :] = (x1 * sint + x2 * cost).astype(BF16)
            q_ref[h, QK_NOPE + QK_ROPE:MLA_QK_PAD, :] = qt[
                r0 + QK_NOPE + QK_ROPE:r0 + MLA_QK_PAD].astype(BF16)
    else:
        q = _dot(cq, wuq_ref[...]) * Q_PRESCALE
        for h in range(MLA_HEADS):
            c0 = h * MLA_QK_PAD
            q_ref[h, :, 0:QK_NOPE] = q[:, c0:c0 + QK_NOPE].astype(BF16)
            q_ref[h, :, QK_NOPE:MLA_QK_PAD] = _rope(
                q[:, c0 + QK_NOPE:c0 + MLA_QK_PAD], cos, slo, shi).astype(BF16)


def _mla_proj(x, g, w_dq, q_norm, w_uq, w_dkv_pad, kv_norm, tables, tables_t=None, tm_want=512):
    q_transposed = tables_t is not None
    rows, d = x.shape
    qlora = w_dq.shape[1]
    lora = kv_norm.shape[1]
    qw = MLA_HEADS * MLA_QK_PAD
    kw = w_dkv_pad.shape[1]
    tm = _row_tile(rows, tm_want)
    full = lambda a: pl.BlockSpec(a.shape, lambda i: (0,) * a.ndim)
    rowspec = lambda n: pl.BlockSpec((tm, n), lambda i: (i, 0))
    blocks = (tm * d * 4 + (d * qlora + qlora * qw + d * kw) * 2 + 4 * tm * V7X_LANES * 4
              + tm * (qw * 2 + lora * 4 + QK_ROPE * 4 + V7X_LANES * 2))
    resident = tm * (d * 6 + qw * 4 + kw * 4 + qlora * 8)
    in_specs = [rowspec(d), full(g), full(w_dq), full(q_norm), full(w_uq),
                full(w_dkv_pad), full(kv_norm),
                rowspec(V7X_LANES), rowspec(V7X_LANES), rowspec(V7X_LANES)]
    operands = [x, g, w_dq, q_norm, w_uq, w_dkv_pad, kv_norm, *tables]
    if q_transposed:
        in_specs += [pl.BlockSpec((QK_ROPE // 2, tm), lambda i: (0, i))] * 2
        operands += list(tables_t)
        q_spec = pl.BlockSpec((MLA_HEADS, MLA_QK_PAD, tm), lambda i: (0, 0, i))
        q_shape = (MLA_HEADS, MLA_QK_PAD, rows)
    else:
        q_spec = pl.BlockSpec((MLA_HEADS, tm, MLA_QK_PAD), lambda i: (0, i, 0))
        q_shape = (MLA_HEADS, rows, MLA_QK_PAD)
    return pl.pallas_call(
        functools.partial(_mla_proj_body, q_transposed=q_transposed),
        grid=(rows // tm,),
        in_specs=in_specs,
        out_specs=[q_spec, rowspec(lora), rowspec(QK_ROPE), rowspec(V7X_LANES)],
        out_shape=[jax.ShapeDtypeStruct(q_shape, BF16),
                   jax.ShapeDtypeStruct((rows, lora), F32),
                   jax.ShapeDtypeStruct((rows, QK_ROPE), F32),
                   jax.ShapeDtypeStruct((rows, V7X_LANES), BF16)],
        compiler_params=_params(("parallel",), blocks, resident),
        name="mla_proj",
    )(*operands)


def _kv_up_body(ckv_ref, kpad_ref, wuk_ref, wuvt_ref, k_ref, vt_ref):
    c = ckv_ref[...].astype(BF16)
    kn = _dot(c, wuk_ref[...])
    vt = lax.dot_general(wuvt_ref[...], c, NT_DIMS, preferred_element_type=F32)
    ts = c.shape[0]
    vt_ref[:, 0:V_DIM, :] = vt.astype(BF16).reshape(MLA_HEADS, V_DIM, ts)
    ones_row = lax.broadcasted_iota(jnp.int32, (MLA_HEADS, V_EXT - V_DIM, ts), 1) == 0
    vt_ref[:, V_DIM:V_EXT, :] = ones_row.astype(BF16)
    kpad = kpad_ref[...]
    for h in range(MLA_HEADS):
        k_ref[h, :, 0:QK_NOPE] = kn[:, h * QK_NOPE:(h + 1) * QK_NOPE].astype(BF16)
        k_ref[h, :, QK_NOPE:MLA_QK_PAD] = kpad


def _kv_up(ckv, kpad, w_uk, w_uvt, ts):
    rows, lora = ckv.shape
    assert rows % ts == 0
    rowspec = lambda n: pl.BlockSpec((ts, n), lambda i: (i, 0))
    full = lambda a: pl.BlockSpec(a.shape, lambda i: (0,) * a.ndim)
    kw = MLA_HEADS * MLA_QK_PAD
    vw = MLA_HEADS * V_DIM
    blocks = ts * (lora * 4 + V7X_LANES * 2 + kw * 2 + vw * 2) + 2 * lora * vw * 2
    resident = ts * (kw + vw) * 4
    return pl.pallas_call(
        _kv_up_body,
        grid=(rows // ts,),
        in_specs=[rowspec(lora), rowspec(V7X_LANES), full(w_uk), full(w_uvt)],
        out_specs=[pl.BlockSpec((MLA_HEADS, ts, MLA_QK_PAD), lambda i: (0, i, 0)),
                   pl.BlockSpec((None, MLA_HEADS, V_EXT, ts), lambda i: (i, 0, 0, 0))],
        out_shape=[jax.ShapeDtypeStruct((MLA_HEADS, rows, MLA_QK_PAD), BF16),
                   jax.ShapeDtypeStruct((rows // ts, MLA_HEADS, V_EXT, ts), BF16)],
        compiler_params=_params(("parallel",), blocks, resident),
        name="kv_up",
    )(ckv, kpad, w_uk, w_uvt)


ATTN_TILE = 512
ATTN_Q_PER_K = 1
ATTN_SCORE_BUFFERS = 3


def _scores_t(h, qt_ref, k_ref):
    return _dot(k_ref[h], qt_ref[h])


def _softmax_pv_t(h, st, vt_ref, m_ref, acc_ref, mask):
    if mask is not None:
        st = jnp.where(mask, st, -jnp.inf)
    m_prev = m_ref[h]
    m_next = jnp.maximum(m_prev, jnp.max(st, axis=0, keepdims=True))
    alpha = jnp.exp2(m_prev - m_next)
    p = jnp.exp2(st - m_next)
    m_ref[h] = m_next
    acc_ref[h] = alpha * acc_ref[h] + _dot(vt_ref[h], p.astype(BF16))


def _attn_block_t(q_ref, k_ref, vt_ref, m_ref, acc_ref, s_ref, mask):
    nbuf = s_ref.shape[0]
    for h in range(nbuf - 1):
        s_ref[h] = _scores_t(h, q_ref, k_ref)
    for h in range(MLA_HEADS):
        ahead = h + nbuf - 1
        if ahead < MLA_HEADS:
            s_ref[ahead % nbuf] = _scores_t(ahead, q_ref, k_ref)
        _softmax_pv_t(h, s_ref[h % nbuf], vt_ref, m_ref, acc_ref, mask)


def _attn_causal_body(qi_ref, kj_ref, q_ref, k_ref, vt_ref, o_ref, m_ref, acc_ref, s_ref):
    step = pl.program_id(0)
    qi = qi_ref[step]
    kj = kj_ref[step]
    tq, tk = q_ref.shape[2], k_ref.shape[1]
    per_q = tq // tk

    @pl.when(kj == 0)
    def _():
        m_ref[...] = jnp.full(m_ref.shape, NEG_BIG, F32)
        acc_ref[...] = jnp.zeros_like(acc_ref)

    @pl.when(kj < qi * per_q)
    def _():
        _attn_block_t(q_ref, k_ref, vt_ref, m_ref, acc_ref, s_ref, None)

    @pl.when(kj >= qi * per_q)
    def _():
        key_off = 0 if per_q == 1 else (kj - qi * per_q) * tk
        key_chunk = (lax.broadcasted_iota(jnp.int32, (tk, tq), 0) + key_off) // CHUNK
        query_chunk = lax.broadcasted_iota(jnp.int32, (tk, tq), 1) // CHUNK
        _attn_block_t(q_ref, k_ref, vt_ref, m_ref, acc_ref, s_ref, key_chunk <= query_chunk)

    @pl.when(kj == qi * per_q + per_q - 1)
    def _():
        for h in range(MLA_HEADS):
            out = acc_ref[h, 0:V_DIM, :] / acc_ref[h, V_DIM:V_DIM + 1, :]
            o_ref[:, h * V_DIM:(h + 1) * V_DIM] = out.T.astype(o_ref.dtype)


def _attn_causal(q, k, vt, tq, tk):
    rows = q.shape[2]
    assert tk % CHUNK == 0 and tq % tk == 0 and rows % tq == 0 and vt.shape[3] == tk
    per_q = tq // tk
    pairs = [(i, j) for i in range(rows // tq) for j in range(per_q * (i + 1))]
    qi = jnp.asarray(np.array([a for a, _ in pairs], np.int32))
    kj = jnp.asarray(np.array([b for _, b in pairs], np.int32))
    vw = MLA_HEADS * V_DIM
    blocks = MLA_HEADS * (tq * MLA_QK_PAD + tk * (MLA_QK_PAD + V_EXT)) * 2 + tq * vw * 2
    resident = (MLA_HEADS * (V_EXT + 8) * tq + (ATTN_SCORE_BUFFERS + 1) * tk * tq) * 4
    grid_spec = pltpu.PrefetchScalarGridSpec(
        num_scalar_prefetch=2,
        grid=(len(pairs),),
        in_specs=[pl.BlockSpec((MLA_HEADS, MLA_QK_PAD, tq), lambda p, qi, kj: (0, 0, qi[p])),
                  pl.BlockSpec((MLA_HEADS, tk, MLA_QK_PAD), lambda p, qi, kj: (0, kj[p], 0)),
                  pl.BlockSpec((None, MLA_HEADS, V_EXT, tk), lambda p, qi, kj: (kj[p], 0, 0, 0))],
        out_specs=pl.BlockSpec((tq, vw), lambda p, qi, kj: (qi[p], 0)),
        scratch_shapes=[pltpu.VMEM((MLA_HEADS, 1, tq), F32),
                        pltpu.VMEM((MLA_HEADS, V_EXT, tq), F32),
                        pltpu.VMEM((ATTN_SCORE_BUFFERS, tk, tq), F32)])
    return pl.pallas_call(
        _attn_causal_body,
        grid_spec=grid_spec,
        out_shape=jax.ShapeDtypeStruct((rows, vw), BF16),
        compiler_params=_params(("arbitrary",), blocks, resident),
        name="mla_attn_causal",
    )(qi, kj, q, k, vt)


def _attn_latent_body(q_ref, ckvp_ref, kpep_ref, ckvn_ref, kpadn_ref, wukt_ref, wuv_ref, o_ref,
                      ckv_ref, kpe_ref, qa_ref, qp_ref, *, q_pos0):
    tq = q_ref.shape[1]
    past = ckvp_ref.shape[0]
    n_keys = ckv_ref.shape[0]
    n_valid = past + tq

    ckv_ref[0:past, :] = ckvp_ref[...].astype(BF16)
    ckv_ref[past:n_valid, :] = ckvn_ref[...].astype(BF16)
    ckv_ref[n_valid:n_keys, :] = jnp.zeros((n_keys - n_valid, ckv_ref.shape[1]), BF16)
    kpe_ref[...] = jnp.zeros_like(kpe_ref)
    kpe_ref[0:past, 0:QK_ROPE] = kpep_ref[...].astype(BF16)
    kpe_ref[past:n_valid, :] = kpadn_ref[...]

    for h in range(MLA_HEADS):
        rows = slice(h * tq, (h + 1) * tq)
        qa_ref[rows, :] = _dot(q_ref[h, :, 0:QK_NOPE], wukt_ref[h]).astype(BF16)
        qp_ref[rows, :] = q_ref[h, :, QK_NOPE:MLA_QK_PAD]

    s = (lax.dot_general(qa_ref[...], ckv_ref[...], NT_DIMS, preferred_element_type=F32)
         + lax.dot_general(qp_ref[...], kpe_ref[...], NT_DIMS, preferred_element_type=F32))
    s = s.reshape(MLA_HEADS, tq, n_keys)
    query_chunk = (lax.broadcasted_iota(jnp.int32, (tq, n_keys), 0) + q_pos0) // CHUNK
    kpos = lax.broadcasted_iota(jnp.int32, (tq, n_keys), 1)
    mask = (kpos // CHUNK <= query_chunk) & (kpos < n_valid)
    s = jnp.where(mask[None], s, -jnp.inf)
    p = jnp.exp2(s - jnp.max(s, axis=2, keepdims=True))
    inv_l = 1.0 / jnp.sum(p, axis=2, keepdims=True)
    lat = _dot(p.astype(BF16).reshape(MLA_HEADS * tq, n_keys), ckv_ref[...])
    lat = (lat.reshape(MLA_HEADS, tq, -1) * inv_l).astype(BF16)
    for h in range(MLA_HEADS):
        o_ref[:, h * V_DIM:(h + 1) * V_DIM] = _dot(lat[h], wuv_ref[h]).astype(o_ref.dtype)


def _attn_latent(q, ckv_past, kpe_past, ckv_new, kpad_new, w_ukt, w_uv, layer, batch, q_pos0):
    tq = q.shape[1] // batch
    past, lora = ckv_past.shape[2], ckv_past.shape[3]
    n_keys = -(-(past + tq) // V7X_LANES) * V7X_LANES
    vw = MLA_HEADS * V_DIM
    rows = MLA_HEADS * tq
    blocks = (MLA_HEADS * tq * MLA_QK_PAD * 2 + past * (lora + V7X_LANES) * 4
              + tq * (lora * 4 + V7X_LANES * 2) + tq * vw * 2)
    resident = (2 * MLA_HEADS * lora * QK_NOPE * 2 + n_keys * (lora + V7X_LANES) * 2
                + rows * (lora + V7X_LANES) * 2 + 4 * rows * n_keys * 4 + 2 * rows * lora * 4)
    return pl.pallas_call(
        functools.partial(_attn_latent_body, q_pos0=q_pos0),
        grid=(batch,),
        in_specs=[pl.BlockSpec((MLA_HEADS, tq, MLA_QK_PAD), lambda b: (0, b, 0)),
                  pl.BlockSpec((None, None, past, lora), lambda b: (layer, b, 0, 0)),
                  pl.BlockSpec((None, None, past, QK_ROPE), lambda b: (layer, b, 0, 0)),
                  pl.BlockSpec((tq, lora), lambda b: (b, 0)),
                  pl.BlockSpec((tq, V7X_LANES), lambda b: (b, 0)),
                  _const_spec(w_ukt), _const_spec(w_uv)],
        out_specs=pl.BlockSpec((tq, vw), lambda b: (b, 0)),
        out_shape=jax.ShapeDtypeStruct((q.shape[1], vw), BF16),
        scratch_shapes=[pltpu.VMEM((n_keys, lora), BF16), pltpu.VMEM((n_keys, V7X_LANES), BF16),
                        pltpu.VMEM((rows, lora), BF16), pltpu.VMEM((rows, V7X_LANES), BF16)],
        compiler_params=_params(("parallel",), blocks, resident),
        name="mla_attn_latent",
    )(q, ckv_past, kpe_past, ckv_new, kpad_new, w_ukt, w_uv)


def _rope_tables(pos):
    half = QK_ROPE // 2
    inv = ROPE_THETA ** (-jnp.arange(half, dtype=F32) / half)
    ang = pos.astype(F32)[:, None] * inv
    cos, sin = jnp.cos(ang), jnp.sin(ang)
    z = lambda n: jnp.zeros((pos.shape[0], n), F32)
    pad = V7X_LANES - QK_ROPE
    tables = (jnp.concatenate([cos, cos, z(pad)], axis=1),
              jnp.concatenate([-sin, z(half + pad)], axis=1),
              jnp.concatenate([z(half), sin, z(pad)], axis=1))
    return tables, (cos.T, sin.T)


def _prep_mla(w_uq, w_dkv, w_ukv):
    qlora = w_uq.shape[0]
    lora = w_ukv.shape[0]
    wq = w_uq.reshape(qlora, MLA_HEADS, QK_NOPE + QK_ROPE)
    wq = jnp.pad(wq, ((0, 0), (0, 0), (0, MLA_QK_PAD - QK_NOPE - QK_ROPE)))
    w_uq_pad = wq.reshape(qlora, MLA_HEADS * MLA_QK_PAD).astype(BF16)
    w_dkv_pad = jnp.pad(w_dkv, ((0, 0), (0, V7X_LANES - QK_ROPE))).astype(BF16)
    wkv = w_ukv.reshape(lora, MLA_HEADS, QK_NOPE + V_DIM)
    w_uk = wkv[:, :, :QK_NOPE].reshape(lora, MLA_HEADS * QK_NOPE).astype(BF16)
    w_uvt = wkv[:, :, QK_NOPE:].reshape(lora, MLA_HEADS * V_DIM).T.astype(BF16)
    w_ukt_heads = jnp.transpose(wkv[:, :, :QK_NOPE], (1, 2, 0)).astype(BF16)
    w_uv_heads = jnp.transpose(wkv[:, :, QK_NOPE:], (1, 0, 2)).astype(BF16)
    return w_uq_pad, w_dkv_pad, w_uk, w_uvt, w_ukt_heads, w_uv_heads


def _stack(arrays):
    return arrays[0][None] if len(arrays) == 1 else jnp.stack(arrays)


def _trunk(x, batch, seq, past_len, mem_k, mem_v, s_hgrn, s_conv, ckv_past, kpe_past, p):
    depth = p["norm_g"].shape[0]
    pos = past_len + jnp.arange(seq, dtype=jnp.int32)
    tables, tables_t = _rope_tables(pos)
    tables = tuple(jnp.tile(t, (batch, 1)) for t in tables)
    causal = past_len == 0
    hs, cs, ckvs, kpes = [], [], [], []
    for l in range(depth):
        g = lambda i: p["norm_g"][l, i][None, :]
        x = _ffn(x, g(0), p["ffn_w1"], p["ffn_w2"], g(1), l, 0)
        mem = (g(4), p["w_mem_q"][l], mem_k, mem_v, p["w_mem_o"][l], g(5), l)
        if l % 2 == 0:
            e = l // 2
            proj = _norm_matmul(x, g(2), p["w_in0"][e], BF16, tm_want=1024, tn_want=1792)
            oa, s_new = _hgrn(proj, p["lbs"][l][None, :], p["hgrn_gnorm"][e][None, :],
                              s_hgrn[e], batch, seq)
            x, c_new = _even_out(oa, proj, p["conv_wt"][e], s_conv[e], p["w_out0"][e],
                                 g(3), x, mem, batch, seq)
            hs.append(s_new)
            cs.append(c_new)
        else:
            o = l // 2
            q, ckv_new, kpe_new, kpad_new = _mla_proj(
                x, g(2), p["mla_w_dq"][o], p["mla_q_norm"][o][None, :],
                p["w_uq_pad_t"][o] if causal else p["w_uq_pad"][o],
                p["w_dkv_pad"][o], p["mla_kv_norm"][o][None, :], tables,
                tables_t if causal else None)
            if causal:
                assert batch == 1
                tk = _row_tile(seq, ATTN_TILE)
                tq = tk * ATTN_Q_PER_K if seq % (tk * ATTN_Q_PER_K) == 0 else tk
                k, vt = _kv_up(ckv_new, kpad_new, p["w_uk"][o], p["w_uvt"][o], tk)
                att = _attn_causal(q, k, vt, tq, tk)
            else:
                att = _attn_latent(q, ckv_past, kpe_past, ckv_new, kpad_new,
                                   p["w_ukt_heads"][o], p["w_uv_heads"][o], o, batch, past_len)
            x = _proj_residual(att, p["mla_w_o"][o], g(3), x, mem, batch, seq)
            ckvs.append(ckv_new.reshape(batch, seq, -1))
            kpes.append(kpe_new.reshape(batch, seq, -1))
        x = _ffn(x, g(6), p["ffn_w1"], p["ffn_w2"], g(7), l, 1)
    return (x.reshape(batch, seq, -1), _stack(hs), _stack(cs), _stack(ckvs), _stack(kpes))


def kernel(x_prompt, x_sample, mem_prompt, state_hgrn, state_conv, cache_ckv, cache_kpe, cache_mem_k, cache_mem_v, norm_g, ffn_w1, ffn_w2, w_in0, hgrn_lb, hgrn_gnorm, conv_w, w_out0, mla_w_dq, mla_q_norm, mla_w_uq, mla_w_dkv, mla_kv_norm, mla_w_ukv, mla_w_o, mem_norm, w_mem_q, w_mem_kv, w_mem_o):
    batch, seq, d = x_prompt.shape
    dec_batch, dec_seq, _ = x_sample.shape
    past_len = cache_ckv.shape[2]
    depth = norm_g.shape[0]
    n_mem = mem_prompt.shape[1]
    mem_width = MEM_HEADS * MEM_DIM

    prepped = [_prep_mla(mla_w_uq[o], mla_w_dkv[o], mla_w_ukv[o]) for o in range(mla_w_uq.shape[0])]
    p = dict(
        norm_g=norm_g,
        ffn_w1=ffn_w1.astype(BF16), ffn_w2=ffn_w2.astype(BF16),
        w_in0=w_in0.astype(BF16), w_out0=w_out0.astype(BF16),
        lbs=jnp.cumsum(jax.nn.softmax(hgrn_lb.astype(F32), axis=0), axis=0),
        hgrn_gnorm=hgrn_gnorm,
        conv_wt=jnp.swapaxes(conv_w, 1, 2),
        mla_w_dq=mla_w_dq.astype(BF16), mla_q_norm=mla_q_norm, mla_kv_norm=mla_kv_norm,
        w_uq_pad=[t[0] for t in prepped], w_uq_pad_t=[t[0].T for t in prepped],
        w_dkv_pad=[t[1] for t in prepped],
        w_uk=[t[2] for t in prepped], w_uvt=[t[3] for t in prepped],
        w_ukt_heads=[t[4] for t in prepped], w_uv_heads=[t[5] for t in prepped],
        mla_w_o=mla_w_o.astype(BF16),
        w_mem_q=w_mem_q.astype(BF16), w_mem_o=w_mem_o.astype(BF16),
    )

    mem_rows = mem_prompt.reshape(batch * n_mem, d)
    mks, mvs = [], []
    for l in range(depth):
        kv = _norm_matmul(mem_rows, mem_norm[l][None, :], w_mem_kv[l].astype(BF16), F32,
                          tn_want=2 * mem_width)
        kv = kv.reshape(batch, n_mem, 2, mem_width)
        mks.append(kv[:, :, 0])
        mvs.append(kv[:, :, 1])
    mem_k_p = jnp.stack(mks)
    mem_v_p = jnp.stack(mvs)

    n_even = state_hgrn.shape[0]
    h0 = jnp.zeros((n_even, batch) + state_hgrn.shape[2:], F32)
    c0 = jnp.zeros((n_even, batch) + state_conv.shape[2:], F32)
    y_p, p_hgrn, p_conv, p_ckv, p_kpe = _trunk(
        x_prompt.reshape(batch * seq, d), batch, seq, 0, jnp.swapaxes(mem_k_p, 2, 3), mem_v_p,
        h0, c0, None, None, p)
    y_s, s_hgrn, s_conv, s_ckv, s_kpe = _trunk(
        x_sample.reshape(dec_batch * dec_seq, d), dec_batch, dec_seq, past_len,
        jnp.transpose(cache_mem_k, (0, 1, 3, 4, 2)).reshape(depth, dec_batch, mem_width, n_mem),
        cache_mem_v.reshape(depth, dec_batch, n_mem, mem_width),
        state_hgrn, state_conv, cache_ckv, cache_kpe, p)

    mem_shape = (depth, batch, n_mem, MEM_HEADS, MEM_DIM)
    return (y_p, y_s, p_hgrn, p_conv, p_ckv, p_kpe,
            mem_k_p.reshape(mem_shape), mem_v_p.reshape(mem_shape),
            s_hgrn, s_conv, s_ckv, s_kpe)
```

```python
import functools

import jax
import jax.numpy as jnp
import numpy as np
from jax import lax
from jax.experimental import pallas as pl
from jax.experimental.pallas import tpu as pltpu

F32 = jnp.float32
BF16 = jnp.bfloat16

V7X_LANES = 128
V7X_SUBLANES = 8
V7X_VMEM_BYTES = 64 * 1024 * 1024
MIB = 1024 * 1024
SPILL_ROOM_BYTES = 8 * MIB
VMEM_RESERVED_BYTES = 2 * MIB

EPS = 1e-6
CHUNK = 64
HGRN_HEADS = 8
HGRN_DK = 128
HGRN_SUB = 16
HGRN_BLOCK_ROWS = 256
MLA_HEADS = 16
QK_NOPE = 128
QK_ROPE = 64
V_DIM = 128
V_EXT = V_DIM + 16
MLA_QK_PAD = 256
MLA_SCALE = (QK_NOPE + QK_ROPE) ** -0.5
ROPE_THETA = 10000.0
MEM_HEADS = 4
MEM_DIM = 128
CONV_W = 3
NEG_BIG = -1e30
LOG2E = 1.4426950408889634

NT_DIMS = (((1,), (1,)), ((), ()))
TN_DIMS = (((0,), (0,)), ((), ()))


def _vmem_limit(pipelined_bytes, resident_bytes):
    want = 2 * pipelined_bytes + resident_bytes + SPILL_ROOM_BYTES
    return int(min(want, V7X_VMEM_BYTES - VMEM_RESERVED_BYTES))


def _params(semantics, pipelined_bytes, resident_bytes):
    return pltpu.CompilerParams(
        dimension_semantics=semantics,
        vmem_limit_bytes=_vmem_limit(pipelined_bytes, resident_bytes))


def _row_tile(rows, want):
    if rows <= want:
        return rows
    t = want - want % V7X_LANES
    while t > 0 and rows % t:
        t -= V7X_LANES
    assert t > 0, (rows, want)
    return t


ROW_CHUNK = 256


def _row_chunks(rows):
    return [slice(r0, min(r0 + ROW_CHUNK, rows)) for r0 in range(0, rows, ROW_CHUNK)]


def _rms(x, g):
    ms = jnp.mean(x * x, axis=-1, keepdims=True)
    return x * lax.rsqrt(ms + EPS) * g


def _sigmoid(x):
    return 1.0 / (1.0 + jnp.exp(-x))


def _dot(a, b):
    return jnp.dot(a, b, preferred_element_type=F32)


FFN_COL_CHUNK = 512


def _ffn_hidden(xn, w1g_ref, w1u_ref):
    gate = _dot(xn, w1g_ref[...])
    up = _dot(xn, w1u_ref[...])
    return (gate * _sigmoid(gate) * up).astype(BF16)


def _ffn_body(x_ref, gpre_ref, w1g_ref, w1u_ref, w2_ref, gpost_ref, o_ref, xn_ref):
    k = pl.program_id(1)
    last = pl.num_programs(1) - 1
    tm, d = o_ref.shape

    @pl.when(k == 0)
    def _():
        gpre = gpre_ref[...]
        for rs in _row_chunks(tm):
            xn = _rms(x_ref[rs, :], gpre).astype(BF16)
            xn_ref[rs, :] = xn
            o_ref[rs, :] = _dot(_ffn_hidden(xn, w1g_ref, w1u_ref), w2_ref[...])

    @pl.when((k > 0) & (k < last))
    def _():
        o_ref[...] += _dot(_ffn_hidden(xn_ref[...], w1g_ref, w1u_ref), w2_ref[...])

    @pl.when(k == last)
    def _():
        h = _ffn_hidden(xn_ref[...], w1g_ref, w1u_ref)
        ss = jnp.zeros((tm, 1), F32)
        for c0 in range(0, d, FFN_COL_CHUNK):
            cs = slice(c0, c0 + FFN_COL_CHUNK)
            y = o_ref[:, cs] + _dot(h, w2_ref[:, cs])
            o_ref[:, cs] = y
            ss = ss + jnp.sum(y * y, axis=-1, keepdims=True)
        scale = 0.5 * lax.rsqrt(ss * (1.0 / d) + EPS)
        o_ref[...] = x_ref[...] + o_ref[...] * scale * gpost_ref[...]


def _ffn(x, g_pre, w1, w2, g_post, layer, half, tm_want=1024, tf=512):
    rows, d = x.shape
    dff = w2.shape[-2]
    tm = _row_tile(rows, tm_want)
    nk = dff // tf
    assert dff % tf == 0 and nk >= 2 and d % FFN_COL_CHUNK == 0
    blocks = 2 * tm * d * 4 + 3 * d * tf * 2
    resident = tm * d * (2 + 4) + 4 * tm * tf * 4
    return pl.pallas_call(
        _ffn_body,
        grid=(rows // tm, nk),
        in_specs=[
            pl.BlockSpec((tm, d), lambda i, k: (i, 0)),
            pl.BlockSpec((1, d), lambda i, k: (0, 0)),
            pl.BlockSpec((None, None, d, tf), lambda i, k: (layer, half, 0, k)),
            pl.BlockSpec((None, None, d, tf), lambda i, k: (layer, half, 0, nk + k)),
            pl.BlockSpec((None, None, tf, d), lambda i, k: (layer, half, k, 0)),
            pl.BlockSpec((1, d), lambda i, k: (0, 0)),
        ],
        out_specs=pl.BlockSpec((tm, d), lambda i, k: (i, 0)),
        out_shape=jax.ShapeDtypeStruct((rows, d), F32),
        scratch_shapes=[pltpu.VMEM((tm, d), BF16)],
        compiler_params=_params(("parallel", "arbitrary"), blocks, resident),
        name="ffn",
    )(x, g_pre, w1, w1, w2, g_post)


def _norm_matmul_body(x_ref, g_ref, w_ref, o_ref, xn_ref):
    j = pl.program_id(1)

    @pl.when(j == 0)
    def _():
        g = g_ref[...]
        for rs in _row_chunks(x_ref.shape[0]):
            xn = _rms(x_ref[rs, :], g).astype(BF16)
            xn_ref[rs, :] = xn
            o_ref[rs, :] = _dot(xn, w_ref[...]).astype(o_ref.dtype)

    @pl.when(j > 0)
    def _():
        o_ref[...] = _dot(xn_ref[...], w_ref[...]).astype(o_ref.dtype)


def _norm_matmul(x, g, w, out_dtype, tm_want=512, tn_want=1024):
    rows, d = x.shape
    n = w.shape[1]
    tm = _row_tile(rows, tm_want)
    tn = _row_tile(n, tn_want)
    blocks = tm * d * 4 + d * tn * 2 + tm * tn * 4
    resident = tm * d * 2 + tm * tn * 4
    return pl.pallas_call(
        _norm_matmul_body,
        grid=(rows // tm, n // tn),
        in_specs=[
            pl.BlockSpec((tm, d), lambda i, j: (i, 0)),
            pl.BlockSpec((1, d), lambda i, j: (0, 0)),
            pl.BlockSpec((d, tn), lambda i, j: (0, j)),
        ],
        out_specs=pl.BlockSpec((tm, tn), lambda i, j: (i, j)),
        out_shape=jax.ShapeDtypeStruct((rows, n), out_dtype),
        scratch_shapes=[pltpu.VMEM((tm, d), BF16)],
        compiler_params=_params(("parallel", "arbitrary"), blocks, resident),
        name="norm_matmul",
    )(x, g, w)


def _mem_sublayer(x, gpre_ref, wq_ref, mk_ref, mv_ref, wo_ref, gpost_ref):
    q = _dot(_rms(x, gpre_ref[...]).astype(BF16), wq_ref[...])
    outs = []
    for h in range(MEM_HEADS):
        hs = slice(h * MEM_DIM, (h + 1) * MEM_DIM)
        kh = mk_ref[:, hs].astype(BF16)
        vh = mv_ref[:, hs].astype(BF16)
        s = lax.dot_general(q[:, hs].astype(BF16), kh, NT_DIMS,
                            preferred_element_type=F32) * (MEM_DIM ** -0.5)
        p = jnp.exp(s - jnp.max(s, axis=1, keepdims=True))
        p = p * (1.0 / jnp.sum(p, axis=1, keepdims=True))
        outs.append(_dot(p.astype(BF16), vh).astype(BF16))
    o = jnp.concatenate(outs, axis=1)
    return x + _rms(_dot(o, wo_ref[...]), gpost_ref[...])


def _const_spec(a):
    return pl.BlockSpec(a.shape, lambda *_: (0,) * a.ndim, pipeline_mode=pl.Buffered(1))


def _mem_specs(mem_k, w_q, w_o, g_pre, g_post, layer):
    n_mem, width = mem_k.shape[2], mem_k.shape[3]
    mem = pl.BlockSpec((None, None, n_mem, width), lambda b, t: (layer, b, 0, 0))
    specs = [_const_spec(g_pre), _const_spec(w_q), mem, mem, _const_spec(w_o),
             _const_spec(g_post)]
    nbytes = (w_q.size + w_o.size) * 2 + 4 * n_mem * width * 4
    return specs, nbytes


def _proj_residual_body(a_ref, w_ref, g_ref, x_ref,
                        gpre_ref, wq_ref, mk_ref, mv_ref, wo_ref, gpost_ref, o_ref):
    x1 = x_ref[...] + _rms(_dot(a_ref[...], w_ref[...]), g_ref[...])
    o_ref[...] = _mem_sublayer(x1, gpre_ref, wq_ref, mk_ref, mv_ref, wo_ref, gpost_ref)


def _proj_residual(a, w, g, x, mem, batch, seq, tm_want=512):
    g_pre, w_q, mem_k, mem_v, w_o, g_post, layer = mem
    kdim = a.shape[1]
    d = w.shape[1]
    tm = _row_tile(seq, tm_want)
    nt = seq // tm
    row = lambda n: pl.BlockSpec((tm, n), lambda b, t: (b * nt + t, 0))
    mem_specs, mem_bytes = _mem_specs(mem_k, w_q, w_o, g_pre, g_post, layer)
    blocks = tm * kdim * 2 + 2 * tm * d * 4
    resident = kdim * d * 2 + mem_bytes + 4 * tm * d * 4
    return pl.pallas_call(
        _proj_residual_body,
        grid=(batch, nt),
        in_specs=[row(kdim), _const_spec(w), _const_spec(g), row(d)] + mem_specs,
        out_specs=row(d),
        out_shape=jax.ShapeDtypeStruct(x.shape, F32),
        compiler_params=_params(("parallel", "parallel"), blocks, resident),
        name="proj_residual_mem",
    )(a, w, g, x, g_pre, w_q, mem_k, mem_v, w_o, g_post)


def _hgrn_body(qa_ref, fa_ref, ia_ref, ga_ref, lb_ref, gn_ref, s0_ref,
               oa_ref, sout_ref, st_ref, o_ref):
    c = pl.program_id(1)
    rows = qa_ref.shape[0]
    width = qa_ref.shape[1]
    nsub = rows // HGRN_SUB

    @pl.when(c == 0)
    def _():
        for h in range(HGRN_HEADS):
            st_ref[h] = s0_ref[h].T

    lb = lb_ref[...]
    qa = qa_ref[...].astype(F32)
    q = qa * _sigmoid(qa) * (HGRN_DK ** -0.5)
    f = lb + (1.0 - lb) * _sigmoid(fa_ref[...].astype(F32))
    k = 1.0 - f
    g = jnp.log(f) * LOG2E
    v = ia_ref[...].astype(F32)

    r_i = lax.broadcasted_iota(jnp.int32, (rows, rows), 0)
    c_i = lax.broadcasted_iota(jnp.int32, (rows, rows), 1)
    tri = ((c_i <= r_i) & ((c_i // HGRN_SUB) == (r_i // HGRN_SUB))).astype(BF16)
    g1 = g.astype(BF16)
    rem = g - g1.astype(F32)
    g2 = rem.astype(BF16)
    g3 = (rem - g2.astype(F32)).astype(BF16)
    bl = _dot(tri, g1) + _dot(tri, g2) + _dot(tri, g3)

    qe = (q * jnp.exp2(bl)).astype(BF16)
    row8 = lax.broadcasted_iota(jnp.int32, (V7X_SUBLANES, width), 0)
    tiles = HGRN_SUB // V7X_SUBLANES

    for i in range(nsub):
        r0 = i * HGRN_SUB
        bl_i = bl[r0:r0 + HGRN_SUB]
        q_i = q[r0:r0 + HGRN_SUB]
        k_i = k[r0:r0 + HGRN_SUB]
        v_i = v[r0:r0 + HGRN_SUB]
        b_end = bl_i[HGRN_SUB - 1:HGRN_SUB]
        ke_i = (k_i * jnp.exp2(b_end - bl_i)).astype(BF16)
        dec_i = jnp.exp2(b_end)
        v_bf = v_i.astype(BF16)

        diag = [[jnp.zeros((V7X_SUBLANES, HGRN_DK), F32) for _ in range(tiles)]
                for _ in range(HGRN_HEADS)]
        for s in range(HGRN_SUB):
            ks, bs, vs = k_i[s:s + 1], bl_i[s:s + 1], v_i[s:s + 1]
            for rt in range(s // V7X_SUBLANES, tiles):
                rsl = slice(rt * V7X_SUBLANES, (rt + 1) * V7X_SUBLANES)
                rel = bl_i[rsl] - bs
                if rt == s // V7X_SUBLANES:
                    rel = jnp.where(row8 >= s % V7X_SUBLANES, rel, -jnp.inf)
                w = q_i[rsl] * ks * jnp.exp2(rel)
                for h in range(HGRN_HEADS):
                    hs = slice(h * HGRN_DK, (h + 1) * HGRN_DK)
                    col = jnp.sum(w[:, hs], axis=1, keepdims=True)
                    diag[h][rt] = diag[h][rt] + col * vs[:, hs]

        for h in range(HGRN_HEADS):
            hs = slice(h * HGRN_DK, (h + 1) * HGRN_DK)
            st = st_ref[h]
            inter = lax.dot_general(qe[r0:r0 + HGRN_SUB, hs], st.astype(BF16), NT_DIMS,
                                    preferred_element_type=F32)
            upd = lax.dot_general(v_bf[:, hs], ke_i[:, hs], TN_DIMS,
                                  preferred_element_type=F32)
            st_ref[h] = st * dec_i[:, hs] + upd
            o_ref[r0:r0 + HGRN_SUB, hs] = inter + jnp.concatenate(diag[h], axis=0)

    gn = gn_ref[...]
    for h in range(HGRN_HEADS):
        hs = slice(h * HGRN_DK, (h + 1) * HGRN_DK)
        ga = ga_ref[:, hs].astype(F32)
        oa_ref[:, hs] = (_rms(o_ref[:, hs], gn) * (ga * _sigmoid(ga))).astype(oa_ref.dtype)

    @pl.when(c == pl.num_programs(1) - 1)
    def _():
        for h in range(HGRN_HEADS):
            sout_ref[h] = st_ref[h].T


def _hgrn(proj, lb, gnorm, s0, batch, seq):
    width = HGRN_HEADS * HGRN_DK
    rows = min(HGRN_BLOCK_ROWS, seq)
    nc = seq // rows
    assert seq % rows == 0 and rows % HGRN_SUB == 0
    col = lambda j: pl.BlockSpec((rows, width), lambda b, c: (b * nc + c, j))
    state_spec = pl.BlockSpec((None, HGRN_HEADS, HGRN_DK, HGRN_DK), lambda b, c: (b, 0, 0, 0))
    blocks = 4 * rows * width * 4 + rows * width * 2 + 2 * HGRN_HEADS * HGRN_DK * HGRN_DK * 4
    resident = HGRN_HEADS * HGRN_DK * HGRN_DK * 4 + 16 * rows * width * 4
    return pl.pallas_call(
        _hgrn_body,
        grid=(batch, nc),
        in_specs=[col(0), col(1), col(2), col(3),
                  pl.BlockSpec((1, width), lambda b, c: (0, 0)),
                  pl.BlockSpec((1, HGRN_DK), lambda b, c: (0, 0)),
                  state_spec],
        out_specs=[pl.BlockSpec((rows, width), lambda b, c: (b * nc + c, 0)), state_spec],
        out_shape=[jax.ShapeDtypeStruct((batch * seq, width), BF16),
                   jax.ShapeDtypeStruct(s0.shape, F32)],
        scratch_shapes=[pltpu.VMEM((HGRN_HEADS, HGRN_DK, HGRN_DK), F32),
                        pltpu.VMEM((rows, width), F32)],
        compiler_params=_params(("parallel", "arbitrary"), blocks, resident),
        name="hgrn2",
    )(proj, proj, proj, proj, lb, gnorm, s0)


CONV_PAD = 8


def _even_out_body(oa_ref, bg_ref, cg_ref, hb_ref, cw_ref, cs_ref, w_ref, g_ref, x_ref,
                   gpre_ref, wq_ref, mk_ref, mv_ref, wo_ref, gpost_ref,
                   o_ref, cnew_ref, ubuf_ref):
    t = pl.program_id(1)
    tm = oa_ref.shape[0]
    half = oa_ref.shape[1]
    lo = CONV_PAD - (CONV_W - 1)

    @pl.when(t == 0)
    def _():
        ubuf_ref[lo:CONV_PAD, :] = cs_ref[...]

    u = cg_ref[...].astype(F32) * hb_ref[...].astype(F32)
    ubuf_ref[CONV_PAD:CONV_PAD + tm, :] = u
    yb = u * cw_ref[CONV_W - 1:CONV_W, :]
    for j in range(CONV_W - 1):
        yb = yb + ubuf_ref[lo + j:lo + j + tm, :] * cw_ref[j:j + 1, :]
    ob = (bg_ref[...].astype(F32) * yb).astype(BF16)
    y = _dot(oa_ref[...], w_ref[0:half, :]) + _dot(ob, w_ref[half:2 * half, :])
    x1 = x_ref[...] + _rms(y, g_ref[...])
    o_ref[...] = _mem_sublayer(x1, gpre_ref, wq_ref, mk_ref, mv_ref, wo_ref, gpost_ref)

    tail = ubuf_ref[lo + tm:CONV_PAD + tm, :]
    ubuf_ref[lo:CONV_PAD, :] = tail

    @pl.when(t == pl.num_programs(1) - 1)
    def _():
        cnew_ref[...] = tail


def _even_out(oa, proj, conv_wt, conv_state, w_out, g, x, mem, batch, seq, tm_want=512):
    g_pre, w_q, mem_k, mem_v, w_o, g_post, layer = mem
    half = oa.shape[1]
    d = x.shape[1]
    tm = _row_tile(seq, tm_want)
    nt = seq // tm
    row = lambda b, t: (b * nt + t, 0)
    pcol = lambda j: pl.BlockSpec((tm, half), lambda b, t: (b * nt + t, j))
    cstate = pl.BlockSpec((None, CONV_W - 1, half), lambda b, t: (b, 0, 0))
    mem_specs, mem_bytes = _mem_specs(mem_k, w_q, w_o, g_pre, g_post, layer)
    blocks = tm * half * 4 * proj.dtype.itemsize + 2 * tm * d * 4
    resident = (2 * half * d * 2 + mem_bytes + (tm + CONV_PAD) * half * 4 + 4 * tm * half * 4
                + 4 * tm * d * 4)
    return pl.pallas_call(
        _even_out_body,
        grid=(batch, nt),
        in_specs=[pl.BlockSpec((tm, half), row), pcol(4), pcol(5), pcol(6),
                  _const_spec(conv_wt), cstate, _const_spec(w_out), _const_spec(g),
                  pl.BlockSpec((tm, d), row)] + mem_specs,
        out_specs=[pl.BlockSpec((tm, d), row), cstate],
        out_shape=[jax.ShapeDtypeStruct(x.shape, F32),
                   jax.ShapeDtypeStruct(conv_state.shape, F32)],
        scratch_shapes=[pltpu.VMEM((tm + CONV_PAD, half), F32)],
        compiler_params=_params(("parallel", "arbitrary"), blocks, resident),
        name="even_out_mem",
    )(oa, proj, proj, proj, conv_wt, conv_state, w_out, g, x,
      g_pre, w_q, mem_k, mem_v, w_o, g_post)


Q_PRESCALE = MLA_SCALE * LOG2E


def _rope(x, cos, sin_lo, sin_hi):
    half = QK_ROPE // 2
    return (x * cos
            + pltpu.roll(x, V7X_LANES - half, axis=1) * sin_lo
            + pltpu.roll(x, half, axis=1) * sin_hi)


def _mla_proj_body(x_ref, g_ref, wdq_ref, qn_ref, wuq_ref, wdkv_ref, kvn_ref,
                   cos_ref, slo_ref, shi_ref, *rest, q_transposed):
    if q_transposed:
        cost_ref, sint_ref, q_ref, ckv_ref, kpe_ref, kpad_ref = rest
    else:
        q_ref, ckv_ref, kpe_ref, kpad_ref = rest
    cos, slo, shi = cos_ref[...], slo_ref[...], shi_ref[...]
    lora = ckv_ref.shape[1]
    xn = _rms(x_ref[...], g_ref[...]).astype(BF16)
    cq = _rms(_dot(xn, wdq_ref[...]), qn_ref[...]).astype(BF16)
    ckr = _dot(xn, wdkv_ref[...])
    ckv_ref[...] = _rms(ckr[:, :lora], kvn_ref[...])
    kp = _rope(ckr[:, lora:lora + V7X_LANES], cos, slo, shi)
    kpe_ref[...] = kp[:, :QK_ROPE]
    kpad_ref[...] = kp.astype(BF16)
    if q_transposed:
        half = QK_ROPE // 2
        cost, sint = cost_ref[...], sint_ref[...]
        qt = lax.dot_general(wuq_ref[...], cq, NT_DIMS, preferred_element_type=F32) * Q_PRESCALE
        for h in range(MLA_HEADS):
            r0 = h * MLA_QK_PAD
            x1 = qt[r0 + QK_NOPE:r0 + QK_NOPE + half]
            x2 = qt[r0 + QK_NOPE + half:r0 + QK_NOPE + QK_ROPE]
            q_ref[h, 0:QK_NOPE, :] = qt[r0:r0 + QK_NOPE].astype(BF16)
            q_ref[h, QK_NOPE:QK_NOPE + half, :] = (x1 * cost - x2 * sint).astype(BF16)
            q_ref[h, QK_NOPE + half:QK_NOPE + QK_ROPE, :] = (x1 * sint + x2 * cost).astype(BF16)
            q_ref[h, QK_NOPE + QK_ROPE:MLA_QK_PAD, :] = qt[
                r0 + QK_NOPE + QK_ROPE:r0 + MLA_QK_PAD].astype(BF16)
    else:
        q = _dot(cq, wuq_ref[...]) * Q_PRESCALE
        for h in range(MLA_HEADS):
            c0 = h * MLA_QK_PAD
            q_ref[h, :, 0:QK_NOPE] = q[:, c0:c0 + QK_NOPE].astype(BF16)
            q_ref[h, :, QK_NOPE:MLA_QK_PAD] = _rope(
                q[:, c0 + QK_NOPE:c0 + MLA_QK_PAD], cos, slo, shi).astype(BF16)


def _mla_proj(x, g, w_dq, q_norm, w_uq, w_dkv_pad, kv_norm, tables, tables_t=None, tm_want=512):
    q_transposed = tables_t is not None
    rows, d = x.shape
    qlora = w_dq.shape[1]
    lora = kv_norm.shape[1]
    qw = MLA_HEADS * MLA_QK_PAD
    kw = w_dkv_pad.shape[1]
    tm = _row_tile(rows, tm_want)
    full = lambda a: pl.BlockSpec(a.shape, lambda i: (0,) * a.ndim)
    rowspec = lambda n: pl.BlockSpec((tm, n), lambda i: (i, 0))
    blocks = (tm * d * 4 + (d * qlora + qlora * qw + d * kw) * 2 + 4 * tm * V7X_LANES * 4
              + tm * (qw * 2 + lora * 4 + QK_ROPE * 4 + V7X_LANES * 2))
    resident = tm * (d * 6 + qw * 4 + kw * 4 + qlora * 8)
    in_specs = [rowspec(d), full(g), full(w_dq), full(q_norm), full(w_uq),
                full(w_dkv_pad), full(kv_norm),
                rowspec(V7X_LANES), rowspec(V7X_LANES), rowspec(V7X_LANES)]
    operands = [x, g, w_dq, q_norm, w_uq, w_dkv_pad, kv_norm, *tables]
    if q_transposed:
        in_specs += [pl.BlockSpec((QK_ROPE // 2, tm), lambda i: (0, i))] * 2
        operands += list(tables_t)
        q_spec = pl.BlockSpec((MLA_HEADS, MLA_QK_PAD, tm), lambda i: (0, 0, i))
        q_shape = (MLA_HEADS, MLA_QK_PAD, rows)
    else:
        q_spec = pl.BlockSpec((MLA_HEADS, tm, MLA_QK_PAD), lambda i: (0, i, 0))
        q_shape = (MLA_HEADS, rows, MLA_QK_PAD)
    return pl.pallas_call(
        functools.partial(_mla_proj_body, q_transposed=q_transposed),
        grid=(rows // tm,),
        in_specs=in_specs,
        out_specs=[q_spec, rowspec(lora), rowspec(QK_ROPE), rowspec(V7X_LANES)],
        out_shape=[jax.ShapeDtypeStruct(q_shape, BF16),
                   jax.ShapeDtypeStruct((rows, lora), F32),
                   jax.ShapeDtypeStruct((rows, QK_ROPE), F32),
                   jax.ShapeDtypeStruct((rows, V7X_LANES), BF16)],
        compiler_params=_params(("parallel",), blocks, resident),
        name="mla_proj",
    )(*operands)


def _kv_up_body(ckv_ref, kpad_ref, wuk_ref, wuv_ref, k_ref, vt_ref):
    c = ckv_ref[...].astype(BF16)
    kn = _dot(c, wuk_ref[...])
    v = _dot(c, wuv_ref[...])
    ts = c.shape[0]
    for h in range(MLA_HEADS):
        vt_ref[h, 0:V_DIM, :] = v[:, h * V_DIM:(h + 1) * V_DIM].T.astype(BF16)
    ones_row = lax.broadcasted_iota(jnp.int32, (MLA_HEADS, V_EXT - V_DIM, ts), 1) == 0
    vt_ref[:, V_DIM:V_EXT, :] = ones_row.astype(BF16)
    kpad = kpad_ref[...]
    for h in range(MLA_HEADS):
        k_ref[h, :, 0:QK_NOPE] = kn[:, h * QK_NOPE:(h + 1) * QK_NOPE].astype(BF16)
        k_ref[h, :, QK_NOPE:MLA_QK_PAD] = kpad


def _kv_up(ckv, kpad, w_uk, w_uv_all, ts):
    rows, lora = ckv.shape
    assert rows % ts == 0
    rowspec = lambda n: pl.BlockSpec((ts, n), lambda i: (i, 0))
    full = lambda a: pl.BlockSpec(a.shape, lambda i: (0,) * a.ndim)
    kw = MLA_HEADS * MLA_QK_PAD
    vw = MLA_HEADS * V_DIM
    blocks = ts * (lora * 4 + V7X_LANES * 2 + kw * 2 + vw * 2) + 2 * lora * vw * 2
    resident = ts * (kw + vw) * 4
    return pl.pallas_call(
        _kv_up_body,
        grid=(rows // ts,),
        in_specs=[rowspec(lora), rowspec(V7X_LANES), full(w_uk), full(w_uv_all)],
        out_specs=[pl.BlockSpec((MLA_HEADS, ts, MLA_QK_PAD), lambda i: (0, i, 0)),
                   pl.BlockSpec((None, MLA_HEADS, V_EXT, ts), lambda i: (i, 0, 0, 0))],
        out_shape=[jax.ShapeDtypeStruct((MLA_HEADS, rows, MLA_QK_PAD), BF16),
                   jax.ShapeDtypeStruct((rows // ts, MLA_HEADS, V_EXT, ts), BF16)],
        compiler_params=_params(("parallel",), blocks, resident),
        name="kv_up",
    )(ckv, kpad, w_uk, w_uv_all)


ATTN_TILE = 512
ATTN_Q_PER_K = 1
ATTN_SCORE_BUFFERS = 3


def _scores_t(h, qt_ref, k_ref):
    return _dot(k_ref[h], qt_ref[h])


def _softmax_pv_t(h, st, vt_ref, m_ref, acc_ref, mask):
    if mask is not None:
        st = jnp.where(mask, st, -jnp.inf)
    m_prev = m_ref[h]
    m_next = jnp.maximum(m_prev, jnp.max(st, axis=0, keepdims=True))
    alpha = jnp.exp2(m_prev - m_next)
    p = jnp.exp2(st - m_next)
    m_ref[h] = m_next
    acc_ref[h] = alpha * acc_ref[h] + _dot(vt_ref[h], p.astype(BF16))


def _attn_block_t(q_ref, k_ref, vt_ref, m_ref, acc_ref, s_ref, mask):
    nbuf = s_ref.shape[0]
    for h in range(nbuf - 1):
        s_ref[h] = _scores_t(h, q_ref, k_ref)
    for h in range(MLA_HEADS):
        ahead = h + nbuf - 1
        if ahead < MLA_HEADS:
            s_ref[ahead % nbuf] = _scores_t(ahead, q_ref, k_ref)
        _softmax_pv_t(h, s_ref[h % nbuf], vt_ref, m_ref, acc_ref, mask)


def _attn_causal_body(qi_ref, kj_ref, q_ref, k_ref, vt_ref, o_ref, m_ref, acc_ref, s_ref):
    step = pl.program_id(0)
    qi = qi_ref[step]
    kj = kj_ref[step]
    tq, tk = q_ref.shape[2], k_ref.shape[1]
    per_q = tq // tk

    @pl.when(kj == 0)
    def _():
        m_ref[...] = jnp.full(m_ref.shape, NEG_BIG, F32)
        acc_ref[...] = jnp.zeros_like(acc_ref)

    @pl.when(kj < qi * per_q)
    def _():
        _attn_block_t(q_ref, k_ref, vt_ref, m_ref, acc_ref, s_ref, None)

    @pl.when(kj >= qi * per_q)
    def _():
        key_off = 0 if per_q == 1 else (kj - qi * per_q) * tk
        key_chunk = (lax.broadcasted_iota(jnp.int32, (tk, tq), 0) + key_off) // CHUNK
        query_chunk = lax.broadcasted_iota(jnp.int32, (tk, tq), 1) // CHUNK
        _attn_block_t(q_ref, k_ref, vt_ref, m_ref, acc_ref, s_ref, key_chunk <= query_chunk)

    @pl.when(kj == qi * per_q + per_q - 1)
    def _():
        for h in range(MLA_HEADS):
            out = acc_ref[h, 0:V_DIM, :] / acc_ref[h, V_DIM:V_DIM + 1, :]
            o_ref[:, h * V_DIM:(h + 1) * V_DIM] = out.T.astype(o_ref.dtype)


def _attn_causal(q, k, vt, tq, tk):
    rows = q.shape[2]
    assert tk % CHUNK == 0 and tq % tk == 0 and rows % tq == 0 and vt.shape[3] == tk
    per_q = tq // tk
    pairs = [(i, j) for i in range(rows // tq) for j in range(per_q * (i + 1))]
    qi = jnp.asarray(np.array([a for a, _ in pairs], np.int32))
    kj = jnp.asarray(np.array([b for _, b in pairs], np.int32))
    vw = MLA_HEADS * V_DIM
    blocks = MLA_HEADS * (tq * MLA_QK_PAD + tk * (MLA_QK_PAD + V_EXT)) * 2 + tq * vw * 2
    resident = (MLA_HEADS * (V_EXT + 8) * tq + (ATTN_SCORE_BUFFERS + 1) * tk * tq) * 4
    grid_spec = pltpu.PrefetchScalarGridSpec(
        num_scalar_prefetch=2,
        grid=(len(pairs),),
        in_specs=[pl.BlockSpec((MLA_HEADS, MLA_QK_PAD, tq), lambda p, qi, kj: (0, 0, qi[p])),
                  pl.BlockSpec((MLA_HEADS, tk, MLA_QK_PAD), lambda p, qi, kj: (0, kj[p], 0)),
                  pl.BlockSpec((None, MLA_HEADS, V_EXT, tk), lambda p, qi, kj: (kj[p], 0, 0, 0))],
        out_specs=pl.BlockSpec((tq, vw), lambda p, qi, kj: (qi[p], 0)),
        scratch_shapes=[pltpu.VMEM((MLA_HEADS, 1, tq), F32),
                        pltpu.VMEM((MLA_HEADS, V_EXT, tq), F32),
                        pltpu.VMEM((ATTN_SCORE_BUFFERS, tk, tq), F32)])
    return pl.pallas_call(
        _attn_causal_body,
        grid_spec=grid_spec,
        out_shape=jax.ShapeDtypeStruct((rows, vw), BF16),
        compiler_params=_params(("arbitrary",), blocks, resident),
        name="mla_attn_causal",
    )(qi, kj, q, k, vt)


def _attn_latent_body(q_ref, ckvp_ref, kpep_ref, ckvn_ref, kpadn_ref, wukt_ref, wuv_ref, o_ref,
                      ckv_ref, kpe_ref, qa_ref, qp_ref, *, q_pos0):
    tq = q_ref.shape[1]
    past = ckvp_ref.shape[0]
    n_keys = ckv_ref.shape[0]
    n_valid = past + tq

    ckv_ref[0:past, :] = ckvp_ref[...].astype(BF16)
    ckv_ref[past:n_valid, :] = ckvn_ref[...].astype(BF16)
    ckv_ref[n_valid:n_keys, :] = jnp.zeros((n_keys - n_valid, ckv_ref.shape[1]), BF16)
    kpe_ref[...] = jnp.zeros_like(kpe_ref)
    kpe_ref[0:past, 0:QK_ROPE] = kpep_ref[...].astype(BF16)
    kpe_ref[past:n_valid, :] = kpadn_ref[...]

    for h in range(MLA_HEADS):
        rows = slice(h * tq, (h + 1) * tq)
        qa_ref[rows, :] = _dot(q_ref[h, :, 0:QK_NOPE], wukt_ref[h]).astype(BF16)
        qp_ref[rows, :] = q_ref[h, :, QK_NOPE:MLA_QK_PAD]

    s = (lax.dot_general(qa_ref[...], ckv_ref[...], NT_DIMS, preferred_element_type=F32)
         + lax.dot_general(qp_ref[...], kpe_ref[...], NT_DIMS, preferred_element_type=F32))
    s = s.reshape(MLA_HEADS, tq, n_keys)
    query_chunk = (lax.broadcasted_iota(jnp.int32, (tq, n_keys), 0) + q_pos0) // CHUNK
    kpos = lax.broadcasted_iota(jnp.int32, (tq, n_keys), 1)
    mask = (kpos // CHUNK <= query_chunk) & (kpos < n_valid)
    s = jnp.where(mask[None], s, -jnp.inf)
    p = jnp.exp2(s - jnp.max(s, axis=2, keepdims=True))
    inv_l = 1.0 / jnp.sum(p, axis=2, keepdims=True)
    lat = _dot(p.astype(BF16).reshape(MLA_HEADS * tq, n_keys), ckv_ref[...])
    lat = (lat.reshape(MLA_HEADS, tq, -1) * inv_l).astype(BF16)
    for h in range(MLA_HEADS):
        o_ref[:, h * V_DIM:(h + 1) * V_DIM] = _dot(lat[h], wuv_ref[h]).astype(o_ref.dtype)


def _attn_latent(q, ckv_past, kpe_past, ckv_new, kpad_new, w_ukt, w_uv, layer, batch, q_pos0):
    tq = q.shape[1] // batch
    past, lora = ckv_past.shape[2], ckv_past.shape[3]
    n_keys = -(-(past + tq) // V7X_LANES) * V7X_LANES
    vw = MLA_HEADS * V_DIM
    rows = MLA_HEADS * tq
    blocks = (MLA_HEADS * tq * MLA_QK_PAD * 2 + past * (lora + V7X_LANES) * 4
              + tq * (lora * 4 + V7X_LANES * 2) + tq * vw * 2)
    resident = (2 * MLA_HEADS * lora * QK_NOPE * 2 + n_keys * (lora + V7X_LANES) * 2
                + rows * (lora + V7X_LANES) * 2 + 4 * rows * n_keys * 4 + 2 * rows * lora * 4)
    return pl.pallas_call(
        functools.partial(_attn_latent_body, q_pos0=q_pos0),
        grid=(batch,),
        in_specs=[pl.BlockSpec((MLA_HEADS, tq, MLA_QK_PAD), lambda b: (0, b, 0)),
                  pl.BlockSpec((None, None, past, lora), lambda b: (layer, b, 0, 0)),
                  pl.BlockSpec((None, None, past, QK_ROPE), lambda b: (layer, b, 0, 0)),
                  pl.BlockSpec((tq, lora), lambda b: (b, 0)),
                  pl.BlockSpec((tq, V7X_LANES), lambda b: (b, 0)),
                  _const_spec(w_ukt), _const_spec(w_uv)],
        out_specs=pl.BlockSpec((tq, vw), lambda b: (b, 0)),
        out_shape=jax.ShapeDtypeStruct((q.shape[1], vw), BF16),
        scratch_shapes=[pltpu.VMEM((n_keys, lora), BF16), pltpu.VMEM((n_keys, V7X_LANES), BF16),
                        pltpu.VMEM((rows, lora), BF16), pltpu.VMEM((rows, V7X_LANES), BF16)],
        compiler_params=_params(("parallel",), blocks, resident),
        name="mla_attn_latent",
    )(q, ckv_past, kpe_past, ckv_new, kpad_new, w_ukt, w_uv)


def _rope_tables(pos):
    half = QK_ROPE // 2
    inv = ROPE_THETA ** (-jnp.arange(half, dtype=F32) / half)
    ang = pos.astype(F32)[:, None] * inv
    cos, sin = jnp.cos(ang), jnp.sin(ang)
    z = lambda n: jnp.zeros((pos.shape[0], n), F32)
    pad = V7X_LANES - QK_ROPE
    tables = (jnp.concatenate([cos, cos, z(pad)], axis=1),
              jnp.concatenate([-sin, z(half + pad)], axis=1),
              jnp.concatenate([z(half), sin, z(pad)], axis=1))
    return tables, (cos.T, sin.T)


def _prep_mla(w_uq, w_dkv, w_ukv):
    qlora = w_uq.shape[0]
    lora = w_ukv.shape[0]
    wq = w_uq.reshape(qlora, MLA_HEADS, QK_NOPE + QK_ROPE)
    wq = jnp.pad(wq, ((0, 0), (0, 0), (0, MLA_QK_PAD - QK_NOPE - QK_ROPE)))
    w_uq_pad = wq.reshape(qlora, MLA_HEADS * MLA_QK_PAD).astype(BF16)
    w_dkv_pad = jnp.pad(w_dkv, ((0, 0), (0, V7X_LANES - QK_ROPE))).astype(BF16)
    wkv = w_ukv.reshape(lora, MLA_HEADS, QK_NOPE + V_DIM)
    w_uk = wkv[:, :, :QK_NOPE].reshape(lora, MLA_HEADS * QK_NOPE).astype(BF16)
    w_uv_all = wkv[:, :, QK_NOPE:].reshape(lora, MLA_HEADS * V_DIM).astype(BF16)
    w_ukt_heads = jnp.transpose(wkv[:, :, :QK_NOPE], (1, 2, 0)).astype(BF16)
    w_uv_heads = jnp.transpose(wkv[:, :, QK_NOPE:], (1, 0, 2)).astype(BF16)
    return w_uq_pad, w_dkv_pad, w_uk, w_uv_all, w_ukt_heads, w_uv_heads


def _stack(arrays):
    return arrays[0][None] if len(arrays) == 1 else jnp.stack(arrays)


def _trunk(x, batch, seq, past_len, mem_k, mem_v, s_hgrn, s_conv, ckv_past, kpe_past, p):
    depth = p["norm_g"].shape[0]
    rows = batch * seq
    pos = past_len + jnp.arange(seq, dtype=jnp.int32)
    tables, tables_t = _rope_tables(pos)
    tables = tuple(jnp.tile(t, (batch, 1)) for t in tables)
    causal = past_len == 0
    hs, cs, ckvs, kpes = [], [], [], []
    for l in range(depth):
        g = lambda i: p["norm_g"][l, i][None, :]
        x = _ffn(x, g(0), p["ffn_w1"], p["ffn_w2"], g(1), l, 0)
        mem = (g(4), p["w_mem_q"][l], mem_k, mem_v, p["w_mem_o"][l], g(5), l)
        if l % 2 == 0:
            e = l // 2
            proj = _norm_matmul(x, g(2), p["w_in0"][e], BF16, tm_want=1024)
            oa, s_new = _hgrn(proj, p["lbs"][l][None, :], p["hgrn_gnorm"][e][None, :],
                              s_hgrn[e], batch, seq)
            x, c_new = _even_out(oa, proj, p["conv_wt"][e], s_conv[e], p["w_out0"][e],
                                 g(3), x, mem, batch, seq)
            hs.append(s_new)
            cs.append(c_new)
        else:
            o = l // 2
            q, ckv_new, kpe_new, kpad_new = _mla_proj(
                x, g(2), p["mla_w_dq"][o], p["mla_q_norm"][o][None, :],
                p["w_uq_pad_t"][o] if causal else p["w_uq_pad"][o],
                p["w_dkv_pad"][o], p["mla_kv_norm"][o][None, :], tables,
                tables_t if causal else None)
            if causal:
                assert batch == 1
                tk = _row_tile(seq, ATTN_TILE)
                tq = tk * ATTN_Q_PER_K if seq % (tk * ATTN_Q_PER_K) == 0 else tk
                k, vt = _kv_up(ckv_new, kpad_new, p["w_uk"][o], p["w_uv_all"][o], tk)
                att = _attn_causal(q, k, vt, tq, tk)
            else:
                att = _attn_latent(q, ckv_past, kpe_past, ckv_new, kpad_new,
                                   p["w_ukt_heads"][o], p["w_uv_heads"][o], o, batch, past_len)
            x = _proj_residual(att, p["mla_w_o"][o], g(3), x, mem, batch, seq)
            ckvs.append(ckv_new.reshape(batch, seq, -1))
            kpes.append(kpe_new.reshape(batch, seq, -1))
        x = _ffn(x, g(6), p["ffn_w1"], p["ffn_w2"], g(7), l, 1)
    return (x.reshape(batch, seq, -1), _stack(hs), _stack(cs), _stack(ckvs), _stack(kpes))


def kernel(x_prompt, x_sample, mem_prompt, state_hgrn, state_conv, cache_ckv, cache_kpe, cache_mem_k, cache_mem_v, norm_g, ffn_w1, ffn_w2, w_in0, hgrn_lb, hgrn_gnorm, conv_w, w_out0, mla_w_dq, mla_q_norm, mla_w_uq, mla_w_dkv, mla_kv_norm, mla_w_ukv, mla_w_o, mem_norm, w_mem_q, w_mem_kv, w_mem_o):
    batch, seq, d = x_prompt.shape
    dec_batch, dec_seq, _ = x_sample.shape
    past_len = cache_ckv.shape[2]
    depth = norm_g.shape[0]
    n_mem = mem_prompt.shape[1]
    mem_width = MEM_HEADS * MEM_DIM

    prepped = [_prep_mla(mla_w_uq[o], mla_w_dkv[o], mla_w_ukv[o]) for o in range(mla_w_uq.shape[0])]
    p = dict(
        norm_g=norm_g,
        ffn_w1=ffn_w1.astype(BF16), ffn_w2=ffn_w2.astype(BF16),
        w_in0=w_in0.astype(BF16), w_out0=w_out0.astype(BF16),
        lbs=jnp.cumsum(jax.nn.softmax(hgrn_lb.astype(F32), axis=0), axis=0),
        hgrn_gnorm=hgrn_gnorm,
        conv_wt=jnp.swapaxes(conv_w, 1, 2),
        mla_w_dq=mla_w_dq.astype(BF16), mla_q_norm=mla_q_norm, mla_kv_norm=mla_kv_norm,
        w_uq_pad=[t[0] for t in prepped], w_uq_pad_t=[t[0].T for t in prepped],
        w_dkv_pad=[t[1] for t in prepped],
        w_uk=[t[2] for t in prepped], w_uv_all=[t[3] for t in prepped],
        w_ukt_heads=[t[4] for t in prepped], w_uv_heads=[t[5] for t in prepped],
        mla_w_o=mla_w_o.astype(BF16),
        w_mem_q=w_mem_q.astype(BF16), w_mem_o=w_mem_o.astype(BF16),
    )

    mem_rows = mem_prompt.reshape(batch * n_mem, d)
    mks, mvs = [], []
    for l in range(depth):
        kv = _norm_matmul(mem_rows, mem_norm[l][None, :], w_mem_kv[l].astype(BF16), F32,
                          tn_want=2 * mem_width)
        kv = kv.reshape(batch, n_mem, 2, mem_width)
        mks.append(kv[:, :, 0])
        mvs.append(kv[:, :, 1])
    mem_k_p = jnp.stack(mks)
    mem_v_p = jnp.stack(mvs)

    n_even = state_hgrn.shape[0]
    h0 = jnp.zeros((n_even, batch) + state_hgrn.shape[2:], F32)
    c0 = jnp.zeros((n_even, batch) + state_conv.shape[2:], F32)
    y_p, p_hgrn, p_conv, p_ckv, p_kpe = _trunk(
        x_prompt.reshape(batch * seq, d), batch, seq, 0, mem_k_p, mem_v_p, h0, c0, None, None, p)
    y_s, s_hgrn, s_conv, s_ckv, s_kpe = _trunk(
        x_sample.reshape(dec_batch * dec_seq, d), dec_batch, dec_seq, past_len,
        cache_mem_k.reshape(depth, dec_batch, n_mem, mem_width),
        cache_mem_v.reshape(depth, dec_batch, n_mem, mem_width),
        state_hgrn, state_conv, cache_ckv, cache_kpe, p)

    mem_shape = (depth, batch, n_mem, MEM_HEADS, MEM_DIM)
    return (y_p, y_s, p_hgrn, p_conv, p_ckv, p_kpe,
            mem_k_p.reshape(mem_shape), mem_v_p.reshape(mem_shape),
            s_hgrn, s_conv, s_ckv, s_kpe)
```
